```python
import math, functools
import jax, jax.numpy as jnp
from jax import lax
import numpy as np

D_MODEL = 1024
BATCH = 4
SEQ = 8192
DEPTH = 1
DEC_BATCH = 32
DEC_SEQ = 8
PAST_LEN = 16384
PAGE_SIZE = 128

ATT_HEADS = 8
KV_HEADS = 4
HEAD_DIM = 64
ATT_WIDTH = ATT_HEADS * HEAD_DIM
ROT_DIM = HEAD_DIM // 4
ROPE_THETA = 500000.0
IDX_HEADS = 8
IDX_DIM = 64
IDX_ROT_DIM = IDX_DIM // 4
TOPK_MAX = 256
Q_BLOCK = 128
HG_HEADS = 4
HG_DK = 128
HG_DV = 128
HG_WIDTH = HG_HEADS * HG_DV
HG_CHUNK = 64
MIX_WIDTH = ATT_WIDTH + HG_WIDTH
PROJ_SIZES = (ATT_WIDTH,
              KV_HEADS * HEAD_DIM,
              KV_HEADS * HEAD_DIM,
              IDX_HEADS * IDX_DIM,
              IDX_DIM,
              IDX_HEADS,
              HG_HEADS * HG_DK,
              HG_HEADS * HG_DK,
              HG_WIDTH,
              HG_WIDTH)
N_IN = (ATT_WIDTH + 2 * KV_HEADS * HEAD_DIM + IDX_HEADS * IDX_DIM + IDX_DIM + IDX_HEADS
        + 2 * HG_HEADS * HG_DK + 2 * HG_WIDTH)
N_GROUPS = 4
EXPERTS_PER_GROUP = 4
N_EXPERTS = N_GROUPS * EXPERTS_PER_GROUP
TOP_K_INNER = 2
D_EXPERT = 512
MOE_BLOCK = 256
EPS = 1e-6

kernel_name = "hymba_dsa_hgrn2_hmoe_adaln_step"


def _rmsnorm(x, g):
    xf = x.astype(jnp.float32)
    y = xf * lax.rsqrt(jnp.mean(xf * xf, axis=-1, keepdims=True) + EPS)
    return (y * g.astype(jnp.float32)).astype(x.dtype)


def _ada(c, w, b, n):
    m = jnp.einsum('bd,de->be', jax.nn.silu(c), w) + b
    return jnp.split(m[:, None, :], n, axis=-1)


def _rope_partial(x, pos, rot_dim):
    half = rot_dim // 2
    inv_freq = ROPE_THETA ** (-jnp.arange(half, dtype=jnp.float32) * (2.0 / rot_dim))
    ang = pos.astype(jnp.float32)[:, None] * inv_freq[None, :]
    cos, sin = jnp.cos(ang)[:, None, :], jnp.sin(ang)[:, None, :]
    xr = x[..., :rot_dim].astype(jnp.float32)
    x1, x2 = xr[..., :half], xr[..., half:]
    rot = jnp.concatenate([x1 * cos - x2 * sin, x2 * cos + x1 * sin], axis=-1).astype(x.dtype)
    return jnp.concatenate([rot, x[..., rot_dim:]], axis=-1)


def _project(h, w_in, idx_k_g, pos):
    B, T, _ = h.shape
    z = jnp.einsum('btd,de->bte', h, w_in)
    pts, acc = [], 0
    for s in PROJ_SIZES[:-1]:
        acc += s
        pts.append(acc)
    q, k, v, qi, ki, wi, hq, hf, hi, hg = jnp.split(z, pts, axis=-1)
    q = _rope_partial(q.reshape(B, T, ATT_HEADS, HEAD_DIM), pos, ROT_DIM)
    k = _rope_partial(k.reshape(B, T, KV_HEADS, HEAD_DIM), pos, ROT_DIM)
    v = v.reshape(B, T, KV_HEADS, HEAD_DIM)
    qi = _rope_partial(qi.reshape(B, T, IDX_HEADS, IDX_DIM), pos, IDX_ROT_DIM)
    ki = _rope_partial(_rmsnorm(ki, idx_k_g)[:, :, None, :], pos, IDX_ROT_DIM)[:, :, 0, :]
    wi = wi * (IDX_HEADS ** -0.5)
    return q, k, v, qi, ki, wi, hq, hf, hi, hg


def _select_attend(q, qi, wi, ki_all, q_pos, gather_kv):
    B, Tq = q.shape[:2]
    L = ki_all.shape[1]
    topk = min(TOPK_MAX, L // 4)
    s = jnp.einsum('bqhd,bsd->bqhs', qi.astype(jnp.float32), ki_all.astype(jnp.float32)) * (IDX_DIM ** -0.5)
    score = jnp.einsum('bqhs,bqh->bqs', jax.nn.relu(s), wi.astype(jnp.float32))
    admissible = jnp.arange(L)[None, :] <= q_pos[:, None]
    score = jnp.where(admissible[None], score, -jnp.inf)
    _, sel = lax.top_k(score, topk)
    valid = sel <= q_pos[None, :, None]
    k_sel, v_sel = gather_kv(sel)
    qg = q.reshape(B, Tq, KV_HEADS, ATT_HEADS // KV_HEADS, HEAD_DIM).astype(jnp.float32)
    logits = jnp.einsum('bqhgd,bqkhd->bqhgk', qg, k_sel.astype(jnp.float32)) * (HEAD_DIM ** -0.5)
    logits = jnp.where(valid[:, :, None, None, :], logits, -jnp.inf)
    p = jax.nn.softmax(logits, axis=-1)
    o = jnp.einsum('bqhgk,bqkhd->bqhgd', p, v_sel.astype(jnp.float32))
    return o.reshape(B, Tq, ATT_WIDTH).astype(q.dtype)


def _prompt_attention(q, k, v, qi, wi, ki):
    B, T = q.shape[:2]
    nb = T // Q_BLOCK
    b_idx = jnp.arange(B)[:, None, None]

    def gather(sel):
        return k[b_idx, sel], v[b_idx, sel]

    def split_blocks(a):
        return a.reshape(B, nb, Q_BLOCK, *a.shape[2:]).swapaxes(0, 1)

    def blk(args):
        qb, qib, wib, posb = args
        return _select_attend(qb, qib, wib, ki, posb, gather)

    out = lax.map(blk, (split_blocks(q), split_blocks(qi), split_blocks(wi),
                        jnp.arange(T).reshape(nb, Q_BLOCK)))
    return out.swapaxes(0, 1).reshape(B, T, ATT_WIDTH)


def _sample_attention(q, k_new, v_new, qi, wi, ki_new, cache_k_l, cache_v_l, cache_ki_l, page_table):
    DB, T = q.shape[:2]
    n_pages = PAST_LEN // PAGE_SIZE
    past = n_pages * PAGE_SIZE
    ki_past = cache_ki_l[page_table].reshape(DB, past, IDX_DIM).astype(ki_new.dtype)
    ki_all = jnp.concatenate([ki_past, ki_new], axis=1)
    q_pos = past + jnp.arange(T)
    b_idx = jnp.arange(DB)[:, None, None]

    def gather(sel):
        in_past = (sel < past)[..., None, None]
        ps = jnp.minimum(sel, past - 1)
        phys = page_table[b_idx, ps // PAGE_SIZE]
        off = ps % PAGE_SIZE
        ns = jnp.clip(sel - past, 0, T - 1)
        kk = jnp.where(in_past, cache_k_l[phys, off].astype(k_new.dtype), k_new[b_idx, ns])
        vv = jnp.where(in_past, cache_v_l[phys, off].astype(v_new.dtype), v_new[b_idx, ns])
        return kk, vv

    return _select_attend(q, qi, wi, ki_all, q_pos, gather)


def _gla_chunked(q, k, v, log_f, s0):
    B, T, H, DK = q.shape
    C = min(HG_CHUNK, T)
    pad = (-T) % C
    if pad:
        pw = ((0, 0), (0, pad), (0, 0), (0, 0))
        q, k, v, log_f = (jnp.pad(a, pw) for a in (q, k, v, log_f))
    n = (T + pad) // C

    def blocks(a):
        return a.reshape(B, n, C, H, a.shape[-1]).transpose(1, 0, 3, 2, 4)

    q, k, v, log_f = blocks(q), blocks(k), blocks(v), blocks(log_f)
    b = jnp.cumsum(log_f, axis=3)
    b_last = b[:, :, :, -1]
    q_in = q * jnp.exp(b)
    k_in = k * jnp.exp(-b)
    k_st = k * jnp.exp(b_last[:, :, :, None] - b)
    causal = jnp.tril(jnp.ones((C, C), dtype=bool))
    att = jnp.where(causal, jnp.einsum('nbhtd,nbhsd->nbhts', q_in, k_in), 0.0)
    intra = jnp.einsum('nbhts,nbhse->nbhte', att, v)

    def step(S, inp):
        qc, kc, vc, blc, ic = inp
        o = jnp.einsum('bhtd,bhde->bhte', qc, S) + ic
        S = jnp.exp(blc)[..., None] * S + jnp.einsum('bhsd,bhse->bhde', kc, vc)
        return S, o

    s_T, o = lax.scan(step, s0, (q_in, k_st, v, b_last, intra))
    o = o.transpose(1, 0, 3, 2, 4).reshape(B, n * C, H, HG_DV)[:, :T]
    return o, s_T


def _hgrn2(hq, hf, hi, hg, lb, out_g, s0):
    B, T, _ = hq.shape
    q = hq.astype(jnp.float32).reshape(B, T, HG_HEADS, HG_DK) * (HG_DK ** -0.5)
    f = lb + (1.0 - lb) * jax.nn.sigmoid(hf.astype(jnp.float32))
    f = f.reshape(B, T, HG_HEADS, HG_DK)
    k = 1.0 - f
    v = hi.astype(jnp.float32).reshape(B, T, HG_HEADS, HG_DV)
    o, s_T = _gla_chunked(q, k, v, jnp.log(f), s0.astype(jnp.float32))
    o = _rmsnorm(o, out_g.reshape(HG_HEADS, HG_DV)).reshape(B, T, HG_WIDTH)
    o = o * jax.nn.silu(hg.astype(jnp.float32))
    return o.astype(hq.dtype), s_T.astype(s0.dtype)


def _expert_dispatch(xf, experts, weights, w1, w3, w2):
    N, D = xf.shape
    K = experts.shape[1]
    A = N * K
    flat_e = experts.reshape(-1)
    order = jnp.argsort(flat_e)
    se = flat_e[order]
    counts = jnp.bincount(flat_e, length=N_EXPERTS)
    padded = (counts + MOE_BLOCK - 1) // MOE_BLOCK * MOE_BLOCK
    pad_end = jnp.cumsum(padded)
    pad_start = pad_end - padded
    start = jnp.cumsum(counts) - counts
    dest = pad_start[se] + jnp.arange(A) - start[se]
    n_blocks = -(-A // MOE_BLOCK) + N_EXPERTS
    rows = n_blocks * MOE_BLOCK
    tok_of_assign = (order // K).astype(jnp.int32)
    tok = jnp.full((rows,), N, dtype=jnp.int32).at[dest].set(tok_of_assign)
    xpad = jnp.concatenate([xf, jnp.zeros((1, D), xf.dtype)], axis=0)
    xin = xpad[tok].reshape(n_blocks, MOE_BLOCK, D)
    blk_e = jnp.minimum(jnp.searchsorted(pad_end, jnp.arange(n_blocks) * MOE_BLOCK, side='right'),
                        N_EXPERTS - 1)

    def run(args):
        xb, e = args
        return (jax.nn.silu(xb @ w1[e]) * (xb @ w3[e])) @ w2[e]

    yb = lax.map(run, (xin, blk_e)).reshape(rows, D)
    contrib = yb[dest] * weights.reshape(-1)[order][:, None].astype(yb.dtype)
    return jnp.zeros((N, D), yb.dtype).at[tok_of_assign].add(contrib)


def _moe(h, w_group, b_group, w_er, b_er, w1, w3, w2):
    B, T, D = h.shape
    xf = h.reshape(-1, D)
    N = xf.shape[0]
    gp = jax.nn.softmax((xf @ w_group + b_group).astype(jnp.float32), axis=-1)
    g_w, g_sel = lax.top_k(gp, 1)
    elog = (xf @ w_er + b_er).astype(jnp.float32).reshape(N, N_GROUPS, EXPERTS_PER_GROUP)
    elog_g = jnp.take_along_axis(elog, g_sel[:, :, None], axis=1)[:, 0]
    e_w, e_sel = lax.top_k(jax.nn.softmax(elog_g, axis=-1), TOP_K_INNER)
    e_w = e_w / jnp.sum(e_w, axis=-1, keepdims=True)
    weights = g_w * e_w
    experts = g_sel * EXPERTS_PER_GROUP + e_sel
    return _expert_dispatch(xf, experts, weights, w1, w3, w2).reshape(B, T, D)


def _block(x, c, pos, s0, attend, lb, ada_w, ada_b, norm1_g, norm2_g, w_in, idx_k_g, attn_out_g,
           hg_out_g, w_out, w_group, b_group, w_er, b_er, w1, w3, w2):
    sh1, sc1, gt1, sh2, sc2, gt2 = _ada(c, ada_w, ada_b, 6)
    h = _rmsnorm(x, norm1_g) * (1.0 + sc1) + sh1
    q, k, v, qi, ki, wi, hq, hf, hi, hg = _project(h, w_in, idx_k_g, pos)
    a = _rmsnorm(attend(q, k, v, qi, wi, ki), attn_out_g)
    r, s_T = _hgrn2(hq, hf, hi, hg, lb, hg_out_g, s0)
    mix = jnp.einsum('bte,ed->btd', jnp.concatenate([a, r], axis=-1), w_out)
    x = x + gt1 * mix
    h = _rmsnorm(x, norm2_g) * (1.0 + sc2) + sh2
    x = x + gt2 * _moe(h, w_group, b_group, w_er, b_er, w1, w3, w2)
    return x, k, v, ki, s_T


def _final(x, c, g, w, b):
    sh, sc = _ada(c, w, b, 2)
    return _rmsnorm(x, g) * (1.0 + sc) + sh


def setup_inputs(seed: int = 0) -> dict:
    key = jax.random.key(seed)
    ks = jax.random.split(key, 32)
    f32 = jnp.float32
    n_pages = PAST_LEN // PAGE_SIZE
    n_used = DEC_BATCH * n_pages
    n_pool = n_used + n_used // 4

    def nrm(k, shape, s):
        return jax.random.normal(k, shape, f32) * s

    def gain(k, shape):
        return 1.0 + 0.02 * jax.random.normal(k, shape, f32)

    page_table = jax.random.permutation(ks[0], n_pool)[:n_used].reshape(DEC_BATCH, n_pages).astype(jnp.int32)
    d = D_MODEL
    return {
        "x_prompt": nrm(ks[1], (BATCH, SEQ, d), 1.0),
        "x_sample": nrm(ks[2], (DEC_BATCH, DEC_SEQ, d), 1.0),
        "cache_k": nrm(ks[3], (DEPTH, n_pool, PAGE_SIZE, KV_HEADS, HEAD_DIM), 1.0),
        "cache_v": nrm(ks[4], (DEPTH, n_pool, PAGE_SIZE, KV_HEADS, HEAD_DIM), 1.0),
        "cache_kidx": nrm(ks[5], (DEPTH, n_pool, PAGE_SIZE, IDX_DIM), 1.0),
        "state_hgrn": nrm(ks[6], (DEPTH, DEC_BATCH, HG_HEADS, HG_DK, HG_DV), 0.3),
        "page_table": page_table,
        "c_prompt": nrm(ks[7], (BATCH, d), 1.0),
        "c_sample": nrm(ks[8], (DEC_BATCH, d), 1.0),
        "ada_w": nrm(ks[9], (DEPTH, d, 6 * d), 0.3 * d ** -0.5),
        "ada_b": nrm(ks[10], (DEPTH, 6 * d), 0.02),
        "norm1_g": gain(ks[11], (DEPTH, d)),
        "norm2_g": gain(ks[12], (DEPTH, d)),
        "w_in": nrm(ks[13], (DEPTH, d, N_IN), d ** -0.5),
        "idx_k_g": gain(ks[14], (DEPTH, IDX_DIM)),
        "hg_lb_logits": nrm(ks[15], (DEPTH + 1, HG_HEADS * HG_DK), 0.1),
        "attn_out_g": gain(ks[16], (DEPTH, ATT_WIDTH)),
        "hg_out_g": gain(ks[17], (DEPTH, HG_WIDTH)),
        "w_out": nrm(ks[18], (DEPTH, MIX_WIDTH, d), MIX_WIDTH ** -0.5),
        "w_group": nrm(ks[19], (DEPTH, d, N_GROUPS), d ** -0.5),
        "b_group": nrm(ks[20], (DEPTH, N_GROUPS), 0.01),
        "w_expert_router": nrm(ks[21], (DEPTH, d, N_EXPERTS), d ** -0.5),
        "b_expert_router": nrm(ks[22], (DEPTH, N_EXPERTS), 0.01),
        "w1": nrm(ks[23], (DEPTH, N_EXPERTS, d, D_EXPERT), d ** -0.5),
        "w3": nrm(ks[24], (DEPTH, N_EXPERTS, d, D_EXPERT), d ** -0.5),
        "w2": nrm(ks[25], (DEPTH, N_EXPERTS, D_EXPERT, d), D_EXPERT ** -0.5),
        "final_g": gain(ks[26], (d,)),
        "ada_final_w": nrm(ks[27], (d, 2 * d), 0.3 * d ** -0.5),
        "ada_final_b": nrm(ks[28], (2 * d,), 0.02),
    }


def reference(x_prompt, x_sample, cache_k, cache_v, cache_kidx, state_hgrn, page_table, c_prompt, c_sample,
              ada_w, ada_b, norm1_g, norm2_g, w_in, idx_k_g, hg_lb_logits, attn_out_g, hg_out_g, w_out,
              w_group, b_group, w_expert_router, b_expert_router, w1, w3, w2, final_g, ada_final_w,
              ada_final_b):
    lb_all = jnp.cumsum(jax.nn.softmax(hg_lb_logits.astype(jnp.float32), axis=0), axis=0)
    pos_p = jnp.arange(x_prompt.shape[1])
    pos_s = PAST_LEN + jnp.arange(x_sample.shape[1])
    s0_p = jnp.zeros((x_prompt.shape[0], HG_HEADS, HG_DK, HG_DV), x_prompt.dtype)
    yp, ys = x_prompt, x_sample
    kp_l, vp_l, kip_l, sp_l, ks_l, vs_l, kis_l, ss_l = [], [], [], [], [], [], [], []
    for l in range(DEPTH):
        wl = (ada_w[l], ada_b[l], norm1_g[l], norm2_g[l], w_in[l], idx_k_g[l], attn_out_g[l], hg_out_g[l],
              w_out[l], w_group[l], b_group[l], w_expert_router[l], b_expert_router[l], w1[l], w3[l], w2[l])
        yp, kp, vp, kip, sp = _block(yp, c_prompt, pos_p, s0_p, _prompt_attention, lb_all[l], *wl)
        attend_s = functools.partial(_sample_attention, cache_k_l=cache_k[l], cache_v_l=cache_v[l],
                                     cache_ki_l=cache_kidx[l], page_table=page_table)
        ys, ks_, vs_, kis, ss = _block(ys, c_sample, pos_s, state_hgrn[l], attend_s, lb_all[l], *wl)
        kp_l.append(kp); vp_l.append(vp); kip_l.append(kip); sp_l.append(sp)
        ks_l.append(ks_); vs_l.append(vs_); kis_l.append(kis); ss_l.append(ss)
    y_prompt = _final(yp, c_prompt, final_g, ada_final_w, ada_final_b)
    y_sample = _final(ys, c_sample, final_g, ada_final_w, ada_final_b)
    return (y_prompt, y_sample,
            jnp.stack(kp_l), jnp.stack(vp_l), jnp.stack(kip_l), jnp.stack(sp_l),
            jnp.stack(ks_l), jnp.stack(vs_l), jnp.stack(kis_l), jnp.stack(ss_l))
```

```python
import functools
import math

import jax
import jax.numpy as jnp
from jax import lax
from jax.experimental import pallas as pl
from jax.experimental.pallas import tpu as pltpu

F32 = jnp.float32
BF16 = jnp.bfloat16
I32 = jnp.int32

ATT_HEADS = 8
KV_HEADS = 4
HEAD_DIM = 64
ATT_WIDTH = ATT_HEADS * HEAD_DIM
KV_WIDTH = KV_HEADS * HEAD_DIM
ROT_HALF = HEAD_DIM // 8
ROPE_THETA = 500000.0
IDX_HEADS = 8
IDX_DIM = 64
TOPK_MAX = 256
HG_HEADS = 4
HG_DK = 128
HG_DV = 128
HG_WIDTH = HG_HEADS * HG_DV
HG_CHUNK = 64
N_GROUPS = 4
EXPERTS_PER_GROUP = 4
N_EXPERTS = N_GROUPS * EXPERTS_PER_GROUP
D_EXPERT = 512
MOE_BLOCK = 256
EPS = 1e-6

LANES = 128
INT_MIN = -2 ** 31
NEG_BIG = -1e30
VMEM_LIMIT = 56 * 1024 * 1024
PROJ_TILE = 256
ATT_CHUNK = 512
SAMPLE_PAGES_PER_STEP = 8

_Z_Q, _Z_K, _Z_V, _Z_QI, _Z_HQ, _Z_HF, _Z_HI, _Z_HG, _Z_KI, _Z_WI, _Z_END = (
    0, 512, 768, 1024, 1536, 2048, 2560, 3072, 3584, 3712, 3840)


def _cparams(sem):
    return pltpu.CompilerParams(dimension_semantics=sem, vmem_limit_bytes=VMEM_LIMIT)


def _split_bf16(x):
    hi = x.astype(BF16)
    lo = (x - hi.astype(F32)).astype(BF16)
    return hi, lo


def _dot(a, b):
    return jnp.dot(a, b, preferred_element_type=F32)


def _dot_nt(a, b):
    return lax.dot_general(a, b, (((1,), (1,)), ((), ())), preferred_element_type=F32)


def _dot_tn(a, b):
    return lax.dot_general(a, b, (((0,), (0,)), ((), ())), preferred_element_type=F32)


def _dot3(a, b):
    ah, al = _split_bf16(a)
    bh, bl = _split_bf16(b)
    return _dot(ah, bh) + (_dot(ah, bl) + _dot(al, bh))


def _dot3_nt(a, b):
    ah, al = _split_bf16(a)
    bh, bl = _split_bf16(b)
    return _dot_nt(ah, bh) + (_dot_nt(ah, bl) + _dot_nt(al, bh))


def _silu(x):
    return x * (1.0 / (1.0 + jnp.exp(-x)))


def _sigmoid(x):
    return 1.0 / (1.0 + jnp.exp(-x))


def _ada_kernel(c_ref, w_ref, b_ref, o_ref):
    o_ref[...] = _dot3(_silu(c_ref[...]), w_ref[...]) + b_ref[...]


def _ada(c, w, b):
    r, d = c.shape
    e = w.shape[1]
    te = 1024
    return pl.pallas_call(
        _ada_kernel,
        out_shape=jax.ShapeDtypeStruct((r, e), F32),
        grid=(e // te,),
        in_specs=[pl.BlockSpec((r, d), lambda j: (0, 0)),
                  pl.BlockSpec((d, te), lambda j: (0, j)),
                  pl.BlockSpec((1, te), lambda j: (0, j))],
        out_specs=pl.BlockSpec((r, te), lambda j: (0, j)),
        compiler_params=_cparams(("arbitrary",)),
        name="ada",
    )(c, w, b.reshape(1, e))


def _rope_tables(pos):
    p = pos.shape[0]
    inv = ROPE_THETA ** (-jnp.arange(ROT_HALF, dtype=F32) * (2.0 / (2 * ROT_HALF)))
    ang = pos.astype(F32)[:, None] * inv[None, :]
    c, s = jnp.cos(ang), jnp.sin(ang)
    rest = HEAD_DIM - 2 * ROT_HALF
    one, zero, z8 = jnp.ones((p, rest), F32), jnp.zeros((p, rest), F32), jnp.zeros((p, ROT_HALF), F32)
    cos64 = jnp.concatenate([c, c, one], axis=1)
    sa64 = jnp.concatenate([-s, z8, zero], axis=1)
    sb64 = jnp.concatenate([z8, s, zero], axis=1)
    dup = lambda t: jnp.concatenate([t, t], axis=1)
    return dup(cos64), dup(sa64), dup(sb64)


def _proj_kernel(x_ref, sc_ref, sh_ref, g_ref, w_ref, gk_ref, cos_ref, sa_ref, sb_ref,
                 q_ref, k_ref, v_ref, kb_ref, vd_ref, qi_ref, ki_ref, wi_ref,
                 hq_ref, hf_ref, hi_ref, hg_ref):
    x = x_ref[...]
    ms = jnp.mean(x * x, axis=-1, keepdims=True)
    h = x * lax.rsqrt(ms + EPS) * g_ref[...]
    h = h * (1.0 + sc_ref[0]) + sh_ref[0]
    z = _dot(h.astype(BF16), w_ref[...])
    cos, sa, sb = cos_ref[...], sa_ref[...], sb_ref[...]

    def rope(zs):
        n = zs.shape[1] // LANES
        rep = (lambda t: jnp.concatenate([t] * n, axis=1)) if n > 1 else (lambda t: t)
        w = zs.shape[1]
        return (zs * rep(cos) + pltpu.roll(zs, w - ROT_HALF, 1) * rep(sa)
                + pltpu.roll(zs, ROT_HALF, 1) * rep(sb))

    q_ref[...] = (rope(z[:, _Z_Q:_Z_K]) * (HEAD_DIM ** -0.5)).astype(BF16)
    k = rope(z[:, _Z_K:_Z_V])
    k_ref[...] = k
    kb_ref[...] = k.astype(BF16)
    v = z[:, _Z_V:_Z_QI]
    v_ref[...] = v
    lane = lax.broadcasted_iota(I32, (x.shape[0], LANES), 1)
    first = lane < HEAD_DIM
    for j in range(KV_WIDTH // LANES):
        t = v[:, j * LANES:(j + 1) * LANES]
        r = pltpu.roll(t, HEAD_DIM, 1)
        vd_ref[:, (2 * j) * LANES:(2 * j + 1) * LANES] = jnp.where(first, t, r).astype(BF16)
        vd_ref[:, (2 * j + 1) * LANES:(2 * j + 2) * LANES] = jnp.where(first, r, t).astype(BF16)
    qi_ref[...] = rope(z[:, _Z_QI:_Z_HQ]) * (IDX_DIM ** -0.5)
    hq_ref[...] = z[:, _Z_HQ:_Z_HF]
    hf_ref[...] = z[:, _Z_HF:_Z_HI]
    hi_ref[...] = z[:, _Z_HI:_Z_HG]
    hg_ref[...] = z[:, _Z_HG:_Z_KI]
    ks = z[:, _Z_KI:_Z_WI]
    kms = jnp.sum(ks * ks, axis=-1, keepdims=True) * (1.0 / IDX_DIM)
    kn = ks * lax.rsqrt(kms + EPS) * gk_ref[...]
    ki_ref[...] = rope(kn)[:, :IDX_DIM]
    wi_ref[...] = z[:, _Z_WI:_Z_WI + IDX_HEADS] * (IDX_HEADS ** -0.5)


def _proj(x2d, sc, sh, g1, w_r, gk, tabs, tiles_per_group):
    n, d = x2d.shape
    tm = min(PROJ_TILE, n)
    nt = n // tm
    r = sc.shape[1]
    pt = tabs[0].shape[0] // tm
    row = lambda w: pl.BlockSpec((tm, w), lambda i: (i, 0))
    mod = pl.BlockSpec((1, r, d), lambda i: (i // tiles_per_group, 0, 0))
    tab = pl.BlockSpec((tm, LANES), lambda i: (i % pt, 0))
    outs = [(ATT_WIDTH, BF16), (KV_WIDTH, F32), (KV_WIDTH, F32), (KV_WIDTH, BF16), (2 * KV_WIDTH, BF16),
            (ATT_WIDTH, F32), (IDX_DIM, F32), (IDX_HEADS, F32),
            (HG_WIDTH, F32), (HG_WIDTH, F32), (HG_WIDTH, F32), (HG_WIDTH, F32)]
    return pl.pallas_call(
        _proj_kernel,
        out_shape=[jax.ShapeDtypeStruct((n, w), t) for w, t in outs],
        grid=(nt,),
        in_specs=[row(d), mod, mod,
                  pl.BlockSpec((1, d), lambda i: (0, 0)),
                  pl.BlockSpec(w_r.shape, lambda i: (0, 0)),
                  pl.BlockSpec((1, LANES), lambda i: (0, 0)),
                  tab, tab, tab],
        out_specs=[row(w) for w, _ in outs],
        compiler_params=_cparams(("arbitrary",)),
        name="proj",
    )(x2d, sc, sh, g1.reshape(1, d), w_r, gk, *tabs)


def _hgrn_kernel(hq_ref, hf_ref, hi_ref, hg_ref, lb_ref, g_ref, s0_ref, r_ref, st_ref, s_scr, *, chunk, nchunk):
    t = pl.program_id(1)

    @pl.when(t == 0)
    def _():
        for hh in range(HG_HEADS):
            s_scr[hh] = s0_ref[0, hh].T

    row = lax.broadcasted_iota(I32, (chunk, chunk), 0)
    col = lax.broadcasted_iota(I32, (chunk, chunk), 1)
    causal = col <= row
    tri = causal.astype(BF16)
    for c in range(nchunk):
        rs = slice(c * chunk, (c + 1) * chunk)
        for hh in range(HG_HEADS):
            ls = slice(hh * HG_DK, (hh + 1) * HG_DK)
            lb = lb_ref[:, ls]
            f = lb + (1.0 - lb) * _sigmoid(hf_ref[0, rs, ls])
            logf = jnp.log(f)
            kk = 1.0 - f
            q = hq_ref[0, rs, ls] * (HG_DK ** -0.5)
            v = hi_ref[0, rs, ls]
            l0 = logf.astype(BF16)
            r1 = logf - l0.astype(F32)
            l1 = r1.astype(BF16)
            l2 = (r1 - l1.astype(F32)).astype(BF16)
            b = _dot(tri, l0) + (_dot(tri, l1) + _dot(tri, l2))
            bl = b[chunk - 1:chunk, :]
            q_in = (q * jnp.exp(b)).astype(BF16)
            k_in = (kk * jnp.exp(-b)).astype(BF16)
            k_st = (kk * jnp.exp(bl - b)).astype(BF16)
            vb = v.astype(BF16)
            att = jnp.where(causal, _dot_nt(q_in, k_in), 0.0)
            st = s_scr[hh]
            o = _dot_nt(q_in, st.astype(BF16)) + _dot(att.astype(BF16), vb)
            s_scr[hh] = st * jnp.exp(bl) + _dot_tn(vb, k_st)
            on = o * lax.rsqrt(jnp.mean(o * o, axis=-1, keepdims=True) + EPS) * g_ref[:, ls]
            r_ref[0, rs, ls] = (on * _silu(hg_ref[0, rs, ls])).astype(BF16)

    @pl.when(t == pl.num_programs(1) - 1)
    def _():
        for hh in range(HG_HEADS):
            st_ref[0, hh] = s_scr[hh].T


def _hgrn(hq, hf, hi, hg, lb, out_g, s0):
    b, t, w = hq.shape
    chunk = min(HG_CHUNK, t)
    assert t % chunk == 0
    nchunk = min(4, t // chunk)
    tt = chunk * nchunk
    seq = pl.BlockSpec((1, tt, w), lambda i, j: (i, j, 0))
    vec = pl.BlockSpec((1, w), lambda i, j: (0, 0))
    state = pl.BlockSpec((1, HG_HEADS, HG_DK, HG_DV), lambda i, j: (i, 0, 0, 0))
    return pl.pallas_call(
        functools.partial(_hgrn_kernel, chunk=chunk, nchunk=nchunk),
        out_shape=[jax.ShapeDtypeStruct((b, t, w), BF16), jax.ShapeDtypeStruct(s0.shape, F32)],
        grid=(b, t // tt),
        in_specs=[seq, seq, seq, seq, vec, vec, state],
        out_specs=[seq, state],
        scratch_shapes=[pltpu.VMEM((HG_HEADS, HG_DV, HG_DK), F32)],
        compiler_params=_cparams(("arbitrary", "arbitrary")),
        name="hgrn",
    )(hq, hf, hi, hg, lb.reshape(1, w), out_g.reshape(1, w), s0)


def _order_key(score):
    bits = pltpu.bitcast(score, I32)
    bits = jnp.where(bits == INT_MIN, 0, bits)
    return jnp.where(bits >= 0, bits, bits ^ 0x7FFFFFFF)


def _select_threshold(load_keys, nchunk, tq, ck, topk, idx_bits):
    def count(pred):
        def body(c, acc):
            m = jnp.where(pred(load_keys(c), c), 1.0, 0.0)
            part = m[:, 0:LANES]
            for j in range(1, ck // LANES):
                part = part + m[:, j * LANES:(j + 1) * LANES]
            return acc + part
        acc = lax.fori_loop(0, nchunk, body, jnp.zeros((tq, LANES), F32))
        return jnp.sum(acc, axis=1, keepdims=True)

    kf = float(topk)

    def bis_cond(st):
        it, _, done, _ = st
        return jnp.logical_and(it < 32, jnp.min(done) < 0.5)

    def bis_body(st):
        it, tu, done, tsel = st
        cand_u = tu | lax.shift_left(jnp.int32(1), 31 - it)
        cand_s = cand_u ^ INT_MIN
        cnt = count(lambda key, c: key >= cand_s)
        active = done < 0.5
        tu = jnp.where(jnp.logical_and(active, cnt >= kf), cand_u, tu)
        hit = jnp.logical_and(active, cnt == kf)
        tsel = jnp.where(hit, cand_s - 1, tsel)
        done = jnp.where(hit, 1.0, done)
        return it + 1, tu, done, tsel

    zero_i = jnp.zeros((tq, 1), I32)
    _, tu, done, tsel = lax.while_loop(
        bis_cond, bis_body, (jnp.int32(0), zero_i, jnp.zeros((tq, 1), F32), zero_i))
    ts = tu ^ INT_MIN
    need_tie = jnp.logical_and(done < 0.5, tu != 0)
    thr = jnp.where(done > 0.5, tsel, ts)

    def tie_fn():
        rank = kf - count(lambda key, c: key > ts)

        def jbody(it, ju):
            cand = ju | lax.shift_left(jnp.int32(1), idx_bits - 1 - it)

            def pred(key, c):
                pos = c * ck + lax.broadcasted_iota(I32, (tq, ck), 1)
                return jnp.logical_and(key == ts, pos < cand)
            return jnp.where(count(pred) < rank, cand, ju)
        ju = lax.fori_loop(0, idx_bits, jbody, zero_i)
        return jnp.where(need_tie, ju, -1)

    jsel = lax.cond(jnp.max(jnp.where(need_tie, 1.0, 0.0)) > 0.5, tie_fn,
                    lambda: jnp.full((tq, 1), -1, I32))
    return thr, jsel


def _selected(key, pos, thr, jsel):
    return jnp.logical_or(key > thr, jnp.logical_and(key == thr, pos <= jsel))


def _prompt_attn_kernel(q_ref, qi_ref, wi_ref, kit_ref, kt_ref, vd_ref, ga_ref, o_ref, key_scr, *, topk, idx_bits):
    tq, ck = q_ref.shape[1], ATT_CHUNK
    i = pl.program_id(1)
    nchunk = (i * tq + tq + ck - 1) // ck
    qpos = i * tq + lax.broadcasted_iota(I32, (tq, 1), 0)
    qi_hi, qi_lo = _split_bf16(qi_ref[0])
    w = wi_ref[0]

    def score_body(c, carry):
        off = pl.multiple_of(c * ck, ck)
        kt_hi, kt_lo = _split_bf16(kit_ref[0, :, pl.ds(off, ck)])
        acc = jnp.zeros((tq, ck), F32)
        for hh in range(IDX_HEADS):
            hs = slice(hh * IDX_DIM, (hh + 1) * IDX_DIM)
            s = _dot(qi_hi[:, hs], kt_hi) + (_dot(qi_hi[:, hs], kt_lo) + _dot(qi_lo[:, hs], kt_hi))
            acc = acc + jnp.maximum(s, 0.0) * w[:, hh:hh + 1]
        kpos = off + lax.broadcasted_iota(I32, (tq, ck), 1)
        key_scr[:, pl.ds(off, ck)] = jnp.where(kpos <= qpos, _order_key(acc), INT_MIN)
        return carry
    lax.fori_loop(0, nchunk, score_body, 0)

    load_keys = lambda c: key_scr[:, pl.ds(pl.multiple_of(c * ck, ck), ck)]
    thr, jsel = _select_threshold(load_keys, nchunk, tq, ck, topk, idx_bits)

    q = q_ref[0]
    q2 = [jnp.concatenate([q[:, (2 * m) * HEAD_DIM:(2 * m + 1) * HEAD_DIM],
                           q[:, (2 * m + 1) * HEAD_DIM:(2 * m + 2) * HEAD_DIM]], axis=0)
          for m in range(KV_HEADS)]

    def attn_body(c, carry):
        off = pl.multiple_of(c * ck, ck)
        kpos = off + lax.broadcasted_iota(I32, (tq, ck), 1)
        sel = _selected(load_keys(c), kpos, thr, jsel)
        sel2 = jnp.concatenate([sel, sel], axis=0)
        new = []
        for m in range(KV_HEADS):
            mx, l, acc = carry[m]
            lg = _dot(q2[m], kt_ref[0, m * HEAD_DIM:(m + 1) * HEAD_DIM, pl.ds(off, ck)])
            lg = jnp.where(sel2, lg, NEG_BIG)
            mn = jnp.maximum(mx, jnp.max(lg, axis=-1, keepdims=True))
            p = jnp.where(sel2, jnp.exp(lg - mn), 0.0)
            alpha = jnp.exp(mx - mn)
            l = l * alpha + jnp.sum(p, axis=-1, keepdims=True)
            acc = acc * alpha + _dot(p.astype(BF16), vd_ref[0, pl.ds(off, ck), m * LANES:(m + 1) * LANES])
            new.append((mn, l, acc))
        return tuple(new)

    init = tuple((jnp.full((2 * tq, 1), NEG_BIG, F32), jnp.zeros((2 * tq, 1), F32),
                  jnp.zeros((2 * tq, LANES), F32)) for _ in range(KV_HEADS))
    res = lax.fori_loop(0, nchunk, attn_body, init)
    lane = lax.broadcasted_iota(I32, (tq, LANES), 1)
    tiles = []
    for m in range(KV_HEADS):
        _, l, acc = res[m]
        o = acc / l
        tiles.append(jnp.where(lane < HEAD_DIM, o[:tq], o[tq:]))
    a = jnp.concatenate(tiles, axis=1)
    a = a * lax.rsqrt(jnp.mean(a * a, axis=-1, keepdims=True) + EPS) * ga_ref[...]
    o_ref[0] = a.astype(BF16)


def _prompt_attention(q, qi, wi, kit, kt, vd, ga):
    b, t, _ = q.shape
    tq = LANES
    assert t % ATT_CHUNK == 0
    topk = min(TOPK_MAX, t // 4)
    blk = lambda w: pl.BlockSpec((1, tq, w), lambda bi, i: (bi, i, 0))
    full = lambda r, c: pl.BlockSpec((1, r, c), lambda bi, i: (bi, 0, 0))
    return pl.pallas_call(
        functools.partial(_prompt_attn_kernel, topk=topk, idx_bits=max(1, (t - 1).bit_length())),
        out_shape=jax.ShapeDtypeStruct((b, t, ATT_WIDTH), BF16),
        grid=(b, t // tq),
        in_specs=[blk(ATT_WIDTH), blk(ATT_WIDTH), blk(IDX_HEADS),
                  full(IDX_DIM, t), full(KV_WIDTH, t), full(t, 2 * KV_WIDTH),
                  pl.BlockSpec((1, ATT_WIDTH), lambda bi, i: (0, 0))],
        out_specs=blk(ATT_WIDTH),
        scratch_shapes=[pltpu.VMEM((tq, t), I32)],
        compiler_params=_cparams(("arbitrary", "arbitrary")),
        name="prompt_attn",
    )(q, qi, wi, kit, kt, vd, ga.reshape(1, ATT_WIDTH))


def _sample_score_kernel(pt_ref, qi_ref, w_ref, kin_ref, *rest, npg, nsteps, tnew, topk, idx_bits):
    page_refs = rest[:npg]
    key_ref, thr_ref, j_ref = rest[npg:npg + 3]
    s = pl.program_id(1)
    last = s == nsteps
    tq = tnew
    page = page_refs[0].shape[1]
    qi_hi, qi_lo = _split_bf16(qi_ref[0])
    w = w_ref[0]
    qrow = lax.broadcasted_iota(I32, (tq, page), 0)
    col = lax.broadcasted_iota(I32, (tq, page), 1)
    new_ok = jnp.logical_and(col <= qrow, col < tnew)
    for p in range(npg):
        ki = page_refs[p][0]
        if p == 0:
            ki = jnp.where(last, kin_ref[0], ki)
        k_hi, k_lo = _split_bf16(ki)
        sc = _dot_nt(qi_hi, k_hi) + (_dot_nt(qi_hi, k_lo) + _dot_nt(qi_lo, k_hi))
        sc = jnp.maximum(sc, 0.0) * w
        acc = sc[0:tq]
        for hh in range(1, IDX_HEADS):
            acc = acc + sc[hh * tq:(hh + 1) * tq]
        ok = jnp.logical_or(jnp.logical_not(last), new_ok) if p == 0 else jnp.logical_not(last)
        off = pl.multiple_of((s * npg + p) * page, page)
        key_ref[0, :, pl.ds(off, page)] = jnp.where(ok, _order_key(acc), INT_MIN)

    @pl.when(last)
    def _():
        ck = npg * page
        load_keys = lambda c: key_ref[0, :, pl.ds(pl.multiple_of(c * ck, ck), ck)]
        thr, jsel = _select_threshold(load_keys, nsteps + 1, tq, ck, topk, idx_bits)
        thr_ref[0] = jnp.broadcast_to(thr, (tq, LANES))
        j_ref[0] = jnp.broadcast_to(jsel, (tq, LANES))


def _sample_attn_kernel(pt_ref, q_ref, key_ref, thr_ref, j_ref, kn_ref, vn_ref, ga_ref, *rest, npg, nsteps, tnew):
    k_refs = rest[:npg]
    v_refs = rest[npg:2 * npg]
    o_ref = rest[2 * npg]
    m_scr, l_scr, acc_scr = rest[2 * npg + 1:]
    s = pl.program_id(1)
    last = s == nsteps
    tq = tnew
    page = k_refs[0].shape[1]
    ck = npg * page

    @pl.when(s == 0)
    def _():
        m_scr[...] = jnp.full(m_scr.shape, NEG_BIG, F32)
        l_scr[...] = jnp.zeros(l_scr.shape, F32)
        acc_scr[...] = jnp.zeros(acc_scr.shape, F32)

    pos = s * ck + lax.broadcasted_iota(I32, (tq, ck), 1)
    sel = _selected(key_ref[0], pos, thr_ref[0][:, 0:1], j_ref[0][:, 0:1])
    sel2 = jnp.concatenate([sel, sel], axis=0)
    lane = lax.broadcasted_iota(I32, (page, LANES), 1)
    first = lane < HEAD_DIM
    kp, vdp = [], []
    for p in range(npg):
        kpg, vpg = k_refs[p][0], v_refs[p][0]
        if p == 0:
            kpg = jnp.where(last, kn_ref[0], kpg)
            vpg = jnp.where(last, vn_ref[0], vpg)
        kp.append(kpg.astype(BF16))
        tiles = []
        for j in range(KV_WIDTH // LANES):
            t = vpg[:, j * LANES:(j + 1) * LANES]
            r = pltpu.roll(t, HEAD_DIM, 1)
            tiles.append(jnp.where(first, t, r).astype(BF16))
            tiles.append(jnp.where(first, r, t).astype(BF16))
        vdp.append(tiles)
    for m in range(KV_HEADS):
        q2 = q_ref[0, m]
        lg = jnp.concatenate([_dot_nt(q2, kp[p][:, m * HEAD_DIM:(m + 1) * HEAD_DIM]) for p in range(npg)], axis=1)
        lg = jnp.where(sel2, lg, NEG_BIG)
        mx = m_scr[m]
        mn = jnp.maximum(mx, jnp.max(lg, axis=-1, keepdims=True))
        pr = jnp.where(sel2, jnp.exp(lg - mn), 0.0)
        alpha = jnp.exp(mx - mn)
        l_scr[m] = l_scr[m] * alpha + jnp.sum(pr, axis=-1, keepdims=True)
        pv = _dot(pr[:, 0:page].astype(BF16), vdp[0][m])
        for p in range(1, npg):
            pv = pv + _dot(pr[:, p * page:(p + 1) * page].astype(BF16), vdp[p][m])
        acc_scr[m] = acc_scr[m] * alpha + pv
        m_scr[m] = mn

    @pl.when(last)
    def _():
        ln = lax.broadcasted_iota(I32, (tq, LANES), 1)
        tiles = []
        for m in range(KV_HEADS):
            o = acc_scr[m] / l_scr[m]
            tiles.append(jnp.where(ln < HEAD_DIM, o[:tq], o[tq:]))
        a = jnp.concatenate(tiles, axis=1)
        a = a * lax.rsqrt(jnp.mean(a * a, axis=-1, keepdims=True) + EPS) * ga_ref[...]
        o_ref[0] = a.astype(BF16)


def _sample_attention(q, qi, wi, ki_new, k_new, v_new, cache_k, cache_v, cache_ki, page_table, ga):
    db, tn, _ = q.shape
    n_pool, page = cache_ki.shape[0], cache_ki.shape[1]
    n_pages = page_table.shape[1]
    npg = math.gcd(SAMPLE_PAGES_PER_STEP, n_pages)
    nsteps = n_pages // npg
    ck = npg * page
    lpad = (nsteps + 1) * ck
    topk = min(TOPK_MAX, (n_pages * page + tn) // 4)
    assert tn <= page
    ck3 = cache_k.reshape(n_pool, page, KV_WIDTH)
    cv3 = cache_v.reshape(n_pool, page, KV_WIDTH)
    pad_rows = lambda a: jnp.pad(a, ((0, 0), (0, page - tn), (0, 0)))
    qi_hq = qi.reshape(db, tn, IDX_HEADS, IDX_DIM).transpose(0, 2, 1, 3).reshape(db, IDX_HEADS * tn, IDX_DIM)
    w_hq = wi.transpose(0, 2, 1).reshape(db, IDX_HEADS * tn, 1)
    q2 = (q.reshape(db, tn, KV_HEADS, ATT_HEADS // KV_HEADS, HEAD_DIM).transpose(0, 2, 3, 1, 4)
          .reshape(db, KV_HEADS, (ATT_HEADS // KV_HEADS) * tn, HEAD_DIM))
    pt_flat = page_table.reshape(-1).astype(I32)

    def page_spec(width, p):
        def imap(b, s, pt):
            return (pt[b * n_pages + jnp.minimum(s * npg + p, n_pages - 1)], 0, 0)
        return pl.BlockSpec((1, page, width), imap)

    per_b = lambda r, c: pl.BlockSpec((1, r, c), lambda b, s, pt: (b, 0, 0))
    keys, thr, jsel = pl.pallas_call(
        functools.partial(_sample_score_kernel, npg=npg, nsteps=nsteps, tnew=tn, topk=topk,
                          idx_bits=max(1, (lpad - 1).bit_length())),
        out_shape=[jax.ShapeDtypeStruct((db, tn, lpad), I32),
                   jax.ShapeDtypeStruct((db, tn, LANES), I32),
                   jax.ShapeDtypeStruct((db, tn, LANES), I32)],
        grid_spec=pltpu.PrefetchScalarGridSpec(
            num_scalar_prefetch=1,
            grid=(db, nsteps + 1),
            in_specs=[per_b(IDX_HEADS * tn, IDX_DIM), per_b(IDX_HEADS * tn, 1), per_b(page, IDX_DIM)]
                     + [page_spec(IDX_DIM, p) for p in range(npg)],
            out_specs=[per_b(tn, lpad), per_b(tn, LANES), per_b(tn, LANES)]),
        compiler_params=_cparams(("arbitrary", "arbitrary")),
        name="sample_score",
    )(pt_flat, qi_hq, w_hq, pad_rows(ki_new), *([cache_ki] * npg))

    a = pl.pallas_call(
        functools.partial(_sample_attn_kernel, npg=npg, nsteps=nsteps, tnew=tn),
        out_shape=jax.ShapeDtypeStruct((db, tn, ATT_WIDTH), BF16),
        grid_spec=pltpu.PrefetchScalarGridSpec(
            num_scalar_prefetch=1,
            grid=(db, nsteps + 1),
            in_specs=[pl.BlockSpec((1, KV_HEADS, (ATT_HEADS // KV_HEADS) * tn, HEAD_DIM),
                                   lambda b, s, pt: (b, 0, 0, 0)),
                      pl.BlockSpec((1, tn, ck), lambda b, s, pt: (b, 0, s)),
                      per_b(tn, LANES), per_b(tn, LANES),
                      per_b(page, KV_WIDTH), per_b(page, KV_WIDTH),
                      pl.BlockSpec((1, ATT_WIDTH), lambda b, s, pt: (0, 0))]
                     + [page_spec(KV_WIDTH, p) for p in range(npg)] * 2,
            out_specs=per_b(tn, ATT_WIDTH),
            scratch_shapes=[pltpu.VMEM((KV_HEADS, 2 * tn, 1), F32), pltpu.VMEM((KV_HEADS, 2 * tn, 1), F32),
                            pltpu.VMEM((KV_HEADS, 2 * tn, LANES), F32)]),
        compiler_params=_cparams(("arbitrary", "arbitrary")),
        name="sample_attn",
    )(pt_flat, q2, keys, thr, jsel, pad_rows(k_new), pad_rows(v_new), ga.reshape(1, ATT_WIDTH),
      *([ck3] * npg), *([cv3] * npg))
    return a


def _outproj_kernel(a_ref, r_ref, x_ref, gt_ref, sc_ref, sh_ref, g_ref, wa_ref, wr_ref, wrt_ref, brt_ref,
                    x2_ref, h2_ref, route_ref):
    mix = _dot(a_ref[...], wa_ref[...]) + _dot(r_ref[...], wr_ref[...])
    x2 = x_ref[...] + gt_ref[0] * mix
    x2_ref[...] = x2
    h = x2 * lax.rsqrt(jnp.mean(x2 * x2, axis=-1, keepdims=True) + EPS) * g_ref[...]
    h = h * (1.0 + sc_ref[0]) + sh_ref[0]
    h2_ref[...] = h
    lg = (_dot3(h, wrt_ref[...]) + brt_ref[...]).T
    grp = [lg[g:g + 1] for g in range(N_GROUPS)]
    gmax = functools.reduce(jnp.maximum, grp)
    gden = functools.reduce(lambda u, v: u + v, [jnp.exp(g - gmax) for g in grp])
    gsel = jnp.full(gmax.shape, N_GROUPS - 1, I32)
    for g in range(N_GROUPS - 2, -1, -1):
        gsel = jnp.where(grp[g] == gmax, g, gsel)
    gw = 1.0 / gden
    el = []
    for e in range(EXPERTS_PER_GROUP):
        v = lg[N_GROUPS + e:N_GROUPS + e + 1]
        for g in range(1, N_GROUPS):
            row = N_GROUPS + g * EXPERTS_PER_GROUP + e
            v = jnp.where(gsel == g, lg[row:row + 1], v)
        el.append(v)
    emax = functools.reduce(jnp.maximum, el)
    e0 = jnp.full(emax.shape, EXPERTS_PER_GROUP - 1, I32)
    for e in range(EXPERTS_PER_GROUP - 2, -1, -1):
        e0 = jnp.where(el[e] == emax, e, e0)
    rest = [jnp.where(e0 == e, -jnp.inf, el[e]) for e in range(EXPERTS_PER_GROUP)]
    rmax = functools.reduce(jnp.maximum, rest)
    e1 = jnp.full(emax.shape, EXPERTS_PER_GROUP - 1, I32)
    for e in range(EXPERTS_PER_GROUP - 2, -1, -1):
        e1 = jnp.where(jnp.logical_and(rest[e] == rmax, e0 != e), e, e1)
    p1 = jnp.exp(rmax - emax)
    w0 = gw * (1.0 / (1.0 + p1))
    w1 = gw * (p1 / (1.0 + p1))
    base = gsel * EXPERTS_PER_GROUP
    rid = lax.broadcasted_iota(I32, lg.shape, 0)
    rt = jnp.where(rid == 0, (base + e0).astype(F32),
                   jnp.where(rid == 1, (base + e1).astype(F32),
                             jnp.where(rid == 2, w0, jnp.where(rid == 3, w1, 0.0))))
    route_ref[...] = rt.T


def _outproj(a, r, x2d, gt, sc, sh, g2, wa, wr, wrt, brt, tiles_per_group):
    n, d = x2d.shape
    tm = min(PROJ_TILE, n)
    rr = gt.shape[1]
    row = lambda w: pl.BlockSpec((tm, w), lambda i: (i, 0))
    mod = pl.BlockSpec((1, rr, d), lambda i: (i // tiles_per_group, 0, 0))
    const = lambda s: pl.BlockSpec(s, lambda i: (0, 0))
    return pl.pallas_call(
        _outproj_kernel,
        out_shape=[jax.ShapeDtypeStruct((n, d), F32), jax.ShapeDtypeStruct((n, d), F32),
                   jax.ShapeDtypeStruct((n, LANES), F32)],
        grid=(n // tm,),
        in_specs=[row(ATT_WIDTH), row(HG_WIDTH), row(d), mod, mod, mod, const((1, d)),
                  const(wa.shape), const(wr.shape), const(wrt.shape), const(brt.shape)],
        out_specs=[row(d), row(d), row(LANES)],
        compiler_params=_cparams(("arbitrary",)),
        name="outproj",
    )(a, r, x2d, gt, sc, sh, g2.reshape(1, d), wa, wr, wrt, brt)


def _gather_rows(idx_hbm_row, src_hbm, dst, idx_smem, sems, nrows):
    cp = pltpu.make_async_copy(idx_hbm_row, idx_smem, sems.at[0])
    cp.start()
    cp.wait()

    def row_copy(r):
        return pltpu.make_async_copy(src_hbm.at[pl.ds(idx_smem[0, r], 1), :], dst.at[pl.ds(r, 1), :], sems.at[1])

    def start(r, c):
        row_copy(r).start()
        return c

    def wait(r, c):
        row_copy(r).wait()
        return c
    lax.fori_loop(0, nrows, start, 0)
    lax.fori_loop(0, nrows, wait, 0)


def _ffn_kernel(be_ref, nb_ref, tok_ref, h_ref, w1_ref, w3_ref, w2_ref, y_ref, xbuf, idx_smem, sems):
    j = pl.program_id(0)
    blk = xbuf.shape[0]

    @pl.when(j < nb_ref[0])
    def _():
        _gather_rows(tok_ref.at[j], h_ref, xbuf, idx_smem, sems, blk)
        xb = xbuf[...].astype(BF16)
        u = _dot(xb, w1_ref[0])
        g = _dot(xb, w3_ref[0])
        y_ref[...] = _dot((_silu(u) * g).astype(BF16), w2_ref[0])

    @pl.when(j >= nb_ref[0])
    def _():
        y_ref[...] = jnp.zeros(y_ref.shape, F32)


def _ffn(blk_e, n_used, tok, h2, w1, w3, w2, blk):
    nb = tok.shape[0]
    d = h2.shape[1]
    wspec = lambda s: pl.BlockSpec((1,) + s, lambda j, be, nu: (be[j], 0, 0))
    return pl.pallas_call(
        _ffn_kernel,
        out_shape=jax.ShapeDtypeStruct((nb * blk, d), F32),
        grid_spec=pltpu.PrefetchScalarGridSpec(
            num_scalar_prefetch=2,
            grid=(nb,),
            in_specs=[pl.BlockSpec(memory_space=pl.ANY), pl.BlockSpec(memory_space=pl.ANY),
                      wspec(w1.shape[1:]), wspec(w3.shape[1:]), wspec(w2.shape[1:])],
            out_specs=pl.BlockSpec((blk, d), lambda j, be, nu: (j, 0)),
            scratch_shapes=[pltpu.VMEM((blk, d), F32), pltpu.SMEM((1, blk), I32),
                            pltpu.SemaphoreType.DMA((2,))]),
        compiler_params=_cparams(("arbitrary",)),
        name="moe_ffn",
    )(blk_e, n_used, tok, h2, w1, w3, w2)


def _combine_kernel(d0_ref, d1_ref, y_ref, x2_ref, route_ref, gt_ref, sc_ref, sh_ref, g_ref, o_ref,
                    y0, y1, idx_smem, sems):
    i = pl.program_id(0)
    tm = y0.shape[0]
    _gather_rows(d0_ref.at[i], y_ref, y0, idx_smem, sems, tm)
    _gather_rows(d1_ref.at[i], y_ref, y1, idx_smem, sems, tm)
    rt = route_ref[...]
    moe = y0[...] * rt[:, 2:3] + y1[...] * rt[:, 3:4]
    x = x2_ref[...] + gt_ref[0] * moe
    y = x * lax.rsqrt(jnp.mean(x * x, axis=-1, keepdims=True) + EPS) * g_ref[...]
    o_ref[...] = y * (1.0 + sc_ref[0]) + sh_ref[0]


def _combine(d0, d1, yb, x2, route, gt, sc, sh, gf, tiles_per_group):
    n, d = x2.shape
    tm = d0.shape[2]
    rr = gt.shape[1]
    row = pl.BlockSpec((tm, d), lambda i: (i, 0))
    mod = pl.BlockSpec((1, rr, d), lambda i: (i // tiles_per_group, 0, 0))
    hbm = pl.BlockSpec(memory_space=pl.ANY)
    return pl.pallas_call(
        _combine_kernel,
        out_shape=jax.ShapeDtypeStruct((n, d), F32),
        grid=(n // tm,),
        in_specs=[hbm, hbm, hbm, row, pl.BlockSpec((tm, LANES), lambda i: (i, 0)), mod, mod, mod,
                  pl.BlockSpec((1, d), lambda i: (0, 0))],
        out_specs=row,
        scratch_shapes=[pltpu.VMEM((tm, d), F32), pltpu.VMEM((tm, d), F32), pltpu.SMEM((1, tm), I32),
                        pltpu.SemaphoreType.DMA((2,))],
        compiler_params=_cparams(("arbitrary",)),
        name="moe_combine",
    )(d0, d1, yb, x2, route, gt, sc, sh, gf.reshape(1, d))


def _dispatch(route, blk):
    n = route.shape[0]
    flat_e = route[:, 0:2].astype(I32).reshape(-1)
    a = flat_e.shape[0]
    onehot = (flat_e[:, None] == jnp.arange(N_EXPERTS, dtype=I32)[None, :]).astype(I32)
    csum = jnp.cumsum(onehot, axis=0)
    rank = jnp.sum((csum - onehot) * onehot, axis=1)
    counts = csum[-1]
    padded = (counts + blk - 1) // blk * blk
    pad_end = jnp.cumsum(padded)
    pad_start = pad_end - padded
    dest = pad_start[flat_e] + rank
    nb = -(-a // blk) + N_EXPERTS
    tok = jnp.zeros((nb * blk,), I32).at[dest].set(jnp.arange(a, dtype=I32) // 2)
    blk_e = jnp.minimum(jnp.searchsorted(pad_end, jnp.arange(nb, dtype=I32) * blk, side='right'),
                        N_EXPERTS - 1).astype(I32)
    n_used = (pad_end[-1] // blk).astype(I32).reshape(1)
    dest2 = dest.reshape(n, 2)
    return tok.reshape(nb, 1, blk), blk_e, n_used, dest2[:, 0], dest2[:, 1]


def _mods(m, n_chunks, per_token_rows):
    parts = jnp.split(m, n_chunks, axis=-1)
    if per_token_rows is None:
        return [p[:, None, :] for p in parts]
    g, d = parts[0].shape
    tm = min(PROJ_TILE, g * per_token_rows)
    return [jnp.repeat(p, per_token_rows, axis=0).reshape(-1, tm, d) for p in parts]


def _layer(x, mod6, modf, pos, s0, attend, lb, wts, final_g, per_token):
    b, t, d = x.shape
    n = b * t
    (norm1_g, norm2_g, w_r, gk, ga, hg_out_g, wa, wr, wrt, brt, w1, w3, w2) = wts
    tm = min(PROJ_TILE, n)
    tiles_per_group = 1 if per_token else t // tm
    sh1, sc1, gt1, sh2, sc2, gt2 = _mods(mod6, 6, t if per_token else None)
    shf, scf = _mods(modf, 2, t if per_token else None)
    tabs = _rope_tables(jnp.tile(pos, tm // t) if per_token else pos)
    x2d = x.reshape(n, d)
    (q, k, v, kb, vd, qi, ki, wi, hq, hf, hi, hg) = _proj(x2d, sc1, sh1, norm1_g, w_r, gk, tabs, tiles_per_group)
    sq = lambda arr: arr.reshape(b, t, arr.shape[-1])
    r, s_t = _hgrn(sq(hq), sq(hf), sq(hi), sq(hg), lb, hg_out_g, s0)
    a = attend(sq(q), sq(qi), sq(wi), sq(ki), sq(k), sq(v), sq(kb), sq(vd), ga)
    x2, h2, route = _outproj(a.reshape(n, -1), r.reshape(n, -1), x2d, gt1, sc2, sh2, norm2_g,
                             wa, wr, wrt, brt, tiles_per_group)
    blk = MOE_BLOCK if n >= 8 * MOE_BLOCK else 64
    tok, blk_e, n_used, d0, d1 = _dispatch(route, blk)
    yb = _ffn(blk_e, n_used, tok, h2, w1, w3, w2, blk)
    y = _combine(d0.reshape(n // tm, 1, tm), d1.reshape(n // tm, 1, tm), yb, x2, route, gt2, scf, shf,
                 final_g, tiles_per_group)
    return (y.reshape(b, t, d), k.reshape(b, t, KV_HEADS, HEAD_DIM), v.reshape(b, t, KV_HEADS, HEAD_DIM),
            sq(ki), s_t)


def kernel(x_prompt, x_sample, cache_k, cache_v, cache_kidx, state_hgrn, page_table, c_prompt, c_sample,
           ada_w, ada_b, norm1_g, norm2_g, w_in, idx_k_g, hg_lb_logits, attn_out_g, hg_out_g, w_out,
           w_group, b_group, w_expert_router, b_expert_router, w1, w3, w2, final_g, ada_final_w, ada_final_b):
    depth = ada_w.shape[0]
    assert depth == 1, "the final adaLN norm is fused into the (single) layer's combine kernel"
    bp, tp, d = x_prompt.shape
    bs, ts, _ = x_sample.shape
    past = page_table.shape[1] * cache_kidx.shape[2]
    lb_all = jnp.cumsum(jax.nn.softmax(hg_lb_logits.astype(F32), axis=0), axis=0)
    n_c = bp + bs
    c_all = jnp.concatenate([c_prompt, c_sample, jnp.zeros((-n_c % 16, d), F32)], axis=0)
    modf = _ada(c_all, ada_final_w, ada_final_b)
    l = 0
    mod6 = _ada(c_all, ada_w[l], ada_b[l])

    seg = [0]
    for s in (ATT_WIDTH, KV_WIDTH, KV_WIDTH, IDX_HEADS * IDX_DIM, IDX_DIM, IDX_HEADS,
              HG_WIDTH, HG_WIDTH, HG_WIDTH, HG_WIDTH):
        seg.append(seg[-1] + s)
    wl = w_in[l]
    col = lambda i: wl[:, seg[i]:seg[i + 1]]
    zpad = lambda w: jnp.zeros((d, w), wl.dtype)
    w_r = jnp.concatenate([col(0), col(1), col(2), col(3), col(6), col(7), col(8), col(9),
                           col(4), zpad(LANES - IDX_DIM), col(5), zpad(LANES - IDX_HEADS)], axis=1).astype(BF16)
    gk = jnp.concatenate([idx_k_g[l], jnp.zeros((LANES - IDX_DIM,), F32)]).reshape(1, LANES)
    wa = w_out[l, :ATT_WIDTH].astype(BF16)
    wr = w_out[l, ATT_WIDTH:].astype(BF16)
    n_rt = N_GROUPS + N_EXPERTS
    wrt = jnp.concatenate([w_group[l], w_expert_router[l], jnp.zeros((d, LANES - n_rt), F32)], axis=1)
    brt = jnp.concatenate([b_group[l], b_expert_router[l], jnp.zeros((LANES - n_rt,), F32)]).reshape(1, LANES)
    wts = (norm1_g[l], norm2_g[l], w_r, gk, attn_out_g[l], hg_out_g[l], wa, wr, wrt, brt,
           w1[l].astype(BF16), w3[l].astype(BF16), w2[l].astype(BF16))

    def attend_p(q, qi, wi, ki, k, v, kb, vd, ga):
        return _prompt_attention(q, qi, wi, ki.transpose(0, 2, 1), kb.transpose(0, 2, 1), vd, ga)

    def attend_s(q, qi, wi, ki, k, v, kb, vd, ga):
        return _sample_attention(q, qi, wi, ki, k, v, cache_k[l], cache_v[l], cache_kidx[l], page_table, ga)

    s0_p = jnp.zeros((bp, HG_HEADS, HG_DK, HG_DV), F32)
    yp, kp, vp, kip, sp = _layer(x_prompt, mod6[:bp], modf[:bp], jnp.arange(tp), s0_p, attend_p,
                                 lb_all[l], wts, final_g, per_token=False)
    ys, ks, vs, kis, ss = _layer(x_sample, mod6[bp:n_c], modf[bp:n_c], past + jnp.arange(ts), state_hgrn[l],
                                 attend_s, lb_all[l], wts, final_g, per_token=True)
    return (yp, ys, kp[None], vp[None], kip[None], sp[None], ks[None], vs[None], kis[None], ss[None])
```

```python
import functools
import math

import jax
import jax.numpy as jnp
from jax import lax
from jax.experimental import pallas as pl
from jax.experimental.pallas import tpu as pltpu

F32 = jnp.float32
BF16 = jnp.bfloat16
I32 = jnp.int32

ATT_HEADS = 8
KV_HEADS = 4
HEAD_DIM = 64
ATT_WIDTH = ATT_HEADS * HEAD_DIM
KV_WIDTH = KV_HEADS * HEAD_DIM
ROT_HALF = HEAD_DIM // 8
ROPE_THETA = 500000.0
IDX_HEADS = 8
IDX_DIM = 64
TOPK_MAX = 256
HG_HEADS = 4
HG_DK = 128
HG_DV = 128
HG_WIDTH = HG_HEADS * HG_DV
HG_CHUNK = 64
N_GROUPS = 4
EXPERTS_PER_GROUP = 4
N_EXPERTS = N_GROUPS * EXPERTS_PER_GROUP
D_EXPERT = 512
MOE_BLOCK = 256
EPS = 1e-6

LANES = 128
INT_MIN = -2 ** 31
NEG_BIG = -1e30
VMEM_LIMIT = 56 * 1024 * 1024
PROJ_TILE = 256
ATT_CHUNK = 512
SAMPLE_PAGES_PER_STEP = 16

_Z_Q, _Z_K, _Z_V, _Z_QI, _Z_HQ, _Z_HF, _Z_HI, _Z_HG, _Z_KI, _Z_WI, _Z_END = (
    0, 512, 768, 1024, 1536, 2048, 2560, 3072, 3584, 3712, 3840)


def _cparams(sem):
    return pltpu.CompilerParams(dimension_semantics=sem, vmem_limit_bytes=VMEM_LIMIT)


def _split_bf16(x):
    hi = x.astype(BF16)
    lo = (x - hi.astype(F32)).astype(BF16)
    return hi, lo


def _dot(a, b):
    return jnp.dot(a, b, preferred_element_type=F32)


def _dot_nt(a, b):
    return lax.dot_general(a, b, (((1,), (1,)), ((), ())), preferred_element_type=F32)


def _dot_tn(a, b):
    return lax.dot_general(a, b, (((0,), (0,)), ((), ())), preferred_element_type=F32)


def _dot3(a, b):
    ah, al = _split_bf16(a)
    bh, bl = _split_bf16(b)
    return _dot(ah, bh) + (_dot(ah, bl) + _dot(al, bh))


def _dot3_nt(a, b):
    ah, al = _split_bf16(a)
    bh, bl = _split_bf16(b)
    return _dot_nt(ah, bh) + (_dot_nt(ah, bl) + _dot_nt(al, bh))


def _silu(x):
    return x * (1.0 / (1.0 + jnp.exp(-x)))


def _sigmoid(x):
    return 1.0 / (1.0 + jnp.exp(-x))


def _ada_kernel(c_ref, w_ref, b_ref, o_ref):
    o_ref[...] = _dot3(_silu(c_ref[...]), w_ref[...]) + b_ref[...]


def _ada(c, w, b):
    r, d = c.shape
    e = w.shape[1]
    te = 1024
    return pl.pallas_call(
        _ada_kernel,
        out_shape=jax.ShapeDtypeStruct((r, e), F32),
        grid=(e // te,),
        in_specs=[pl.BlockSpec((r, d), lambda j: (0, 0)),
                  pl.BlockSpec((d, te), lambda j: (0, j)),
                  pl.BlockSpec((1, te), lambda j: (0, j))],
        out_specs=pl.BlockSpec((r, te), lambda j: (0, j)),
        compiler_params=_cparams(("arbitrary",)),
        name="ada",
    )(c, w, b.reshape(1, e))


def _rope_tables(pos):
    p = pos.shape[0]
    inv = ROPE_THETA ** (-jnp.arange(ROT_HALF, dtype=F32) * (2.0 / (2 * ROT_HALF)))
    ang = pos.astype(F32)[:, None] * inv[None, :]
    c, s = jnp.cos(ang), jnp.sin(ang)
    rest = HEAD_DIM - 2 * ROT_HALF
    one, zero, z8 = jnp.ones((p, rest), F32), jnp.zeros((p, rest), F32), jnp.zeros((p, ROT_HALF), F32)
    cos64 = jnp.concatenate([c, c, one], axis=1)
    sa64 = jnp.concatenate([-s, z8, zero], axis=1)
    sb64 = jnp.concatenate([z8, s, zero], axis=1)
    dup = lambda t: jnp.concatenate([t, t], axis=1)
    return dup(cos64), dup(sa64), dup(sb64)


def _proj_kernel(x_ref, sc_ref, sh_ref, g_ref, w_ref, gk_ref, cos_ref, sa_ref, sb_ref,
                 q_ref, k_ref, v_ref, kb_ref, vb_ref, qi_ref, ki_ref, wi_ref,
                 hq_ref, hf_ref, hi_ref, hg_ref):
    x = x_ref[...]
    ms = jnp.mean(x * x, axis=-1, keepdims=True)
    h = x * lax.rsqrt(ms + EPS) * g_ref[...]
    h = h * (1.0 + sc_ref[0]) + sh_ref[0]
    z = _dot(h.astype(BF16), w_ref[...])
    cos, sa, sb = cos_ref[...], sa_ref[...], sb_ref[...]

    def rope(zs):
        n = zs.shape[1] // LANES
        rep = (lambda t: jnp.concatenate([t] * n, axis=1)) if n > 1 else (lambda t: t)
        w = zs.shape[1]
        return (zs * rep(cos) + pltpu.roll(zs, w - ROT_HALF, 1) * rep(sa)
                + pltpu.roll(zs, ROT_HALF, 1) * rep(sb))

    q_ref[...] = (rope(z[:, _Z_Q:_Z_K]) * (HEAD_DIM ** -0.5)).astype(BF16)
    k = rope(z[:, _Z_K:_Z_V])
    k_ref[...] = k
    kb_ref[...] = k.astype(BF16)
    v = z[:, _Z_V:_Z_QI]
    v_ref[...] = v
    vb_ref[...] = v.astype(BF16)
    qi_ref[...] = rope(z[:, _Z_QI:_Z_HQ]) * (IDX_DIM ** -0.5)
    hq_ref[...] = z[:, _Z_HQ:_Z_HF]
    hf_ref[...] = z[:, _Z_HF:_Z_HI]
    hi_ref[...] = z[:, _Z_HI:_Z_HG]
    hg_ref[...] = z[:, _Z_HG:_Z_KI]
    ks = z[:, _Z_KI:_Z_WI]
    kms = jnp.sum(ks * ks, axis=-1, keepdims=True) * (1.0 / IDX_DIM)
    kn = ks * lax.rsqrt(kms + EPS) * gk_ref[...]
    ki_ref[...] = rope(kn)[:, :IDX_DIM]
    wi_ref[...] = z[:, _Z_WI:_Z_WI + IDX_HEADS] * (IDX_HEADS ** -0.5)


def _proj(x2d, sc, sh, g1, w_r, gk, tabs, tiles_per_group):
    n, d = x2d.shape
    tm = min(PROJ_TILE, n)
    nt = n // tm
    r = sc.shape[1]
    pt = tabs[0].shape[0] // tm
    row = lambda w: pl.BlockSpec((tm, w), lambda i: (i, 0))
    mod = pl.BlockSpec((1, r, d), lambda i: (i // tiles_per_group, 0, 0))
    tab = pl.BlockSpec((tm, LANES), lambda i: (i % pt, 0))
    outs = [(ATT_WIDTH, BF16), (KV_WIDTH, F32), (KV_WIDTH, F32), (KV_WIDTH, BF16), (KV_WIDTH, BF16),
            (ATT_WIDTH, F32), (IDX_DIM, F32), (IDX_HEADS, F32),
            (HG_WIDTH, F32), (HG_WIDTH, F32), (HG_WIDTH, F32), (HG_WIDTH, F32)]
    return pl.pallas_call(
        _proj_kernel,
        out_shape=[jax.ShapeDtypeStruct((n, w), t) for w, t in outs],
        grid=(nt,),
        in_specs=[row(d), mod, mod,
                  pl.BlockSpec((1, d), lambda i: (0, 0)),
                  pl.BlockSpec(w_r.shape, lambda i: (0, 0)),
                  pl.BlockSpec((1, LANES), lambda i: (0, 0)),
                  tab, tab, tab],
        out_specs=[row(w) for w, _ in outs],
        compiler_params=_cparams(("arbitrary",)),
        name="proj",
    )(x2d, sc, sh, g1.reshape(1, d), w_r, gk, *tabs)


def _hgrn_kernel(hq_ref, hf_ref, hi_ref, hg_ref, lb_ref, g_ref, s0_ref, r_ref, st_ref, s_scr, *, chunk, nchunk):
    t = pl.program_id(1)

    @pl.when(t == 0)
    def _():
        for hh in range(HG_HEADS):
            s_scr[hh] = s0_ref[0, hh].T

    row = lax.broadcasted_iota(I32, (chunk, chunk), 0)
    col = lax.broadcasted_iota(I32, (chunk, chunk), 1)
    causal = col <= row
    tri = causal.astype(BF16)
    for c in range(nchunk):
        rs = slice(c * chunk, (c + 1) * chunk)
        for hh in range(HG_HEADS):
            ls = slice(hh * HG_DK, (hh + 1) * HG_DK)
            lb = lb_ref[:, ls]
            f = lb + (1.0 - lb) * _sigmoid(hf_ref[0, rs, ls])
            logf = jnp.log(f)
            kk = 1.0 - f
            q = hq_ref[0, rs, ls] * (HG_DK ** -0.5)
            v = hi_ref[0, rs, ls]
            l0 = logf.astype(BF16)
            r1 = logf - l0.astype(F32)
            l1 = r1.astype(BF16)
            l2 = (r1 - l1.astype(F32)).astype(BF16)
            b = _dot(tri, l0) + (_dot(tri, l1) + _dot(tri, l2))
            bl = b[chunk - 1:chunk, :]
            q_in = (q * jnp.exp(b)).astype(BF16)
            k_in = (kk * jnp.exp(-b)).astype(BF16)
            k_st = (kk * jnp.exp(bl - b)).astype(BF16)
            vb = v.astype(BF16)
            att = jnp.where(causal, _dot_nt(q_in, k_in), 0.0)
            st = s_scr[hh]
            o = _dot_nt(q_in, st.astype(BF16)) + _dot(att.astype(BF16), vb)
            s_scr[hh] = st * jnp.exp(bl) + _dot_tn(vb, k_st)
            on = o * lax.rsqrt(jnp.mean(o * o, axis=-1, keepdims=True) + EPS) * g_ref[:, ls]
            r_ref[0, rs, ls] = (on * _silu(hg_ref[0, rs, ls])).astype(BF16)

    @pl.when(t == pl.num_programs(1) - 1)
    def _():
        for hh in range(HG_HEADS):
            st_ref[0, hh] = s_scr[hh].T


def _hgrn(hq, hf, hi, hg, lb, out_g, s0):
    b, t, w = hq.shape
    chunk = min(HG_CHUNK, t)
    assert t % chunk == 0
    nchunk = min(4, t // chunk)
    tt = chunk * nchunk
    seq = pl.BlockSpec((1, tt, w), lambda i, j: (i, j, 0))
    vec = pl.BlockSpec((1, w), lambda i, j: (0, 0))
    state = pl.BlockSpec((1, HG_HEADS, HG_DK, HG_DV), lambda i, j: (i, 0, 0, 0))
    return pl.pallas_call(
        functools.partial(_hgrn_kernel, chunk=chunk, nchunk=nchunk),
        out_shape=[jax.ShapeDtypeStruct((b, t, w), BF16), jax.ShapeDtypeStruct(s0.shape, F32)],
        grid=(b, t // tt),
        in_specs=[seq, seq, seq, seq, vec, vec, state],
        out_specs=[seq, state],
        scratch_shapes=[pltpu.VMEM((HG_HEADS, HG_DV, HG_DK), F32)],
        compiler_params=_cparams(("arbitrary", "arbitrary")),
        name="hgrn",
    )(hq, hf, hi, hg, lb.reshape(1, w), out_g.reshape(1, w), s0)


def _order_key(score):
    bits = pltpu.bitcast(score, I32)
    bits = jnp.where(bits == INT_MIN, 0, bits)
    return jnp.where(bits >= 0, bits, bits ^ 0x7FFFFFFF)


def _select_threshold(load_keys, store_keys, nchunk, tq, ck, topk, idx_bits):
    def count(pred):
        def body(c, acc):
            m = jnp.where(pred(load_keys(c), c), 1.0, 0.0)
            part = m[:, 0:LANES]
            for j in range(1, ck // LANES):
                part = part + m[:, j * LANES:(j + 1) * LANES]
            return acc + part
        acc = lax.fori_loop(0, nchunk, body, jnp.zeros((tq, LANES), F32))
        return jnp.sum(acc, axis=1, keepdims=True)

    kf = float(topk)

    def bis_cond(st):
        it, _, done, _ = st
        return jnp.logical_and(it < 32, jnp.min(done) < 0.5)

    def bis_body(st):
        it, tu, done, tsel = st
        cand_u = tu | lax.shift_left(jnp.int32(1), 31 - it)
        cand_s = cand_u ^ INT_MIN
        cnt = count(lambda key, c: key >= cand_s)
        active = done < 0.5
        tu = jnp.where(jnp.logical_and(active, cnt >= kf), cand_u, tu)
        hit = jnp.logical_and(active, cnt == kf)
        tsel = jnp.where(hit, cand_s - 1, tsel)
        done = jnp.where(hit, 1.0, done)
        return it + 1, tu, done, tsel

    zero_i = jnp.zeros((tq, 1), I32)
    _, tu, done, tsel = lax.while_loop(
        bis_cond, bis_body, (jnp.int32(0), zero_i, jnp.zeros((tq, 1), F32), zero_i))
    ts = tu ^ INT_MIN
    need_tie = jnp.logical_and(done < 0.5, tu != 0)
    thr = jnp.where(done > 0.5, tsel, ts)

    @pl.when(jnp.max(jnp.where(need_tie, 1.0, 0.0)) > 0.5)
    def _():
        rank = kf - count(lambda key, c: key > ts)
        pos_of = lambda c: c * ck + lax.broadcasted_iota(I32, (tq, ck), 1)

        def jbody(it, ju):
            cand = ju | lax.shift_left(jnp.int32(1), idx_bits - 1 - it)
            pred = lambda key, c: jnp.logical_and(key == ts, pos_of(c) < cand)
            return jnp.where(count(pred) < rank, cand, ju)
        ju = lax.fori_loop(0, idx_bits, jbody, zero_i)

        def fix(c, carry):
            key = load_keys(c)
            bump = jnp.logical_and(jnp.logical_and(key == ts, pos_of(c) <= ju), need_tie)
            store_keys(c, jnp.where(bump, key + 1, key))
            return carry
        lax.fori_loop(0, nchunk, fix, 0)
    return thr


def _prompt_attn_kernel(q_ref, qi_ref, wi_ref, kit_ref, kt_ref, vb_ref, ga_ref, o_ref,
                        key_scr, m_scr, l_scr, acc_scr, *, topk, idx_bits):
    tq, ck = q_ref.shape[1], ATT_CHUNK
    i = pl.program_id(1)
    nchunk = (i * tq + tq + ck - 1) // ck
    qpos = i * tq + lax.broadcasted_iota(I32, (tq, 1), 0)
    w = wi_ref[0]
    first = lax.broadcasted_iota(I32, (tq, LANES), 1) < HEAD_DIM

    lhs = []
    for j in range(IDX_HEADS // 2):
        t = qi_ref[0, :, j * LANES:(j + 1) * LANES]
        hi = t.astype(BF16).astype(F32)
        lo = t - hi
        rh, rl = pltpu.roll(hi, HEAD_DIM, 1), pltpu.roll(lo, HEAD_DIM, 1)
        lhs.append(jnp.concatenate([jnp.where(first, hi, rh), jnp.where(first, lo, 0.0)], axis=1).astype(BF16))
        lhs.append(jnp.concatenate([jnp.where(first, rh, hi), jnp.where(first, rl, 0.0)], axis=1).astype(BF16))

    def score_body(c, carry):
        off = pl.multiple_of(c * ck, ck)
        kt4 = kit_ref[0, :, pl.ds(off, ck)]
        acc = jnp.zeros((tq, ck), F32)
        for hh in range(IDX_HEADS):
            acc = acc + jnp.maximum(_dot(lhs[hh], kt4), 0.0) * w[:, hh:hh + 1]
        kpos = off + lax.broadcasted_iota(I32, (tq, ck), 1)
        key_scr[:, pl.ds(off, ck)] = jnp.where(kpos <= qpos, _order_key(acc), INT_MIN)
        return carry
    lax.fori_loop(0, nchunk, score_body, 0)

    def load_keys(c):
        return key_scr[:, pl.ds(pl.multiple_of(c * ck, ck), ck)]

    def store_keys(c, val):
        key_scr[:, pl.ds(pl.multiple_of(c * ck, ck), ck)] = val
    thr = _select_threshold(load_keys, store_keys, nchunk, tq, ck, topk, idx_bits)

    q = q_ref[0]
    q2 = [jnp.concatenate([q[:, (2 * m) * HEAD_DIM:(2 * m + 1) * HEAD_DIM],
                           q[:, (2 * m + 1) * HEAD_DIM:(2 * m + 2) * HEAD_DIM]], axis=0)
          for m in range(KV_HEADS)]
    m_scr[...] = jnp.full(m_scr.shape, NEG_BIG, F32)
    l_scr[...] = jnp.zeros(l_scr.shape, F32)
    acc_scr[...] = jnp.zeros(acc_scr.shape, F32)

    def attn_body(c, carry):
        off = pl.multiple_of(c * ck, ck)
        sel = load_keys(c) > thr
        sel2 = jnp.concatenate([sel, sel], axis=0)
        for m in range(KV_HEADS):
            lg = _dot(q2[m], kt_ref[0, m * HEAD_DIM:(m + 1) * HEAD_DIM, pl.ds(off, ck)])
            lg = jnp.where(sel2, lg, NEG_BIG)
            mx = m_scr[m]
            mn = jnp.maximum(mx, jnp.max(lg, axis=-1, keepdims=True))
            p = jnp.exp(lg - mn)
            alpha = jnp.exp(mx - mn)
            l_scr[m] = l_scr[m] * alpha + jnp.sum(p, axis=-1, keepdims=True)
            tile = (m // 2) * LANES
            acc_scr[m] = acc_scr[m] * alpha + _dot(p.astype(BF16), vb_ref[0, pl.ds(off, ck), tile:tile + LANES])
            m_scr[m] = mn
        return carry
    lax.fori_loop(0, nchunk, attn_body, 0)
    a = jnp.concatenate([_merge_head_pair(acc_scr[m] / l_scr[m], m, tq, first) for m in range(KV_HEADS)], axis=1)
    a = a * lax.rsqrt(jnp.mean(a * a, axis=-1, keepdims=True) + EPS) * ga_ref[...]
    o_ref[0] = a.astype(BF16)


def _prompt_attention(q, qi, wi, kit, kt, vd, ga):
    b, t, _ = q.shape
    tq = LANES
    assert t % ATT_CHUNK == 0
    topk = min(TOPK_MAX, t // 4)
    blk = lambda w: pl.BlockSpec((1, tq, w), lambda bi, i: (bi, i, 0))
    full = lambda r, c: pl.BlockSpec((1, r, c), lambda bi, i: (bi, 0, 0))
    return pl.pallas_call(
        functools.partial(_prompt_attn_kernel, topk=topk, idx_bits=max(1, (t - 1).bit_length())),
        out_shape=jax.ShapeDtypeStruct((b, t, ATT_WIDTH), BF16),
        grid=(b, t // tq),
        in_specs=[blk(ATT_WIDTH), blk(ATT_WIDTH), blk(IDX_HEADS),
                  full(4 * IDX_DIM, t), full(KV_WIDTH, t), full(t, KV_WIDTH),
                  pl.BlockSpec((1, ATT_WIDTH), lambda bi, i: (0, 0))],
        out_specs=blk(ATT_WIDTH),
        scratch_shapes=[pltpu.VMEM((tq, t), I32), pltpu.VMEM((KV_HEADS, 2 * tq, 1), F32),
                        pltpu.VMEM((KV_HEADS, 2 * tq, 1), F32), pltpu.VMEM((KV_HEADS, 2 * tq, LANES), F32)],
        compiler_params=_cparams(("arbitrary", "arbitrary")),
        name="prompt_attn",
    )(q, qi, wi, kit, kt, vd, ga.reshape(1, ATT_WIDTH))


def _sample_score_kernel(pt_ref, qi_ref, w_ref, kin_ref, *rest, npg, nsteps, tnew, topk, idx_bits):
    page_refs = rest[:npg]
    key_ref, thr_ref = rest[npg:npg + 2]
    s = pl.program_id(1)
    last = s == nsteps
    tq = tnew
    page = page_refs[0].shape[1]
    ck = npg * page
    qi_hi, qi_lo = _split_bf16(qi_ref[0])
    pages = [page_refs[p][0] for p in range(npg)]
    pages[0] = jnp.where(last, kin_ref[0], pages[0])
    k_hi, k_lo = _split_bf16(jnp.concatenate(pages, axis=0))
    sc = _dot_nt(qi_hi, k_hi) + (_dot_nt(qi_hi, k_lo) + _dot_nt(qi_lo, k_hi))
    sc = jnp.maximum(sc, 0.0) * w_ref[0]
    acc = sc[0:tq]
    for hh in range(1, IDX_HEADS):
        acc = acc + sc[hh * tq:(hh + 1) * tq]
    qrow = lax.broadcasted_iota(I32, (tq, ck), 0)
    col = lax.broadcasted_iota(I32, (tq, ck), 1)
    ok = jnp.logical_or(jnp.logical_not(last), jnp.logical_and(col <= qrow, col < tnew))
    key_ref[0, :, pl.ds(pl.multiple_of(s * ck, ck), ck)] = jnp.where(ok, _order_key(acc), INT_MIN)

    @pl.when(last)
    def _():
        def load_keys(c):
            return key_ref[0, :, pl.ds(pl.multiple_of(c * ck, ck), ck)]

        def store_keys(c, val):
            key_ref[0, :, pl.ds(pl.multiple_of(c * ck, ck), ck)] = val
        thr = _select_threshold(load_keys, store_keys, nsteps + 1, tq, ck, topk, idx_bits)
        thr_ref[0] = jnp.broadcast_to(thr, (tq, LANES))


def _merge_head_pair(acc, m, tq, first):
    g0, g1 = acc[:tq], acc[tq:]
    if m % 2 == 0:
        return jnp.where(first, g0, pltpu.roll(g1, HEAD_DIM, 1))
    return jnp.where(first, pltpu.roll(g0, HEAD_DIM, 1), g1)


def _sample_attn_kernel(pt_ref, q_ref, key_ref, thr_ref, kn_ref, vn_ref, ga_ref, *rest, npg, nsteps, tnew):
    k_refs = rest[:npg]
    v_refs = rest[npg:2 * npg]
    o_ref = rest[2 * npg]
    m_scr, l_scr, acc_scr = rest[2 * npg + 1:]
    s = pl.program_id(1)
    last = s == nsteps
    tq = tnew

    @pl.when(s == 0)
    def _():
        m_scr[...] = jnp.full(m_scr.shape, NEG_BIG, F32)
        l_scr[...] = jnp.zeros(l_scr.shape, F32)
        acc_scr[...] = jnp.zeros(acc_scr.shape, F32)

    sel = key_ref[0] > thr_ref[0][:, 0:1]
    sel2 = jnp.concatenate([sel, sel], axis=0)
    kps = [k_refs[p][0] for p in range(npg)]
    vps = [v_refs[p][0] for p in range(npg)]
    kps[0] = jnp.where(last, kn_ref[0], kps[0])
    vps[0] = jnp.where(last, vn_ref[0], vps[0])
    kall = jnp.concatenate(kps, axis=0).astype(BF16)
    vall = jnp.concatenate(vps, axis=0).astype(BF16)
    for m in range(KV_HEADS):
        lg = _dot_nt(q_ref[0, m], kall[:, m * HEAD_DIM:(m + 1) * HEAD_DIM])
        lg = jnp.where(sel2, lg, NEG_BIG)
        mx = m_scr[m]
        mn = jnp.maximum(mx, jnp.max(lg, axis=-1, keepdims=True))
        pr = jnp.exp(lg - mn)
        alpha = jnp.exp(mx - mn)
        l_scr[m] = l_scr[m] * alpha + jnp.sum(pr, axis=-1, keepdims=True)
        tile = (m // 2) * LANES
        acc_scr[m] = acc_scr[m] * alpha + _dot(pr.astype(BF16), vall[:, tile:tile + LANES])
        m_scr[m] = mn

    @pl.when(last)
    def _():
        first = lax.broadcasted_iota(I32, (tq, LANES), 1) < HEAD_DIM
        a = jnp.concatenate([_merge_head_pair(acc_scr[m] / l_scr[m], m, tq, first) for m in range(KV_HEADS)],
                            axis=1)
        a = a * lax.rsqrt(jnp.mean(a * a, axis=-1, keepdims=True) + EPS) * ga_ref[...]
        o_ref[0] = a.astype(BF16)


def _sample_attention(q, qi, wi, ki_new, k_new, v_new, cache_k, cache_v, cache_ki, page_table, ga):
    db, tn, _ = q.shape
    n_pool, page = cache_ki.shape[0], cache_ki.shape[1]
    n_pages = page_table.shape[1]
    npg = math.gcd(SAMPLE_PAGES_PER_STEP, n_pages)
    nsteps = n_pages // npg
    ck = npg * page
    lpad = (nsteps + 1) * ck
    topk = min(TOPK_MAX, (n_pages * page + tn) // 4)
    assert tn <= page
    ck3 = cache_k.reshape(n_pool, page, KV_WIDTH)
    cv3 = cache_v.reshape(n_pool, page, KV_WIDTH)
    pad_rows = lambda a: jnp.pad(a, ((0, 0), (0, page - tn), (0, 0)))
    qi_hq = qi.reshape(db, tn, IDX_HEADS, IDX_DIM).transpose(0, 2, 1, 3).reshape(db, IDX_HEADS * tn, IDX_DIM)
    w_hq = wi.transpose(0, 2, 1).reshape(db, IDX_HEADS * tn, 1)
    q2 = (q.reshape(db, tn, KV_HEADS, ATT_HEADS // KV_HEADS, HEAD_DIM).transpose(0, 2, 3, 1, 4)
          .reshape(db, KV_HEADS, (ATT_HEADS // KV_HEADS) * tn, HEAD_DIM))
    pt_flat = page_table.reshape(-1).astype(I32)

    def page_spec(width, p):
        def imap(b, s, pt):
            return (pt[b * n_pages + jnp.minimum(s * npg + p, n_pages - 1)], 0, 0)
        return pl.BlockSpec((1, page, width), imap)

    per_b = lambda r, c: pl.BlockSpec((1, r, c), lambda b, s, pt: (b, 0, 0))
    keys, thr = pl.pallas_call(
        functools.partial(_sample_score_kernel, npg=npg, nsteps=nsteps, tnew=tn, topk=topk,
                          idx_bits=max(1, (lpad - 1).bit_length())),
        out_shape=[jax.ShapeDtypeStruct((db, tn, lpad), I32),
                   jax.ShapeDtypeStruct((db, tn, LANES), I32)],
        grid_spec=pltpu.PrefetchScalarGridSpec(
            num_scalar_prefetch=1,
            grid=(db, nsteps + 1),
            in_specs=[per_b(IDX_HEADS * tn, IDX_DIM), per_b(IDX_HEADS * tn, 1), per_b(page, IDX_DIM)]
                     + [page_spec(IDX_DIM, p) for p in range(npg)],
            out_specs=[per_b(tn, lpad), per_b(tn, LANES)]),
        compiler_params=_cparams(("arbitrary", "arbitrary")),
        name="sample_score",
    )(pt_flat, qi_hq, w_hq, pad_rows(ki_new), *([cache_ki] * npg))

    a = pl.pallas_call(
        functools.partial(_sample_attn_kernel, npg=npg, nsteps=nsteps, tnew=tn),
        out_shape=jax.ShapeDtypeStruct((db, tn, ATT_WIDTH), BF16),
        grid_spec=pltpu.PrefetchScalarGridSpec(
            num_scalar_prefetch=1,
            grid=(db, nsteps + 1),
            in_specs=[pl.BlockSpec((1, KV_HEADS, (ATT_HEADS // KV_HEADS) * tn, HEAD_DIM),
                                   lambda b, s, pt: (b, 0, 0, 0)),
                      pl.BlockSpec((1, tn, ck), lambda b, s, pt: (b, 0, s)),
                      per_b(tn, LANES),
                      per_b(page, KV_WIDTH), per_b(page, KV_WIDTH),
                      pl.BlockSpec((1, ATT_WIDTH), lambda b, s, pt: (0, 0))]
                     + [page_spec(KV_WIDTH, p) for p in range(npg)] * 2,
            out_specs=per_b(tn, ATT_WIDTH),
            scratch_shapes=[pltpu.VMEM((KV_HEADS, 2 * tn, 1), F32), pltpu.VMEM((KV_HEADS, 2 * tn, 1), F32),
                            pltpu.VMEM((KV_HEADS, 2 * tn, LANES), F32)]),
        compiler_params=_cparams(("arbitrary", "arbitrary")),
        name="sample_attn",
    )(pt_flat, q2, keys, thr, pad_rows(k_new), pad_rows(v_new), ga.reshape(1, ATT_WIDTH),
      *([ck3] * npg), *([cv3] * npg))
    return a


def _outproj_kernel(a_ref, r_ref, x_ref, gt_ref, sc_ref, sh_ref, g_ref, wa_ref, wr_ref, wrt_ref, brt_ref,
                    x2_ref, h2_ref, route_ref):
    mix = _dot(a_ref[...], wa_ref[...]) + _dot(r_ref[...], wr_ref[...])
    x2 = x_ref[...] + gt_ref[0] * mix
    x2_ref[...] = x2
    h = x2 * lax.rsqrt(jnp.mean(x2 * x2, axis=-1, keepdims=True) + EPS) * g_ref[...]
    h = h * (1.0 + sc_ref[0]) + sh_ref[0]
    h2_ref[...] = h
    lg = (_dot3(h, wrt_ref[...]) + brt_ref[...]).T
    grp = [lg[g:g + 1] for g in range(N_GROUPS)]
    gmax = functools.reduce(jnp.maximum, grp)
    gden = functools.reduce(lambda u, v: u + v, [jnp.exp(g - gmax) for g in grp])
    gsel = jnp.full(gmax.shape, N_GROUPS - 1, I32)
    for g in range(N_GROUPS - 2, -1, -1):
        gsel = jnp.where(grp[g] == gmax, g, gsel)
    gw = 1.0 / gden
    el = []
    for e in range(EXPERTS_PER_GROUP):
        v = lg[N_GROUPS + e:N_GROUPS + e + 1]
        for g in range(1, N_GROUPS):
            row = N_GROUPS + g * EXPERTS_PER_GROUP + e
            v = jnp.where(gsel == g, lg[row:row + 1], v)
        el.append(v)
    emax = functools.reduce(jnp.maximum, el)
    e0 = jnp.full(emax.shape, EXPERTS_PER_GROUP - 1, I32)
    for e in range(EXPERTS_PER_GROUP - 2, -1, -1):
        e0 = jnp.where(el[e] == emax, e, e0)
    rest = [jnp.where(e0 == e, -jnp.inf, el[e]) for e in range(EXPERTS_PER_GROUP)]
    rmax = functools.reduce(jnp.maximum, rest)
    e1 = jnp.full(emax.shape, EXPERTS_PER_GROUP - 1, I32)
    for e in range(EXPERTS_PER_GROUP - 2, -1, -1):
        e1 = jnp.where(jnp.logical_and(rest[e] == rmax, e0 != e), e, e1)
    p1 = jnp.exp(rmax - emax)
    w0 = gw * (1.0 / (1.0 + p1))
    w1 = gw * (p1 / (1.0 + p1))
    base = gsel * EXPERTS_PER_GROUP
    rid = lax.broadcasted_iota(I32, lg.shape, 0)
    rt = jnp.where(rid == 0, (base + e0).astype(F32),
                   jnp.where(rid == 1, (base + e1).astype(F32),
                             jnp.where(rid == 2, w0, jnp.where(rid == 3, w1, 0.0))))
    route_ref[...] = rt.T


def _outproj(a, r, x2d, gt, sc, sh, g2, wa, wr, wrt, brt, tiles_per_group):
    n, d = x2d.shape
    tm = min(PROJ_TILE, n)
    rr = gt.shape[1]
    row = lambda w: pl.BlockSpec((tm, w), lambda i: (i, 0))
    mod = pl.BlockSpec((1, rr, d), lambda i: (i // tiles_per_group, 0, 0))
    const = lambda s: pl.BlockSpec(s, lambda i: (0, 0))
    return pl.pallas_call(
        _outproj_kernel,
        out_shape=[jax.ShapeDtypeStruct((n, d), F32), jax.ShapeDtypeStruct((n, d), F32),
                   jax.ShapeDtypeStruct((n, LANES), F32)],
        grid=(n // tm,),
        in_specs=[row(ATT_WIDTH), row(HG_WIDTH), row(d), mod, mod, mod, const((1, d)),
                  const(wa.shape), const(wr.shape), const(wrt.shape), const(brt.shape)],
        out_specs=[row(d), row(d), row(LANES)],
        compiler_params=_cparams(("arbitrary",)),
        name="outproj",
    )(a, r, x2d, gt, sc, sh, g2.reshape(1, d), wa, wr, wrt, brt)


def _gather_pipeline(step, nsteps, idx_hbm, src_hbm, buf, idx_smem, isem, rsem):
    nrows = buf.shape[1]
    slot = step % 2

    def idx_copy(b, sl):
        return pltpu.make_async_copy(idx_hbm.at[b], idx_smem.at[sl], isem.at[sl])

    def start_rows(sl):
        for r in range(nrows):
            pltpu.make_async_copy(src_hbm.at[pl.ds(idx_smem[sl, 0, r], 1), :],
                                  buf.at[sl, pl.ds(r, 1), :], rsem.at[sl]).start()

    @pl.when(step == 0)
    def _():
        idx_copy(0, 0).start()
        idx_copy(0, 0).wait()
        start_rows(0)

        @pl.when(nsteps > 1)
        def _():
            idx_copy(1, 1).start()

    @pl.when(step + 1 < nsteps)
    def _():
        idx_copy(step + 1, 1 - slot).wait()
        start_rows(1 - slot)

    @pl.when(step + 2 < nsteps)
    def _():
        idx_copy(step + 2, slot).start()

    pltpu.make_async_copy(src_hbm.at[pl.ds(0, nrows), :], buf.at[slot], rsem.at[slot]).wait()
    return slot


def _ffn_kernel(be_ref, nb_ref, tok_ref, h_ref, w1_ref, w3_ref, w2_ref, y_ref, xbuf, idx_smem, isem, rsem):
    j = pl.program_id(0)

    @pl.when(j < nb_ref[0])
    def _():
        slot = _gather_pipeline(j, nb_ref[0], tok_ref, h_ref, xbuf, idx_smem, isem, rsem)
        xb = xbuf[slot].astype(BF16)
        u = _dot(xb, w1_ref[0])
        g = _dot(xb, w3_ref[0])
        y_ref[...] = _dot((_silu(u) * g).astype(BF16), w2_ref[0])

    @pl.when(j >= nb_ref[0])
    def _():
        y_ref[...] = jnp.zeros(y_ref.shape, F32)


def _ffn(blk_e, n_used, tok, h2, w1, w3, w2, blk):
    nb = tok.shape[0]
    d = h2.shape[1]
    wspec = lambda s: pl.BlockSpec((1,) + s, lambda j, be, nu: (be[j], 0, 0))
    return pl.pallas_call(
        _ffn_kernel,
        out_shape=jax.ShapeDtypeStruct((nb * blk, d), F32),
        grid_spec=pltpu.PrefetchScalarGridSpec(
            num_scalar_prefetch=2,
            grid=(nb,),
            in_specs=[pl.BlockSpec(memory_space=pl.ANY), pl.BlockSpec(memory_space=pl.ANY),
                      wspec(w1.shape[1:]), wspec(w3.shape[1:]), wspec(w2.shape[1:])],
            out_specs=pl.BlockSpec((blk, d), lambda j, be, nu: (j, 0)),
            scratch_shapes=[pltpu.VMEM((2, blk, d), F32), pltpu.SMEM((2, 1, blk), I32),
                            pltpu.SemaphoreType.DMA((2,)), pltpu.SemaphoreType.DMA((2,))]),
        compiler_params=_cparams(("arbitrary",)),
        name="moe_ffn",
    )(blk_e, n_used, tok, h2, w1, w3, w2)


def _combine_kernel(dd_ref, y_ref, x2_ref, route_ref, gt_ref, sc_ref, sh_ref, g_ref, o_ref,
                    ybuf, idx_smem, isem, rsem):
    tm = x2_ref.shape[0]
    slot = _gather_pipeline(pl.program_id(0), pl.num_programs(0), dd_ref, y_ref, ybuf, idx_smem, isem, rsem)
    rt = route_ref[...]
    moe = ybuf[slot, 0:tm] * rt[:, 2:3] + ybuf[slot, tm:2 * tm] * rt[:, 3:4]
    x = x2_ref[...] + gt_ref[0] * moe
    y = x * lax.rsqrt(jnp.mean(x * x, axis=-1, keepdims=True) + EPS) * g_ref[...]
    o_ref[...] = y * (1.0 + sc_ref[0]) + sh_ref[0]


def _combine(dd, yb, x2, route, gt, sc, sh, gf, tiles_per_group):
    n, d = x2.shape
    tm = dd.shape[2] // 2
    rr = gt.shape[1]
    row = pl.BlockSpec((tm, d), lambda i: (i, 0))
    mod = pl.BlockSpec((1, rr, d), lambda i: (i // tiles_per_group, 0, 0))
    hbm = pl.BlockSpec(memory_space=pl.ANY)
    return pl.pallas_call(
        _combine_kernel,
        out_shape=jax.ShapeDtypeStruct((n, d), F32),
        grid=(n // tm,),
        in_specs=[hbm, hbm, row, pl.BlockSpec((tm, LANES), lambda i: (i, 0)), mod, mod, mod,
                  pl.BlockSpec((1, d), lambda i: (0, 0))],
        out_specs=row,
        scratch_shapes=[pltpu.VMEM((2, 2 * tm, d), F32), pltpu.SMEM((2, 1, 2 * tm), I32),
                        pltpu.SemaphoreType.DMA((2,)), pltpu.SemaphoreType.DMA((2,))],
        compiler_params=_cparams(("arbitrary",)),
        name="moe_combine",
    )(dd, yb, x2, route, gt, sc, sh, gf.reshape(1, d))


def _dispatch(route, blk):
    n = route.shape[0]
    flat_e = route[:, 0:2].astype(I32).reshape(-1)
    a = flat_e.shape[0]
    onehot = (flat_e[:, None] == jnp.arange(N_EXPERTS, dtype=I32)[None, :]).astype(I32)
    csum = jnp.cumsum(onehot, axis=0)
    rank = jnp.sum((csum - onehot) * onehot, axis=1)
    counts = csum[-1]
    padded = (counts + blk - 1) // blk * blk
    pad_end = jnp.cumsum(padded)
    pad_start = pad_end - padded
    dest = pad_start[flat_e] + rank
    nb = -(-a // blk) + N_EXPERTS
    tok = jnp.zeros((nb * blk,), I32).at[dest].set(jnp.arange(a, dtype=I32) // 2)
    blk_e = jnp.minimum(jnp.searchsorted(pad_end, jnp.arange(nb, dtype=I32) * blk, side='right'),
                        N_EXPERTS - 1).astype(I32)
    n_used = (pad_end[-1] // blk).astype(I32).reshape(1)
    dest2 = dest.reshape(n, 2)
    return tok.reshape(nb, 1, blk), blk_e, n_used, dest2[:, 0], dest2[:, 1]


def _mods(m, n_chunks, per_token_rows):
    parts = jnp.split(m, n_chunks, axis=-1)
    if per_token_rows is None:
        return [p[:, None, :] for p in parts]
    g, d = parts[0].shape
    tm = min(PROJ_TILE, g * per_token_rows)
    return [jnp.repeat(p, per_token_rows, axis=0).reshape(-1, tm, d) for p in parts]


def _layer(x, mod6, modf, pos, s0, attend, lb, wts, final_g, per_token):
    b, t, d = x.shape
    n = b * t
    (norm1_g, norm2_g, w_r, gk, ga, hg_out_g, wa, wr, wrt, brt, w1, w3, w2) = wts
    tm = min(PROJ_TILE, n)
    tiles_per_group = 1 if per_token else t // tm
    sh1, sc1, gt1, sh2, sc2, gt2 = _mods(mod6, 6, t if per_token else None)
    shf, scf = _mods(modf, 2, t if per_token else None)
    tabs = _rope_tables(jnp.tile(pos, tm // t) if per_token else pos)
    x2d = x.reshape(n, d)
    (q, k, v, kb, vd, qi, ki, wi, hq, hf, hi, hg) = _proj(x2d, sc1, sh1, norm1_g, w_r, gk, tabs, tiles_per_group)
    sq = lambda arr: arr.reshape(b, t, arr.shape[-1])
    r, s_t = _hgrn(sq(hq), sq(hf), sq(hi), sq(hg), lb, hg_out_g, s0)
    a = attend(sq(q), sq(qi), sq(wi), sq(ki), sq(k), sq(v), sq(kb), sq(vd), ga)
    x2, h2, route = _outproj(a.reshape(n, -1), r.reshape(n, -1), x2d, gt1, sc2, sh2, norm2_g,
                             wa, wr, wrt, brt, tiles_per_group)
    blk = MOE_BLOCK if n >= 8 * MOE_BLOCK else 64
    tok, blk_e, n_used, d0, d1 = _dispatch(route, blk)
    yb = _ffn(blk_e, n_used, tok, h2, w1, w3, w2, blk)
    dd = jnp.concatenate([d0.reshape(n // tm, 1, tm), d1.reshape(n // tm, 1, tm)], axis=2)
    y = _combine(dd, yb, x2, route, gt2, scf, shf, final_g, tiles_per_group)
    return (y.reshape(b, t, d), k.reshape(b, t, KV_HEADS, HEAD_DIM), v.reshape(b, t, KV_HEADS, HEAD_DIM),
            sq(ki), s_t)


def kernel(x_prompt, x_sample, cache_k, cache_v, cache_kidx, state_hgrn, page_table, c_prompt, c_sample,
           ada_w, ada_b, norm1_g, norm2_g, w_in, idx_k_g, hg_lb_logits, attn_out_g, hg_out_g, w_out,
           w_group, b_group, w_expert_router, b_expert_router, w1, w3, w2, final_g, ada_final_w, ada_final_b):
    depth = ada_w.shape[0]
    assert depth == 1, "the final adaLN norm is fused into the (single) layer's combine kernel"
    bp, tp, d = x_prompt.shape
    bs, ts, _ = x_sample.shape
    past = page_table.shape[1] * cache_kidx.shape[2]
    lb_all = jnp.cumsum(jax.nn.softmax(hg_lb_logits.astype(F32), axis=0), axis=0)
    n_c = bp + bs
    c_all = jnp.concatenate([c_prompt, c_sample, jnp.zeros((-n_c % 16, d), F32)], axis=0)
    modf = _ada(c_all, ada_final_w, ada_final_b)
    l = 0
    mod6 = _ada(c_all, ada_w[l], ada_b[l])

    seg = [0]
    for s in (ATT_WIDTH, KV_WIDTH, KV_WIDTH, IDX_HEADS * IDX_DIM, IDX_DIM, IDX_HEADS,
              HG_WIDTH, HG_WIDTH, HG_WIDTH, HG_WIDTH):
        seg.append(seg[-1] + s)
    wl = w_in[l]
    col = lambda i: wl[:, seg[i]:seg[i + 1]]
    zpad = lambda w: jnp.zeros((d, w), wl.dtype)
    w_r = jnp.concatenate([col(0), col(1), col(2), col(3), col(6), col(7), col(8), col(9),
                           col(4), zpad(LANES - IDX_DIM), col(5), zpad(LANES - IDX_HEADS)], axis=1).astype(BF16)
    gk = jnp.concatenate([idx_k_g[l], jnp.zeros((LANES - IDX_DIM,), F32)]).reshape(1, LANES)
    wa = w_out[l, :ATT_WIDTH].astype(BF16)
    wr = w_out[l, ATT_WIDTH:].astype(BF16)
    n_rt = N_GROUPS + N_EXPERTS
    wrt = jnp.concatenate([w_group[l], w_expert_router[l], jnp.zeros((d, LANES - n_rt), F32)], axis=1)
    brt = jnp.concatenate([b_group[l], b_expert_router[l], jnp.zeros((LANES - n_rt,), F32)]).reshape(1, LANES)
    wts = (norm1_g[l], norm2_g[l], w_r, gk, attn_out_g[l], hg_out_g[l], wa, wr, wrt, brt,
           w1[l].astype(BF16), w3[l].astype(BF16), w2[l].astype(BF16))

    def attend_p(q, qi, wi, ki, k, v, kb, vd, ga):
        kit = ki.transpose(0, 2, 1)
        hi = kit.astype(BF16)
        lo = (kit - hi.astype(F32)).astype(BF16)
        return _prompt_attention(q, qi, wi, jnp.concatenate([hi, lo, hi, hi], axis=1), kb.transpose(0, 2, 1), vd, ga)

    n_pool = cache_k.shape[1]
    pool = lambda c: c.reshape((depth * n_pool,) + c.shape[2:])

    def attend_s(q, qi, wi, ki, k, v, kb, vd, ga):
        return _sample_attention(q, qi, wi, ki, k, v, pool(cache_k), pool(cache_v), pool(cache_kidx),
                                 page_table + l * n_pool, ga)

    s0_p = jnp.zeros((bp, HG_HEADS, HG_DK, HG_DV), F32)
    yp, kp, vp, kip, sp = _layer(x_prompt, mod6[:bp], modf[:bp], jnp.arange(tp), s0_p, attend_p,
                                 lb_all[l], wts, final_g, per_token=False)
    ys, ks, vs, kis, ss = _layer(x_sample, mod6[bp:n_c], modf[bp:n_c], past + jnp.arange(ts), state_hgrn[l],
                                 attend_s, lb_all[l], wts, final_g, per_token=True)
    return (yp, ys, kp[None], vp[None], kip[None], sp[None], ks[None], vs[None], kis[None], ss[None])
```

```python
import functools
import math

import jax
import jax.numpy as jnp
from jax import lax
from jax.experimental import pallas as pl
from jax.experimental.pallas import tpu as pltpu

F32 = jnp.float32
BF16 = jnp.bfloat16
I32 = jnp.int32

ATT_HEADS = 8
KV_HEADS = 4
HEAD_DIM = 64
ATT_WIDTH = ATT_HEADS * HEAD_DIM
KV_WIDTH = KV_HEADS * HEAD_DIM
ROT_HALF = HEAD_DIM // 8
ROPE_THETA = 500000.0
IDX_HEADS = 8
IDX_DIM = 64
TOPK_MAX = 256
HG_HEADS = 4
HG_DK = 128
HG_DV = 128
HG_WIDTH = HG_HEADS * HG_DV
HG_CHUNK = 64
N_GROUPS = 4
EXPERTS_PER_GROUP = 4
N_EXPERTS = N_GROUPS * EXPERTS_PER_GROUP
D_EXPERT = 512
MOE_BLOCK = 256
EPS = 1e-6

LANES = 128
INT_MIN = -2 ** 31
NEG_BIG = -1e30
VMEM_LIMIT = 56 * 1024 * 1024
PROJ_TILE = 256
ATT_CHUNK = 512
VT_ROWS = HEAD_DIM + 16
LOG2E = 1.4426950408889634
BISECT_BITS_PER_TRIP = 2
SAMPLE_SCORE_PAGES = 16
SAMPLE_ATTN_PAGES = 16

_Z_Q, _Z_K, _Z_V, _Z_QI, _Z_HQ, _Z_HF, _Z_HI, _Z_HG, _Z_KI, _Z_WI, _Z_END = (
    0, 512, 768, 1024, 1536, 2048, 2560, 3072, 3584, 3712, 3840)


def _cparams(sem):
    return pltpu.CompilerParams(dimension_semantics=sem, vmem_limit_bytes=VMEM_LIMIT)


def _split_bf16(x):
    hi = x.astype(BF16)
    lo = (x - hi.astype(F32)).astype(BF16)
    return hi, lo


def _dot(a, b):
    return jnp.dot(a, b, preferred_element_type=F32)


def _dot_nt(a, b):
    return lax.dot_general(a, b, (((1,), (1,)), ((), ())), preferred_element_type=F32)


def _dot_tn(a, b):
    return lax.dot_general(a, b, (((0,), (0,)), ((), ())), preferred_element_type=F32)


def _dot3(a, b):
    ah, al = _split_bf16(a)
    bh, bl = _split_bf16(b)
    return _dot(ah, bh) + (_dot(ah, bl) + _dot(al, bh))


def _dot3_nt(a, b):
    ah, al = _split_bf16(a)
    bh, bl = _split_bf16(b)
    return _dot_nt(ah, bh) + (_dot_nt(ah, bl) + _dot_nt(al, bh))


def _silu(x):
    return x * (1.0 / (1.0 + jnp.exp(-x)))


def _sigmoid(x):
    return 1.0 / (1.0 + jnp.exp(-x))


def _ada_kernel(c_ref, w_ref, b_ref, o_ref):
    o_ref[...] = _dot3(_silu(c_ref[...]), w_ref[...]) + b_ref[...]


def _ada(c, w, b):
    r, d = c.shape
    e = w.shape[1]
    te = 1024
    return pl.pallas_call(
        _ada_kernel,
        out_shape=jax.ShapeDtypeStruct((r, e), F32),
        grid=(e // te,),
        in_specs=[pl.BlockSpec((r, d), lambda j: (0, 0)),
                  pl.BlockSpec((d, te), lambda j: (0, j)),
                  pl.BlockSpec((1, te), lambda j: (0, j))],
        out_specs=pl.BlockSpec((r, te), lambda j: (0, j)),
        compiler_params=_cparams(("arbitrary",)),
        name="ada",
    )(c, w, b.reshape(1, e))


def _rope_tables(pos):
    p = pos.shape[0]
    inv = ROPE_THETA ** (-jnp.arange(ROT_HALF, dtype=F32) * (2.0 / (2 * ROT_HALF)))
    ang = pos.astype(F32)[:, None] * inv[None, :]
    c, s = jnp.cos(ang), jnp.sin(ang)
    rest = HEAD_DIM - 2 * ROT_HALF
    one, zero, z8 = jnp.ones((p, rest), F32), jnp.zeros((p, rest), F32), jnp.zeros((p, ROT_HALF), F32)
    cos64 = jnp.concatenate([c, c, one], axis=1)
    sa64 = jnp.concatenate([-s, z8, zero], axis=1)
    sb64 = jnp.concatenate([z8, s, zero], axis=1)
    dup = lambda t: jnp.concatenate([t, t], axis=1)
    return dup(cos64), dup(sa64), dup(sb64)


def _proj_kernel(x_ref, sc_ref, sh_ref, g_ref, w_ref, gk_ref, cos_ref, sa_ref, sb_ref,
                 q_ref, k_ref, v_ref, kb_ref, vb_ref, qi_ref, ki_ref, wi_ref,
                 hq_ref, hf_ref, hi_ref, hg_ref):
    x = x_ref[...]
    ms = jnp.mean(x * x, axis=-1, keepdims=True)
    h = x * lax.rsqrt(ms + EPS) * g_ref[...]
    h = h * (1.0 + sc_ref[0]) + sh_ref[0]
    z = _dot(h.astype(BF16), w_ref[...])
    cos, sa, sb = cos_ref[...], sa_ref[...], sb_ref[...]

    def rope(zs):
        n = zs.shape[1] // LANES
        rep = (lambda t: jnp.concatenate([t] * n, axis=1)) if n > 1 else (lambda t: t)
        w = zs.shape[1]
        return (zs * rep(cos) + pltpu.roll(zs, w - ROT_HALF, 1) * rep(sa)
                + pltpu.roll(zs, ROT_HALF, 1) * rep(sb))

    q_ref[...] = (rope(z[:, _Z_Q:_Z_K]) * (HEAD_DIM ** -0.5 * LOG2E)).astype(BF16)
    k = rope(z[:, _Z_K:_Z_V])
    k_ref[...] = k
    kb_ref[...] = k.astype(BF16)
    v = z[:, _Z_V:_Z_QI]
    v_ref[...] = v
    vb_ref[...] = v.astype(BF16)
    qi_ref[...] = rope(z[:, _Z_QI:_Z_HQ]) * (IDX_DIM ** -0.5)
    hq_ref[...] = z[:, _Z_HQ:_Z_HF]
    hf_ref[...] = z[:, _Z_HF:_Z_HI]
    hi_ref[...] = z[:, _Z_HI:_Z_HG]
    hg_ref[...] = z[:, _Z_HG:_Z_KI]
    ks = z[:, _Z_KI:_Z_WI]
    kms = jnp.sum(ks * ks, axis=-1, keepdims=True) * (1.0 / IDX_DIM)
    kn = ks * lax.rsqrt(kms + EPS) * gk_ref[...]
    ki_ref[...] = rope(kn)[:, :IDX_DIM]
    wi_ref[...] = z[:, _Z_WI:_Z_WI + IDX_HEADS] * (IDX_HEADS ** -0.5)


def _proj(x2d, sc, sh, g1, w_r, gk, tabs, tiles_per_group):
    n, d = x2d.shape
    tm = min(PROJ_TILE, n)
    nt = n // tm
    r = sc.shape[1]
    pt = tabs[0].shape[0] // tm
    row = lambda w: pl.BlockSpec((tm, w), lambda i: (i, 0))
    mod = pl.BlockSpec((1, r, d), lambda i: (i // tiles_per_group, 0, 0))
    tab = pl.BlockSpec((tm, LANES), lambda i: (i % pt, 0))
    outs = [(ATT_WIDTH, BF16), (KV_WIDTH, F32), (KV_WIDTH, F32), (KV_WIDTH, BF16), (KV_WIDTH, BF16),
            (ATT_WIDTH, F32), (IDX_DIM, F32), (IDX_HEADS, F32),
            (HG_WIDTH, F32), (HG_WIDTH, F32), (HG_WIDTH, F32), (HG_WIDTH, F32)]
    return pl.pallas_call(
        _proj_kernel,
        out_shape=[jax.ShapeDtypeStruct((n, w), t) for w, t in outs],
        grid=(nt,),
        in_specs=[row(d), mod, mod,
                  pl.BlockSpec((1, d), lambda i: (0, 0)),
                  pl.BlockSpec(w_r.shape, lambda i: (0, 0)),
                  pl.BlockSpec((1, LANES), lambda i: (0, 0)),
                  tab, tab, tab],
        out_specs=[row(w) for w, _ in outs],
        compiler_params=_cparams(("arbitrary",)),
        name="proj",
    )(x2d, sc, sh, g1.reshape(1, d), w_r, gk, *tabs)


def _hgrn_kernel(hq_ref, hf_ref, hi_ref, hg_ref, lb_ref, g_ref, s0_ref, r_ref, st_ref, s_scr, *, chunk, nchunk):
    t = pl.program_id(1)

    @pl.when(t == 0)
    def _():
        for hh in range(HG_HEADS):
            s_scr[hh] = s0_ref[0, hh].T

    row = lax.broadcasted_iota(I32, (chunk, chunk), 0)
    col = lax.broadcasted_iota(I32, (chunk, chunk), 1)
    causal = col <= row
    tri = causal.astype(BF16)
    for c in range(nchunk):
        rs = slice(c * chunk, (c + 1) * chunk)
        for hh in range(HG_HEADS):
            ls = slice(hh * HG_DK, (hh + 1) * HG_DK)
            lb = lb_ref[:, ls]
            f = lb + (1.0 - lb) * _sigmoid(hf_ref[0, rs, ls])
            logf = jnp.log(f)
            kk = 1.0 - f
            q = hq_ref[0, rs, ls] * (HG_DK ** -0.5)
            v = hi_ref[0, rs, ls]
            l0 = logf.astype(BF16)
            r1 = logf - l0.astype(F32)
            l1 = r1.astype(BF16)
            l2 = (r1 - l1.astype(F32)).astype(BF16)
            b = _dot(tri, l0) + (_dot(tri, l1) + _dot(tri, l2))
            bl = b[chunk - 1:chunk, :]
            q_in = (q * jnp.exp(b)).astype(BF16)
            k_in = (kk * jnp.exp(-b)).astype(BF16)
            k_st = (kk * jnp.exp(bl - b)).astype(BF16)
            vb = v.astype(BF16)
            att = jnp.where(causal, _dot_nt(q_in, k_in), 0.0)
            st = s_scr[hh]
            o = _dot_nt(q_in, st.astype(BF16)) + _dot(att.astype(BF16), vb)
            s_scr[hh] = st * jnp.exp(bl) + _dot_tn(vb, k_st)
            on = o * lax.rsqrt(jnp.mean(o * o, axis=-1, keepdims=True) + EPS) * g_ref[:, ls]
            r_ref[0, rs, ls] = (on * _silu(hg_ref[0, rs, ls])).astype(BF16)

    @pl.when(t == pl.num_programs(1) - 1)
    def _():
        for hh in range(HG_HEADS):
            st_ref[0, hh] = s_scr[hh].T


def _hgrn(hq, hf, hi, hg, lb, out_g, s0):
    b, t, w = hq.shape
    chunk = min(HG_CHUNK, t)
    assert t % chunk == 0
    nchunk = min(4, t // chunk)
    tt = chunk * nchunk
    seq = pl.BlockSpec((1, tt, w), lambda i, j: (i, j, 0))
    vec = pl.BlockSpec((1, w), lambda i, j: (0, 0))
    state = pl.BlockSpec((1, HG_HEADS, HG_DK, HG_DV), lambda i, j: (i, 0, 0, 0))
    return pl.pallas_call(
        functools.partial(_hgrn_kernel, chunk=chunk, nchunk=nchunk),
        out_shape=[jax.ShapeDtypeStruct((b, t, w), BF16), jax.ShapeDtypeStruct(s0.shape, F32)],
        grid=(b, t // tt),
        in_specs=[seq, seq, seq, seq, vec, vec, state],
        out_specs=[seq, state],
        scratch_shapes=[pltpu.VMEM((HG_HEADS, HG_DV, HG_DK), F32)],
        compiler_params=_cparams(("arbitrary", "arbitrary")),
        name="hgrn",
    )(hq, hf, hi, hg, lb.reshape(1, w), out_g.reshape(1, w), s0)


def _order_key(score):
    bits = pltpu.bitcast(score, I32)
    bits = jnp.where(bits == INT_MIN, 0, bits)
    return jnp.where(bits >= 0, bits, bits ^ 0x7FFFFFFF)


def _select_threshold(load_keys, store_keys, nchunk, tq, ck, topk, idx_bits, keys_on_rows=False):
    kax = 0 if keys_on_rows else 1
    kshape = (ck, tq) if keys_on_rows else (tq, ck)
    vshape = (1, tq) if keys_on_rows else (tq, 1)

    def count(pred):
        def body(c, acc):
            m = jnp.where(pred(load_keys(c), c), 1.0, 0.0)
            if keys_on_rows:
                parts = [m[j * 8:(j + 1) * 8] for j in range(ck // 8)]
                while len(parts) > 1:
                    parts = [a + b for a, b in zip(parts[0::2], parts[1::2])]
                part = parts[0]
            else:
                part = m[:, 0:LANES]
                for j in range(1, ck // LANES):
                    part = part + m[:, j * LANES:(j + 1) * LANES]
            return acc + part
        acc = lax.fori_loop(0, nchunk, body, jnp.zeros((8, tq) if keys_on_rows else (tq, LANES), F32))
        return jnp.sum(acc, axis=kax, keepdims=True)

    kf = float(topk)

    def bis_cond(st):
        it, _, done, _ = st
        return jnp.logical_and(it < 32, jnp.min(done) < 0.5)

    def bis_body(st):
        it, tu, done, tsel = st
        for _ in range(BISECT_BITS_PER_TRIP):
            cand_u = tu | lax.shift_left(jnp.int32(1), 31 - it)
            cand_s = cand_u ^ INT_MIN
            cnt = count(lambda key, c: key >= cand_s)
            active = done < 0.5
            tu = jnp.where(jnp.logical_and(active, cnt >= kf), cand_u, tu)
            hit = jnp.logical_and(active, cnt == kf)
            tsel = jnp.where(hit, cand_s - 1, tsel)
            done = jnp.where(hit, 1.0, done)
            it = it + 1
        return it, tu, done, tsel

    zero_i = jnp.zeros(vshape, I32)
    _, tu, done, tsel = lax.while_loop(
        bis_cond, bis_body, (jnp.int32(0), zero_i, jnp.zeros(vshape, F32), zero_i))
    ts = tu ^ INT_MIN
    need_tie = jnp.logical_and(done < 0.5, tu != 0)
    thr = jnp.where(done > 0.5, tsel, ts)

    @pl.when(jnp.max(jnp.where(need_tie, 1.0, 0.0)) > 0.5)
    def _():
        rank = kf - count(lambda key, c: key > ts)
        pos_of = lambda c: c * ck + lax.broadcasted_iota(I32, kshape, kax)

        def jbody(it, ju):
            cand = ju | lax.shift_left(jnp.int32(1), idx_bits - 1 - it)
            pred = lambda key, c: jnp.logical_and(key == ts, pos_of(c) < cand)
            return jnp.where(count(pred) < rank, cand, ju)
        ju = lax.fori_loop(0, idx_bits, jbody, zero_i)

        def fix(c, carry):
            key = load_keys(c)
            bump = jnp.logical_and(jnp.logical_and(key == ts, pos_of(c) <= ju), need_tie)
            store_keys(c, jnp.where(bump, key + 1, key))
            return carry
        lax.fori_loop(0, nchunk, fix, 0)
    return thr


def _prompt_attn_kernel(q_ref, qi_ref, wt_ref, k4_ref, kb_ref, vt_ref, ga_ref, o_ref,
                        key_scr, m_scr, acc_scr, *, topk, idx_bits):
    tq, ck = q_ref.shape[1], ATT_CHUNK
    i = pl.program_id(1)
    nchunk = (i * tq + tq + ck - 1) // ck
    qpos = i * tq + lax.broadcasted_iota(I32, (1, tq), 1)

    qit = qi_ref[0].T
    hit = qit.astype(BF16)
    lot = (qit - hit.astype(F32)).astype(BF16)
    zero = jnp.zeros((IDX_DIM, tq), BF16)

    def idx_rhs(hh):
        hs = slice(hh * IDX_DIM, (hh + 1) * IDX_DIM)
        return jnp.concatenate([hit[hs], hit[hs], lot[hs], zero], axis=0)
    rhs_s = [jnp.concatenate([idx_rhs(2 * p), idx_rhs(2 * p + 1)], axis=1) for p in range(IDX_HEADS // 2)]
    wt = wt_ref[0]

    def score_body(c, carry):
        off = pl.multiple_of(c * ck, ck)
        k4 = k4_ref[0, pl.ds(off, ck), :]
        acc = jnp.zeros((ck, tq), F32)
        for p in range(IDX_HEADS // 2):
            s = _dot(k4, rhs_s[p])
            acc = acc + jnp.maximum(s[:, :tq], 0.0) * wt[2 * p:2 * p + 1]
            acc = acc + jnp.maximum(s[:, tq:], 0.0) * wt[2 * p + 1:2 * p + 2]
        kpos = off + lax.broadcasted_iota(I32, (ck, 1), 0)
        key_scr[pl.ds(off, ck), :] = jnp.where(kpos <= qpos, _order_key(acc), INT_MIN)
        return carry
    lax.fori_loop(0, nchunk, score_body, 0)

    def load_keys(c):
        return key_scr[pl.ds(pl.multiple_of(c * ck, ck), ck), :]

    def store_keys(c, val):
        key_scr[pl.ds(pl.multiple_of(c * ck, ck), ck), :] = val
    thr = _select_threshold(load_keys, store_keys, nchunk, tq, ck, topk, idx_bits, keys_on_rows=True)

    qt = q_ref[0].astype(F32).T.astype(BF16)
    rhs_a = []
    for m in range(KV_HEADS):
        blk = jnp.concatenate([qt[(2 * m) * HEAD_DIM:(2 * m + 1) * HEAD_DIM],
                               qt[(2 * m + 1) * HEAD_DIM:(2 * m + 2) * HEAD_DIM]], axis=1)
        parts = [jnp.zeros((HEAD_DIM, 2 * tq), BF16)] * m + [blk] + [jnp.zeros((HEAD_DIM, 2 * tq), BF16)] * (KV_HEADS - 1 - m)
        rhs_a.append(jnp.concatenate(parts, axis=0))
    m_scr[...] = jnp.full(m_scr.shape, NEG_BIG, F32)
    acc_scr[...] = jnp.zeros(acc_scr.shape, F32)

    def attn_body(c, carry):
        off = pl.multiple_of(c * ck, ck)
        kc = kb_ref[0, pl.ds(off, ck), :]
        sel = load_keys(c) > thr
        lgs = [_dot(kc, rhs_a[m]) for m in range(KV_HEADS)]
        ps, alphas = [], []
        for m in range(KV_HEADS):
            lg = jnp.concatenate([jnp.where(sel, lgs[m][:, :tq], NEG_BIG),
                                  jnp.where(sel, lgs[m][:, tq:], NEG_BIG)], axis=1)
            mx = m_scr[m]
            mn = jnp.maximum(mx, jnp.max(lg, axis=0, keepdims=True))
            ps.append(jnp.exp2(lg - mn).astype(BF16))
            alphas.append(jnp.exp2(mx - mn))
            m_scr[m] = mn
        for m in range(KV_HEADS):
            acc_scr[m] = acc_scr[m] * alphas[m] + _dot(vt_ref[0, m, :, pl.ds(off, ck)], ps[m])
        return carry
    lax.fori_loop(0, nchunk, attn_body, 0)
    pieces = []
    for m in range(KV_HEADS):
        acc = acc_scr[m]
        o = acc[0:HEAD_DIM] / acc[HEAD_DIM:HEAD_DIM + 1]
        pieces += [o[:, :tq], o[:, tq:]]
    at = jnp.concatenate(pieces, axis=0)
    at = at * lax.rsqrt(jnp.mean(at * at, axis=0, keepdims=True) + EPS)
    o_ref[0] = (at.T * ga_ref[...]).astype(BF16)


def _prompt_attention(q, qi, wi, ki, kb, vb, ga):
    b, t, _ = q.shape
    tq = LANES
    assert t % ATT_CHUNK == 0
    topk = min(TOPK_MAX, t // 4)
    ki_hi = ki.astype(BF16)
    ki_lo = (ki - ki_hi.astype(F32)).astype(BF16)
    k4 = jnp.concatenate([ki_hi, ki_lo, ki_hi, ki_hi], axis=-1)
    vt = vb.reshape(b, t, KV_HEADS, HEAD_DIM).transpose(0, 2, 3, 1)
    vt = jnp.concatenate([vt, jnp.ones((b, KV_HEADS, VT_ROWS - HEAD_DIM, t), BF16)], axis=2)
    blk = lambda w: pl.BlockSpec((1, tq, w), lambda bi, i: (bi, i, 0))
    full = lambda r, c: pl.BlockSpec((1, r, c), lambda bi, i: (bi, 0, 0))
    return pl.pallas_call(
        functools.partial(_prompt_attn_kernel, topk=topk, idx_bits=max(1, (t - 1).bit_length())),
        out_shape=jax.ShapeDtypeStruct((b, t, ATT_WIDTH), BF16),
        grid=(b, t // tq),
        in_specs=[blk(ATT_WIDTH), blk(ATT_WIDTH),
                  pl.BlockSpec((1, IDX_HEADS, tq), lambda bi, i: (bi, 0, i)),
                  full(t, 4 * IDX_DIM), full(t, KV_WIDTH),
                  pl.BlockSpec((1, KV_HEADS, VT_ROWS, t), lambda bi, i: (bi, 0, 0, 0)),
                  pl.BlockSpec((1, ATT_WIDTH), lambda bi, i: (0, 0))],
        out_specs=blk(ATT_WIDTH),
        scratch_shapes=[pltpu.VMEM((t, tq), I32), pltpu.VMEM((KV_HEADS, 1, 2 * tq), F32),
                        pltpu.VMEM((KV_HEADS, VT_ROWS, 2 * tq), F32)],
        compiler_params=_cparams(("arbitrary", "arbitrary")),
        name="prompt_attn",
    )(q, qi, wi.transpose(0, 2, 1), k4, kb, vt, ga.reshape(1, ATT_WIDTH))


def _sample_score_kernel(pt_ref, qi_ref, w_ref, kin_ref, *rest, npg, nsteps, tnew, topk, idx_bits):
    page_refs = rest[:npg]
    key_ref, thr_ref = rest[npg:npg + 2]
    s = pl.program_id(1)
    last = s == nsteps
    tq = tnew
    page = page_refs[0].shape[1]
    ck = npg * page
    qi_hi, qi_lo = _split_bf16(qi_ref[0])
    pages = [page_refs[p][...] for p in range(npg)]
    pages[0] = jnp.where(last, kin_ref[0], pages[0])
    k_hi, k_lo = _split_bf16(jnp.concatenate(pages, axis=1))
    sc = _dot(qi_hi, k_hi) + (_dot(qi_hi, k_lo) + _dot(qi_lo, k_hi))
    sc = jnp.maximum(sc, 0.0) * w_ref[0]
    acc = sc[0:tq]
    for hh in range(1, IDX_HEADS):
        acc = acc + sc[hh * tq:(hh + 1) * tq]
    qrow = lax.broadcasted_iota(I32, (tq, ck), 0)
    col = lax.broadcasted_iota(I32, (tq, ck), 1)
    ok = jnp.logical_or(jnp.logical_not(last), jnp.logical_and(col <= qrow, col < tnew))
    key_ref[0, :, pl.ds(pl.multiple_of(s * ck, ck), ck)] = jnp.where(ok, _order_key(acc), INT_MIN)

    @pl.when(last)
    def _():
        def load_keys(c):
            return key_ref[0, :, pl.ds(pl.multiple_of(c * ck, ck), ck)]

        def store_keys(c, val):
            key_ref[0, :, pl.ds(pl.multiple_of(c * ck, ck), ck)] = val
        thr = _select_threshold(load_keys, store_keys, nsteps + 1, tq, ck, topk, idx_bits)
        thr_ref[0] = jnp.broadcast_to(thr, (tq, LANES))


def _sample_attn_kernel(pt_ref, q_ref, key_ref, thr_ref, kn_ref, vn_ref, ga_ref, *rest, npg, nsteps, tnew):
    k_refs = rest[:npg]
    v_refs = rest[npg:2 * npg]
    o_ref = rest[2 * npg]
    m_scr, l_scr, acc_scr = rest[2 * npg + 1:]
    s = pl.program_id(1)
    last = s == nsteps
    tq = tnew

    @pl.when(s == 0)
    def _():
        m_scr[...] = jnp.full(m_scr.shape, NEG_BIG, F32)
        l_scr[...] = jnp.zeros(l_scr.shape, F32)
        acc_scr[...] = jnp.zeros(acc_scr.shape, F32)

    sel = key_ref[0] > thr_ref[0][:, 0:1]
    sel2 = jnp.concatenate([sel, sel], axis=0)
    for m in range(KV_HEADS):
        kps = [k_refs[p][m] for p in range(npg)]
        vps = [v_refs[p][m] for p in range(npg)]
        kps[0] = jnp.where(last, kn_ref[0, m], kps[0])
        vps[0] = jnp.where(last, vn_ref[0, m], vps[0])
        kmt = jnp.concatenate(kps, axis=1).astype(BF16)
        vmt = jnp.concatenate(vps, axis=1).astype(BF16)
        lg = jnp.where(sel2, _dot(q_ref[0, m], kmt), NEG_BIG)
        mx = m_scr[m]
        mn = jnp.maximum(mx, jnp.max(lg, axis=-1, keepdims=True))
        pr = jnp.exp2(lg - mn)
        alpha = jnp.exp2(mx - mn)
        l_scr[m] = l_scr[m] * alpha + jnp.sum(pr, axis=-1, keepdims=True)
        acc_scr[m] = acc_scr[m] * alpha + _dot_nt(pr.astype(BF16), vmt)
        m_scr[m] = mn

    @pl.when(last)
    def _():
        pieces = []
        for m in range(KV_HEADS):
            o = acc_scr[m] / l_scr[m]
            pieces += [o[:tq], o[tq:]]
        a = jnp.concatenate(pieces, axis=1)
        a = a * lax.rsqrt(jnp.mean(a * a, axis=-1, keepdims=True) + EPS) * ga_ref[...]
        o_ref[0] = a.astype(BF16)


def _sample_attention(q, qi, wi, ki_new, k_new, v_new, cache_k, cache_v, cache_ki, layer, page_table, ga):
    db, tn, _ = q.shape
    page = cache_ki.shape[2]
    cache_k = cache_k.transpose(0, 1, 3, 4, 2)
    cache_v = cache_v.transpose(0, 1, 3, 4, 2)
    cache_ki = cache_ki.transpose(0, 1, 3, 2)
    n_pages = page_table.shape[1]
    npg_s = math.gcd(SAMPLE_SCORE_PAGES, n_pages)
    npg_a = math.gcd(SAMPLE_ATTN_PAGES, npg_s)
    nsteps_s, nsteps_a = n_pages // npg_s, n_pages // npg_a
    lpad = (nsteps_s + 1) * npg_s * page
    topk = min(TOPK_MAX, (n_pages * page + tn) // 4)
    assert tn <= page
    pad_keys = lambda a: jnp.pad(a, [(0, 0)] * (a.ndim - 1) + [(0, page - tn)])
    kin = pad_keys(ki_new.transpose(0, 2, 1))
    heads = lambda a: pad_keys(a.reshape(db, tn, KV_HEADS, HEAD_DIM).transpose(0, 2, 3, 1))
    qi_hq = qi.reshape(db, tn, IDX_HEADS, IDX_DIM).transpose(0, 2, 1, 3).reshape(db, IDX_HEADS * tn, IDX_DIM)
    w_hq = wi.transpose(0, 2, 1).reshape(db, IDX_HEADS * tn, 1)
    q2 = (q.reshape(db, tn, KV_HEADS, ATT_HEADS // KV_HEADS, HEAD_DIM).transpose(0, 2, 3, 1, 4)
          .reshape(db, KV_HEADS, (ATT_HEADS // KV_HEADS) * tn, HEAD_DIM))
    pt_flat = page_table.reshape(-1).astype(I32)

    def page_spec(tail, p, npg):
        def imap(b, s, pt):
            return (layer, pt[b * n_pages + jnp.minimum(s * npg + p, n_pages - 1)]) + (0,) * len(tail)
        return pl.BlockSpec((None, None) + tail, imap)

    per_b = lambda *tail: pl.BlockSpec((1,) + tail, lambda b, s, pt: (b,) + (0,) * len(tail))
    keys, thr = pl.pallas_call(
        functools.partial(_sample_score_kernel, npg=npg_s, nsteps=nsteps_s, tnew=tn, topk=topk,
                          idx_bits=max(1, (lpad - 1).bit_length())),
        out_shape=[jax.ShapeDtypeStruct((db, tn, lpad), I32),
                   jax.ShapeDtypeStruct((db, tn, LANES), I32)],
        grid_spec=pltpu.PrefetchScalarGridSpec(
            num_scalar_prefetch=1,
            grid=(db, nsteps_s + 1),
            in_specs=[per_b(IDX_HEADS * tn, IDX_DIM), per_b(IDX_HEADS * tn, 1), per_b(IDX_DIM, page)]
                     + [page_spec((IDX_DIM, page), p, npg_s) for p in range(npg_s)],
            out_specs=[per_b(tn, lpad), per_b(tn, LANES)]),
        compiler_params=_cparams(("arbitrary", "arbitrary")),
        name="sample_score",
    )(pt_flat, qi_hq, w_hq, kin, *([cache_ki] * npg_s))

    kv_pages = [page_spec((KV_HEADS, HEAD_DIM, page), p, npg_a) for p in range(npg_a)]
    a = pl.pallas_call(
        functools.partial(_sample_attn_kernel, npg=npg_a, nsteps=nsteps_a, tnew=tn),
        out_shape=jax.ShapeDtypeStruct((db, tn, ATT_WIDTH), BF16),
        grid_spec=pltpu.PrefetchScalarGridSpec(
            num_scalar_prefetch=1,
            grid=(db, nsteps_a + 1),
            in_specs=[per_b(KV_HEADS, (ATT_HEADS // KV_HEADS) * tn, HEAD_DIM),
                      pl.BlockSpec((1, tn, npg_a * page), lambda b, s, pt: (b, 0, s)),
                      per_b(tn, LANES),
                      per_b(KV_HEADS, HEAD_DIM, page), per_b(KV_HEADS, HEAD_DIM, page),
                      pl.BlockSpec((1, ATT_WIDTH), lambda b, s, pt: (0, 0))]
                     + kv_pages + kv_pages,
            out_specs=per_b(tn, ATT_WIDTH),
            scratch_shapes=[pltpu.VMEM((KV_HEADS, 2 * tn, 1), F32), pltpu.VMEM((KV_HEADS, 2 * tn, 1), F32),
                            pltpu.VMEM((KV_HEADS, 2 * tn, HEAD_DIM), F32)]),
        compiler_params=_cparams(("arbitrary", "arbitrary")),
        name="sample_attn",
    )(pt_flat, q2, keys, thr, heads(k_new), heads(v_new), ga.reshape(1, ATT_WIDTH),
      *([cache_k] * npg_a), *([cache_v] * npg_a))
    return a


def _outproj_kernel(a_ref, r_ref, x_ref, gt_ref, sc_ref, sh_ref, g_ref, wa_ref, wr_ref, wrt_ref, brt_ref,
                    x2_ref, h2_ref, route_ref):
    mix = _dot(a_ref[...], wa_ref[...]) + _dot(r_ref[...], wr_ref[...])
    x2 = x_ref[...] + gt_ref[0] * mix
    x2_ref[...] = x2
    h = x2 * lax.rsqrt(jnp.mean(x2 * x2, axis=-1, keepdims=True) + EPS) * g_ref[...]
    h = h * (1.0 + sc_ref[0]) + sh_ref[0]
    h2_ref[...] = h
    lg = (_dot3(h, wrt_ref[...]) + brt_ref[...]).T
    grp = [lg[g:g + 1] for g in range(N_GROUPS)]
    gmax = functools.reduce(jnp.maximum, grp)
    gden = functools.reduce(lambda u, v: u + v, [jnp.exp(g - gmax) for g in grp])
    gsel = jnp.full(gmax.shape, N_GROUPS - 1, I32)
    for g in range(N_GROUPS - 2, -1, -1):
        gsel = jnp.where(grp[g] == gmax, g, gsel)
    gw = 1.0 / gden
    el = []
    for e in range(EXPERTS_PER_GROUP):
        v = lg[N_GROUPS + e:N_GROUPS + e + 1]
        for g in range(1, N_GROUPS):
            row = N_GROUPS + g * EXPERTS_PER_GROUP + e
            v = jnp.where(gsel == g, lg[row:row + 1], v)
        el.append(v)
    emax = functools.reduce(jnp.maximum, el)
    e0 = jnp.full(emax.shape, EXPERTS_PER_GROUP - 1, I32)
    for e in range(EXPERTS_PER_GROUP - 2, -1, -1):
        e0 = jnp.where(el[e] == emax, e, e0)
    rest = [jnp.where(e0 == e, -jnp.inf, el[e]) for e in range(EXPERTS_PER_GROUP)]
    rmax = functools.reduce(jnp.maximum, rest)
    e1 = jnp.full(emax.shape, EXPERTS_PER_GROUP - 1, I32)
    for e in range(EXPERTS_PER_GROUP - 2, -1, -1):
        e1 = jnp.where(jnp.logical_and(rest[e] == rmax, e0 != e), e, e1)
    p1 = jnp.exp(rmax - emax)
    w0 = gw * (1.0 / (1.0 + p1))
    w1 = gw * (p1 / (1.0 + p1))
    base = gsel * EXPERTS_PER_GROUP
    rid = lax.broadcasted_iota(I32, lg.shape, 0)
    rt = jnp.where(rid == 0, (base + e0).astype(F32),
                   jnp.where(rid == 1, (base + e1).astype(F32),
                             jnp.where(rid == 2, w0, jnp.where(rid == 3, w1, 0.0))))
    route_ref[...] = rt.T


def _outproj(a, r, x2d, gt, sc, sh, g2, wa, wr, wrt, brt, tiles_per_group):
    n, d = x2d.shape
    tm = min(PROJ_TILE, n)
    rr = gt.shape[1]
    row = lambda w: pl.BlockSpec((tm, w), lambda i: (i, 0))
    mod = pl.BlockSpec((1, rr, d), lambda i: (i // tiles_per_group, 0, 0))
    const = lambda s: pl.BlockSpec(s, lambda i: (0, 0))
    return pl.pallas_call(
        _outproj_kernel,
        out_shape=[jax.ShapeDtypeStruct((n, d), F32), jax.ShapeDtypeStruct((n, d), F32),
                   jax.ShapeDtypeStruct((n, LANES), F32)],
        grid=(n // tm,),
        in_specs=[row(ATT_WIDTH), row(HG_WIDTH), row(d), mod, mod, mod, const((1, d)),
                  const(wa.shape), const(wr.shape), const(wrt.shape), const(brt.shape)],
        out_specs=[row(d), row(d), row(LANES)],
        compiler_params=_cparams(("arbitrary",)),
        name="outproj",
    )(a, r, x2d, gt, sc, sh, g2.reshape(1, d), wa, wr, wrt, brt)


def _gather_pipeline(step, nsteps, idx_hbm, src_hbm, buf, idx_smem, isem, rsem):
    nrows = buf.shape[1]
    slot = step % 2

    def idx_copy(b, sl):
        return pltpu.make_async_copy(idx_hbm.at[b], idx_smem.at[sl], isem.at[sl])

    def start_rows(sl):
        for r in range(nrows):
            pltpu.make_async_copy(src_hbm.at[pl.ds(idx_smem[sl, 0, r], 1), :],
                                  buf.at[sl, pl.ds(r, 1), :], rsem.at[sl]).start()

    @pl.when(step == 0)
    def _():
        idx_copy(0, 0).start()
        idx_copy(0, 0).wait()
        start_rows(0)

        @pl.when(nsteps > 1)
        def _():
            idx_copy(1, 1).start()

    @pl.when(step + 1 < nsteps)
    def _():
        idx_copy(step + 1, 1 - slot).wait()
        start_rows(1 - slot)

    @pl.when(step + 2 < nsteps)
    def _():
        idx_copy(step + 2, slot).start()

    pltpu.make_async_copy(src_hbm.at[pl.ds(0, nrows), :], buf.at[slot], rsem.at[slot]).wait()
    return slot


def _ffn_kernel(be_ref, nb_ref, tok_ref, h_ref, w1_ref, w3_ref, w2_ref, y_ref, xbuf, idx_smem, isem, rsem):
    j = pl.program_id(0)

    @pl.when(j < nb_ref[0])
    def _():
        slot = _gather_pipeline(j, nb_ref[0], tok_ref, h_ref, xbuf, idx_smem, isem, rsem)
        xb = xbuf[slot].astype(BF16)
        u = _dot(xb, w1_ref[0])
        g = _dot(xb, w3_ref[0])
        y_ref[...] = _dot((_silu(u) * g).astype(BF16), w2_ref[0])

    @pl.when(j >= nb_ref[0])
    def _():
        y_ref[...] = jnp.zeros(y_ref.shape, F32)


def _ffn(blk_e, n_used, tok, h2, w1, w3, w2, blk):
    nb = tok.shape[0]
    d = h2.shape[1]
    wspec = lambda s: pl.BlockSpec((1,) + s, lambda j, be, nu: (be[j], 0, 0))
    return pl.pallas_call(
        _ffn_kernel,
        out_shape=jax.ShapeDtypeStruct((nb * blk, d), F32),
        grid_spec=pltpu.PrefetchScalarGridSpec(
            num_scalar_prefetch=2,
            grid=(nb,),
            in_specs=[pl.BlockSpec(memory_space=pl.ANY), pl.BlockSpec(memory_space=pl.ANY),
                      wspec(w1.shape[1:]), wspec(w3.shape[1:]), wspec(w2.shape[1:])],
            out_specs=pl.BlockSpec((blk, d), lambda j, be, nu: (j, 0)),
            scratch_shapes=[pltpu.VMEM((2, blk, d), F32), pltpu.SMEM((2, 1, blk), I32),
                            pltpu.SemaphoreType.DMA((2,)), pltpu.SemaphoreType.DMA((2,))]),
        compiler_params=_cparams(("arbitrary",)),
        name="moe_ffn",
    )(blk_e, n_used, tok, h2, w1, w3, w2)


def _combine_kernel(dd_ref, y_ref, x2_ref, route_ref, gt_ref, sc_ref, sh_ref, g_ref, o_ref,
                    ybuf, idx_smem, isem, rsem):
    tm = x2_ref.shape[0]
    slot = _gather_pipeline(pl.program_id(0), pl.num_programs(0), dd_ref, y_ref, ybuf, idx_smem, isem, rsem)
    rt = route_ref[...]
    moe = ybuf[slot, 0:tm] * rt[:, 2:3] + ybuf[slot, tm:2 * tm] * rt[:, 3:4]
    x = x2_ref[...] + gt_ref[0] * moe
    y = x * lax.rsqrt(jnp.mean(x * x, axis=-1, keepdims=True) + EPS) * g_ref[...]
    o_ref[...] = y * (1.0 + sc_ref[0]) + sh_ref[0]


def _combine(dd, yb, x2, route, gt, sc, sh, gf, tiles_per_group):
    n, d = x2.shape
    tm = dd.shape[2] // 2
    rr = gt.shape[1]
    row = pl.BlockSpec((tm, d), lambda i: (i, 0))
    mod = pl.BlockSpec((1, rr, d), lambda i: (i // tiles_per_group, 0, 0))
    hbm = pl.BlockSpec(memory_space=pl.ANY)
    return pl.pallas_call(
        _combine_kernel,
        out_shape=jax.ShapeDtypeStruct((n, d), F32),
        grid=(n // tm,),
        in_specs=[hbm, hbm, row, pl.BlockSpec((tm, LANES), lambda i: (i, 0)), mod, mod, mod,
                  pl.BlockSpec((1, d), lambda i: (0, 0))],
        out_specs=row,
        scratch_shapes=[pltpu.VMEM((2, 2 * tm, d), F32), pltpu.SMEM((2, 1, 2 * tm), I32),
                        pltpu.SemaphoreType.DMA((2,)), pltpu.SemaphoreType.DMA((2,))],
        compiler_params=_cparams(("arbitrary",)),
        name="moe_combine",
    )(dd, yb, x2, route, gt, sc, sh, gf.reshape(1, d))


def _dispatch(route, blk):
    n = route.shape[0]
    flat_e = route[:, 0:2].astype(I32).reshape(-1)
    a = flat_e.shape[0]
    onehot = (flat_e[:, None] == jnp.arange(N_EXPERTS, dtype=I32)[None, :]).astype(I32)
    csum = jnp.cumsum(onehot, axis=0)
    rank = jnp.sum((csum - onehot) * onehot, axis=1)
    counts = csum[-1]
    padded = (counts + blk - 1) // blk * blk
    pad_end = jnp.cumsum(padded)
    pad_start = pad_end - padded
    dest = pad_start[flat_e] + rank
    nb = -(-a // blk) + N_EXPERTS
    tok = jnp.zeros((nb * blk,), I32).at[dest].set(jnp.arange(a, dtype=I32) // 2)
    blk_e = jnp.minimum(jnp.searchsorted(pad_end, jnp.arange(nb, dtype=I32) * blk, side='right'),
                        N_EXPERTS - 1).astype(I32)
    n_used = (pad_end[-1] // blk).astype(I32).reshape(1)
    dest2 = dest.reshape(n, 2)
    return tok.reshape(nb, 1, blk), blk_e, n_used, dest2[:, 0], dest2[:, 1]


def _mods(m, n_chunks, per_token_rows):
    parts = jnp.split(m, n_chunks, axis=-1)
    if per_token_rows is None:
        return [p[:, None, :] for p in parts]
    g, d = parts[0].shape
    tm = min(PROJ_TILE, g * per_token_rows)
    return [jnp.repeat(p, per_token_rows, axis=0).reshape(-1, tm, d) for p in parts]


def _layer(x, mod6, modf, pos, s0, attend, lb, wts, final_g, per_token):
    b, t, d = x.shape
    n = b * t
    (norm1_g, norm2_g, w_r, gk, ga, hg_out_g, wa, wr, wrt, brt, w1, w3, w2) = wts
    tm = min(PROJ_TILE, n)
    tiles_per_group = 1 if per_token else t // tm
    sh1, sc1, gt1, sh2, sc2, gt2 = _mods(mod6, 6, t if per_token else None)
    shf, scf = _mods(modf, 2, t if per_token else None)
    tabs = _rope_tables(jnp.tile(pos, tm // t) if per_token else pos)
    x2d = x.reshape(n, d)
    (q, k, v, kb, vd, qi, ki, wi, hq, hf, hi, hg) = _proj(x2d, sc1, sh1, norm1_g, w_r, gk, tabs, tiles_per_group)
    sq = lambda arr: arr.reshape(b, t, arr.shape[-1])
    r, s_t = _hgrn(sq(hq), sq(hf), sq(hi), sq(hg), lb, hg_out_g, s0)
    a = attend(sq(q), sq(qi), sq(wi), sq(ki), sq(k), sq(v), sq(kb), sq(vd), ga)
    x2, h2, route = _outproj(a.reshape(n, -1), r.reshape(n, -1), x2d, gt1, sc2, sh2, norm2_g,
                             wa, wr, wrt, brt, tiles_per_group)
    blk = MOE_BLOCK if n >= 8 * MOE_BLOCK else 64
    tok, blk_e, n_used, d0, d1 = _dispatch(route, blk)
    yb = _ffn(blk_e, n_used, tok, h2, w1, w3, w2, blk)
    dd = jnp.concatenate([d0.reshape(n // tm, 1, tm), d1.reshape(n // tm, 1, tm)], axis=2)
    y = _combine(dd, yb, x2, route, gt2, scf, shf, final_g, tiles_per_group)
    return (y.reshape(b, t, d), k.reshape(b, t, KV_HEADS, HEAD_DIM), v.reshape(b, t, KV_HEADS, HEAD_DIM),
            sq(ki), s_t)


def kernel(x_prompt, x_sample, cache_k, cache_v, cache_kidx, state_hgrn, page_table, c_prompt, c_sample,
           ada_w, ada_b, norm1_g, norm2_g, w_in, idx_k_g, hg_lb_logits, attn_out_g, hg_out_g, w_out,
           w_group, b_group, w_expert_router, b_expert_router, w1, w3, w2, final_g, ada_final_w, ada_final_b):
    depth = ada_w.shape[0]
    assert depth == 1, "the final adaLN norm is fused into the (single) layer's combine kernel"
    bp, tp, d = x_prompt.shape
    bs, ts, _ = x_sample.shape
    past = page_table.shape[1] * cache_kidx.shape[2]
    lb_all = jnp.cumsum(jax.nn.softmax(hg_lb_logits.astype(F32), axis=0), axis=0)
    n_c = bp + bs
    c_all = jnp.concatenate([c_prompt, c_sample, jnp.zeros((-n_c % 16, d), F32)], axis=0)
    modf = _ada(c_all, ada_final_w, ada_final_b)
    l = 0
    mod6 = _ada(c_all, ada_w[l], ada_b[l])

    seg = [0]
    for s in (ATT_WIDTH, KV_WIDTH, KV_WIDTH, IDX_HEADS * IDX_DIM, IDX_DIM, IDX_HEADS,
              HG_WIDTH, HG_WIDTH, HG_WIDTH, HG_WIDTH):
        seg.append(seg[-1] + s)
    wl = w_in[l]
    col = lambda i: wl[:, seg[i]:seg[i + 1]]
    zpad = lambda w: jnp.zeros((d, w), wl.dtype)
    w_r = jnp.concatenate([col(0), col(1), col(2), col(3), col(6), col(7), col(8), col(9),
                           col(4), zpad(LANES - IDX_DIM), col(5), zpad(LANES - IDX_HEADS)], axis=1).astype(BF16)
    gk = jnp.concatenate([idx_k_g[l], jnp.zeros((LANES - IDX_DIM,), F32)]).reshape(1, LANES)
    wa = w_out[l, :ATT_WIDTH].astype(BF16)
    wr = w_out[l, ATT_WIDTH:].astype(BF16)
    n_rt = N_GROUPS + N_EXPERTS
    wrt = jnp.concatenate([w_group[l], w_expert_router[l], jnp.zeros((d, LANES - n_rt), F32)], axis=1)
    brt = jnp.concatenate([b_group[l], b_expert_router[l], jnp.zeros((LANES - n_rt,), F32)]).reshape(1, LANES)
    wts = (norm1_g[l], norm2_g[l], w_r, gk, attn_out_g[l], hg_out_g[l], wa, wr, wrt, brt,
           w1[l].astype(BF16), w3[l].astype(BF16), w2[l].astype(BF16))

    def attend_p(q, qi, wi, ki, k, v, kb, vb, ga):
        return _prompt_attention(q, qi, wi, ki, kb, vb, ga)

    def attend_s(q, qi, wi, ki, k, v, kb, vb, ga):
        return _sample_attention(q, qi, wi, ki, k, v, cache_k, cache_v, cache_kidx, l, page_table, ga)

    s0_p = jnp.zeros((bp, HG_HEADS, HG_DK, HG_DV), F32)
    yp, kp, vp, kip, sp = _layer(x_prompt, mod6[:bp], modf[:bp], jnp.arange(tp), s0_p, attend_p,
                                 lb_all[l], wts, final_g, per_token=False)
    ys, ks, vs, kis, ss = _layer(x_sample, mod6[bp:n_c], modf[bp:n_c], past + jnp.arange(ts), state_hgrn[l],
                                 attend_s, lb_all[l], wts, final_g, per_token=True)
    return (yp, ys, kp[None], vp[None], kip[None], sp[None], ks[None], vs[None], kis[None], ss[None])
```

```python
import functools
import math

import jax
import jax.numpy as jnp
from jax import lax
from jax.experimental import pallas as pl
from jax.experimental.pallas import tpu as pltpu

F32 = jnp.float32
BF16 = jnp.bfloat16
I32 = jnp.int32

ATT_HEADS = 8
KV_HEADS = 4
HEAD_DIM = 64
ATT_WIDTH = ATT_HEADS * HEAD_DIM
KV_WIDTH = KV_HEADS * HEAD_DIM
ROT_HALF = HEAD_DIM // 8
ROPE_THETA = 500000.0
IDX_HEADS = 8
IDX_DIM = 64
TOPK_MAX = 256
HG_HEADS = 4
HG_DK = 128
HG_DV = 128
HG_WIDTH = HG_HEADS * HG_DV
HG_CHUNK = 64
N_GROUPS = 4
EXPERTS_PER_GROUP = 4
N_EXPERTS = N_GROUPS * EXPERTS_PER_GROUP
D_EXPERT = 512
MOE_BLOCK = 256
EPS = 1e-6

LANES = 128
INT_MIN = -2 ** 31
NEG_BIG = -1e30
VMEM_LIMIT = 56 * 1024 * 1024
PROJ_TILE = 256
ATT_CHUNK = 1024
VT_ROWS = HEAD_DIM + 16
LOG2E = 1.4426950408889634
BISECT_BITS_PER_TRIP = 4
SAMPLE_SCORE_PAGES = 16
SAMPLE_ATTN_PAGES = 16

_Z_Q, _Z_K, _Z_V, _Z_QI, _Z_HQ, _Z_HF, _Z_HI, _Z_HG, _Z_KI, _Z_WI, _Z_END = (
    0, 512, 768, 1024, 1536, 2048, 2560, 3072, 3584, 3712, 3840)


def _cparams(sem):
    return pltpu.CompilerParams(dimension_semantics=sem, vmem_limit_bytes=VMEM_LIMIT)


def _split_bf16(x):
    hi = x.astype(BF16)
    lo = (x - hi.astype(F32)).astype(BF16)
    return hi, lo


def _dot(a, b):
    return jnp.dot(a, b, preferred_element_type=F32)


def _dot_nt(a, b):
    return lax.dot_general(a, b, (((1,), (1,)), ((), ())), preferred_element_type=F32)


def _dot_tn(a, b):
    return lax.dot_general(a, b, (((0,), (0,)), ((), ())), preferred_element_type=F32)


def _dot3(a, b):
    ah, al = _split_bf16(a)
    bh, bl = _split_bf16(b)
    return _dot(ah, bh) + (_dot(ah, bl) + _dot(al, bh))


def _dot3_nt(a, b):
    ah, al = _split_bf16(a)
    bh, bl = _split_bf16(b)
    return _dot_nt(ah, bh) + (_dot_nt(ah, bl) + _dot_nt(al, bh))


def _silu(x):
    return x * (1.0 / (1.0 + jnp.exp(-x)))


def _sigmoid(x):
    return 1.0 / (1.0 + jnp.exp(-x))


def _ada_kernel(c_ref, w_ref, b_ref, o_ref):
    o_ref[...] = _dot3(_silu(c_ref[...]), w_ref[...]) + b_ref[...]


def _ada(c, w, b):
    r, d = c.shape
    e = w.shape[1]
    te = 1024
    return pl.pallas_call(
        _ada_kernel,
        out_shape=jax.ShapeDtypeStruct((r, e), F32),
        grid=(e // te,),
        in_specs=[pl.BlockSpec((r, d), lambda j: (0, 0)),
                  pl.BlockSpec((d, te), lambda j: (0, j)),
                  pl.BlockSpec((1, te), lambda j: (0, j))],
        out_specs=pl.BlockSpec((r, te), lambda j: (0, j)),
        compiler_params=_cparams(("arbitrary",)),
        name="ada",
    )(c, w, b.reshape(1, e))


def _rope_tables(pos):
    p = pos.shape[0]
    inv = ROPE_THETA ** (-jnp.arange(ROT_HALF, dtype=F32) * (2.0 / (2 * ROT_HALF)))
    ang = pos.astype(F32)[:, None] * inv[None, :]
    c, s = jnp.cos(ang), jnp.sin(ang)
    rest = HEAD_DIM - 2 * ROT_HALF
    one, zero, z8 = jnp.ones((p, rest), F32), jnp.zeros((p, rest), F32), jnp.zeros((p, ROT_HALF), F32)
    cos64 = jnp.concatenate([c, c, one], axis=1)
    sa64 = jnp.concatenate([-s, z8, zero], axis=1)
    sb64 = jnp.concatenate([z8, s, zero], axis=1)
    dup = lambda t: jnp.concatenate([t, t], axis=1)
    return dup(cos64), dup(sa64), dup(sb64)


def _proj_kernel(x_ref, sc_ref, sh_ref, g_ref, w_ref, gk_ref, cos_ref, sa_ref, sb_ref,
                 q_ref, k_ref, v_ref, kb_ref, vb_ref, qi_ref, ki_ref, wi_ref,
                 hq_ref, hf_ref, hi_ref, hg_ref):
    x = x_ref[...]
    ms = jnp.mean(x * x, axis=-1, keepdims=True)
    h = x * lax.rsqrt(ms + EPS) * g_ref[...]
    h = h * (1.0 + sc_ref[0]) + sh_ref[0]
    z = _dot(h.astype(BF16), w_ref[...])
    cos, sa, sb = cos_ref[...], sa_ref[...], sb_ref[...]

    def rope(zs):
        n = zs.shape[1] // LANES
        rep = (lambda t: jnp.concatenate([t] * n, axis=1)) if n > 1 else (lambda t: t)
        w = zs.shape[1]
        return (zs * rep(cos) + pltpu.roll(zs, w - ROT_HALF, 1) * rep(sa)
                + pltpu.roll(zs, ROT_HALF, 1) * rep(sb))

    q_ref[...] = (rope(z[:, _Z_Q:_Z_K]) * (HEAD_DIM ** -0.5 * LOG2E)).astype(BF16)
    k = rope(z[:, _Z_K:_Z_V])
    k_ref[...] = k
    kb_ref[...] = k.astype(BF16)
    v = z[:, _Z_V:_Z_QI]
    v_ref[...] = v
    vb_ref[...] = v.astype(BF16)
    qi_ref[...] = rope(z[:, _Z_QI:_Z_HQ]) * (IDX_DIM ** -0.5)
    hq_ref[...] = z[:, _Z_HQ:_Z_HF]
    hf_ref[...] = z[:, _Z_HF:_Z_HI]
    hi_ref[...] = z[:, _Z_HI:_Z_HG]
    hg_ref[...] = z[:, _Z_HG:_Z_KI]
    ks = z[:, _Z_KI:_Z_WI]
    kms = jnp.sum(ks * ks, axis=-1, keepdims=True) * (1.0 / IDX_DIM)
    kn = ks * lax.rsqrt(kms + EPS) * gk_ref[...]
    ki_ref[...] = rope(kn)[:, :IDX_DIM]
    wi_ref[...] = z[:, _Z_WI:_Z_WI + IDX_HEADS] * (IDX_HEADS ** -0.5)


def _proj(x2d, sc, sh, g1, w_r, gk, tabs, tiles_per_group):
    n, d = x2d.shape
    tm = min(PROJ_TILE, n)
    nt = n // tm
    r = sc.shape[1]
    pt = tabs[0].shape[0] // tm
    row = lambda w: pl.BlockSpec((tm, w), lambda i: (i, 0))
    mod = pl.BlockSpec((1, r, d), lambda i: (i // tiles_per_group, 0, 0))
    tab = pl.BlockSpec((tm, LANES), lambda i: (i % pt, 0))
    outs = [(ATT_WIDTH, BF16), (KV_WIDTH, F32), (KV_WIDTH, F32), (KV_WIDTH, BF16), (KV_WIDTH, BF16),
            (ATT_WIDTH, F32), (IDX_DIM, F32), (IDX_HEADS, F32),
            (HG_WIDTH, F32), (HG_WIDTH, F32), (HG_WIDTH, F32), (HG_WIDTH, F32)]
    return pl.pallas_call(
        _proj_kernel,
        out_shape=[jax.ShapeDtypeStruct((n, w), t) for w, t in outs],
        grid=(nt,),
        in_specs=[row(d), mod, mod,
                  pl.BlockSpec((1, d), lambda i: (0, 0)),
                  pl.BlockSpec(w_r.shape, lambda i: (0, 0)),
                  pl.BlockSpec((1, LANES), lambda i: (0, 0)),
                  tab, tab, tab],
        out_specs=[row(w) for w, _ in outs],
        compiler_params=_cparams(("arbitrary",)),
        name="proj",
    )(x2d, sc, sh, g1.reshape(1, d), w_r, gk, *tabs)


def _hgrn_kernel(hq_ref, hf_ref, hi_ref, hg_ref, lb_ref, g_ref, s0_ref, r_ref, st_ref, s_scr, *, chunk, nchunk):
    t = pl.program_id(1)

    @pl.when(t == 0)
    def _():
        for hh in range(HG_HEADS):
            s_scr[hh] = s0_ref[0, hh].T

    row = lax.broadcasted_iota(I32, (chunk, chunk), 0)
    col = lax.broadcasted_iota(I32, (chunk, chunk), 1)
    causal = col <= row
    tri = causal.astype(BF16)
    pairs = [(c, hh) for c in range(nchunk) for hh in range(HG_HEADS)]
    rows = lambda c: slice(c * chunk, (c + 1) * chunk)
    lanes = lambda hh: slice(hh * HG_DK, (hh + 1) * HG_DK)
    kks, bs = {}, {}
    for c, hh in pairs:
        lb = lb_ref[:, lanes(hh)]
        f = lb + (1.0 - lb) * _sigmoid(hf_ref[0, rows(c), lanes(hh)])
        logf = jnp.log(f)
        kks[c, hh] = 1.0 - f
        l0 = logf.astype(BF16)
        r1 = logf - l0.astype(F32)
        l1 = r1.astype(BF16)
        l2 = (r1 - l1.astype(F32)).astype(BF16)
        bs[c, hh] = _dot(tri, l0) + (_dot(tri, l1) + _dot(tri, l2))
    q_ins, intras, upds, decays = {}, {}, {}, {}
    for c, hh in pairs:
        b, kk = bs[c, hh], kks[c, hh]
        bl = b[chunk - 1:chunk, :]
        q_in = (hq_ref[0, rows(c), lanes(hh)] * (HG_DK ** -0.5) * jnp.exp(b)).astype(BF16)
        k_in = (kk * jnp.exp(-b)).astype(BF16)
        k_st = (kk * jnp.exp(bl - b)).astype(BF16)
        vb = hi_ref[0, rows(c), lanes(hh)].astype(BF16)
        att = jnp.where(causal, _dot_nt(q_in, k_in), 0.0)
        q_ins[c, hh] = q_in
        intras[c, hh] = _dot(att.astype(BF16), vb)
        upds[c, hh] = _dot_tn(vb, k_st)
        decays[c, hh] = jnp.exp(bl)
    for c, hh in pairs:
        st = s_scr[hh]
        o = _dot_nt(q_ins[c, hh], st.astype(BF16)) + intras[c, hh]
        s_scr[hh] = st * decays[c, hh] + upds[c, hh]
        on = o * lax.rsqrt(jnp.mean(o * o, axis=-1, keepdims=True) + EPS) * g_ref[:, lanes(hh)]
        r_ref[0, rows(c), lanes(hh)] = (on * _silu(hg_ref[0, rows(c), lanes(hh)])).astype(BF16)

    @pl.when(t == pl.num_programs(1) - 1)
    def _():
        for hh in range(HG_HEADS):
            st_ref[0, hh] = s_scr[hh].T


def _hgrn(hq, hf, hi, hg, lb, out_g, s0):
    b, t, w = hq.shape
    chunk = min(HG_CHUNK, t)
    assert t % chunk == 0
    nchunk = min(4, t // chunk)
    tt = chunk * nchunk
    seq = pl.BlockSpec((1, tt, w), lambda i, j: (i, j, 0))
    vec = pl.BlockSpec((1, w), lambda i, j: (0, 0))
    state = pl.BlockSpec((1, HG_HEADS, HG_DK, HG_DV), lambda i, j: (i, 0, 0, 0))
    return pl.pallas_call(
        functools.partial(_hgrn_kernel, chunk=chunk, nchunk=nchunk),
        out_shape=[jax.ShapeDtypeStruct((b, t, w), BF16), jax.ShapeDtypeStruct(s0.shape, F32)],
        grid=(b, t // tt),
        in_specs=[seq, seq, seq, seq, vec, vec, state],
        out_specs=[seq, state],
        scratch_shapes=[pltpu.VMEM((HG_HEADS, HG_DV, HG_DK), F32)],
        compiler_params=_cparams(("arbitrary", "arbitrary")),
        name="hgrn",
    )(hq, hf, hi, hg, lb.reshape(1, w), out_g.reshape(1, w), s0)


def _order_key(score):
    bits = pltpu.bitcast(score, I32)
    bits = jnp.where(bits == INT_MIN, 0, bits)
    return jnp.where(bits >= 0, bits, bits ^ 0x7FFFFFFF)


def _select_threshold(load_keys, store_keys, nchunk, tq, ck, topk, idx_bits, keys_on_rows=False, halves=None):
    kax = 0 if keys_on_rows else 1
    kshape = (ck, tq) if keys_on_rows else (tq, ck)
    vshape = (1, tq) if keys_on_rows else (tq, 1)

    def count(pred):
        def body(c, acc):
            m = jnp.where(pred(load_keys(c), c), 1.0, 0.0)
            if keys_on_rows:
                parts = [m[j * 8:(j + 1) * 8] for j in range(ck // 8)]
                while len(parts) > 1:
                    parts = [a + b for a, b in zip(parts[0::2], parts[1::2])]
                part = parts[0]
            else:
                part = m[:, 0:LANES]
                for j in range(1, ck // LANES):
                    part = part + m[:, j * LANES:(j + 1) * LANES]
            return acc + part
        acc = lax.fori_loop(0, nchunk, body, jnp.zeros((8, tq) if keys_on_rows else (tq, LANES), F32))
        return jnp.sum(acc, axis=kax, keepdims=True)

    kf = float(topk)
    zero_i = jnp.zeros(vshape, I32)

    def bisect(nbits, count_ge, done0):
        def cond(st):
            return jnp.logical_and(st[0] < nbits, jnp.min(st[2]) < 0.5)

        def body(st):
            it, tu, done, hit_u = st
            for _ in range(BISECT_BITS_PER_TRIP):
                cand = tu | lax.shift_left(jnp.int32(1), nbits - 1 - it)
                cnt = count_ge(cand)
                active = done < 0.5
                tu = jnp.where(jnp.logical_and(active, cnt >= kf), cand, tu)
                hit = jnp.logical_and(active, cnt == kf)
                hit_u = jnp.where(hit, cand, hit_u)
                done = jnp.where(hit, 1.0, done)
                it = it + 1
            return it, tu, done, hit_u
        return lax.while_loop(cond, body, (jnp.int32(0), zero_i, done0, zero_i))[1:]

    if halves is None:
        tu, done, hit_u = bisect(32, lambda u: count(lambda key, c: key >= (u ^ INT_MIN)), jnp.zeros(vshape, F32))
        ts = tu ^ INT_MIN
        tsel = (hit_u ^ INT_MIN) - 1
    else:
        load_hi, load_lo, store_lo = halves

        def count16(load, pred):
            def body(c, acc):
                m = jnp.where(pred(load(c)), jnp.int16(1), jnp.int16(0))
                parts = [m[j * 16:(j + 1) * 16] for j in range(ck // 16)]
                while len(parts) > 1:
                    parts = [a + b for a, b in zip(parts[0::2], parts[1::2])]
                return acc + parts[0].astype(I32).astype(F32)
            acc = lax.fori_loop(0, nchunk, body, jnp.zeros((16, tq), F32))
            return jnp.sum(acc, axis=0, keepdims=True)

        s16 = lambda u: (u - 32768).astype(jnp.int16)
        tu_hi, done, hit_hi = bisect(16, lambda u: count16(load_hi, lambda h: h >= s16(u)),
                                     jnp.zeros(vshape, F32))
        t_hi = s16(tu_hi)
        above = count16(load_hi, lambda h: h > t_hi)

        def mask_lo(c, carry):
            store_lo(c, jnp.where(load_hi(c) == t_hi, load_lo(c), jnp.int16(-32768)))
            return carry
        lax.fori_loop(0, nchunk, mask_lo, 0)
        first_done = done
        tu_lo, done, hit_lo = bisect(16, lambda u: above + count16(load_lo, lambda lo: lo >= s16(u)), done)
        base = (tu_hi - 32768) * 65536
        ts = base + tu_lo
        tsel = jnp.where(first_done > 0.5, (hit_hi - 32768) * 65536, base + hit_lo) - 1
        tu = ts ^ INT_MIN
    need_tie = jnp.logical_and(done < 0.5, tu != 0)
    thr = jnp.where(done > 0.5, tsel, ts)

    @pl.when(jnp.max(jnp.where(need_tie, 1.0, 0.0)) > 0.5)
    def _():
        rank = kf - count(lambda key, c: key > ts)
        pos_of = lambda c: c * ck + lax.broadcasted_iota(I32, kshape, kax)

        def jbody(it, ju):
            cand = ju | lax.shift_left(jnp.int32(1), idx_bits - 1 - it)
            pred = lambda key, c: jnp.logical_and(key == ts, pos_of(c) < cand)
            return jnp.where(count(pred) < rank, cand, ju)
        ju = lax.fori_loop(0, idx_bits, jbody, zero_i)

        def fix(c, carry):
            key = load_keys(c)
            bump = jnp.logical_and(jnp.logical_and(key == ts, pos_of(c) <= ju), need_tie)
            store_keys(c, jnp.where(bump, key + 1, key))
            return carry
        lax.fori_loop(0, nchunk, fix, 0)
    return thr


def _prompt_attn_kernel(q_ref, qi_ref, wt_ref, k4_ref, kb_ref, vt_ref, ga_ref, o_ref,
                        key_scr, hi_scr, lo_scr, m_scr, acc_scr, *, topk, idx_bits):
    tq, ck = q_ref.shape[1], ATT_CHUNK
    i = pl.program_id(1)
    nchunk = (i * tq + tq + ck - 1) // ck
    qpos = i * tq + lax.broadcasted_iota(I32, (1, tq), 1)

    qit = qi_ref[0].T
    hit = qit.astype(BF16)
    lot = (qit - hit.astype(F32)).astype(BF16)
    zero = jnp.zeros((IDX_DIM, tq), BF16)

    def idx_rhs(hh):
        hs = slice(hh * IDX_DIM, (hh + 1) * IDX_DIM)
        return jnp.concatenate([hit[hs], hit[hs], lot[hs], zero], axis=0)
    rhs_s = [jnp.concatenate([idx_rhs(2 * p), idx_rhs(2 * p + 1)], axis=1) for p in range(IDX_HEADS // 2)]
    wt = wt_ref[0]

    def score_body(c, carry):
        off = pl.multiple_of(c * ck, ck)
        k4 = k4_ref[0, pl.ds(off, ck), :]
        acc = jnp.zeros((ck, tq), F32)
        for p in range(IDX_HEADS // 2):
            s = _dot(k4, rhs_s[p])
            acc = acc + jnp.maximum(s[:, :tq], 0.0) * wt[2 * p:2 * p + 1]
            acc = acc + jnp.maximum(s[:, tq:], 0.0) * wt[2 * p + 1:2 * p + 2]
        kpos = off + lax.broadcasted_iota(I32, (ck, 1), 0)
        key = jnp.where(kpos <= qpos, _order_key(acc), INT_MIN)
        key_scr[pl.ds(off, ck), :] = key
        hi_scr[pl.ds(off, ck), :] = (key >> 16).astype(jnp.int16)
        lo_scr[pl.ds(off, ck), :] = ((key & 0xFFFF) - 32768).astype(jnp.int16)
        return carry
    lax.fori_loop(0, nchunk, score_body, 0)

    chunk = lambda c: pl.ds(pl.multiple_of(c * ck, ck), ck)

    def store_keys(c, val):
        key_scr[chunk(c), :] = val

    def store_lo(c, val):
        lo_scr[chunk(c), :] = val
    thr = _select_threshold(lambda c: key_scr[chunk(c), :], store_keys, nchunk, tq, ck, topk, idx_bits,
                            keys_on_rows=True,
                            halves=(lambda c: hi_scr[chunk(c), :], lambda c: lo_scr[chunk(c), :], store_lo))

    qt = q_ref[0].astype(F32).T.astype(BF16)
    rhs_a = []
    for m in range(KV_HEADS):
        blk = jnp.concatenate([qt[(2 * m) * HEAD_DIM:(2 * m + 1) * HEAD_DIM],
                               qt[(2 * m + 1) * HEAD_DIM:(2 * m + 2) * HEAD_DIM]], axis=1)
        parts = [jnp.zeros((HEAD_DIM, 2 * tq), BF16)] * m + [blk] + [jnp.zeros((HEAD_DIM, 2 * tq), BF16)] * (KV_HEADS - 1 - m)
        rhs_a.append(jnp.concatenate(parts, axis=0))
    m_scr[...] = jnp.full(m_scr.shape, NEG_BIG, F32)
    acc_scr[...] = jnp.zeros(acc_scr.shape, F32)

    def attn_body(c, carry):
        off = pl.multiple_of(c * ck, ck)
        kc = kb_ref[0, pl.ds(off, ck), :]
        sel = key_scr[pl.ds(off, ck), :] > thr
        def logits(m):
            return _dot(kc, rhs_a[m])

        def softmax(m, lg):
            lg = jnp.concatenate([jnp.where(sel, lg[:, :tq], NEG_BIG), jnp.where(sel, lg[:, tq:], NEG_BIG)], axis=1)
            mx = m_scr[m]
            mn = jnp.maximum(mx, jnp.max(lg, axis=0, keepdims=True))
            m_scr[m] = mn
            return jnp.exp2(lg - mn).astype(BF16), jnp.exp2(mx - mn)

        def accumulate(m, p, alpha):
            acc_scr[m] = acc_scr[m] * alpha + _dot(vt_ref[0, m, :, pl.ds(off, ck)], p)

        lg = {0: logits(0), 1: logits(1)}
        for m in range(KV_HEADS):
            p, alpha = softmax(m, lg.pop(m))
            if m + 2 < KV_HEADS:
                lg[m + 2] = logits(m + 2)
            accumulate(m, p, alpha)
        return carry
    lax.fori_loop(0, nchunk, attn_body, 0)
    pieces = []
    for m in range(KV_HEADS):
        acc = acc_scr[m]
        o = acc[0:HEAD_DIM] / acc[HEAD_DIM:HEAD_DIM + 1]
        pieces += [o[:, :tq], o[:, tq:]]
    at = jnp.concatenate(pieces, axis=0)
    at = at * lax.rsqrt(jnp.mean(at * at, axis=0, keepdims=True) + EPS)
    o_ref[0] = (at.T * ga_ref[...]).astype(BF16)


def _prompt_attention(q, qi, wi, ki, kb, vb, ga):
    b, t, _ = q.shape
    tq = LANES
    assert t % ATT_CHUNK == 0
    topk = min(TOPK_MAX, t // 4)
    ki_hi = ki.astype(BF16)
    ki_lo = (ki - ki_hi.astype(F32)).astype(BF16)
    k4 = jnp.concatenate([ki_hi, ki_lo, ki_hi, ki_hi], axis=-1)
    vt = vb.reshape(b, t, KV_HEADS, HEAD_DIM).transpose(0, 2, 3, 1)
    vt = jnp.concatenate([vt, jnp.ones((b, KV_HEADS, VT_ROWS - HEAD_DIM, t), BF16)], axis=2)
    blk = lambda w: pl.BlockSpec((1, tq, w), lambda bi, i: (bi, i, 0))
    full = lambda r, c: pl.BlockSpec((1, r, c), lambda bi, i: (bi, 0, 0))
    return pl.pallas_call(
        functools.partial(_prompt_attn_kernel, topk=topk, idx_bits=max(1, (t - 1).bit_length())),
        out_shape=jax.ShapeDtypeStruct((b, t, ATT_WIDTH), BF16),
        grid=(b, t // tq),
        in_specs=[blk(ATT_WIDTH), blk(ATT_WIDTH),
                  pl.BlockSpec((1, IDX_HEADS, tq), lambda bi, i: (bi, 0, i)),
                  full(t, 4 * IDX_DIM), full(t, KV_WIDTH),
                  pl.BlockSpec((1, KV_HEADS, VT_ROWS, t), lambda bi, i: (bi, 0, 0, 0)),
                  pl.BlockSpec((1, ATT_WIDTH), lambda bi, i: (0, 0))],
        out_specs=blk(ATT_WIDTH),
        scratch_shapes=[pltpu.VMEM((t, tq), I32), pltpu.VMEM((t, tq), jnp.int16), pltpu.VMEM((t, tq), jnp.int16),
                        pltpu.VMEM((KV_HEADS, 1, 2 * tq), F32),
                        pltpu.VMEM((KV_HEADS, VT_ROWS, 2 * tq), F32)],
        compiler_params=_cparams(("arbitrary", "arbitrary")),
        name="prompt_attn",
    )(q, qi, wi.transpose(0, 2, 1), k4, kb, vt, ga.reshape(1, ATT_WIDTH))


def _sample_score_kernel(pt_ref, qi_ref, w_ref, kin_ref, *rest, npg, nsteps, tnew, topk, idx_bits):
    page_refs = rest[:npg]
    key_ref, thr_ref = rest[npg:npg + 2]
    s = pl.program_id(1)
    last = s == nsteps
    tq = tnew
    page = page_refs[0].shape[1]
    ck = npg * page
    pages = [page_refs[p][...] for p in range(npg)]
    pages[0] = jnp.where(last, kin_ref[0], pages[0])
    k_hi, k_lo = _split_bf16(jnp.concatenate(pages, axis=1))
    sc = _dot(qi_ref[0], jnp.concatenate([k_hi, k_lo, k_hi], axis=0))
    sc = jnp.maximum(sc, 0.0) * w_ref[0]
    acc = sc[0:tq]
    for hh in range(1, IDX_HEADS):
        acc = acc + sc[hh * tq:(hh + 1) * tq]
    qrow = lax.broadcasted_iota(I32, (tq, ck), 0)
    col = lax.broadcasted_iota(I32, (tq, ck), 1)
    ok = jnp.logical_or(jnp.logical_not(last), jnp.logical_and(col <= qrow, col < tnew))
    key_ref[0, :, pl.ds(pl.multiple_of(s * ck, ck), ck)] = jnp.where(ok, _order_key(acc), INT_MIN)

    @pl.when(last)
    def _():
        def load_keys(c):
            return key_ref[0, :, pl.ds(pl.multiple_of(c * ck, ck), ck)]

        def store_keys(c, val):
            key_ref[0, :, pl.ds(pl.multiple_of(c * ck, ck), ck)] = val
        thr = _select_threshold(load_keys, store_keys, nsteps + 1, tq, ck, topk, idx_bits)
        thr_ref[0] = jnp.broadcast_to(thr, (tq, LANES))


def _sample_attn_kernel(pt_ref, q_ref, key_ref, thr_ref, kn_ref, vn_ref, ga_ref, *rest, npg, nsteps, tnew):
    k_refs = rest[:npg]
    v_refs = rest[npg:2 * npg]
    o_ref = rest[2 * npg]
    m_scr, l_scr, acc_scr = rest[2 * npg + 1:]
    s = pl.program_id(1)
    last = s == nsteps
    tq = tnew

    @pl.when(s == 0)
    def _():
        m_scr[...] = jnp.full(m_scr.shape, NEG_BIG, F32)
        l_scr[...] = jnp.zeros(l_scr.shape, F32)
        acc_scr[...] = jnp.zeros(acc_scr.shape, F32)

    sel = key_ref[0] > thr_ref[0][:, 0:1]
    sel2 = jnp.concatenate([sel, sel], axis=0)
    for m in range(KV_HEADS):
        kps = [k_refs[p][m] for p in range(npg)]
        vps = [v_refs[p][m] for p in range(npg)]
        kps[0] = jnp.where(last, kn_ref[0, m], kps[0])
        vps[0] = jnp.where(last, vn_ref[0, m], vps[0])
        kmt = jnp.concatenate(kps, axis=1).astype(BF16)
        vmt = jnp.concatenate(vps, axis=1).astype(BF16)
        lg = jnp.where(sel2, _dot(q_ref[0, m], kmt), NEG_BIG)
        mx = m_scr[m]
        mn = jnp.maximum(mx, jnp.max(lg, axis=-1, keepdims=True))
        pr = jnp.exp2(lg - mn)
        alpha = jnp.exp2(mx - mn)
        l_scr[m] = l_scr[m] * alpha + jnp.sum(pr, axis=-1, keepdims=True)
        acc_scr[m] = acc_scr[m] * alpha + _dot_nt(pr.astype(BF16), vmt)
        m_scr[m] = mn

    @pl.when(last)
    def _():
        pieces = []
        for m in range(KV_HEADS):
            o = acc_scr[m] / l_scr[m]
            pieces += [o[:tq], o[tq:]]
        a = jnp.concatenate(pieces, axis=1)
        a = a * lax.rsqrt(jnp.mean(a * a, axis=-1, keepdims=True) + EPS) * ga_ref[...]
        o_ref[0] = a.astype(BF16)


def _sample_attention(q, qi, wi, ki_new, k_new, v_new, cache_k, cache_v, cache_ki, layer, page_table, ga):
    db, tn, _ = q.shape
    page = cache_ki.shape[2]
    cache_k = cache_k.transpose(0, 1, 3, 4, 2)
    cache_v = cache_v.transpose(0, 1, 3, 4, 2)
    cache_ki = cache_ki.transpose(0, 1, 3, 2)
    n_pages = page_table.shape[1]
    npg_s = math.gcd(SAMPLE_SCORE_PAGES, n_pages)
    npg_a = math.gcd(SAMPLE_ATTN_PAGES, npg_s)
    nsteps_s, nsteps_a = n_pages // npg_s, n_pages // npg_a
    lpad = (nsteps_s + 1) * npg_s * page
    topk = min(TOPK_MAX, (n_pages * page + tn) // 4)
    assert tn <= page
    pad_keys = lambda a: jnp.pad(a, [(0, 0)] * (a.ndim - 1) + [(0, page - tn)])
    kin = pad_keys(ki_new.transpose(0, 2, 1))
    heads = lambda a: pad_keys(a.reshape(db, tn, KV_HEADS, HEAD_DIM).transpose(0, 2, 3, 1))
    qi_hq = qi.reshape(db, tn, IDX_HEADS, IDX_DIM).transpose(0, 2, 1, 3).reshape(db, IDX_HEADS * tn, IDX_DIM)
    qi_hi = qi_hq.astype(BF16)
    qi_hq = jnp.concatenate([qi_hi, qi_hi, (qi_hq - qi_hi.astype(F32)).astype(BF16)], axis=-1)
    w_hq = wi.transpose(0, 2, 1).reshape(db, IDX_HEADS * tn, 1)
    q2 = (q.reshape(db, tn, KV_HEADS, ATT_HEADS // KV_HEADS, HEAD_DIM).transpose(0, 2, 3, 1, 4)
          .reshape(db, KV_HEADS, (ATT_HEADS // KV_HEADS) * tn, HEAD_DIM))
    pt_flat = page_table.reshape(-1).astype(I32)

    def page_spec(tail, p, npg):
        def imap(b, s, pt):
            return (layer, pt[b * n_pages + jnp.minimum(s * npg + p, n_pages - 1)]) + (0,) * len(tail)
        return pl.BlockSpec((None, None) + tail, imap)

    per_b = lambda *tail: pl.BlockSpec((1,) + tail, lambda b, s, pt: (b,) + (0,) * len(tail))
    keys, thr = pl.pallas_call(
        functools.partial(_sample_score_kernel, npg=npg_s, nsteps=nsteps_s, tnew=tn, topk=topk,
                          idx_bits=max(1, (lpad - 1).bit_length())),
        out_shape=[jax.ShapeDtypeStruct((db, tn, lpad), I32),
                   jax.ShapeDtypeStruct((db, tn, LANES), I32)],
        grid_spec=pltpu.PrefetchScalarGridSpec(
            num_scalar_prefetch=1,
            grid=(db, nsteps_s + 1),
            in_specs=[per_b(IDX_HEADS * tn, 3 * IDX_DIM), per_b(IDX_HEADS * tn, 1), per_b(IDX_DIM, page)]
                     + [page_spec((IDX_DIM, page), p, npg_s) for p in range(npg_s)],
            out_specs=[per_b(tn, lpad), per_b(tn, LANES)]),
        compiler_params=_cparams(("arbitrary", "arbitrary")),
        name="sample_score",
    )(pt_flat, qi_hq, w_hq, kin, *([cache_ki] * npg_s))

    kv_pages = [page_spec((KV_HEADS, HEAD_DIM, page), p, npg_a) for p in range(npg_a)]
    a = pl.pallas_call(
        functools.partial(_sample_attn_kernel, npg=npg_a, nsteps=nsteps_a, tnew=tn),
        out_shape=jax.ShapeDtypeStruct((db, tn, ATT_WIDTH), BF16),
        grid_spec=pltpu.PrefetchScalarGridSpec(
            num_scalar_prefetch=1,
            grid=(db, nsteps_a + 1),
            in_specs=[per_b(KV_HEADS, (ATT_HEADS // KV_HEADS) * tn, HEAD_DIM),
                      pl.BlockSpec((1, tn, npg_a * page), lambda b, s, pt: (b, 0, s)),
                      per_b(tn, LANES),
                      per_b(KV_HEADS, HEAD_DIM, page), per_b(KV_HEADS, HEAD_DIM, page),
                      pl.BlockSpec((1, ATT_WIDTH), lambda b, s, pt: (0, 0))]
                     + kv_pages + kv_pages,
            out_specs=per_b(tn, ATT_WIDTH),
            scratch_shapes=[pltpu.VMEM((KV_HEADS, 2 * tn, 1), F32), pltpu.VMEM((KV_HEADS, 2 * tn, 1), F32),
                            pltpu.VMEM((KV_HEADS, 2 * tn, HEAD_DIM), F32)]),
        compiler_params=_cparams(("arbitrary", "arbitrary")),
        name="sample_attn",
    )(pt_flat, q2, keys, thr, heads(k_new), heads(v_new), ga.reshape(1, ATT_WIDTH),
      *([cache_k] * npg_a), *([cache_v] * npg_a))
    return a


def _outproj_kernel(a_ref, r_ref, x_ref, gt_ref, sc_ref, sh_ref, g_ref, wa_ref, wr_ref, wrt_ref, brt_ref,
                    x2_ref, h2_ref, route_ref):
    mix = _dot(a_ref[...], wa_ref[...]) + _dot(r_ref[...], wr_ref[...])
    x2 = x_ref[...] + gt_ref[0] * mix
    x2_ref[...] = x2
    h = x2 * lax.rsqrt(jnp.mean(x2 * x2, axis=-1, keepdims=True) + EPS) * g_ref[...]
    h = h * (1.0 + sc_ref[0]) + sh_ref[0]
    h2_ref[...] = h
    lg = (_dot3(h, wrt_ref[...]) + brt_ref[...]).T
    grp = [lg[g:g + 1] for g in range(N_GROUPS)]
    gmax = functools.reduce(jnp.maximum, grp)
    gden = functools.reduce(lambda u, v: u + v, [jnp.exp(g - gmax) for g in grp])
    gsel = jnp.full(gmax.shape, N_GROUPS - 1, I32)
    for g in range(N_GROUPS - 2, -1, -1):
        gsel = jnp.where(grp[g] == gmax, g, gsel)
    gw = 1.0 / gden
    el = []
    for e in range(EXPERTS_PER_GROUP):
        v = lg[N_GROUPS + e:N_GROUPS + e + 1]
        for g in range(1, N_GROUPS):
            row = N_GROUPS + g * EXPERTS_PER_GROUP + e
            v = jnp.where(gsel == g, lg[row:row + 1], v)
        el.append(v)
    emax = functools.reduce(jnp.maximum, el)
    e0 = jnp.full(emax.shape, EXPERTS_PER_GROUP - 1, I32)
    for e in range(EXPERTS_PER_GROUP - 2, -1, -1):
        e0 = jnp.where(el[e] == emax, e, e0)
    rest = [jnp.where(e0 == e, -jnp.inf, el[e]) for e in range(EXPERTS_PER_GROUP)]
    rmax = functools.reduce(jnp.maximum, rest)
    e1 = jnp.full(emax.shape, EXPERTS_PER_GROUP - 1, I32)
    for e in range(EXPERTS_PER_GROUP - 2, -1, -1):
        e1 = jnp.where(jnp.logical_and(rest[e] == rmax, e0 != e), e, e1)
    p1 = jnp.exp(rmax - emax)
    w0 = gw * (1.0 / (1.0 + p1))
    w1 = gw * (p1 / (1.0 + p1))
    base = gsel * EXPERTS_PER_GROUP
    rid = lax.broadcasted_iota(I32, lg.shape, 0)
    rt = jnp.where(rid == 0, (base + e0).astype(F32),
                   jnp.where(rid == 1, (base + e1).astype(F32),
                             jnp.where(rid == 2, w0, jnp.where(rid == 3, w1, 0.0))))
    route_ref[...] = rt.T


def _outproj(a, r, x2d, gt, sc, sh, g2, wa, wr, wrt, brt, tiles_per_group):
    n, d = x2d.shape
    tm = min(PROJ_TILE, n)
    rr = gt.shape[1]
    row = lambda w: pl.BlockSpec((tm, w), lambda i: (i, 0))
    mod = pl.BlockSpec((1, rr, d), lambda i: (i // tiles_per_group, 0, 0))
    const = lambda s: pl.BlockSpec(s, lambda i: (0, 0))
    return pl.pallas_call(
        _outproj_kernel,
        out_shape=[jax.ShapeDtypeStruct((n, d), F32), jax.ShapeDtypeStruct((n, d), F32),
                   jax.ShapeDtypeStruct((n, LANES), F32)],
        grid=(n // tm,),
        in_specs=[row(ATT_WIDTH), row(HG_WIDTH), row(d), mod, mod, mod, const((1, d)),
                  const(wa.shape), const(wr.shape), const(wrt.shape), const(brt.shape)],
        out_specs=[row(d), row(d), row(LANES)],
        compiler_params=_cparams(("arbitrary",)),
        name="outproj",
    )(a, r, x2d, gt, sc, sh, g2.reshape(1, d), wa, wr, wrt, brt)


def _gather_pipeline(step, nsteps, idx_hbm, src_hbm, buf, idx_smem, isem, rsem):
    nrows = buf.shape[1]
    slot = step % 2

    def idx_copy(b, sl):
        return pltpu.make_async_copy(idx_hbm.at[b], idx_smem.at[sl], isem.at[sl])

    def start_rows(sl):
        for r in range(nrows):
            pltpu.make_async_copy(src_hbm.at[pl.ds(idx_smem[sl, 0, r], 1), :],
                                  buf.at[sl, pl.ds(r, 1), :], rsem.at[sl]).start()

    @pl.when(step == 0)
    def _():
        idx_copy(0, 0).start()
        idx_copy(0, 0).wait()
        start_rows(0)

        @pl.when(nsteps > 1)
        def _():
            idx_copy(1, 1).start()

    @pl.when(step + 1 < nsteps)
    def _():
        idx_copy(step + 1, 1 - slot).wait()
        start_rows(1 - slot)

    @pl.when(step + 2 < nsteps)
    def _():
        idx_copy(step + 2, slot).start()

    pltpu.make_async_copy(src_hbm.at[pl.ds(0, nrows), :], buf.at[slot], rsem.at[slot]).wait()
    return slot


def _ffn_kernel(be_ref, nb_ref, tok_ref, h_ref, w1_ref, w3_ref, w2_ref, y_ref, xbuf, idx_smem, isem, rsem):
    j = pl.program_id(0)

    @pl.when(j < nb_ref[0])
    def _():
        slot = _gather_pipeline(j, nb_ref[0], tok_ref, h_ref, xbuf, idx_smem, isem, rsem)
        xb = xbuf[slot].astype(BF16)
        u = _dot(xb, w1_ref[0])
        g = _dot(xb, w3_ref[0])
        y_ref[...] = _dot((_silu(u) * g).astype(BF16), w2_ref[0])

    @pl.when(j >= nb_ref[0])
    def _():
        y_ref[...] = jnp.zeros(y_ref.shape, F32)


def _ffn(blk_e, n_used, tok, h2, w1, w3, w2, blk):
    nb = tok.shape[0]
    d = h2.shape[1]
    wspec = lambda s: pl.BlockSpec((1,) + s, lambda j, be, nu: (be[j], 0, 0))
    return pl.pallas_call(
        _ffn_kernel,
        out_shape=jax.ShapeDtypeStruct((nb * blk, d), F32),
        grid_spec=pltpu.PrefetchScalarGridSpec(
            num_scalar_prefetch=2,
            grid=(nb,),
            in_specs=[pl.BlockSpec(memory_space=pl.ANY), pl.BlockSpec(memory_space=pl.ANY),
                      wspec(w1.shape[1:]), wspec(w3.shape[1:]), wspec(w2.shape[1:])],
            out_specs=pl.BlockSpec((blk, d), lambda j, be, nu: (j, 0)),
            scratch_shapes=[pltpu.VMEM((2, blk, d), F32), pltpu.SMEM((2, 1, blk), I32),
                            pltpu.SemaphoreType.DMA((2,)), pltpu.SemaphoreType.DMA((2,))]),
        compiler_params=_cparams(("arbitrary",)),
        name="moe_ffn",
    )(blk_e, n_used, tok, h2, w1, w3, w2)


def _combine_kernel(dd_ref, y_ref, x2_ref, route_ref, gt_ref, sc_ref, sh_ref, g_ref, o_ref,
                    ybuf, idx_smem, isem, rsem):
    tm = x2_ref.shape[0]
    slot = _gather_pipeline(pl.program_id(0), pl.num_programs(0), dd_ref, y_ref, ybuf, idx_smem, isem, rsem)
    rt = route_ref[...]
    moe = ybuf[slot, 0:tm] * rt[:, 2:3] + ybuf[slot, tm:2 * tm] * rt[:, 3:4]
    x = x2_ref[...] + gt_ref[0] * moe
    y = x * lax.rsqrt(jnp.mean(x * x, axis=-1, keepdims=True) + EPS) * g_ref[...]
    o_ref[...] = y * (1.0 + sc_ref[0]) + sh_ref[0]


def _combine(dd, yb, x2, route, gt, sc, sh, gf, tiles_per_group):
    n, d = x2.shape
    tm = dd.shape[2] // 2
    rr = gt.shape[1]
    row = pl.BlockSpec((tm, d), lambda i: (i, 0))
    mod = pl.BlockSpec((1, rr, d), lambda i: (i // tiles_per_group, 0, 0))
    hbm = pl.BlockSpec(memory_space=pl.ANY)
    return pl.pallas_call(
        _combine_kernel,
        out_shape=jax.ShapeDtypeStruct((n, d), F32),
        grid=(n // tm,),
        in_specs=[hbm, hbm, row, pl.BlockSpec((tm, LANES), lambda i: (i, 0)), mod, mod, mod,
                  pl.BlockSpec((1, d), lambda i: (0, 0))],
        out_specs=row,
        scratch_shapes=[pltpu.VMEM((2, 2 * tm, d), F32), pltpu.SMEM((2, 1, 2 * tm), I32),
                        pltpu.SemaphoreType.DMA((2,)), pltpu.SemaphoreType.DMA((2,))],
        compiler_params=_cparams(("arbitrary",)),
        name="moe_combine",
    )(dd, yb, x2, route, gt, sc, sh, gf.reshape(1, d))


def _dispatch(route, blk):
    n = route.shape[0]
    flat_e = route[:, 0:2].astype(I32).reshape(-1)
    a = flat_e.shape[0]
    onehot = (flat_e[:, None] == jnp.arange(N_EXPERTS, dtype=I32)[None, :]).astype(I32)
    csum = jnp.cumsum(onehot, axis=0)
    rank = jnp.sum((csum - onehot) * onehot, axis=1)
    counts = csum[-1]
    padded = (counts + blk - 1) // blk * blk
    pad_end = jnp.cumsum(padded)
    pad_start = pad_end - padded
    dest = pad_start[flat_e] + rank
    nb = -(-a // blk) + N_EXPERTS
    tok = jnp.zeros((nb * blk,), I32).at[dest].set(jnp.arange(a, dtype=I32) // 2)
    blk_e = jnp.minimum(jnp.searchsorted(pad_end, jnp.arange(nb, dtype=I32) * blk, side='right'),
                        N_EXPERTS - 1).astype(I32)
    n_used = (pad_end[-1] // blk).astype(I32).reshape(1)
    dest2 = dest.reshape(n, 2)
    return tok.reshape(nb, 1, blk), blk_e, n_used, dest2[:, 0], dest2[:, 1]


def _mods(m, n_chunks, per_token_rows):
    parts = jnp.split(m, n_chunks, axis=-1)
    if per_token_rows is None:
        return [p[:, None, :] for p in parts]
    g, d = parts[0].shape
    tm = min(PROJ_TILE, g * per_token_rows)
    return [jnp.repeat(p, per_token_rows, axis=0).reshape(-1, tm, d) for p in parts]


def _layer(x, mod6, modf, pos, s0, attend, lb, wts, final_g, per_token):
    b, t, d = x.shape
    n = b * t
    (norm1_g, norm2_g, w_r, gk, ga, hg_out_g, wa, wr, wrt, brt, w1, w3, w2) = wts
    tm = min(PROJ_TILE, n)
    tiles_per_group = 1 if per_token else t // tm
    sh1, sc1, gt1, sh2, sc2, gt2 = _mods(mod6, 6, t if per_token else None)
    shf, scf = _mods(modf, 2, t if per_token else None)
    tabs = _rope_tables(jnp.tile(pos, tm // t) if per_token else pos)
    x2d = x.reshape(n, d)
    (q, k, v, kb, vd, qi, ki, wi, hq, hf, hi, hg) = _proj(x2d, sc1, sh1, norm1_g, w_r, gk, tabs, tiles_per_group)
    sq = lambda arr: arr.reshape(b, t, arr.shape[-1])
    r, s_t = _hgrn(sq(hq), sq(hf), sq(hi), sq(hg), lb, hg_out_g, s0)
    a = attend(sq(q), sq(qi), sq(wi), sq(ki), sq(k), sq(v), sq(kb), sq(vd), ga)
    x2, h2, route = _outproj(a.reshape(n, -1), r.reshape(n, -1), x2d, gt1, sc2, sh2, norm2_g,
                             wa, wr, wrt, brt, tiles_per_group)
    blk = MOE_BLOCK if n >= 8 * MOE_BLOCK else 64
    tok, blk_e, n_used, d0, d1 = _dispatch(route, blk)
    yb = _ffn(blk_e, n_used, tok, h2, w1, w3, w2, blk)
    dd = jnp.concatenate([d0.reshape(n // tm, 1, tm), d1.reshape(n // tm, 1, tm)], axis=2)
    y = _combine(dd, yb, x2, route, gt2, scf, shf, final_g, tiles_per_group)
    return (y.reshape(b, t, d), k.reshape(b, t, KV_HEADS, HEAD_DIM), v.reshape(b, t, KV_HEADS, HEAD_DIM),
            sq(ki), s_t)


def kernel(x_prompt, x_sample, cache_k, cache_v, cache_kidx, state_hgrn, page_table, c_prompt, c_sample,
           ada_w, ada_b, norm1_g, norm2_g, w_in, idx_k_g, hg_lb_logits, attn_out_g, hg_out_g, w_out,
           w_group, b_group, w_expert_router, b_expert_router, w1, w3, w2, final_g, ada_final_w, ada_final_b):
    depth = ada_w.shape[0]
    assert depth == 1, "the final adaLN norm is fused into the (single) layer's combine kernel"
    bp, tp, d = x_prompt.shape
    bs, ts, _ = x_sample.shape
    past = page_table.shape[1] * cache_kidx.shape[2]
    lb_all = jnp.cumsum(jax.nn.softmax(hg_lb_logits.astype(F32), axis=0), axis=0)
    n_c = bp + bs
    c_all = jnp.concatenate([c_prompt, c_sample, jnp.zeros((-n_c % 16, d), F32)], axis=0)
    modf = _ada(c_all, ada_final_w, ada_final_b)
    l = 0
    mod6 = _ada(c_all, ada_w[l], ada_b[l])

    seg = [0]
    for s in (ATT_WIDTH, KV_WIDTH, KV_WIDTH, IDX_HEADS * IDX_DIM, IDX_DIM, IDX_HEADS,
              HG_WIDTH, HG_WIDTH, HG_WIDTH, HG_WIDTH):
        seg.append(seg[-1] + s)
    wl = w_in[l]
    col = lambda i: wl[:, seg[i]:seg[i + 1]]
    zpad = lambda w: jnp.zeros((d, w), wl.dtype)
    w_r = jnp.concatenate([col(0), col(1), col(2), col(3), col(6), col(7), col(8), col(9),
                           col(4), zpad(LANES - IDX_DIM), col(5), zpad(LANES - IDX_HEADS)], axis=1).astype(BF16)
    gk = jnp.concatenate([idx_k_g[l], jnp.zeros((LANES - IDX_DIM,), F32)]).reshape(1, LANES)
    wa = w_out[l, :ATT_WIDTH].astype(BF16)
    wr = w_out[l, ATT_WIDTH:].astype(BF16)
    n_rt = N_GROUPS + N_EXPERTS
    wrt = jnp.concatenate([w_group[l], w_expert_router[l], jnp.zeros((d, LANES - n_rt), F32)], axis=1)
    brt = jnp.concatenate([b_group[l], b_expert_router[l], jnp.zeros((LANES - n_rt,), F32)]).reshape(1, LANES)
    wts = (norm1_g[l], norm2_g[l], w_r, gk, attn_out_g[l], hg_out_g[l], wa, wr, wrt, brt,
           w1[l].astype(BF16), w3[l].astype(BF16), w2[l].astype(BF16))

    def attend_p(q, qi, wi, ki, k, v, kb, vb, ga):
        return _prompt_attention(q, qi, wi, ki, kb, vb, ga)

    def attend_s(q, qi, wi, ki, k, v, kb, vb, ga):
        return _sample_attention(q, qi, wi, ki, k, v, cache_k, cache_v, cache_kidx, l, page_table, ga)

    s0_p = jnp.zeros((bp, HG_HEADS, HG_DK, HG_DV), F32)
    yp, kp, vp, kip, sp = _layer(x_prompt, mod6[:bp], modf[:bp], jnp.arange(tp), s0_p, attend_p,
                                 lb_all[l], wts, final_g, per_token=False)
    ys, ks, vs, kis, ss = _layer(x_sample, mod6[bp:n_c], modf[bp:n_c], past + jnp.arange(ts), state_hgrn[l],
                                 attend_s, lb_all[l], wts, final_g, per_token=True)
    return (yp, ys, kp[None], vp[None], kip[None], sp[None], ks[None], vs[None], kis[None], ss[None])
```

```python
import functools
import math

import jax
import jax.numpy as jnp
from jax import lax
from jax.experimental import pallas as pl
from jax.experimental.pallas import tpu as pltpu

F32 = jnp.float32
BF16 = jnp.bfloat16
I32 = jnp.int32

ATT_HEADS = 8
KV_HEADS = 4
HEAD_DIM = 64
ATT_WIDTH = ATT_HEADS * HEAD_DIM
KV_WIDTH = KV_HEADS * HEAD_DIM
ROT_HALF = HEAD_DIM // 8
ROPE_THETA = 500000.0
IDX_HEADS = 8
IDX_DIM = 64
TOPK_MAX = 256
HG_HEADS = 4
HG_DK = 128
HG_DV = 128
HG_WIDTH = HG_HEADS * HG_DV
HG_CHUNK = 64
N_GROUPS = 4
EXPERTS_PER_GROUP = 4
N_EXPERTS = N_GROUPS * EXPERTS_PER_GROUP
D_EXPERT = 512
MOE_BLOCK = 256
EPS = 1e-6

LANES = 128
INT_MIN = -2 ** 31
NEG_BIG = -1e30
VMEM_LIMIT = 56 * 1024 * 1024
PROJ_TILE = 256
ATT_CHUNK = 512
VT_ROWS = HEAD_DIM + 16
LOG2E = 1.4426950408889634
BISECT_BITS_PER_TRIP = 4
SAMPLE_SCORE_PAGES = 16
SAMPLE_ATTN_PAGES = 16

_Z_Q, _Z_K, _Z_V, _Z_QI, _Z_HQ, _Z_HF, _Z_HI, _Z_HG, _Z_KI, _Z_WI, _Z_END = (
    0, 512, 768, 1024, 1536, 2048, 2560, 3072, 3584, 3712, 3840)


def _cparams(sem):
    return pltpu.CompilerParams(dimension_semantics=sem, vmem_limit_bytes=VMEM_LIMIT)


def _split_bf16(x):
    hi = x.astype(BF16)
    lo = (x - hi.astype(F32)).astype(BF16)
    return hi, lo


def _dot(a, b):
    return jnp.dot(a, b, preferred_element_type=F32)


def _dot_nt(a, b):
    return lax.dot_general(a, b, (((1,), (1,)), ((), ())), preferred_element_type=F32)


def _dot_tn(a, b):
    return lax.dot_general(a, b, (((0,), (0,)), ((), ())), preferred_element_type=F32)


def _dot3(a, b):
    ah, al = _split_bf16(a)
    bh, bl = _split_bf16(b)
    return _dot(ah, bh) + (_dot(ah, bl) + _dot(al, bh))


def _dot3_nt(a, b):
    ah, al = _split_bf16(a)
    bh, bl = _split_bf16(b)
    return _dot_nt(ah, bh) + (_dot_nt(ah, bl) + _dot_nt(al, bh))


def _silu(x):
    return x * (1.0 / (1.0 + jnp.exp(-x)))


def _sigmoid(x):
    return 1.0 / (1.0 + jnp.exp(-x))


def _ada_kernel(c_ref, w_ref, b_ref, o_ref):
    o_ref[...] = _dot3(_silu(c_ref[...]), w_ref[...]) + b_ref[...]


def _ada(c, w, b):
    r, d = c.shape
    e = w.shape[1]
    te = 1024
    return pl.pallas_call(
        _ada_kernel,
        out_shape=jax.ShapeDtypeStruct((r, e), F32),
        grid=(e // te,),
        in_specs=[pl.BlockSpec((r, d), lambda j: (0, 0)),
                  pl.BlockSpec((d, te), lambda j: (0, j)),
                  pl.BlockSpec((1, te), lambda j: (0, j))],
        out_specs=pl.BlockSpec((r, te), lambda j: (0, j)),
        compiler_params=_cparams(("arbitrary",)),
        name="ada",
    )(c, w, b.reshape(1, e))


def _rope_tables(pos):
    p = pos.shape[0]
    inv = ROPE_THETA ** (-jnp.arange(ROT_HALF, dtype=F32) * (2.0 / (2 * ROT_HALF)))
    ang = pos.astype(F32)[:, None] * inv[None, :]
    c, s = jnp.cos(ang), jnp.sin(ang)
    rest = HEAD_DIM - 2 * ROT_HALF
    one, zero, z8 = jnp.ones((p, rest), F32), jnp.zeros((p, rest), F32), jnp.zeros((p, ROT_HALF), F32)
    cos64 = jnp.concatenate([c, c, one], axis=1)
    sa64 = jnp.concatenate([-s, z8, zero], axis=1)
    sb64 = jnp.concatenate([z8, s, zero], axis=1)
    dup = lambda t: jnp.concatenate([t, t], axis=1)
    return dup(cos64), dup(sa64), dup(sb64)


def _proj_kernel(x_ref, sc_ref, sh_ref, g_ref, w_ref, gk_ref, cos_ref, sa_ref, sb_ref,
                 q_ref, k_ref, v_ref, kb_ref, vb_ref, qi_ref, ki_ref, wi_ref,
                 hq_ref, hf_ref, hi_ref, hg_ref):
    x = x_ref[...]
    ms = jnp.mean(x * x, axis=-1, keepdims=True)
    h = x * lax.rsqrt(ms + EPS) * g_ref[...]
    h = h * (1.0 + sc_ref[0]) + sh_ref[0]
    z = _dot(h.astype(BF16), w_ref[...])
    cos, sa, sb = cos_ref[...], sa_ref[...], sb_ref[...]

    def rope(zs):
        n = zs.shape[1] // LANES
        rep = (lambda t: jnp.concatenate([t] * n, axis=1)) if n > 1 else (lambda t: t)
        w = zs.shape[1]
        return (zs * rep(cos) + pltpu.roll(zs, w - ROT_HALF, 1) * rep(sa)
                + pltpu.roll(zs, ROT_HALF, 1) * rep(sb))

    q_ref[...] = (rope(z[:, _Z_Q:_Z_K]) * (HEAD_DIM ** -0.5 * LOG2E)).astype(BF16)
    k = rope(z[:, _Z_K:_Z_V])
    k_ref[...] = k
    kb_ref[...] = k.astype(BF16)
    v = z[:, _Z_V:_Z_QI]
    v_ref[...] = v
    vb_ref[...] = v.astype(BF16)
    qi_ref[...] = rope(z[:, _Z_QI:_Z_HQ]) * (IDX_DIM ** -0.5)
    hq_ref[...] = z[:, _Z_HQ:_Z_HF]
    hf_ref[...] = z[:, _Z_HF:_Z_HI]
    hi_ref[...] = z[:, _Z_HI:_Z_HG]
    hg_ref[...] = z[:, _Z_HG:_Z_KI]
    ks = z[:, _Z_KI:_Z_WI]
    kms = jnp.sum(ks * ks, axis=-1, keepdims=True) * (1.0 / IDX_DIM)
    kn = ks * lax.rsqrt(kms + EPS) * gk_ref[...]
    ki_ref[...] = rope(kn)[:, :IDX_DIM]
    wi_ref[...] = z[:, _Z_WI:_Z_WI + IDX_HEADS] * (IDX_HEADS ** -0.5)


def _proj(x2d, sc, sh, g1, w_r, gk, tabs, tiles_per_group):
    n, d = x2d.shape
    tm = min(PROJ_TILE, n)
    nt = n // tm
    r = sc.shape[1]
    pt = tabs[0].shape[0] // tm
    row = lambda w: pl.BlockSpec((tm, w), lambda i: (i, 0))
    mod = pl.BlockSpec((1, r, d), lambda i: (i // tiles_per_group, 0, 0))
    tab = pl.BlockSpec((tm, LANES), lambda i: (i % pt, 0))
    outs = [(ATT_WIDTH, BF16), (KV_WIDTH, F32), (KV_WIDTH, F32), (KV_WIDTH, BF16), (KV_WIDTH, BF16),
            (ATT_WIDTH, F32), (IDX_DIM, F32), (IDX_HEADS, F32),
            (HG_WIDTH, F32), (HG_WIDTH, F32), (HG_WIDTH, F32), (HG_WIDTH, F32)]
    return pl.pallas_call(
        _proj_kernel,
        out_shape=[jax.ShapeDtypeStruct((n, w), t) for w, t in outs],
        grid=(nt,),
        in_specs=[row(d), mod, mod,
                  pl.BlockSpec((1, d), lambda i: (0, 0)),
                  pl.BlockSpec(w_r.shape, lambda i: (0, 0)),
                  pl.BlockSpec((1, LANES), lambda i: (0, 0)),
                  tab, tab, tab],
        out_specs=[row(w) for w, _ in outs],
        compiler_params=_cparams(("arbitrary",)),
        name="proj",
    )(x2d, sc, sh, g1.reshape(1, d), w_r, gk, *tabs)


def _hgrn_kernel(hq_ref, hf_ref, hi_ref, hg_ref, lb_ref, g_ref, s0_ref, r_ref, st_ref, s_scr, *, chunk, nchunk):
    t = pl.program_id(1)

    @pl.when(t == 0)
    def _():
        for hh in range(HG_HEADS):
            s_scr[hh] = s0_ref[0, hh].T

    row = lax.broadcasted_iota(I32, (chunk, chunk), 0)
    col = lax.broadcasted_iota(I32, (chunk, chunk), 1)
    causal = col <= row
    tri = causal.astype(BF16)
    pairs = [(c, hh) for c in range(nchunk) for hh in range(HG_HEADS)]
    rows = lambda c: slice(c * chunk, (c + 1) * chunk)
    lanes = lambda hh: slice(hh * HG_DK, (hh + 1) * HG_DK)
    kks, bs = {}, {}
    for c, hh in pairs:
        lb = lb_ref[:, lanes(hh)]
        f = lb + (1.0 - lb) * _sigmoid(hf_ref[0, rows(c), lanes(hh)])
        logf = jnp.log(f)
        kks[c, hh] = 1.0 - f
        l0 = logf.astype(BF16)
        r1 = logf - l0.astype(F32)
        l1 = r1.astype(BF16)
        l2 = (r1 - l1.astype(F32)).astype(BF16)
        bs[c, hh] = _dot(tri, l0) + (_dot(tri, l1) + _dot(tri, l2))
    q_ins, intras, upds, decays = {}, {}, {}, {}
    for c, hh in pairs:
        b, kk = bs[c, hh], kks[c, hh]
        bl = b[chunk - 1:chunk, :]
        q_in = (hq_ref[0, rows(c), lanes(hh)] * (HG_DK ** -0.5) * jnp.exp(b)).astype(BF16)
        k_in = (kk * jnp.exp(-b)).astype(BF16)
        k_st = (kk * jnp.exp(bl - b)).astype(BF16)
        vb = hi_ref[0, rows(c), lanes(hh)].astype(BF16)
        att = jnp.where(causal, _dot_nt(q_in, k_in), 0.0)
        q_ins[c, hh] = q_in
        intras[c, hh] = _dot(att.astype(BF16), vb)
        upds[c, hh] = _dot_tn(vb, k_st)
        decays[c, hh] = jnp.exp(bl)
    for c, hh in pairs:
        st = s_scr[hh]
        o = _dot_nt(q_ins[c, hh], st.astype(BF16)) + intras[c, hh]
        s_scr[hh] = st * decays[c, hh] + upds[c, hh]
        on = o * lax.rsqrt(jnp.mean(o * o, axis=-1, keepdims=True) + EPS) * g_ref[:, lanes(hh)]
        r_ref[0, rows(c), lanes(hh)] = (on * _silu(hg_ref[0, rows(c), lanes(hh)])).astype(BF16)

    @pl.when(t == pl.num_programs(1) - 1)
    def _():
        for hh in range(HG_HEADS):
            st_ref[0, hh] = s_scr[hh].T


def _hgrn(hq, hf, hi, hg, lb, out_g, s0):
    b, t, w = hq.shape
    chunk = min(HG_CHUNK, t)
    assert t % chunk == 0
    nchunk = min(4, t // chunk)
    tt = chunk * nchunk
    seq = pl.BlockSpec((1, tt, w), lambda i, j: (i, j, 0))
    vec = pl.BlockSpec((1, w), lambda i, j: (0, 0))
    state = pl.BlockSpec((1, HG_HEADS, HG_DK, HG_DV), lambda i, j: (i, 0, 0, 0))
    return pl.pallas_call(
        functools.partial(_hgrn_kernel, chunk=chunk, nchunk=nchunk),
        out_shape=[jax.ShapeDtypeStruct((b, t, w), BF16), jax.ShapeDtypeStruct(s0.shape, F32)],
        grid=(b, t // tt),
        in_specs=[seq, seq, seq, seq, vec, vec, state],
        out_specs=[seq, state],
        scratch_shapes=[pltpu.VMEM((HG_HEADS, HG_DV, HG_DK), F32)],
        compiler_params=_cparams(("arbitrary", "arbitrary")),
        name="hgrn",
    )(hq, hf, hi, hg, lb.reshape(1, w), out_g.reshape(1, w), s0)


def _order_key(score):
    bits = pltpu.bitcast(score, I32)
    bits = jnp.where(bits == INT_MIN, 0, bits)
    return jnp.where(bits >= 0, bits, bits ^ 0x7FFFFFFF)


def _select_threshold(load_keys, store_keys, nchunk, tq, ck, topk, idx_bits, keys_on_rows=False, halves=None):
    kax = 0 if keys_on_rows else 1
    kshape = (ck, tq) if keys_on_rows else (tq, ck)
    vshape = (1, tq) if keys_on_rows else (tq, 1)

    def count(pred):
        def body(c, acc):
            m = jnp.where(pred(load_keys(c), c), 1.0, 0.0)
            if keys_on_rows:
                parts = [m[j * 8:(j + 1) * 8] for j in range(ck // 8)]
                while len(parts) > 1:
                    parts = [a + b for a, b in zip(parts[0::2], parts[1::2])]
                part = parts[0]
            else:
                part = m[:, 0:LANES]
                for j in range(1, ck // LANES):
                    part = part + m[:, j * LANES:(j + 1) * LANES]
            return acc + part
        acc = lax.fori_loop(0, nchunk, body, jnp.zeros((8, tq) if keys_on_rows else (tq, LANES), F32))
        return jnp.sum(acc, axis=kax, keepdims=True)

    kf = float(topk)
    zero_i = jnp.zeros(vshape, I32)

    def bisect(nbits, count_ge, done0):
        def cond(st):
            return jnp.logical_and(st[0] < nbits, jnp.min(st[2]) < 0.5)

        def body(st):
            it, tu, done, hit_u = st
            for _ in range(BISECT_BITS_PER_TRIP):
                cand = tu | lax.shift_left(jnp.int32(1), nbits - 1 - it)
                cnt = count_ge(cand)
                active = done < 0.5
                tu = jnp.where(jnp.logical_and(active, cnt >= kf), cand, tu)
                hit = jnp.logical_and(active, cnt == kf)
                hit_u = jnp.where(hit, cand, hit_u)
                done = jnp.where(hit, 1.0, done)
                it = it + 1
            return it, tu, done, hit_u
        return lax.while_loop(cond, body, (jnp.int32(0), zero_i, done0, zero_i))[1:]

    if halves is None:
        tu, done, hit_u = bisect(32, lambda u: count(lambda key, c: key >= (u ^ INT_MIN)), jnp.zeros(vshape, F32))
        ts = tu ^ INT_MIN
        tsel = (hit_u ^ INT_MIN) - 1
    else:
        load_hi, load_lo, store_lo = halves

        def count16(load, pred):
            def body(c, acc):
                m = jnp.where(pred(load(c)), jnp.int16(1), jnp.int16(0))
                parts = [m[j * 16:(j + 1) * 16] for j in range(ck // 16)]
                while len(parts) > 1:
                    parts = [a + b for a, b in zip(parts[0::2], parts[1::2])]
                return acc + parts[0].astype(I32).astype(F32)
            acc = lax.fori_loop(0, nchunk, body, jnp.zeros((16, tq), F32))
            return jnp.sum(acc, axis=0, keepdims=True)

        s16 = lambda u: (u - 32768).astype(jnp.int16)
        tu_hi, done, hit_hi = bisect(16, lambda u: count16(load_hi, lambda h: h >= s16(u)),
                                     jnp.zeros(vshape, F32))
        t_hi = s16(tu_hi)
        above = count16(load_hi, lambda h: h > t_hi)

        def mask_lo(c, carry):
            store_lo(c, jnp.where(load_hi(c) == t_hi, load_lo(c), jnp.int16(-32768)))
            return carry
        lax.fori_loop(0, nchunk, mask_lo, 0)
        first_done = done
        tu_lo, done, hit_lo = bisect(16, lambda u: above + count16(load_lo, lambda lo: lo >= s16(u)), done)
        base = (tu_hi - 32768) * 65536
        ts = base + tu_lo
        tsel = jnp.where(first_done > 0.5, (hit_hi - 32768) * 65536, base + hit_lo) - 1
        tu = ts ^ INT_MIN
    need_tie = jnp.logical_and(done < 0.5, tu != 0)
    thr = jnp.where(done > 0.5, tsel, ts)

    @pl.when(jnp.max(jnp.where(need_tie, 1.0, 0.0)) > 0.5)
    def _():
        rank = kf - count(lambda key, c: key > ts)
        pos_of = lambda c: c * ck + lax.broadcasted_iota(I32, kshape, kax)

        def jbody(it, ju):
            cand = ju | lax.shift_left(jnp.int32(1), idx_bits - 1 - it)
            pred = lambda key, c: jnp.logical_and(key == ts, pos_of(c) < cand)
            return jnp.where(count(pred) < rank, cand, ju)
        ju = lax.fori_loop(0, idx_bits, jbody, zero_i)

        def fix(c, carry):
            key = load_keys(c)
            bump = jnp.logical_and(jnp.logical_and(key == ts, pos_of(c) <= ju), need_tie)
            store_keys(c, jnp.where(bump, key + 1, key))
            return carry
        lax.fori_loop(0, nchunk, fix, 0)
    return thr


def _prompt_attn_kernel(q_ref, qi_ref, wt_ref, k4_ref, kb_ref, vt_ref, ga_ref, o_ref,
                        key_scr, hi_scr, lo_scr, m_scr, acc_scr, *, topk, idx_bits):
    tq, ck = q_ref.shape[1], ATT_CHUNK
    i = pl.program_id(1)
    nchunk = (i * tq + tq + ck - 1) // ck
    qpos = i * tq + lax.broadcasted_iota(I32, (1, tq), 1)

    qit = qi_ref[0].T
    hit = qit.astype(BF16)
    lot = (qit - hit.astype(F32)).astype(BF16)
    zero = jnp.zeros((IDX_DIM, tq), BF16)

    def idx_rhs(hh):
        hs = slice(hh * IDX_DIM, (hh + 1) * IDX_DIM)
        return jnp.concatenate([hit[hs], hit[hs], lot[hs], zero], axis=0)
    rhs_s = [jnp.concatenate([idx_rhs(2 * p), idx_rhs(2 * p + 1)], axis=1) for p in range(IDX_HEADS // 2)]
    wt = wt_ref[0]

    def score_body(c, carry):
        off = pl.multiple_of(c * ck, ck)
        k4 = k4_ref[0, pl.ds(off, ck), :]
        acc = jnp.zeros((ck, tq), F32)
        for p in range(IDX_HEADS // 2):
            s = _dot(k4, rhs_s[p])
            acc = acc + jnp.maximum(s[:, :tq], 0.0) * wt[2 * p:2 * p + 1]
            acc = acc + jnp.maximum(s[:, tq:], 0.0) * wt[2 * p + 1:2 * p + 2]
        kpos = off + lax.broadcasted_iota(I32, (ck, 1), 0)
        key = jnp.where(kpos <= qpos, _order_key(acc), INT_MIN)
        key_scr[pl.ds(off, ck), :] = key
        hi_scr[pl.ds(off, ck), :] = (key >> 16).astype(jnp.int16)
        lo_scr[pl.ds(off, ck), :] = ((key & 0xFFFF) - 32768).astype(jnp.int16)
        return carry
    lax.fori_loop(0, nchunk, score_body, 0)

    chunk = lambda c: pl.ds(pl.multiple_of(c * ck, ck), ck)

    def store_keys(c, val):
        key_scr[chunk(c), :] = val

    def store_lo(c, val):
        lo_scr[chunk(c), :] = val
    thr = _select_threshold(lambda c: key_scr[chunk(c), :], store_keys, nchunk, tq, ck, topk, idx_bits,
                            keys_on_rows=True,
                            halves=(lambda c: hi_scr[chunk(c), :], lambda c: lo_scr[chunk(c), :], store_lo))

    qt = q_ref[0].astype(F32).T.astype(BF16)
    rhs_a = []
    for m in range(KV_HEADS):
        blk = jnp.concatenate([qt[(2 * m) * HEAD_DIM:(2 * m + 1) * HEAD_DIM],
                               qt[(2 * m + 1) * HEAD_DIM:(2 * m + 2) * HEAD_DIM]], axis=1)
        parts = [jnp.zeros((HEAD_DIM, 2 * tq), BF16)] * m + [blk] + [jnp.zeros((HEAD_DIM, 2 * tq), BF16)] * (KV_HEADS - 1 - m)
        rhs_a.append(jnp.concatenate(parts, axis=0))
    m_scr[...] = jnp.full(m_scr.shape, NEG_BIG, F32)
    acc_scr[...] = jnp.zeros(acc_scr.shape, F32)

    def attn_body(c, carry):
        off = pl.multiple_of(c * ck, ck)
        kc = kb_ref[0, pl.ds(off, ck), :]
        sel = key_scr[pl.ds(off, ck), :] > thr
        def logits(m):
            return _dot(kc, rhs_a[m])

        def softmax(m, lg):
            lg = jnp.concatenate([jnp.where(sel, lg[:, :tq], NEG_BIG), jnp.where(sel, lg[:, tq:], NEG_BIG)], axis=1)
            mx = m_scr[m]
            mn = jnp.maximum(mx, jnp.max(lg, axis=0, keepdims=True))
            m_scr[m] = mn
            return jnp.exp2(lg - mn).astype(BF16), jnp.exp2(mx - mn)

        def accumulate(m, p, alpha):
            acc_scr[m] = acc_scr[m] * alpha + _dot(vt_ref[0, m, :, pl.ds(off, ck)], p)

        lgs = [logits(m) for m in range(KV_HEADS)]
        pas = [softmax(m, lgs[m]) for m in range(KV_HEADS)]
        for m in range(KV_HEADS):
            accumulate(m, *pas[m])
        return carry
    lax.fori_loop(0, nchunk, attn_body, 0)
    pieces = []
    for m in range(KV_HEADS):
        acc = acc_scr[m]
        o = acc[0:HEAD_DIM] / acc[HEAD_DIM:HEAD_DIM + 1]
        pieces += [o[:, :tq], o[:, tq:]]
    at = jnp.concatenate(pieces, axis=0)
    at = at * lax.rsqrt(jnp.mean(at * at, axis=0, keepdims=True) + EPS)
    o_ref[0] = (at.T * ga_ref[...]).astype(BF16)


def _prompt_attention(q, qi, wi, ki, kb, vb, ga):
    b, t, _ = q.shape
    tq = LANES
    assert t % ATT_CHUNK == 0
    topk = min(TOPK_MAX, t // 4)
    ki_hi = ki.astype(BF16)
    ki_lo = (ki - ki_hi.astype(F32)).astype(BF16)
    k4 = jnp.concatenate([ki_hi, ki_lo, ki_hi, ki_hi], axis=-1)
    vt = vb.reshape(b, t, KV_HEADS, HEAD_DIM).transpose(0, 2, 3, 1)
    vt = jnp.concatenate([vt, jnp.ones((b, KV_HEADS, VT_ROWS - HEAD_DIM, t), BF16)], axis=2)
    blk = lambda w: pl.BlockSpec((1, tq, w), lambda bi, i: (bi, i, 0))
    full = lambda r, c: pl.BlockSpec((1, r, c), lambda bi, i: (bi, 0, 0))
    return pl.pallas_call(
        functools.partial(_prompt_attn_kernel, topk=topk, idx_bits=max(1, (t - 1).bit_length())),
        out_shape=jax.ShapeDtypeStruct((b, t, ATT_WIDTH), BF16),
        grid=(b, t // tq),
        in_specs=[blk(ATT_WIDTH), blk(ATT_WIDTH),
                  pl.BlockSpec((1, IDX_HEADS, tq), lambda bi, i: (bi, 0, i)),
                  full(t, 4 * IDX_DIM), full(t, KV_WIDTH),
                  pl.BlockSpec((1, KV_HEADS, VT_ROWS, t), lambda bi, i: (bi, 0, 0, 0)),
                  pl.BlockSpec((1, ATT_WIDTH), lambda bi, i: (0, 0))],
        out_specs=blk(ATT_WIDTH),
        scratch_shapes=[pltpu.VMEM((t, tq), I32), pltpu.VMEM((t, tq), jnp.int16), pltpu.VMEM((t, tq), jnp.int16),
                        pltpu.VMEM((KV_HEADS, 1, 2 * tq), F32),
                        pltpu.VMEM((KV_HEADS, VT_ROWS, 2 * tq), F32)],
        compiler_params=_cparams(("arbitrary", "arbitrary")),
        name="prompt_attn",
    )(q, qi, wi.transpose(0, 2, 1), k4, kb, vt, ga.reshape(1, ATT_WIDTH))


def _sample_score_kernel(pt_ref, qi_ref, w_ref, kin_ref, *rest, npg, nsteps, tnew, topk, idx_bits):
    page_refs = rest[:npg]
    key_ref, thr_ref = rest[npg:npg + 2]
    s = pl.program_id(1)
    last = s == nsteps
    tq = tnew
    page = page_refs[0].shape[1]
    ck = npg * page
    pages = [page_refs[p][...] for p in range(npg)]
    pages[0] = jnp.where(last, kin_ref[0], pages[0])
    k_hi, k_lo = _split_bf16(jnp.concatenate(pages, axis=1))
    sc = _dot(qi_ref[0], jnp.concatenate([k_hi, k_lo, k_hi], axis=0))
    sc = jnp.maximum(sc, 0.0) * w_ref[0]
    acc = sc[0:tq]
    for hh in range(1, IDX_HEADS):
        acc = acc + sc[hh * tq:(hh + 1) * tq]
    qrow = lax.broadcasted_iota(I32, (tq, ck), 0)
    col = lax.broadcasted_iota(I32, (tq, ck), 1)
    ok = jnp.logical_or(jnp.logical_not(last), jnp.logical_and(col <= qrow, col < tnew))
    key_ref[0, :, pl.ds(pl.multiple_of(s * ck, ck), ck)] = jnp.where(ok, _order_key(acc), INT_MIN)

    @pl.when(last)
    def _():
        def load_keys(c):
            return key_ref[0, :, pl.ds(pl.multiple_of(c * ck, ck), ck)]

        def store_keys(c, val):
            key_ref[0, :, pl.ds(pl.multiple_of(c * ck, ck), ck)] = val
        thr = _select_threshold(load_keys, store_keys, nsteps + 1, tq, ck, topk, idx_bits)
        thr_ref[0] = jnp.broadcast_to(thr, (tq, LANES))


def _sample_attn_kernel(pt_ref, q_ref, key_ref, thr_ref, kn_ref, vn_ref, ga_ref, *rest, npg, nsteps, tnew):
    k_refs = rest[:npg]
    v_refs = rest[npg:2 * npg]
    o_ref = rest[2 * npg]
    m_scr, l_scr, acc_scr = rest[2 * npg + 1:]
    s = pl.program_id(1)
    last = s == nsteps
    tq = tnew

    @pl.when(s == 0)
    def _():
        m_scr[...] = jnp.full(m_scr.shape, NEG_BIG, F32)
        l_scr[...] = jnp.zeros(l_scr.shape, F32)
        acc_scr[...] = jnp.zeros(acc_scr.shape, F32)

    sel = key_ref[0] > thr_ref[0][:, 0:1]
    sel2 = jnp.concatenate([sel, sel], axis=0)
    for m in range(KV_HEADS):
        kps = [k_refs[p][m] for p in range(npg)]
        vps = [v_refs[p][m] for p in range(npg)]
        kps[0] = jnp.where(last, kn_ref[0, m], kps[0])
        vps[0] = jnp.where(last, vn_ref[0, m], vps[0])
        kmt = jnp.concatenate(kps, axis=1).astype(BF16)
        vmt = jnp.concatenate(vps, axis=1).astype(BF16)
        lg = jnp.where(sel2, _dot(q_ref[0, m], kmt), NEG_BIG)
        mx = m_scr[m]
        mn = jnp.maximum(mx, jnp.max(lg, axis=-1, keepdims=True))
        pr = jnp.exp2(lg - mn)
        alpha = jnp.exp2(mx - mn)
        l_scr[m] = l_scr[m] * alpha + jnp.sum(pr, axis=-1, keepdims=True)
        acc_scr[m] = acc_scr[m] * alpha + _dot_nt(pr.astype(BF16), vmt)
        m_scr[m] = mn

    @pl.when(last)
    def _():
        pieces = []
        for m in range(KV_HEADS):
            o = acc_scr[m] / l_scr[m]
            pieces += [o[:tq], o[tq:]]
        a = jnp.concatenate(pieces, axis=1)
        a = a * lax.rsqrt(jnp.mean(a * a, axis=-1, keepdims=True) + EPS) * ga_ref[...]
        o_ref[0] = a.astype(BF16)


def _sample_attention(q, qi, wi, ki_new, k_new, v_new, cache_k, cache_v, cache_ki, layer, page_table, ga):
    db, tn, _ = q.shape
    page = cache_ki.shape[2]
    cache_k = cache_k.transpose(0, 1, 3, 4, 2)
    cache_v = cache_v.transpose(0, 1, 3, 4, 2)
    cache_ki = cache_ki.transpose(0, 1, 3, 2)
    n_pages = page_table.shape[1]
    npg_s = math.gcd(SAMPLE_SCORE_PAGES, n_pages)
    npg_a = math.gcd(SAMPLE_ATTN_PAGES, npg_s)
    nsteps_s, nsteps_a = n_pages // npg_s, n_pages // npg_a
    lpad = (nsteps_s + 1) * npg_s * page
    topk = min(TOPK_MAX, (n_pages * page + tn) // 4)
    assert tn <= page
    pad_keys = lambda a: jnp.pad(a, [(0, 0)] * (a.ndim - 1) + [(0, page - tn)])
    kin = pad_keys(ki_new.transpose(0, 2, 1))
    heads = lambda a: pad_keys(a.reshape(db, tn, KV_HEADS, HEAD_DIM).transpose(0, 2, 3, 1))
    qi_hq = qi.reshape(db, tn, IDX_HEADS, IDX_DIM).transpose(0, 2, 1, 3).reshape(db, IDX_HEADS * tn, IDX_DIM)
    qi_hi = qi_hq.astype(BF16)
    qi_hq = jnp.concatenate([qi_hi, qi_hi, (qi_hq - qi_hi.astype(F32)).astype(BF16)], axis=-1)
    w_hq = wi.transpose(0, 2, 1).reshape(db, IDX_HEADS * tn, 1)
    q2 = (q.reshape(db, tn, KV_HEADS, ATT_HEADS // KV_HEADS, HEAD_DIM).transpose(0, 2, 3, 1, 4)
          .reshape(db, KV_HEADS, (ATT_HEADS // KV_HEADS) * tn, HEAD_DIM))
    pt_flat = page_table.reshape(-1).astype(I32)

    def page_spec(tail, p, npg):
        def imap(b, s, pt):
            return (layer, pt[b * n_pages + jnp.minimum(s * npg + p, n_pages - 1)]) + (0,) * len(tail)
        return pl.BlockSpec((None, None) + tail, imap)

    per_b = lambda *tail: pl.BlockSpec((1,) + tail, lambda b, s, pt: (b,) + (0,) * len(tail))
    keys, thr = pl.pallas_call(
        functools.partial(_sample_score_kernel, npg=npg_s, nsteps=nsteps_s, tnew=tn, topk=topk,
                          idx_bits=max(1, (lpad - 1).bit_length())),
        out_shape=[jax.ShapeDtypeStruct((db, tn, lpad), I32),
                   jax.ShapeDtypeStruct((db, tn, LANES), I32)],
        grid_spec=pltpu.PrefetchScalarGridSpec(
            num_scalar_prefetch=1,
            grid=(db, nsteps_s + 1),
            in_specs=[per_b(IDX_HEADS * tn, 3 * IDX_DIM), per_b(IDX_HEADS * tn, 1), per_b(IDX_DIM, page)]
                     + [page_spec((IDX_DIM, page), p, npg_s) for p in range(npg_s)],
            out_specs=[per_b(tn, lpad), per_b(tn, LANES)]),
        compiler_params=_cparams(("arbitrary", "arbitrary")),
        name="sample_score",
    )(pt_flat, qi_hq, w_hq, kin, *([cache_ki] * npg_s))

    kv_pages = [page_spec((KV_HEADS, HEAD_DIM, page), p, npg_a) for p in range(npg_a)]
    a = pl.pallas_call(
        functools.partial(_sample_attn_kernel, npg=npg_a, nsteps=nsteps_a, tnew=tn),
        out_shape=jax.ShapeDtypeStruct((db, tn, ATT_WIDTH), BF16),
        grid_spec=pltpu.PrefetchScalarGridSpec(
            num_scalar_prefetch=1,
            grid=(db, nsteps_a + 1),
            in_specs=[per_b(KV_HEADS, (ATT_HEADS // KV_HEADS) * tn, HEAD_DIM),
                      pl.BlockSpec((1, tn, npg_a * page), lambda b, s, pt: (b, 0, s)),
                      per_b(tn, LANES),
                      per_b(KV_HEADS, HEAD_DIM, page), per_b(KV_HEADS, HEAD_DIM, page),
                      pl.BlockSpec((1, ATT_WIDTH), lambda b, s, pt: (0, 0))]
                     + kv_pages + kv_pages,
            out_specs=per_b(tn, ATT_WIDTH),
            scratch_shapes=[pltpu.VMEM((KV_HEADS, 2 * tn, 1), F32), pltpu.VMEM((KV_HEADS, 2 * tn, 1), F32),
                            pltpu.VMEM((KV_HEADS, 2 * tn, HEAD_DIM), F32)]),
        compiler_params=_cparams(("arbitrary", "arbitrary")),
        name="sample_attn",
    )(pt_flat, q2, keys, thr, heads(k_new), heads(v_new), ga.reshape(1, ATT_WIDTH),
      *([cache_k] * npg_a), *([cache_v] * npg_a))
    return a


def _outproj_kernel(a_ref, r_ref, x_ref, gt_ref, sc_ref, sh_ref, g_ref, wa_ref, wr_ref, wrt_ref, brt_ref,
                    x2_ref, h2_ref, route_ref):
    mix = _dot(a_ref[...], wa_ref[...]) + _dot(r_ref[...], wr_ref[...])
    x2 = x_ref[...] + gt_ref[0] * mix
    x2_ref[...] = x2
    h = x2 * lax.rsqrt(jnp.mean(x2 * x2, axis=-1, keepdims=True) + EPS) * g_ref[...]
    h = h * (1.0 + sc_ref[0]) + sh_ref[0]
    h2_ref[...] = h
    lg = (_dot3(h, wrt_ref[...]) + brt_ref[...]).T
    grp = [lg[g:g + 1] for g in range(N_GROUPS)]
    gmax = functools.reduce(jnp.maximum, grp)
    gden = functools.reduce(lambda u, v: u + v, [jnp.exp(g - gmax) for g in grp])
    gsel = jnp.full(gmax.shape, N_GROUPS - 1, I32)
    for g in range(N_GROUPS - 2, -1, -1):
        gsel = jnp.where(grp[g] == gmax, g, gsel)
    gw = 1.0 / gden
    el = []
    for e in range(EXPERTS_PER_GROUP):
        v = lg[N_GROUPS + e:N_GROUPS + e + 1]
        for g in range(1, N_GROUPS):
            row = N_GROUPS + g * EXPERTS_PER_GROUP + e
            v = jnp.where(gsel == g, lg[row:row + 1], v)
        el.append(v)
    emax = functools.reduce(jnp.maximum, el)
    e0 = jnp.full(emax.shape, EXPERTS_PER_GROUP - 1, I32)
    for e in range(EXPERTS_PER_GROUP - 2, -1, -1):
        e0 = jnp.where(el[e] == emax, e, e0)
    rest = [jnp.where(e0 == e, -jnp.inf, el[e]) for e in range(EXPERTS_PER_GROUP)]
    rmax = functools.reduce(jnp.maximum, rest)
    e1 = jnp.full(emax.shape, EXPERTS_PER_GROUP - 1, I32)
    for e in range(EXPERTS_PER_GROUP - 2, -1, -1):
        e1 = jnp.where(jnp.logical_and(rest[e] == rmax, e0 != e), e, e1)
    p1 = jnp.exp(rmax - emax)
    w0 = gw * (1.0 / (1.0 + p1))
    w1 = gw * (p1 / (1.0 + p1))
    base = gsel * EXPERTS_PER_GROUP
    rid = lax.broadcasted_iota(I32, lg.shape, 0)
    rt = jnp.where(rid == 0, (base + e0).astype(F32),
                   jnp.where(rid == 1, (base + e1).astype(F32),
                             jnp.where(rid == 2, w0, jnp.where(rid == 3, w1, 0.0))))
    route_ref[...] = rt.T


def _outproj(a, r, x2d, gt, sc, sh, g2, wa, wr, wrt, brt, tiles_per_group):
    n, d = x2d.shape
    tm = min(PROJ_TILE, n)
    rr = gt.shape[1]
    row = lambda w: pl.BlockSpec((tm, w), lambda i: (i, 0))
    mod = pl.BlockSpec((1, rr, d), lambda i: (i // tiles_per_group, 0, 0))
    const = lambda s: pl.BlockSpec(s, lambda i: (0, 0))
    return pl.pallas_call(
        _outproj_kernel,
        out_shape=[jax.ShapeDtypeStruct((n, d), F32), jax.ShapeDtypeStruct((n, d), F32),
                   jax.ShapeDtypeStruct((n, LANES), F32)],
        grid=(n // tm,),
        in_specs=[row(ATT_WIDTH), row(HG_WIDTH), row(d), mod, mod, mod, const((1, d)),
                  const(wa.shape), const(wr.shape), const(wrt.shape), const(brt.shape)],
        out_specs=[row(d), row(d), row(LANES)],
        compiler_params=_cparams(("arbitrary",)),
        name="outproj",
    )(a, r, x2d, gt, sc, sh, g2.reshape(1, d), wa, wr, wrt, brt)


def _gather_pipeline(step, nsteps, idx_hbm, src_hbm, buf, idx_smem, isem, rsem):
    nrows = buf.shape[1]
    slot = step % 2

    def idx_copy(b, sl):
        return pltpu.make_async_copy(idx_hbm.at[b], idx_smem.at[sl], isem.at[sl])

    def start_rows(sl):
        for r in range(nrows):
            pltpu.make_async_copy(src_hbm.at[pl.ds(idx_smem[sl, 0, r], 1), :],
                                  buf.at[sl, pl.ds(r, 1), :], rsem.at[sl]).start()

    @pl.when(step == 0)
    def _():
        idx_copy(0, 0).start()
        idx_copy(0, 0).wait()
        start_rows(0)

        @pl.when(nsteps > 1)
        def _():
            idx_copy(1, 1).start()

    @pl.when(step + 1 < nsteps)
    def _():
        idx_copy(step + 1, 1 - slot).wait()
        start_rows(1 - slot)

    @pl.when(step + 2 < nsteps)
    def _():
        idx_copy(step + 2, slot).start()

    pltpu.make_async_copy(src_hbm.at[pl.ds(0, nrows), :], buf.at[slot], rsem.at[slot]).wait()
    return slot


def _ffn_kernel(be_ref, nb_ref, tok_ref, h_ref, w1_ref, w3_ref, w2_ref, y_ref, xbuf, idx_smem, isem, rsem):
    j = pl.program_id(0)

    @pl.when(j < nb_ref[0])
    def _():
        slot = _gather_pipeline(j, nb_ref[0], tok_ref, h_ref, xbuf, idx_smem, isem, rsem)
        xb = xbuf[slot].astype(BF16)
        u = _dot(xb, w1_ref[0])
        g = _dot(xb, w3_ref[0])
        y_ref[...] = _dot((_silu(u) * g).astype(BF16), w2_ref[0])

    @pl.when(j >= nb_ref[0])
    def _():
        y_ref[...] = jnp.zeros(y_ref.shape, F32)


def _ffn(blk_e, n_used, tok, h2, w1, w3, w2, blk):
    nb = tok.shape[0]
    d = h2.shape[1]
    wspec = lambda s: pl.BlockSpec((1,) + s, lambda j, be, nu: (be[j], 0, 0))
    return pl.pallas_call(
        _ffn_kernel,
        out_shape=jax.ShapeDtypeStruct((nb * blk, d), F32),
        grid_spec=pltpu.PrefetchScalarGridSpec(
            num_scalar_prefetch=2,
            grid=(nb,),
            in_specs=[pl.BlockSpec(memory_space=pl.ANY), pl.BlockSpec(memory_space=pl.ANY),
                      wspec(w1.shape[1:]), wspec(w3.shape[1:]), wspec(w2.shape[1:])],
            out_specs=pl.BlockSpec((blk, d), lambda j, be, nu: (j, 0)),
            scratch_shapes=[pltpu.VMEM((2, blk, d), F32), pltpu.SMEM((2, 1, blk), I32),
                            pltpu.SemaphoreType.DMA((2,)), pltpu.SemaphoreType.DMA((2,))]),
        compiler_params=_cparams(("arbitrary",)),
        name="moe_ffn",
    )(blk_e, n_used, tok, h2, w1, w3, w2)


def _combine_kernel(dd_ref, y_ref, x2_ref, route_ref, gt_ref, sc_ref, sh_ref, g_ref, o_ref,
                    ybuf, idx_smem, isem, rsem):
    tm = x2_ref.shape[0]
    slot = _gather_pipeline(pl.program_id(0), pl.num_programs(0), dd_ref, y_ref, ybuf, idx_smem, isem, rsem)
    rt = route_ref[...]
    moe = ybuf[slot, 0:tm] * rt[:, 2:3] + ybuf[slot, tm:2 * tm] * rt[:, 3:4]
    x = x2_ref[...] + gt_ref[0] * moe
    y = x * lax.rsqrt(jnp.mean(x * x, axis=-1, keepdims=True) + EPS) * g_ref[...]
    o_ref[...] = y * (1.0 + sc_ref[0]) + sh_ref[0]


def _combine(dd, yb, x2, route, gt, sc, sh, gf, tiles_per_group):
    n, d = x2.shape
    tm = dd.shape[2] // 2
    rr = gt.shape[1]
    row = pl.BlockSpec((tm, d), lambda i: (i, 0))
    mod = pl.BlockSpec((1, rr, d), lambda i: (i // tiles_per_group, 0, 0))
    hbm = pl.BlockSpec(memory_space=pl.ANY)
    return pl.pallas_call(
        _combine_kernel,
        out_shape=jax.ShapeDtypeStruct((n, d), F32),
        grid=(n // tm,),
        in_specs=[hbm, hbm, row, pl.BlockSpec((tm, LANES), lambda i: (i, 0)), mod, mod, mod,
                  pl.BlockSpec((1, d), lambda i: (0, 0))],
        out_specs=row,
        scratch_shapes=[pltpu.VMEM((2, 2 * tm, d), F32), pltpu.SMEM((2, 1, 2 * tm), I32),
                        pltpu.SemaphoreType.DMA((2,)), pltpu.SemaphoreType.DMA((2,))],
        compiler_params=_cparams(("arbitrary",)),
        name="moe_combine",
    )(dd, yb, x2, route, gt, sc, sh, gf.reshape(1, d))


def _dispatch(route, blk):
    n = route.shape[0]
    flat_e = route[:, 0:2].astype(I32).reshape(-1)
    a = flat_e.shape[0]
    onehot = (flat_e[:, None] == jnp.arange(N_EXPERTS, dtype=I32)[None, :]).astype(I32)
    csum = jnp.cumsum(onehot, axis=0)
    rank = jnp.sum((csum - onehot) * onehot, axis=1)
    counts = csum[-1]
    padded = (counts + blk - 1) // blk * blk
    pad_end = jnp.cumsum(padded)
    pad_start = pad_end - padded
    dest = pad_start[flat_e] + rank
    nb = -(-a // blk) + N_EXPERTS
    tok = jnp.zeros((nb * blk,), I32).at[dest].set(jnp.arange(a, dtype=I32) // 2)
    blk_e = jnp.minimum(jnp.searchsorted(pad_end, jnp.arange(nb, dtype=I32) * blk, side='right'),
                        N_EXPERTS - 1).astype(I32)
    n_used = (pad_end[-1] // blk).astype(I32).reshape(1)
    dest2 = dest.reshape(n, 2)
    return tok.reshape(nb, 1, blk), blk_e, n_used, dest2[:, 0], dest2[:, 1]


def _mods(m, n_chunks, per_token_rows):
    parts = jnp.split(m, n_chunks, axis=-1)
    if per_token_rows is None:
        return [p[:, None, :] for p in parts]
    g, d = parts[0].shape
    tm = min(PROJ_TILE, g * per_token_rows)
    return [jnp.repeat(p, per_token_rows, axis=0).reshape(-1, tm, d) for p in parts]


def _layer(x, mod6, modf, pos, s0, attend, lb, wts, final_g, per_token):
    b, t, d = x.shape
    n = b * t
    (norm1_g, norm2_g, w_r, gk, ga, hg_out_g, wa, wr, wrt, brt, w1, w3, w2) = wts
    tm = min(PROJ_TILE, n)
    tiles_per_group = 1 if per_token else t // tm
    sh1, sc1, gt1, sh2, sc2, gt2 = _mods(mod6, 6, t if per_token else None)
    shf, scf = _mods(modf, 2, t if per_token else None)
    tabs = _rope_tables(jnp.tile(pos, tm // t) if per_token else pos)
    x2d = x.reshape(n, d)
    (q, k, v, kb, vd, qi, ki, wi, hq, hf, hi, hg) = _proj(x2d, sc1, sh1, norm1_g, w_r, gk, tabs, tiles_per_group)
    sq = lambda arr: arr.reshape(b, t, arr.shape[-1])
    r, s_t = _hgrn(sq(hq), sq(hf), sq(hi), sq(hg), lb, hg_out_g, s0)
    a = attend(sq(q), sq(qi), sq(wi), sq(ki), sq(k), sq(v), sq(kb), sq(vd), ga)
    x2, h2, route = _outproj(a.reshape(n, -1), r.reshape(n, -1), x2d, gt1, sc2, sh2, norm2_g,
                             wa, wr, wrt, brt, tiles_per_group)
    blk = MOE_BLOCK if n >= 8 * MOE_BLOCK else 64
    tok, blk_e, n_used, d0, d1 = _dispatch(route, blk)
    yb = _ffn(blk_e, n_used, tok, h2, w1, w3, w2, blk)
    dd = jnp.concatenate([d0.reshape(n // tm, 1, tm), d1.reshape(n // tm, 1, tm)], axis=2)
    y = _combine(dd, yb, x2, route, gt2, scf, shf, final_g, tiles_per_group)
    return (y.reshape(b, t, d), k.reshape(b, t, KV_HEADS, HEAD_DIM), v.reshape(b, t, KV_HEADS, HEAD_DIM),
            sq(ki), s_t)


def kernel(x_prompt, x_sample, cache_k, cache_v, cache_kidx, state_hgrn, page_table, c_prompt, c_sample,
           ada_w, ada_b, norm1_g, norm2_g, w_in, idx_k_g, hg_lb_logits, attn_out_g, hg_out_g, w_out,
           w_group, b_group, w_expert_router, b_expert_router, w1, w3, w2, final_g, ada_final_w, ada_final_b):
    depth = ada_w.shape[0]
    assert depth == 1, "the final adaLN norm is fused into the (single) layer's combine kernel"
    bp, tp, d = x_prompt.shape
    bs, ts, _ = x_sample.shape
    past = page_table.shape[1] * cache_kidx.shape[2]
    lb_all = jnp.cumsum(jax.nn.softmax(hg_lb_logits.astype(F32), axis=0), axis=0)
    n_c = bp + bs
    c_all = jnp.concatenate([c_prompt, c_sample, jnp.zeros((-n_c % 16, d), F32)], axis=0)
    modf = _ada(c_all, ada_final_w, ada_final_b)
    l = 0
    mod6 = _ada(c_all, ada_w[l], ada_b[l])

    seg = [0]
    for s in (ATT_WIDTH, KV_WIDTH, KV_WIDTH, IDX_HEADS * IDX_DIM, IDX_DIM, IDX_HEADS,
              HG_WIDTH, HG_WIDTH, HG_WIDTH, HG_WIDTH):
        seg.append(seg[-1] + s)
    wl = w_in[l]
    col = lambda i: wl[:, seg[i]:seg[i + 1]]
    zpad = lambda w: jnp.zeros((d, w), wl.dtype)
    w_r = jnp.concatenate([col(0), col(1), col(2), col(3), col(6), col(7), col(8), col(9),
                           col(4), zpad(LANES - IDX_DIM), col(5), zpad(LANES - IDX_HEADS)], axis=1).astype(BF16)
    gk = jnp.concatenate([idx_k_g[l], jnp.zeros((LANES - IDX_DIM,), F32)]).reshape(1, LANES)
    wa = w_out[l, :ATT_WIDTH].astype(BF16)
    wr = w_out[l, ATT_WIDTH:].astype(BF16)
    n_rt = N_GROUPS + N_EXPERTS
    wrt = jnp.concatenate([w_group[l], w_expert_router[l], jnp.zeros((d, LANES - n_rt), F32)], axis=1)
    brt = jnp.concatenate([b_group[l], b_expert_router[l], jnp.zeros((LANES - n_rt,), F32)]).reshape(1, LANES)
    wts = (norm1_g[l], norm2_g[l], w_r, gk, attn_out_g[l], hg_out_g[l], wa, wr, wrt, brt,
           w1[l].astype(BF16), w3[l].astype(BF16), w2[l].astype(BF16))

    def attend_p(q, qi, wi, ki, k, v, kb, vb, ga):
        return _prompt_attention(q, qi, wi, ki, kb, vb, ga)

    def attend_s(q, qi, wi, ki, k, v, kb, vb, ga):
        return _sample_attention(q, qi, wi, ki, k, v, cache_k, cache_v, cache_kidx, l, page_table, ga)

    s0_p = jnp.zeros((bp, HG_HEADS, HG_DK, HG_DV), F32)
    yp, kp, vp, kip, sp = _layer(x_prompt, mod6[:bp], modf[:bp], jnp.arange(tp), s0_p, attend_p,
                                 lb_all[l], wts, final_g, per_token=False)
    ys, ks, vs, kis, ss = _layer(x_sample, mod6[bp:n_c], modf[bp:n_c], past + jnp.arange(ts), state_hgrn[l],
                                 attend_s, lb_all[l], wts, final_g, per_token=True)
    return (yp, ys, kp[None], vp[None], kip[None], sp[None], ks[None], vs[None], kis[None], ss[None])
```

```python
import functools
import math

import jax
import jax.numpy as jnp
from jax import lax
from jax.experimental import pallas as pl
from jax.experimental.pallas import tpu as pltpu

F32 = jnp.float32
BF16 = jnp.bfloat16
I32 = jnp.int32

ATT_HEADS = 8
KV_HEADS = 4
HEAD_DIM = 64
ATT_WIDTH = ATT_HEADS * HEAD_DIM
KV_WIDTH = KV_HEADS * HEAD_DIM
ROT_HALF = HEAD_DIM // 8
ROPE_THETA = 500000.0
IDX_HEADS = 8
IDX_DIM = 64
TOPK_MAX = 256
HG_HEADS = 4
HG_DK = 128
HG_DV = 128
HG_WIDTH = HG_HEADS * HG_DV
HG_CHUNK = 64
N_GROUPS = 4
EXPERTS_PER_GROUP = 4
N_EXPERTS = N_GROUPS * EXPERTS_PER_GROUP
D_EXPERT = 512
MOE_BLOCK = 256
EPS = 1e-6

LANES = 128
INT_MIN = -2 ** 31
NEG_BIG = -1e30
VMEM_LIMIT = 56 * 1024 * 1024
PROJ_TILE = 256
ATT_CHUNK = 1024
VT_ROWS = HEAD_DIM + 16
LOG2E = 1.4426950408889634
BISECT_BITS_PER_TRIP = 4
SAMPLE_SCORE_PAGES = 16
SAMPLE_ATTN_PAGES = 16

_Z_Q, _Z_K, _Z_V, _Z_QI, _Z_HQ, _Z_HF, _Z_HI, _Z_HG, _Z_KI, _Z_WI, _Z_END = (
    0, 512, 768, 1024, 1536, 2048, 2560, 3072, 3584, 3712, 3840)


def _cparams(sem):
    return pltpu.CompilerParams(dimension_semantics=sem, vmem_limit_bytes=VMEM_LIMIT)


def _split_bf16(x):
    hi = x.astype(BF16)
    lo = (x - hi.astype(F32)).astype(BF16)
    return hi, lo


def _dot(a, b):
    return jnp.dot(a, b, preferred_element_type=F32)


def _dot_nt(a, b):
    return lax.dot_general(a, b, (((1,), (1,)), ((), ())), preferred_element_type=F32)


def _dot_tn(a, b):
    return lax.dot_general(a, b, (((0,), (0,)), ((), ())), preferred_element_type=F32)


def _dot3(a, b):
    ah, al = _split_bf16(a)
    bh, bl = _split_bf16(b)
    return _dot(ah, bh) + (_dot(ah, bl) + _dot(al, bh))


def _dot3_nt(a, b):
    ah, al = _split_bf16(a)
    bh, bl = _split_bf16(b)
    return _dot_nt(ah, bh) + (_dot_nt(ah, bl) + _dot_nt(al, bh))


def _silu(x):
    return x * (1.0 / (1.0 + jnp.exp(-x)))


def _sigmoid(x):
    return 1.0 / (1.0 + jnp.exp(-x))


def _ada_kernel(c_ref, w_ref, b_ref, o_ref):
    o_ref[...] = _dot3(_silu(c_ref[...]), w_ref[...]) + b_ref[...]


def _ada(c, w, b):
    r, d = c.shape
    e = w.shape[1]
    te = 1024
    return pl.pallas_call(
        _ada_kernel,
        out_shape=jax.ShapeDtypeStruct((r, e), F32),
        grid=(e // te,),
        in_specs=[pl.BlockSpec((r, d), lambda j: (0, 0)),
                  pl.BlockSpec((d, te), lambda j: (0, j)),
                  pl.BlockSpec((1, te), lambda j: (0, j))],
        out_specs=pl.BlockSpec((r, te), lambda j: (0, j)),
        compiler_params=_cparams(("arbitrary",)),
        name="ada",
    )(c, w, b.reshape(1, e))


def _rope_tables(pos):
    p = pos.shape[0]
    inv = ROPE_THETA ** (-jnp.arange(ROT_HALF, dtype=F32) * (2.0 / (2 * ROT_HALF)))
    ang = pos.astype(F32)[:, None] * inv[None, :]
    c, s = jnp.cos(ang), jnp.sin(ang)
    rest = HEAD_DIM - 2 * ROT_HALF
    one, zero, z8 = jnp.ones((p, rest), F32), jnp.zeros((p, rest), F32), jnp.zeros((p, ROT_HALF), F32)
    cos64 = jnp.concatenate([c, c, one], axis=1)
    sa64 = jnp.concatenate([-s, z8, zero], axis=1)
    sb64 = jnp.concatenate([z8, s, zero], axis=1)
    dup = lambda t: jnp.concatenate([t, t], axis=1)
    return dup(cos64), dup(sa64), dup(sb64)


def _proj_kernel(x_ref, sc_ref, sh_ref, g_ref, w_ref, gk_ref, cos_ref, sa_ref, sb_ref,
                 q_ref, k_ref, v_ref, kb_ref, vb_ref, qi_ref, ki_ref, wi_ref,
                 hq_ref, hf_ref, hi_ref, hg_ref):
    x = x_ref[...]
    ms = jnp.mean(x * x, axis=-1, keepdims=True)
    h = x * lax.rsqrt(ms + EPS) * g_ref[...]
    h = h * (1.0 + sc_ref[0]) + sh_ref[0]
    z = _dot(h.astype(BF16), w_ref[...])
    cos, sa, sb = cos_ref[...], sa_ref[...], sb_ref[...]

    def rope(zs):
        n = zs.shape[1] // LANES
        rep = (lambda t: jnp.concatenate([t] * n, axis=1)) if n > 1 else (lambda t: t)
        w = zs.shape[1]
        return (zs * rep(cos) + pltpu.roll(zs, w - ROT_HALF, 1) * rep(sa)
                + pltpu.roll(zs, ROT_HALF, 1) * rep(sb))

    q_ref[...] = (rope(z[:, _Z_Q:_Z_K]) * (HEAD_DIM ** -0.5 * LOG2E)).astype(BF16)
    k = rope(z[:, _Z_K:_Z_V])
    k_ref[...] = k
    kb_ref[...] = k.astype(BF16)
    v = z[:, _Z_V:_Z_QI]
    v_ref[...] = v
    vb_ref[...] = v.astype(BF16)
    qi_ref[...] = rope(z[:, _Z_QI:_Z_HQ]) * (IDX_DIM ** -0.5)
    hq_ref[...] = z[:, _Z_HQ:_Z_HF]
    hf_ref[...] = z[:, _Z_HF:_Z_HI]
    hi_ref[...] = z[:, _Z_HI:_Z_HG]
    hg_ref[...] = z[:, _Z_HG:_Z_KI]
    ks = z[:, _Z_KI:_Z_WI]
    kms = jnp.sum(ks * ks, axis=-1, keepdims=True) * (1.0 / IDX_DIM)
    kn = ks * lax.rsqrt(kms + EPS) * gk_ref[...]
    ki_ref[...] = rope(kn)[:, :IDX_DIM]
    wi_ref[...] = z[:, _Z_WI:_Z_WI + IDX_HEADS] * (IDX_HEADS ** -0.5)


def _proj(x2d, sc, sh, g1, w_r, gk, tabs, tiles_per_group):
    n, d = x2d.shape
    tm = min(PROJ_TILE, n)
    nt = n // tm
    r = sc.shape[1]
    pt = tabs[0].shape[0] // tm
    row = lambda w: pl.BlockSpec((tm, w), lambda i: (i, 0))
    mod = pl.BlockSpec((1, r, d), lambda i: (i // tiles_per_group, 0, 0))
    tab = pl.BlockSpec((tm, LANES), lambda i: (i % pt, 0))
    outs = [(ATT_WIDTH, BF16), (KV_WIDTH, F32), (KV_WIDTH, F32), (KV_WIDTH, BF16), (KV_WIDTH, BF16),
            (ATT_WIDTH, F32), (IDX_DIM, F32), (IDX_HEADS, F32),
            (HG_WIDTH, F32), (HG_WIDTH, F32), (HG_WIDTH, F32), (HG_WIDTH, F32)]
    return pl.pallas_call(
        _proj_kernel,
        out_shape=[jax.ShapeDtypeStruct((n, w), t) for w, t in outs],
        grid=(nt,),
        in_specs=[row(d), mod, mod,
                  pl.BlockSpec((1, d), lambda i: (0, 0)),
                  pl.BlockSpec(w_r.shape, lambda i: (0, 0)),
                  pl.BlockSpec((1, LANES), lambda i: (0, 0)),
                  tab, tab, tab],
        out_specs=[row(w) for w, _ in outs],
        compiler_params=_cparams(("arbitrary",)),
        name="proj",
    )(x2d, sc, sh, g1.reshape(1, d), w_r, gk, *tabs)


def _hgrn_kernel(hq_ref, hf_ref, hi_ref, hg_ref, lb_ref, g_ref, s0_ref, r_ref, st_ref, s_scr, *, chunk, nchunk):
    t = pl.program_id(1)

    @pl.when(t == 0)
    def _():
        for hh in range(HG_HEADS):
            s_scr[hh] = s0_ref[0, hh].T

    row = lax.broadcasted_iota(I32, (chunk, chunk), 0)
    col = lax.broadcasted_iota(I32, (chunk, chunk), 1)
    causal = col <= row
    tri = causal.astype(BF16)
    pairs = [(c, hh) for c in range(nchunk) for hh in range(HG_HEADS)]
    rows = lambda c: slice(c * chunk, (c + 1) * chunk)
    lanes = lambda hh: slice(hh * HG_DK, (hh + 1) * HG_DK)
    kks, bs = {}, {}
    for c, hh in pairs:
        lb = lb_ref[:, lanes(hh)]
        f = lb + (1.0 - lb) * _sigmoid(hf_ref[0, rows(c), lanes(hh)])
        logf = jnp.log(f)
        kks[c, hh] = 1.0 - f
        l0 = logf.astype(BF16)
        r1 = logf - l0.astype(F32)
        l1 = r1.astype(BF16)
        l2 = (r1 - l1.astype(F32)).astype(BF16)
        bs[c, hh] = _dot(tri, l0) + (_dot(tri, l1) + _dot(tri, l2))
    q_ins, intras, upds, decays = {}, {}, {}, {}
    for c, hh in pairs:
        b, kk = bs[c, hh], kks[c, hh]
        bl = b[chunk - 1:chunk, :]
        q_in = (hq_ref[0, rows(c), lanes(hh)] * (HG_DK ** -0.5) * jnp.exp(b)).astype(BF16)
        k_in = (kk * jnp.exp(-b)).astype(BF16)
        k_st = (kk * jnp.exp(bl - b)).astype(BF16)
        vb = hi_ref[0, rows(c), lanes(hh)].astype(BF16)
        att = jnp.where(causal, _dot_nt(q_in, k_in), 0.0)
        q_ins[c, hh] = q_in
        intras[c, hh] = _dot(att.astype(BF16), vb)
        upds[c, hh] = _dot_tn(vb, k_st)
        decays[c, hh] = jnp.exp(bl)
    for c, hh in pairs:
        st = s_scr[hh]
        o = _dot_nt(q_ins[c, hh], st.astype(BF16)) + intras[c, hh]
        s_scr[hh] = st * decays[c, hh] + upds[c, hh]
        on = o * lax.rsqrt(jnp.mean(o * o, axis=-1, keepdims=True) + EPS) * g_ref[:, lanes(hh)]
        r_ref[0, rows(c), lanes(hh)] = (on * _silu(hg_ref[0, rows(c), lanes(hh)])).astype(BF16)

    @pl.when(t == pl.num_programs(1) - 1)
    def _():
        for hh in range(HG_HEADS):
            st_ref[0, hh] = s_scr[hh].T


def _hgrn(hq, hf, hi, hg, lb, out_g, s0):
    b, t, w = hq.shape
    chunk = min(HG_CHUNK, t)
    assert t % chunk == 0
    nchunk = min(4, t // chunk)
    tt = chunk * nchunk
    seq = pl.BlockSpec((1, tt, w), lambda i, j: (i, j, 0))
    vec = pl.BlockSpec((1, w), lambda i, j: (0, 0))
    state = pl.BlockSpec((1, HG_HEADS, HG_DK, HG_DV), lambda i, j: (i, 0, 0, 0))
    return pl.pallas_call(
        functools.partial(_hgrn_kernel, chunk=chunk, nchunk=nchunk),
        out_shape=[jax.ShapeDtypeStruct((b, t, w), BF16), jax.ShapeDtypeStruct(s0.shape, F32)],
        grid=(b, t // tt),
        in_specs=[seq, seq, seq, seq, vec, vec, state],
        out_specs=[seq, state],
        scratch_shapes=[pltpu.VMEM((HG_HEADS, HG_DV, HG_DK), F32)],
        compiler_params=_cparams(("arbitrary", "arbitrary")),
        name="hgrn",
    )(hq, hf, hi, hg, lb.reshape(1, w), out_g.reshape(1, w), s0)


def _order_key(score):
    bits = pltpu.bitcast(score, I32)
    bits = jnp.where(bits == INT_MIN, 0, bits)
    return jnp.where(bits >= 0, bits, bits ^ 0x7FFFFFFF)


def _select_threshold(load_keys, store_keys, nchunk, tq, ck, topk, idx_bits, keys_on_rows=False):
    kax = 0 if keys_on_rows else 1
    kshape = (ck, tq) if keys_on_rows else (tq, ck)
    vshape = (1, tq) if keys_on_rows else (tq, 1)

    def count(pred):
        def body(c, acc):
            m = jnp.where(pred(load_keys(c), c), 1.0, 0.0)
            if keys_on_rows:
                parts = [m[j * 8:(j + 1) * 8] for j in range(ck // 8)]
                while len(parts) > 1:
                    parts = [a + b for a, b in zip(parts[0::2], parts[1::2])]
                part = parts[0]
            else:
                part = m[:, 0:LANES]
                for j in range(1, ck // LANES):
                    part = part + m[:, j * LANES:(j + 1) * LANES]
            return acc + part
        acc = lax.fori_loop(0, nchunk, body, jnp.zeros((8, tq) if keys_on_rows else (tq, LANES), F32))
        return jnp.sum(acc, axis=kax, keepdims=True)

    kf = float(topk)
    zero_i = jnp.zeros(vshape, I32)

    def bisect(nbits, count_ge, done0):
        def cond(st):
            return jnp.logical_and(st[0] < nbits, jnp.min(st[2]) < 0.5)

        def body(st):
            it, tu, done, hit_u = st
            for _ in range(BISECT_BITS_PER_TRIP):
                cand = tu | lax.shift_left(jnp.int32(1), nbits - 1 - it)
                cnt = count_ge(cand)
                active = done < 0.5
                tu = jnp.where(jnp.logical_and(active, cnt >= kf), cand, tu)
                hit = jnp.logical_and(active, cnt == kf)
                hit_u = jnp.where(hit, cand, hit_u)
                done = jnp.where(hit, 1.0, done)
                it = it + 1
            return it, tu, done, hit_u
        return lax.while_loop(cond, body, (jnp.int32(0), zero_i, done0, zero_i))[1:]

    tu, done, hit_u = bisect(32, lambda u: count(lambda key, c: key >= (u ^ INT_MIN)), jnp.zeros(vshape, F32))
    ts = tu ^ INT_MIN
    tsel = (hit_u ^ INT_MIN) - 1
    need_tie = jnp.logical_and(done < 0.5, tu != 0)
    thr = jnp.where(done > 0.5, tsel, ts)

    @pl.when(jnp.max(jnp.where(need_tie, 1.0, 0.0)) > 0.5)
    def _():
        rank = kf - count(lambda key, c: key > ts)
        pos_of = lambda c: c * ck + lax.broadcasted_iota(I32, kshape, kax)

        def jbody(it, ju):
            cand = ju | lax.shift_left(jnp.int32(1), idx_bits - 1 - it)
            pred = lambda key, c: jnp.logical_and(key == ts, pos_of(c) < cand)
            return jnp.where(count(pred) < rank, cand, ju)
        ju = lax.fori_loop(0, idx_bits, jbody, zero_i)

        def fix(c, carry):
            key = load_keys(c)
            bump = jnp.logical_and(jnp.logical_and(key == ts, pos_of(c) <= ju), need_tie)
            store_keys(c, jnp.where(bump, key + 1, key))
            return carry
        lax.fori_loop(0, nchunk, fix, 0)
    return thr


def _prompt_attn_kernel(q_ref, qi_ref, wt_ref, k4_ref, kb_ref, vt_ref, ga_ref, o_ref,
                        key_scr, lg_scr, m_scr, acc_scr, *, topk, idx_bits):
    tq, ck = q_ref.shape[1], ATT_CHUNK
    i = pl.program_id(1)
    nchunk = (i * tq + tq + ck - 1) // ck
    qpos = i * tq + lax.broadcasted_iota(I32, (1, tq), 1)

    qit = qi_ref[0].T
    hit = qit.astype(BF16)
    lot = (qit - hit.astype(F32)).astype(BF16)
    zero = jnp.zeros((IDX_DIM, tq), BF16)

    def idx_rhs(hh):
        hs = slice(hh * IDX_DIM, (hh + 1) * IDX_DIM)
        return jnp.concatenate([hit[hs], hit[hs], lot[hs], zero], axis=0)
    rhs_s = [jnp.concatenate([idx_rhs(2 * p), idx_rhs(2 * p + 1)], axis=1) for p in range(IDX_HEADS // 2)]
    wt = wt_ref[0]

    def score_body(c, carry):
        off = pl.multiple_of(c * ck, ck)
        k4 = k4_ref[0, pl.ds(off, ck), :]
        acc = jnp.zeros((ck, tq), F32)
        for p in range(IDX_HEADS // 2):
            s = _dot(k4, rhs_s[p])
            acc = acc + jnp.maximum(s[:, :tq], 0.0) * wt[2 * p:2 * p + 1]
            acc = acc + jnp.maximum(s[:, tq:], 0.0) * wt[2 * p + 1:2 * p + 2]
        kpos = off + lax.broadcasted_iota(I32, (ck, 1), 0)
        key_scr[pl.ds(off, ck), :] = jnp.where(kpos <= qpos, _order_key(acc), INT_MIN)
        return carry
    lax.fori_loop(0, nchunk, score_body, 0)

    chunk = lambda c: pl.ds(pl.multiple_of(c * ck, ck), ck)

    def store_keys(c, val):
        key_scr[chunk(c), :] = val
    thr = _select_threshold(lambda c: key_scr[chunk(c), :], store_keys, nchunk, tq, ck, topk, idx_bits,
                            keys_on_rows=True)

    qt = q_ref[0].astype(F32).T.astype(BF16)
    rhs_a = []
    for m in range(KV_HEADS):
        blk = jnp.concatenate([qt[(2 * m) * HEAD_DIM:(2 * m + 1) * HEAD_DIM],
                               qt[(2 * m + 1) * HEAD_DIM:(2 * m + 2) * HEAD_DIM]], axis=1)
        parts = [jnp.zeros((HEAD_DIM, 2 * tq), BF16)] * m + [blk] + [jnp.zeros((HEAD_DIM, 2 * tq), BF16)] * (KV_HEADS - 1 - m)
        rhs_a.append(jnp.concatenate(parts, axis=0))
    m_scr[...] = jnp.full(m_scr.shape, NEG_BIG, F32)
    acc_scr[...] = jnp.zeros(acc_scr.shape, F32)

    def logits(c, slot):
        kc = kb_ref[0, chunk(c), :]
        for m in range(KV_HEADS):
            lg_scr[slot, m] = _dot(kc, rhs_a[m])

    def softmax_pv(c, slot):
        sel = key_scr[chunk(c), :] > thr
        pas = []
        for m in range(KV_HEADS):
            lg = lg_scr[slot, m]
            lg = jnp.concatenate([jnp.where(sel, lg[:, :tq], NEG_BIG), jnp.where(sel, lg[:, tq:], NEG_BIG)], axis=1)
            mx = m_scr[m]
            mn = jnp.maximum(mx, jnp.max(lg, axis=0, keepdims=True))
            m_scr[m] = mn
            pas.append((jnp.exp2(lg - mn).astype(BF16), jnp.exp2(mx - mn)))
        for m in range(KV_HEADS):
            p, alpha = pas[m]
            acc_scr[m] = acc_scr[m] * alpha + _dot(vt_ref[0, m, :, chunk(c)], p)

    logits(0, 0)

    def pair_body(j, carry):
        logits(2 * j + 1, 1)
        softmax_pv(2 * j, 0)
        logits(jnp.minimum(2 * j + 2, nchunk - 1), 0)
        softmax_pv(2 * j + 1, 1)
        return carry
    lax.fori_loop(0, nchunk // 2, pair_body, 0)

    @pl.when(nchunk % 2 == 1)
    def _():
        softmax_pv(nchunk - 1, 0)
    pieces = []
    for m in range(KV_HEADS):
        acc = acc_scr[m]
        o = acc[0:HEAD_DIM] / acc[HEAD_DIM:HEAD_DIM + 1]
        pieces += [o[:, :tq], o[:, tq:]]
    at = jnp.concatenate(pieces, axis=0)
    at = at * lax.rsqrt(jnp.mean(at * at, axis=0, keepdims=True) + EPS)
    o_ref[0] = (at.T * ga_ref[...]).astype(BF16)


def _prompt_attention(q, qi, wi, ki, kb, vb, ga):
    b, t, _ = q.shape
    tq = LANES
    assert t % ATT_CHUNK == 0
    topk = min(TOPK_MAX, t // 4)
    ki_hi = ki.astype(BF16)
    ki_lo = (ki - ki_hi.astype(F32)).astype(BF16)
    k4 = jnp.concatenate([ki_hi, ki_lo, ki_hi, ki_hi], axis=-1)
    vt = vb.reshape(b, t, KV_HEADS, HEAD_DIM).transpose(0, 2, 3, 1)
    vt = jnp.concatenate([vt, jnp.ones((b, KV_HEADS, VT_ROWS - HEAD_DIM, t), BF16)], axis=2)
    blk = lambda w: pl.BlockSpec((1, tq, w), lambda bi, i: (bi, i, 0))
    full = lambda r, c: pl.BlockSpec((1, r, c), lambda bi, i: (bi, 0, 0))
    return pl.pallas_call(
        functools.partial(_prompt_attn_kernel, topk=topk, idx_bits=max(1, (t - 1).bit_length())),
        out_shape=jax.ShapeDtypeStruct((b, t, ATT_WIDTH), BF16),
        grid=(b, t // tq),
        in_specs=[blk(ATT_WIDTH), blk(ATT_WIDTH),
                  pl.BlockSpec((1, IDX_HEADS, tq), lambda bi, i: (bi, 0, i)),
                  full(t, 4 * IDX_DIM), full(t, KV_WIDTH),
                  pl.BlockSpec((1, KV_HEADS, VT_ROWS, t), lambda bi, i: (bi, 0, 0, 0)),
                  pl.BlockSpec((1, ATT_WIDTH), lambda bi, i: (0, 0))],
        out_specs=blk(ATT_WIDTH),
        scratch_shapes=[pltpu.VMEM((t, tq), I32), pltpu.VMEM((2, KV_HEADS, ATT_CHUNK, 2 * tq), F32),
                        pltpu.VMEM((KV_HEADS, 1, 2 * tq), F32),
                        pltpu.VMEM((KV_HEADS, VT_ROWS, 2 * tq), F32)],
        compiler_params=_cparams(("arbitrary", "arbitrary")),
        name="prompt_attn",
    )(q, qi, wi.transpose(0, 2, 1), k4, kb, vt, ga.reshape(1, ATT_WIDTH))


def _sample_score_kernel(pt_ref, qi_ref, w_ref, kin_ref, *rest, npg, nsteps, tnew, topk, idx_bits):
    page_refs = rest[:npg]
    key_ref, thr_ref = rest[npg:npg + 2]
    s = pl.program_id(1)
    last = s == nsteps
    tq = tnew
    page = page_refs[0].shape[1]
    ck = npg * page
    pages = [page_refs[p][...] for p in range(npg)]
    pages[0] = jnp.where(last, kin_ref[0], pages[0])
    k_hi, k_lo = _split_bf16(jnp.concatenate(pages, axis=1))
    sc = _dot(qi_ref[0], jnp.concatenate([k_hi, k_lo, k_hi], axis=0))
    sc = jnp.maximum(sc, 0.0) * w_ref[0]
    acc = sc[0:tq]
    for hh in range(1, IDX_HEADS):
        acc = acc + sc[hh * tq:(hh + 1) * tq]
    qrow = lax.broadcasted_iota(I32, (tq, ck), 0)
    col = lax.broadcasted_iota(I32, (tq, ck), 1)
    ok = jnp.logical_or(jnp.logical_not(last), jnp.logical_and(col <= qrow, col < tnew))
    key_ref[0, :, pl.ds(pl.multiple_of(s * ck, ck), ck)] = jnp.where(ok, _order_key(acc), INT_MIN)

    @pl.when(last)
    def _():
        def load_keys(c):
            return key_ref[0, :, pl.ds(pl.multiple_of(c * ck, ck), ck)]

        def store_keys(c, val):
            key_ref[0, :, pl.ds(pl.multiple_of(c * ck, ck), ck)] = val
        thr = _select_threshold(load_keys, store_keys, nsteps + 1, tq, ck, topk, idx_bits)
        thr_ref[0] = jnp.broadcast_to(thr, (tq, LANES))


def _sample_attn_kernel(pt_ref, q_ref, key_ref, thr_ref, kn_ref, vn_ref, ga_ref, *rest, npg, nsteps, tnew):
    k_refs = rest[:npg]
    v_refs = rest[npg:2 * npg]
    o_ref = rest[2 * npg]
    m_scr, l_scr, acc_scr = rest[2 * npg + 1:]
    s = pl.program_id(1)
    last = s == nsteps
    tq = tnew

    @pl.when(s == 0)
    def _():
        m_scr[...] = jnp.full(m_scr.shape, NEG_BIG, F32)
        l_scr[...] = jnp.zeros(l_scr.shape, F32)
        acc_scr[...] = jnp.zeros(acc_scr.shape, F32)

    sel = key_ref[0] > thr_ref[0][:, 0:1]
    sel2 = jnp.concatenate([sel, sel], axis=0)
    for m in range(KV_HEADS):
        kps = [k_refs[p][m] for p in range(npg)]
        vps = [v_refs[p][m] for p in range(npg)]
        kps[0] = jnp.where(last, kn_ref[0, m], kps[0])
        vps[0] = jnp.where(last, vn_ref[0, m], vps[0])
        kmt = jnp.concatenate(kps, axis=1).astype(BF16)
        vmt = jnp.concatenate(vps, axis=1).astype(BF16)
        lg = jnp.where(sel2, _dot(q_ref[0, m], kmt), NEG_BIG)
        mx = m_scr[m]
        mn = jnp.maximum(mx, jnp.max(lg, axis=-1, keepdims=True))
        pr = jnp.exp2(lg - mn)
        alpha = jnp.exp2(mx - mn)
        l_scr[m] = l_scr[m] * alpha + jnp.sum(pr, axis=-1, keepdims=True)
        acc_scr[m] = acc_scr[m] * alpha + _dot_nt(pr.astype(BF16), vmt)
        m_scr[m] = mn

    @pl.when(last)
    def _():
        pieces = []
        for m in range(KV_HEADS):
            o = acc_scr[m] / l_scr[m]
            pieces += [o[:tq], o[tq:]]
        a = jnp.concatenate(pieces, axis=1)
        a = a * lax.rsqrt(jnp.mean(a * a, axis=-1, keepdims=True) + EPS) * ga_ref[...]
        o_ref[0] = a.astype(BF16)


def _sample_attention(q, qi, wi, ki_new, k_new, v_new, cache_k, cache_v, cache_ki, layer, page_table, ga):
    db, tn, _ = q.shape
    page = cache_ki.shape[2]
    cache_k = cache_k.transpose(0, 1, 3, 4, 2)
    cache_v = cache_v.transpose(0, 1, 3, 4, 2)
    cache_ki = cache_ki.transpose(0, 1, 3, 2)
    n_pages = page_table.shape[1]
    npg_s = math.gcd(SAMPLE_SCORE_PAGES, n_pages)
    npg_a = math.gcd(SAMPLE_ATTN_PAGES, npg_s)
    nsteps_s, nsteps_a = n_pages // npg_s, n_pages // npg_a
    lpad = (nsteps_s + 1) * npg_s * page
    topk = min(TOPK_MAX, (n_pages * page + tn) // 4)
    assert tn <= page
    pad_keys = lambda a: jnp.pad(a, [(0, 0)] * (a.ndim - 1) + [(0, page - tn)])
    kin = pad_keys(ki_new.transpose(0, 2, 1))
    heads = lambda a: pad_keys(a.reshape(db, tn, KV_HEADS, HEAD_DIM).transpose(0, 2, 3, 1))
    qi_hq = qi.reshape(db, tn, IDX_HEADS, IDX_DIM).transpose(0, 2, 1, 3).reshape(db, IDX_HEADS * tn, IDX_DIM)
    qi_hi = qi_hq.astype(BF16)
    qi_hq = jnp.concatenate([qi_hi, qi_hi, (qi_hq - qi_hi.astype(F32)).astype(BF16)], axis=-1)
    w_hq = wi.transpose(0, 2, 1).reshape(db, IDX_HEADS * tn, 1)
    q2 = (q.reshape(db, tn, KV_HEADS, ATT_HEADS // KV_HEADS, HEAD_DIM).transpose(0, 2, 3, 1, 4)
          .reshape(db, KV_HEADS, (ATT_HEADS // KV_HEADS) * tn, HEAD_DIM))
    pt_flat = page_table.reshape(-1).astype(I32)

    def page_spec(tail, p, npg):
        def imap(b, s, pt):
            return (layer, pt[b * n_pages + jnp.minimum(s * npg + p, n_pages - 1)]) + (0,) * len(tail)
        return pl.BlockSpec((None, None) + tail, imap)

    per_b = lambda *tail: pl.BlockSpec((1,) + tail, lambda b, s, pt: (b,) + (0,) * len(tail))
    keys, thr = pl.pallas_call(
        functools.partial(_sample_score_kernel, npg=npg_s, nsteps=nsteps_s, tnew=tn, topk=topk,
                          idx_bits=max(1, (lpad - 1).bit_length())),
        out_shape=[jax.ShapeDtypeStruct((db, tn, lpad), I32),
                   jax.ShapeDtypeStruct((db, tn, LANES), I32)],
        grid_spec=pltpu.PrefetchScalarGridSpec(
            num_scalar_prefetch=1,
            grid=(db, nsteps_s + 1),
            in_specs=[per_b(IDX_HEADS * tn, 3 * IDX_DIM), per_b(IDX_HEADS * tn, 1), per_b(IDX_DIM, page)]
                     + [page_spec((IDX_DIM, page), p, npg_s) for p in range(npg_s)],
            out_specs=[per_b(tn, lpad), per_b(tn, LANES)]),
        compiler_params=_cparams(("arbitrary", "arbitrary")),
        name="sample_score",
    )(pt_flat, qi_hq, w_hq, kin, *([cache_ki] * npg_s))

    kv_pages = [page_spec((KV_HEADS, HEAD_DIM, page), p, npg_a) for p in range(npg_a)]
    a = pl.pallas_call(
        functools.partial(_sample_attn_kernel, npg=npg_a, nsteps=nsteps_a, tnew=tn),
        out_shape=jax.ShapeDtypeStruct((db, tn, ATT_WIDTH), BF16),
        grid_spec=pltpu.PrefetchScalarGridSpec(
            num_scalar_prefetch=1,
            grid=(db, nsteps_a + 1),
            in_specs=[per_b(KV_HEADS, (ATT_HEADS // KV_HEADS) * tn, HEAD_DIM),
                      pl.BlockSpec((1, tn, npg_a * page), lambda b, s, pt: (b, 0, s)),
                      per_b(tn, LANES),
                      per_b(KV_HEADS, HEAD_DIM, page), per_b(KV_HEADS, HEAD_DIM, page),
                      pl.BlockSpec((1, ATT_WIDTH), lambda b, s, pt: (0, 0))]
                     + kv_pages + kv_pages,
            out_specs=per_b(tn, ATT_WIDTH),
            scratch_shapes=[pltpu.VMEM((KV_HEADS, 2 * tn, 1), F32), pltpu.VMEM((KV_HEADS, 2 * tn, 1), F32),
                            pltpu.VMEM((KV_HEADS, 2 * tn, HEAD_DIM), F32)]),
        compiler_params=_cparams(("arbitrary", "arbitrary")),
        name="sample_attn",
    )(pt_flat, q2, keys, thr, heads(k_new), heads(v_new), ga.reshape(1, ATT_WIDTH),
      *([cache_k] * npg_a), *([cache_v] * npg_a))
    return a


def _outproj_kernel(a_ref, r_ref, x_ref, gt_ref, sc_ref, sh_ref, g_ref, wa_ref, wr_ref, wrt_ref, brt_ref,
                    x2_ref, h2_ref, route_ref):
    mix = _dot(a_ref[...], wa_ref[...]) + _dot(r_ref[...], wr_ref[...])
    x2 = x_ref[...] + gt_ref[0] * mix
    x2_ref[...] = x2
    h = x2 * lax.rsqrt(jnp.mean(x2 * x2, axis=-1, keepdims=True) + EPS) * g_ref[...]
    h = h * (1.0 + sc_ref[0]) + sh_ref[0]
    h2_ref[...] = h
    lg = (_dot3(h, wrt_ref[...]) + brt_ref[...]).T
    grp = [lg[g:g + 1] for g in range(N_GROUPS)]
    gmax = functools.reduce(jnp.maximum, grp)
    gden = functools.reduce(lambda u, v: u + v, [jnp.exp(g - gmax) for g in grp])
    gsel = jnp.full(gmax.shape, N_GROUPS - 1, I32)
    for g in range(N_GROUPS - 2, -1, -1):
        gsel = jnp.where(grp[g] == gmax, g, gsel)
    gw = 1.0 / gden
    el = []
    for e in range(EXPERTS_PER_GROUP):
        v = lg[N_GROUPS + e:N_GROUPS + e + 1]
        for g in range(1, N_GROUPS):
            row = N_GROUPS + g * EXPERTS_PER_GROUP + e
            v = jnp.where(gsel == g, lg[row:row + 1], v)
        el.append(v)
    emax = functools.reduce(jnp.maximum, el)
    e0 = jnp.full(emax.shape, EXPERTS_PER_GROUP - 1, I32)
    for e in range(EXPERTS_PER_GROUP - 2, -1, -1):
        e0 = jnp.where(el[e] == emax, e, e0)
    rest = [jnp.where(e0 == e, -jnp.inf, el[e]) for e in range(EXPERTS_PER_GROUP)]
    rmax = functools.reduce(jnp.maximum, rest)
    e1 = jnp.full(emax.shape, EXPERTS_PER_GROUP - 1, I32)
    for e in range(EXPERTS_PER_GROUP - 2, -1, -1):
        e1 = jnp.where(jnp.logical_and(rest[e] == rmax, e0 != e), e, e1)
    p1 = jnp.exp(rmax - emax)
    w0 = gw * (1.0 / (1.0 + p1))
    w1 = gw * (p1 / (1.0 + p1))
    base = gsel * EXPERTS_PER_GROUP
    rid = lax.broadcasted_iota(I32, lg.shape, 0)
    rt = jnp.where(rid == 0, (base + e0).astype(F32),
                   jnp.where(rid == 1, (base + e1).astype(F32),
                             jnp.where(rid == 2, w0, jnp.where(rid == 3, w1, 0.0))))
    route_ref[...] = rt.T


def _outproj(a, r, x2d, gt, sc, sh, g2, wa, wr, wrt, brt, tiles_per_group):
    n, d = x2d.shape
    tm = min(PROJ_TILE, n)
    rr = gt.shape[1]
    row = lambda w: pl.BlockSpec((tm, w), lambda i: (i, 0))
    mod = pl.BlockSpec((1, rr, d), lambda i: (i // tiles_per_group, 0, 0))
    const = lambda s: pl.BlockSpec(s, lambda i: (0, 0))
    return pl.pallas_call(
        _outproj_kernel,
        out_shape=[jax.ShapeDtypeStruct((n, d), F32), jax.ShapeDtypeStruct((n, d), F32),
                   jax.ShapeDtypeStruct((n, LANES), F32)],
        grid=(n // tm,),
        in_specs=[row(ATT_WIDTH), row(HG_WIDTH), row(d), mod, mod, mod, const((1, d)),
                  const(wa.shape), const(wr.shape), const(wrt.shape), const(brt.shape)],
        out_specs=[row(d), row(d), row(LANES)],
        compiler_params=_cparams(("arbitrary",)),
        name="outproj",
    )(a, r, x2d, gt, sc, sh, g2.reshape(1, d), wa, wr, wrt, brt)


def _gather_pipeline(step, nsteps, idx_hbm, src_hbm, buf, idx_smem, isem, rsem):
    nrows = buf.shape[1]
    slot = step % 2

    def idx_copy(b, sl):
        return pltpu.make_async_copy(idx_hbm.at[b], idx_smem.at[sl], isem.at[sl])

    def start_rows(sl):
        for r in range(nrows):
            pltpu.make_async_copy(src_hbm.at[pl.ds(idx_smem[sl, 0, r], 1), :],
                                  buf.at[sl, pl.ds(r, 1), :], rsem.at[sl]).start()

    @pl.when(step == 0)
    def _():
        idx_copy(0, 0).start()
        idx_copy(0, 0).wait()
        start_rows(0)

        @pl.when(nsteps > 1)
        def _():
            idx_copy(1, 1).start()

    @pl.when(step + 1 < nsteps)
    def _():
        idx_copy(step + 1, 1 - slot).wait()
        start_rows(1 - slot)

    @pl.when(step + 2 < nsteps)
    def _():
        idx_copy(step + 2, slot).start()

    pltpu.make_async_copy(src_hbm.at[pl.ds(0, nrows), :], buf.at[slot], rsem.at[slot]).wait()
    return slot


def _ffn_kernel(be_ref, nb_ref, tok_ref, h_ref, w1_ref, w3_ref, w2_ref, y_ref, xbuf, idx_smem, isem, rsem):
    j = pl.program_id(0)

    @pl.when(j < nb_ref[0])
    def _():
        slot = _gather_pipeline(j, nb_ref[0], tok_ref, h_ref, xbuf, idx_smem, isem, rsem)
        xb = xbuf[slot].astype(BF16)
        u = _dot(xb, w1_ref[0])
        g = _dot(xb, w3_ref[0])
        y_ref[...] = _dot((_silu(u) * g).astype(BF16), w2_ref[0])

    @pl.when(j >= nb_ref[0])
    def _():
        y_ref[...] = jnp.zeros(y_ref.shape, F32)


def _ffn(blk_e, n_used, tok, h2, w1, w3, w2, blk):
    nb = tok.shape[0]
    d = h2.shape[1]
    wspec = lambda s: pl.BlockSpec((1,) + s, lambda j, be, nu: (be[j], 0, 0))
    return pl.pallas_call(
        _ffn_kernel,
        out_shape=jax.ShapeDtypeStruct((nb * blk, d), F32),
        grid_spec=pltpu.PrefetchScalarGridSpec(
            num_scalar_prefetch=2,
            grid=(nb,),
            in_specs=[pl.BlockSpec(memory_space=pl.ANY), pl.BlockSpec(memory_space=pl.ANY),
                      wspec(w1.shape[1:]), wspec(w3.shape[1:]), wspec(w2.shape[1:])],
            out_specs=pl.BlockSpec((blk, d), lambda j, be, nu: (j, 0)),
            scratch_shapes=[pltpu.VMEM((2, blk, d), F32), pltpu.SMEM((2, 1, blk), I32),
                            pltpu.SemaphoreType.DMA((2,)), pltpu.SemaphoreType.DMA((2,))]),
        compiler_params=_cparams(("arbitrary",)),
        name="moe_ffn",
    )(blk_e, n_used, tok, h2, w1, w3, w2)


def _combine_kernel(dd_ref, y_ref, x2_ref, route_ref, gt_ref, sc_ref, sh_ref, g_ref, o_ref,
                    ybuf, idx_smem, isem, rsem):
    tm = x2_ref.shape[0]
    slot = _gather_pipeline(pl.program_id(0), pl.num_programs(0), dd_ref, y_ref, ybuf, idx_smem, isem, rsem)
    rt = route_ref[...]
    moe = ybuf[slot, 0:tm] * rt[:, 2:3] + ybuf[slot, tm:2 * tm] * rt[:, 3:4]
    x = x2_ref[...] + gt_ref[0] * moe
    y = x * lax.rsqrt(jnp.mean(x * x, axis=-1, keepdims=True) + EPS) * g_ref[...]
    o_ref[...] = y * (1.0 + sc_ref[0]) + sh_ref[0]


def _combine(dd, yb, x2, route, gt, sc, sh, gf, tiles_per_group):
    n, d = x2.shape
    tm = dd.shape[2] // 2
    rr = gt.shape[1]
    row = pl.BlockSpec((tm, d), lambda i: (i, 0))
    mod = pl.BlockSpec((1, rr, d), lambda i: (i // tiles_per_group, 0, 0))
    hbm = pl.BlockSpec(memory_space=pl.ANY)
    return pl.pallas_call(
        _combine_kernel,
        out_shape=jax.ShapeDtypeStruct((n, d), F32),
        grid=(n // tm,),
        in_specs=[hbm, hbm, row, pl.BlockSpec((tm, LANES), lambda i: (i, 0)), mod, mod, mod,
                  pl.BlockSpec((1, d), lambda i: (0, 0))],
        out_specs=row,
        scratch_shapes=[pltpu.VMEM((2, 2 * tm, d), F32), pltpu.SMEM((2, 1, 2 * tm), I32),
                        pltpu.SemaphoreType.DMA((2,)), pltpu.SemaphoreType.DMA((2,))],
        compiler_params=_cparams(("arbitrary",)),
        name="moe_combine",
    )(dd, yb, x2, route, gt, sc, sh, gf.reshape(1, d))


def _dispatch(route, blk):
    n = route.shape[0]
    flat_e = route[:, 0:2].astype(I32).reshape(-1)
    a = flat_e.shape[0]
    onehot = (flat_e[:, None] == jnp.arange(N_EXPERTS, dtype=I32)[None, :]).astype(I32)
    csum = jnp.cumsum(onehot, axis=0)
    rank = jnp.sum((csum - onehot) * onehot, axis=1)
    counts = csum[-1]
    padded = (counts + blk - 1) // blk * blk
    pad_end = jnp.cumsum(padded)
    pad_start = pad_end - padded
    dest = pad_start[flat_e] + rank
    nb = -(-a // blk) + N_EXPERTS
    tok = jnp.zeros((nb * blk,), I32).at[dest].set(jnp.arange(a, dtype=I32) // 2)
    blk_e = jnp.minimum(jnp.searchsorted(pad_end, jnp.arange(nb, dtype=I32) * blk, side='right'),
                        N_EXPERTS - 1).astype(I32)
    n_used = (pad_end[-1] // blk).astype(I32).reshape(1)
    dest2 = dest.reshape(n, 2)
    return tok.reshape(nb, 1, blk), blk_e, n_used, dest2[:, 0], dest2[:, 1]


def _mods(m, n_chunks, per_token_rows):
    parts = jnp.split(m, n_chunks, axis=-1)
    if per_token_rows is None:
        return [p[:, None, :] for p in parts]
    g, d = parts[0].shape
    tm = min(PROJ_TILE, g * per_token_rows)
    return [jnp.repeat(p, per_token_rows, axis=0).reshape(-1, tm, d) for p in parts]


def _layer(x, mod6, modf, pos, s0, attend, lb, wts, final_g, per_token):
    b, t, d = x.shape
    n = b * t
    (norm1_g, norm2_g, w_r, gk, ga, hg_out_g, wa, wr, wrt, brt, w1, w3, w2) = wts
    tm = min(PROJ_TILE, n)
    tiles_per_group = 1 if per_token else t // tm
    sh1, sc1, gt1, sh2, sc2, gt2 = _mods(mod6, 6, t if per_token else None)
    shf, scf = _mods(modf, 2, t if per_token else None)
    tabs = _rope_tables(jnp.tile(pos, tm // t) if per_token else pos)
    x2d = x.reshape(n, d)
    (q, k, v, kb, vd, qi, ki, wi, hq, hf, hi, hg) = _proj(x2d, sc1, sh1, norm1_g, w_r, gk, tabs, tiles_per_group)
    sq = lambda arr: arr.reshape(b, t, arr.shape[-1])
    r, s_t = _hgrn(sq(hq), sq(hf), sq(hi), sq(hg), lb, hg_out_g, s0)
    a = attend(sq(q), sq(qi), sq(wi), sq(ki), sq(k), sq(v), sq(kb), sq(vd), ga)
    x2, h2, route = _outproj(a.reshape(n, -1), r.reshape(n, -1), x2d, gt1, sc2, sh2, norm2_g,
                             wa, wr, wrt, brt, tiles_per_group)
    blk = MOE_BLOCK if n >= 8 * MOE_BLOCK else 64
    tok, blk_e, n_used, d0, d1 = _dispatch(route, blk)
    yb = _ffn(blk_e, n_used, tok, h2, w1, w3, w2, blk)
    dd = jnp.concatenate([d0.reshape(n // tm, 1, tm), d1.reshape(n // tm, 1, tm)], axis=2)
    y = _combine(dd, yb, x2, route, gt2, scf, shf, final_g, tiles_per_group)
    return (y.reshape(b, t, d), k.reshape(b, t, KV_HEADS, HEAD_DIM), v.reshape(b, t, KV_HEADS, HEAD_DIM),
            sq(ki), s_t)


def kernel(x_prompt, x_sample, cache_k, cache_v, cache_kidx, state_hgrn, page_table, c_prompt, c_sample,
           ada_w, ada_b, norm1_g, norm2_g, w_in, idx_k_g, hg_lb_logits, attn_out_g, hg_out_g, w_out,
           w_group, b_group, w_expert_router, b_expert_router, w1, w3, w2, final_g, ada_final_w, ada_final_b):
    depth = ada_w.shape[0]
    assert depth == 1, "the final adaLN norm is fused into the (single) layer's combine kernel"
    bp, tp, d = x_prompt.shape
    bs, ts, _ = x_sample.shape
    past = page_table.shape[1] * cache_kidx.shape[2]
    lb_all = jnp.cumsum(jax.nn.softmax(hg_lb_logits.astype(F32), axis=0), axis=0)
    n_c = bp + bs
    c_all = jnp.concatenate([c_prompt, c_sample, jnp.zeros((-n_c % 16, d), F32)], axis=0)
    modf = _ada(c_all, ada_final_w, ada_final_b)
    l = 0
    mod6 = _ada(c_all, ada_w[l], ada_b[l])

    seg = [0]
    for s in (ATT_WIDTH, KV_WIDTH, KV_WIDTH, IDX_HEADS * IDX_DIM, IDX_DIM, IDX_HEADS,
              HG_WIDTH, HG_WIDTH, HG_WIDTH, HG_WIDTH):
        seg.append(seg[-1] + s)
    wl = w_in[l]
    col = lambda i: wl[:, seg[i]:seg[i + 1]]
    zpad = lambda w: jnp.zeros((d, w), wl.dtype)
    w_r = jnp.concatenate([col(0), col(1), col(2), col(3), col(6), col(7), col(8), col(9),
                           col(4), zpad(LANES - IDX_DIM), col(5), zpad(LANES - IDX_HEADS)], axis=1).astype(BF16)
    gk = jnp.concatenate([idx_k_g[l], jnp.zeros((LANES - IDX_DIM,), F32)]).reshape(1, LANES)
    wa = w_out[l, :ATT_WIDTH].astype(BF16)
    wr = w_out[l, ATT_WIDTH:].astype(BF16)
    n_rt = N_GROUPS + N_EXPERTS
    wrt = jnp.concatenate([w_group[l], w_expert_router[l], jnp.zeros((d, LANES - n_rt), F32)], axis=1)
    brt = jnp.concatenate([b_group[l], b_expert_router[l], jnp.zeros((LANES - n_rt,), F32)]).reshape(1, LANES)
    wts = (norm1_g[l], norm2_g[l], w_r, gk, attn_out_g[l], hg_out_g[l], wa, wr, wrt, brt,
           w1[l].astype(BF16), w3[l].astype(BF16), w2[l].astype(BF16))

    def attend_p(q, qi, wi, ki, k, v, kb, vb, ga):
        return _prompt_attention(q, qi, wi, ki, kb, vb, ga)

    def attend_s(q, qi, wi, ki, k, v, kb, vb, ga):
        return _sample_attention(q, qi, wi, ki, k, v, cache_k, cache_v, cache_kidx, l, page_table, ga)

    s0_p = jnp.zeros((bp, HG_HEADS, HG_DK, HG_DV), F32)
    yp, kp, vp, kip, sp = _layer(x_prompt, mod6[:bp], modf[:bp], jnp.arange(tp), s0_p, attend_p,
                                 lb_all[l], wts, final_g, per_token=False)
    ys, ks, vs, kis, ss = _layer(x_sample, mod6[bp:n_c], modf[bp:n_c], past + jnp.arange(ts), state_hgrn[l],
                                 attend_s, lb_all[l], wts, final_g, per_token=True)
    return (yp, ys, kp[None], vp[None], kip[None], sp[None], ks[None], vs[None], kis[None], ss[None])
```

```python
import functools
import math

import jax
import jax.numpy as jnp
from jax import lax
from jax.experimental import pallas as pl
from jax.experimental.pallas import tpu as pltpu

F32 = jnp.float32
BF16 = jnp.bfloat16
I32 = jnp.int32

ATT_HEADS = 8
KV_HEADS = 4
HEAD_DIM = 64
ATT_WIDTH = ATT_HEADS * HEAD_DIM
KV_WIDTH = KV_HEADS * HEAD_DIM
ROT_HALF = HEAD_DIM // 8
ROPE_THETA = 500000.0
IDX_HEADS = 8
IDX_DIM = 64
TOPK_MAX = 256
HG_HEADS = 4
HG_DK = 128
HG_DV = 128
HG_WIDTH = HG_HEADS * HG_DV
HG_CHUNK = 64
N_GROUPS = 4
EXPERTS_PER_GROUP = 4
N_EXPERTS = N_GROUPS * EXPERTS_PER_GROUP
D_EXPERT = 512
MOE_BLOCK = 256
EPS = 1e-6

LANES = 128
INT_MIN = -2 ** 31
NEG_BIG = -1e30
VMEM_LIMIT = 56 * 1024 * 1024
PROJ_TILE = 256
ATT_CHUNK = 1024
VT_ROWS = HEAD_DIM + 16
LOG2E = 1.4426950408889634
BISECT_BITS_PER_TRIP = 4
SAMPLE_SCORE_PAGES = 16
SAMPLE_ATTN_PAGES = 16

_Z_Q, _Z_K, _Z_V, _Z_QI, _Z_HQ, _Z_HF, _Z_HI, _Z_HG, _Z_KI, _Z_WI, _Z_END = (
    0, 512, 768, 1024, 1536, 2048, 2560, 3072, 3584, 3712, 3840)


def _cparams(sem):
    return pltpu.CompilerParams(dimension_semantics=sem, vmem_limit_bytes=VMEM_LIMIT)


def _split_bf16(x):
    hi = x.astype(BF16)
    lo = (x - hi.astype(F32)).astype(BF16)
    return hi, lo


def _dot(a, b):
    return jnp.dot(a, b, preferred_element_type=F32)


def _dot_nt(a, b):
    return lax.dot_general(a, b, (((1,), (1,)), ((), ())), preferred_element_type=F32)


def _dot_tn(a, b):
    return lax.dot_general(a, b, (((0,), (0,)), ((), ())), preferred_element_type=F32)


def _dot3(a, b):
    ah, al = _split_bf16(a)
    bh, bl = _split_bf16(b)
    return _dot(ah, bh) + (_dot(ah, bl) + _dot(al, bh))


def _dot3_nt(a, b):
    ah, al = _split_bf16(a)
    bh, bl = _split_bf16(b)
    return _dot_nt(ah, bh) + (_dot_nt(ah, bl) + _dot_nt(al, bh))


def _silu(x):
    return x * (1.0 / (1.0 + jnp.exp(-x)))


def _sigmoid(x):
    return 1.0 / (1.0 + jnp.exp(-x))


def _ada_kernel(c_ref, w_ref, b_ref, o_ref):
    o_ref[...] = _dot3(_silu(c_ref[...]), w_ref[...]) + b_ref[...]


def _ada(c, w, b):
    r, d = c.shape
    e = w.shape[1]
    te = 1024
    return pl.pallas_call(
        _ada_kernel,
        out_shape=jax.ShapeDtypeStruct((r, e), F32),
        grid=(e // te,),
        in_specs=[pl.BlockSpec((r, d), lambda j: (0, 0)),
                  pl.BlockSpec((d, te), lambda j: (0, j)),
                  pl.BlockSpec((1, te), lambda j: (0, j))],
        out_specs=pl.BlockSpec((r, te), lambda j: (0, j)),
        compiler_params=_cparams(("arbitrary",)),
        name="ada",
    )(c, w, b.reshape(1, e))


def _rope_tables(pos):
    p = pos.shape[0]
    inv = ROPE_THETA ** (-jnp.arange(ROT_HALF, dtype=F32) * (2.0 / (2 * ROT_HALF)))
    ang = pos.astype(F32)[:, None] * inv[None, :]
    c, s = jnp.cos(ang), jnp.sin(ang)
    rest = HEAD_DIM - 2 * ROT_HALF
    one, zero, z8 = jnp.ones((p, rest), F32), jnp.zeros((p, rest), F32), jnp.zeros((p, ROT_HALF), F32)
    cos64 = jnp.concatenate([c, c, one], axis=1)
    sa64 = jnp.concatenate([-s, z8, zero], axis=1)
    sb64 = jnp.concatenate([z8, s, zero], axis=1)
    dup = lambda t: jnp.concatenate([t, t], axis=1)
    return dup(cos64), dup(sa64), dup(sb64)


def _proj_kernel(x_ref, sc_ref, sh_ref, g_ref, w_ref, gk_ref, cos_ref, sa_ref, sb_ref, *out_refs, seq_major):
    x = x_ref[...]
    ms = jnp.mean(x * x, axis=-1, keepdims=True)
    h = x * lax.rsqrt(ms + EPS) * g_ref[...]
    h = h * (1.0 + sc_ref[0]) + sh_ref[0]
    z = _dot(h.astype(BF16), w_ref[...])
    cos, sa, sb = cos_ref[...], sa_ref[...], sb_ref[...]

    def rope(zs):
        n = zs.shape[1] // LANES
        rep = (lambda t: jnp.concatenate([t] * n, axis=1)) if n > 1 else (lambda t: t)
        w = zs.shape[1]
        return (zs * rep(cos) + pltpu.roll(zs, w - ROT_HALF, 1) * rep(sa)
                + pltpu.roll(zs, ROT_HALF, 1) * rep(sb))

    q = (rope(z[:, _Z_Q:_Z_K]) * (HEAD_DIM ** -0.5 * LOG2E)).astype(BF16)
    k = rope(z[:, _Z_K:_Z_V])
    v = z[:, _Z_V:_Z_QI]
    qi = rope(z[:, _Z_QI:_Z_HQ]) * (IDX_DIM ** -0.5)
    ks = z[:, _Z_KI:_Z_WI]
    kms = jnp.sum(ks * ks, axis=-1, keepdims=True) * (1.0 / IDX_DIM)
    kn = rope(ks * lax.rsqrt(kms + EPS) * gk_ref[...])
    ws = z[:, _Z_WI:_Z_END] * (IDX_HEADS ** -0.5)
    if seq_major:
        q_ref, kt_ref, vt_ref, kb_ref, vto_ref, qi_ref, kit_ref, k4_ref, wt_ref = out_refs[:9]
        kt_ref[0] = k.T
        vt = v.T
        vt_ref[0] = vt
        kb_ref[...] = k.astype(BF16)
        for m in range(KV_HEADS):
            vto_ref[0, m, 0:HEAD_DIM, :] = vt[m * HEAD_DIM:(m + 1) * HEAD_DIM].astype(BF16)
            vto_ref[0, m, HEAD_DIM:VT_ROWS, :] = jnp.ones((VT_ROWS - HEAD_DIM, x.shape[0]), BF16)
        kit_ref[0] = kn.T[:IDX_DIM]
        hi = kn.astype(BF16).astype(F32)
        lo = kn - hi
        k4_ref[...] = jnp.concatenate([hi + pltpu.roll(lo, IDX_DIM, 1), hi + pltpu.roll(hi, IDX_DIM, 1)],
                                      axis=1).astype(BF16)
        wt_ref[0] = ws.T[:IDX_HEADS]
    else:
        q_ref, k_ref, v_ref, qi_ref, ki_ref, wi_ref = out_refs[:6]
        k_ref[...] = k
        v_ref[...] = v
        ki_ref[...] = kn[:, :IDX_DIM]
        wi_ref[...] = ws[:, :IDX_HEADS]
    q_ref[...] = q
    qi_ref[...] = qi
    hq_ref, hf_ref, hi_ref, hg_ref = out_refs[-4:]
    hq_ref[...] = z[:, _Z_HQ:_Z_HF]
    hf_ref[...] = z[:, _Z_HF:_Z_HI]
    hi_ref[...] = z[:, _Z_HI:_Z_HG]
    hg_ref[...] = z[:, _Z_HG:_Z_KI]


def _proj(x2d, sc, sh, g1, w_r, gk, tabs, tiles_per_group, seq_batch):
    n, d = x2d.shape
    tm = min(PROJ_TILE, n)
    nt = n // tm
    r = sc.shape[1]
    pt = tabs[0].shape[0] // tm
    row = lambda w, t: (jax.ShapeDtypeStruct((n, w), t), pl.BlockSpec((tm, w), lambda i: (i, 0)))
    mod = pl.BlockSpec((1, r, d), lambda i: (i // tiles_per_group, 0, 0))
    tab = pl.BlockSpec((tm, LANES), lambda i: (i % pt, 0))
    if seq_batch:
        t = n // seq_batch
        tps = t // tm
        seq = lambda w: (jax.ShapeDtypeStruct((seq_batch, w, t), F32),
                         pl.BlockSpec((1, w, tm), lambda i: (i // tps, 0, i % tps)))
        vto = (jax.ShapeDtypeStruct((seq_batch, KV_HEADS, VT_ROWS, t), BF16),
               pl.BlockSpec((1, KV_HEADS, VT_ROWS, tm), lambda i: (i // tps, 0, 0, i % tps)))
        outs = [row(ATT_WIDTH, BF16), seq(KV_WIDTH), seq(KV_WIDTH), row(KV_WIDTH, BF16), vto,
                row(ATT_WIDTH, F32), seq(IDX_DIM), row(4 * IDX_DIM, BF16), seq(IDX_HEADS)]
    else:
        outs = [row(ATT_WIDTH, BF16), row(KV_WIDTH, F32), row(KV_WIDTH, F32),
                row(ATT_WIDTH, F32), row(IDX_DIM, F32), row(IDX_HEADS, F32)]
    outs += [row(HG_WIDTH, F32)] * 4
    return pl.pallas_call(
        functools.partial(_proj_kernel, seq_major=bool(seq_batch)),
        out_shape=[s for s, _ in outs],
        grid=(nt,),
        in_specs=[pl.BlockSpec((tm, d), lambda i: (i, 0)), mod, mod,
                  pl.BlockSpec((1, d), lambda i: (0, 0)),
                  pl.BlockSpec(w_r.shape, lambda i: (0, 0)),
                  pl.BlockSpec((1, LANES), lambda i: (0, 0)),
                  tab, tab, tab],
        out_specs=[b for _, b in outs],
        compiler_params=_cparams(("arbitrary",)),
        name="proj",
    )(x2d, sc, sh, g1.reshape(1, d), w_r, gk, *tabs)


def _hgrn_kernel(hq_ref, hf_ref, hi_ref, hg_ref, lb_ref, g_ref, s0_ref, r_ref, st_ref, s_scr, *, chunk, nchunk):
    t = pl.program_id(1)

    @pl.when(t == 0)
    def _():
        for hh in range(HG_HEADS):
            s_scr[hh] = s0_ref[0, hh].T

    row = lax.broadcasted_iota(I32, (chunk, chunk), 0)
    col = lax.broadcasted_iota(I32, (chunk, chunk), 1)
    causal = col <= row
    tri = causal.astype(BF16)
    pairs = [(c, hh) for c in range(nchunk) for hh in range(HG_HEADS)]
    rows = lambda c: slice(c * chunk, (c + 1) * chunk)
    lanes = lambda hh: slice(hh * HG_DK, (hh + 1) * HG_DK)
    kks, bs = {}, {}
    for c, hh in pairs:
        lb = lb_ref[:, lanes(hh)]
        f = lb + (1.0 - lb) * _sigmoid(hf_ref[0, rows(c), lanes(hh)])
        logf = jnp.log(f)
        kks[c, hh] = 1.0 - f
        l0 = logf.astype(BF16)
        r1 = logf - l0.astype(F32)
        l1 = r1.astype(BF16)
        l2 = (r1 - l1.astype(F32)).astype(BF16)
        bs[c, hh] = _dot(tri, l0) + (_dot(tri, l1) + _dot(tri, l2))
    q_ins, intras, upds, decays = {}, {}, {}, {}
    for c, hh in pairs:
        b, kk = bs[c, hh], kks[c, hh]
        bl = b[chunk - 1:chunk, :]
        q_in = (hq_ref[0, rows(c), lanes(hh)] * (HG_DK ** -0.5) * jnp.exp(b)).astype(BF16)
        k_in = (kk * jnp.exp(-b)).astype(BF16)
        k_st = (kk * jnp.exp(bl - b)).astype(BF16)
        vb = hi_ref[0, rows(c), lanes(hh)].astype(BF16)
        att = jnp.where(causal, _dot_nt(q_in, k_in), 0.0)
        q_ins[c, hh] = q_in
        intras[c, hh] = _dot(att.astype(BF16), vb)
        upds[c, hh] = _dot_tn(vb, k_st)
        decays[c, hh] = jnp.exp(bl)
    for c, hh in pairs:
        st = s_scr[hh]
        o = _dot_nt(q_ins[c, hh], st.astype(BF16)) + intras[c, hh]
        s_scr[hh] = st * decays[c, hh] + upds[c, hh]
        on = o * lax.rsqrt(jnp.mean(o * o, axis=-1, keepdims=True) + EPS) * g_ref[:, lanes(hh)]
        r_ref[0, rows(c), lanes(hh)] = (on * _silu(hg_ref[0, rows(c), lanes(hh)])).astype(BF16)

    @pl.when(t == pl.num_programs(1) - 1)
    def _():
        for hh in range(HG_HEADS):
            st_ref[0, hh] = s_scr[hh].T


def _hgrn(hq, hf, hi, hg, lb, out_g, s0):
    b, t, w = hq.shape
    chunk = min(HG_CHUNK, t)
    assert t % chunk == 0
    nchunk = min(4, t // chunk)
    tt = chunk * nchunk
    seq = pl.BlockSpec((1, tt, w), lambda i, j: (i, j, 0))
    vec = pl.BlockSpec((1, w), lambda i, j: (0, 0))
    state = pl.BlockSpec((1, HG_HEADS, HG_DK, HG_DV), lambda i, j: (i, 0, 0, 0))
    return pl.pallas_call(
        functools.partial(_hgrn_kernel, chunk=chunk, nchunk=nchunk),
        out_shape=[jax.ShapeDtypeStruct((b, t, w), BF16), jax.ShapeDtypeStruct(s0.shape, F32)],
        grid=(b, t // tt),
        in_specs=[seq, seq, seq, seq, vec, vec, state],
        out_specs=[seq, state],
        scratch_shapes=[pltpu.VMEM((HG_HEADS, HG_DV, HG_DK), F32)],
        compiler_params=_cparams(("arbitrary", "arbitrary")),
        name="hgrn",
    )(hq, hf, hi, hg, lb.reshape(1, w), out_g.reshape(1, w), s0)


def _order_key(score):
    bits = pltpu.bitcast(score, I32)
    bits = jnp.where(bits == INT_MIN, 0, bits)
    return jnp.where(bits >= 0, bits, bits ^ 0x7FFFFFFF)


def _select_threshold(load_keys, store_keys, nchunk, tq, ck, topk, idx_bits, keys_on_rows=False):
    kax = 0 if keys_on_rows else 1
    kshape = (ck, tq) if keys_on_rows else (tq, ck)
    vshape = (1, tq) if keys_on_rows else (tq, 1)

    def count(pred):
        def body(c, acc):
            m = jnp.where(pred(load_keys(c), c), 1.0, 0.0)
            if keys_on_rows:
                parts = [m[j * 8:(j + 1) * 8] for j in range(8)]
                for j in range(8, ck // 8):
                    parts[j % 8] = parts[j % 8] + m[j * 8:(j + 1) * 8]
                while len(parts) > 1:
                    parts = [a + b for a, b in zip(parts[0::2], parts[1::2])]
                part = parts[0]
            else:
                part = m[:, 0:LANES]
                for j in range(1, ck // LANES):
                    part = part + m[:, j * LANES:(j + 1) * LANES]
            return acc + part
        acc = lax.fori_loop(0, nchunk, body, jnp.zeros((8, tq) if keys_on_rows else (tq, LANES), F32))
        return jnp.sum(acc, axis=kax, keepdims=True)

    kf = float(topk)
    zero_i = jnp.zeros(vshape, I32)

    def bisect(nbits, count_ge, done0):
        def cond(st):
            return jnp.logical_and(st[0] < nbits, jnp.min(st[2]) < 0.5)

        def body(st):
            it, tu, done, hit_u = st
            for _ in range(BISECT_BITS_PER_TRIP):
                cand = tu | lax.shift_left(jnp.int32(1), nbits - 1 - it)
                cnt = count_ge(cand)
                active = done < 0.5
                tu = jnp.where(jnp.logical_and(active, cnt >= kf), cand, tu)
                hit = jnp.logical_and(active, cnt == kf)
                hit_u = jnp.where(hit, cand, hit_u)
                done = jnp.where(hit, 1.0, done)
                it = it + 1
            return it, tu, done, hit_u
        return lax.while_loop(cond, body, (jnp.int32(0), zero_i, done0, zero_i))[1:]

    tu, done, hit_u = bisect(32, lambda u: count(lambda key, c: key >= (u ^ INT_MIN)), jnp.zeros(vshape, F32))
    ts = tu ^ INT_MIN
    tsel = (hit_u ^ INT_MIN) - 1
    need_tie = jnp.logical_and(done < 0.5, tu != 0)
    thr = jnp.where(done > 0.5, tsel, ts)

    @pl.when(jnp.max(jnp.where(need_tie, 1.0, 0.0)) > 0.5)
    def _():
        rank = kf - count(lambda key, c: key > ts)
        pos_of = lambda c: c * ck + lax.broadcasted_iota(I32, kshape, kax)

        def jbody(it, ju):
            cand = ju | lax.shift_left(jnp.int32(1), idx_bits - 1 - it)
            pred = lambda key, c: jnp.logical_and(key == ts, pos_of(c) < cand)
            return jnp.where(count(pred) < rank, cand, ju)
        ju = lax.fori_loop(0, idx_bits, jbody, zero_i)

        def fix(c, carry):
            key = load_keys(c)
            bump = jnp.logical_and(jnp.logical_and(key == ts, pos_of(c) <= ju), need_tie)
            store_keys(c, jnp.where(bump, key + 1, key))
            return carry
        lax.fori_loop(0, nchunk, fix, 0)
    return thr


def _prompt_attn_kernel(q_ref, qi_ref, wt_ref, k4_ref, kb_ref, vt_ref, ga_ref, o_ref,
                        key_scr, mm_scr, m_scr, acc_scr, *, topk, idx_bits):
    tq, ck = q_ref.shape[1], ATT_CHUNK
    i = pl.program_id(1)
    nchunk = (i * tq + tq + ck - 1) // ck
    qpos = i * tq + lax.broadcasted_iota(I32, (1, tq), 1)

    qit = qi_ref[0].T
    hit = qit.astype(BF16)
    lot = (qit - hit.astype(F32)).astype(BF16)
    zero = jnp.zeros((IDX_DIM, tq), BF16)

    def idx_rhs(hh):
        hs = slice(hh * IDX_DIM, (hh + 1) * IDX_DIM)
        return jnp.concatenate([hit[hs], hit[hs], lot[hs], zero], axis=0)
    rhs_s = [jnp.concatenate([idx_rhs(2 * p), idx_rhs(2 * p + 1)], axis=1) for p in range(IDX_HEADS // 2)]
    wt = wt_ref[0]

    chunk = lambda c: pl.ds(pl.multiple_of(c * ck, ck), ck)

    def pipelined(mxu_stage, vpu_stage):
        mxu_stage(0, 0)

        def pair_body(j, carry):
            mxu_stage(2 * j + 1, 1)
            vpu_stage(2 * j, 0)
            mxu_stage(jnp.minimum(2 * j + 2, nchunk - 1), 0)
            vpu_stage(2 * j + 1, 1)
            return carry
        lax.fori_loop(0, nchunk // 2, pair_body, 0)

        @pl.when(nchunk % 2 == 1)
        def _():
            vpu_stage(nchunk - 1, 0)

    def score_products(c, slot):
        k4 = k4_ref[0, chunk(c), :]
        for p in range(IDX_HEADS // 2):
            mm_scr[slot, p] = _dot(k4, rhs_s[p])

    def score_keys(c, slot):
        acc = jnp.zeros((ck, tq), F32)
        for p in range(IDX_HEADS // 2):
            s = mm_scr[slot, p]
            acc = acc + jnp.maximum(s[:, :tq], 0.0) * wt[2 * p:2 * p + 1]
            acc = acc + jnp.maximum(s[:, tq:], 0.0) * wt[2 * p + 1:2 * p + 2]
        kpos = c * ck + lax.broadcasted_iota(I32, (ck, 1), 0)
        key_scr[chunk(c), :] = jnp.where(kpos <= qpos, _order_key(acc), INT_MIN)
    pipelined(score_products, score_keys)

    def store_keys(c, val):
        key_scr[chunk(c), :] = val
    thr = _select_threshold(lambda c: key_scr[chunk(c), :], store_keys, nchunk, tq, ck, topk, idx_bits,
                            keys_on_rows=True)

    qt = q_ref[0].astype(F32).T.astype(BF16)
    rhs_a = []
    for m in range(KV_HEADS):
        blk = jnp.concatenate([qt[(2 * m) * HEAD_DIM:(2 * m + 1) * HEAD_DIM],
                               qt[(2 * m + 1) * HEAD_DIM:(2 * m + 2) * HEAD_DIM]], axis=1)
        parts = [jnp.zeros((HEAD_DIM, 2 * tq), BF16)] * m + [blk] + [jnp.zeros((HEAD_DIM, 2 * tq), BF16)] * (KV_HEADS - 1 - m)
        rhs_a.append(jnp.concatenate(parts, axis=0))
    m_scr[...] = jnp.full(m_scr.shape, NEG_BIG, F32)
    acc_scr[...] = jnp.zeros(acc_scr.shape, F32)

    def logits(c, slot):
        kc = kb_ref[0, chunk(c), :]
        for m in range(KV_HEADS):
            mm_scr[slot, m] = _dot(kc, rhs_a[m])

    def softmax_pv(c, slot):
        sel = key_scr[chunk(c), :] > thr
        pas = []
        for m in range(KV_HEADS):
            lg = mm_scr[slot, m]
            lg = jnp.concatenate([jnp.where(sel, lg[:, :tq], NEG_BIG), jnp.where(sel, lg[:, tq:], NEG_BIG)], axis=1)
            mx = m_scr[m]
            mn = jnp.maximum(mx, jnp.max(lg, axis=0, keepdims=True))
            m_scr[m] = mn
            pas.append((jnp.exp2(lg - mn).astype(BF16), jnp.exp2(mx - mn)))
        for m in range(KV_HEADS):
            p, alpha = pas[m]
            acc_scr[m] = acc_scr[m] * alpha + _dot(vt_ref[0, m, :, chunk(c)], p)

    pipelined(logits, softmax_pv)
    pieces = []
    for m in range(KV_HEADS):
        acc = acc_scr[m]
        o = acc[0:HEAD_DIM] / acc[HEAD_DIM:HEAD_DIM + 1]
        pieces += [o[:, :tq], o[:, tq:]]
    at = jnp.concatenate(pieces, axis=0)
    at = at * lax.rsqrt(jnp.mean(at * at, axis=0, keepdims=True) + EPS)
    o_ref[0] = (at.T * ga_ref[...]).astype(BF16)


def _prompt_attention(q, qi, wt, k4, kb, vt, ga):
    b, t, _ = q.shape
    tq = LANES
    assert t % ATT_CHUNK == 0
    topk = min(TOPK_MAX, t // 4)
    blk = lambda w: pl.BlockSpec((1, tq, w), lambda bi, i: (bi, i, 0))
    full = lambda r, c: pl.BlockSpec((1, r, c), lambda bi, i: (bi, 0, 0))
    return pl.pallas_call(
        functools.partial(_prompt_attn_kernel, topk=topk, idx_bits=max(1, (t - 1).bit_length())),
        out_shape=jax.ShapeDtypeStruct((b, t, ATT_WIDTH), BF16),
        grid=(b, t // tq),
        in_specs=[blk(ATT_WIDTH), blk(ATT_WIDTH),
                  pl.BlockSpec((1, IDX_HEADS, tq), lambda bi, i: (bi, 0, i)),
                  full(t, 4 * IDX_DIM), full(t, KV_WIDTH),
                  pl.BlockSpec((1, KV_HEADS, VT_ROWS, t), lambda bi, i: (bi, 0, 0, 0)),
                  pl.BlockSpec((1, ATT_WIDTH), lambda bi, i: (0, 0))],
        out_specs=blk(ATT_WIDTH),
        scratch_shapes=[pltpu.VMEM((t, tq), I32), pltpu.VMEM((2, KV_HEADS, ATT_CHUNK, 2 * tq), F32),
                        pltpu.VMEM((KV_HEADS, 1, 2 * tq), F32),
                        pltpu.VMEM((KV_HEADS, VT_ROWS, 2 * tq), F32)],
        compiler_params=_cparams(("arbitrary", "arbitrary")),
        name="prompt_attn",
    )(q, qi, wt, k4, kb, vt, ga.reshape(1, ATT_WIDTH))


def _sample_score_kernel(pt_ref, qi_ref, w_ref, kin_ref, *rest, npg, nsteps, tnew, topk, idx_bits):
    page_refs = rest[:npg]
    key_ref, thr_ref = rest[npg:npg + 2]
    s = pl.program_id(1)
    last = s == nsteps
    tq = tnew
    page = page_refs[0].shape[1]
    ck = npg * page
    pages = [page_refs[p][...] for p in range(npg)]
    pages[0] = jnp.where(last, kin_ref[0], pages[0])
    k_hi, k_lo = _split_bf16(jnp.concatenate(pages, axis=1))
    sc = _dot(qi_ref[0], jnp.concatenate([k_hi, k_lo, k_hi], axis=0))
    sc = jnp.maximum(sc, 0.0) * w_ref[0]
    acc = sc[0:tq]
    for hh in range(1, IDX_HEADS):
        acc = acc + sc[hh * tq:(hh + 1) * tq]
    qrow = lax.broadcasted_iota(I32, (tq, ck), 0)
    col = lax.broadcasted_iota(I32, (tq, ck), 1)
    ok = jnp.logical_or(jnp.logical_not(last), jnp.logical_and(col <= qrow, col < tnew))
    key_ref[0, :, pl.ds(pl.multiple_of(s * ck, ck), ck)] = jnp.where(ok, _order_key(acc), INT_MIN)

    @pl.when(last)
    def _():
        def load_keys(c):
            return key_ref[0, :, pl.ds(pl.multiple_of(c * ck, ck), ck)]

        def store_keys(c, val):
            key_ref[0, :, pl.ds(pl.multiple_of(c * ck, ck), ck)] = val
        thr = _select_threshold(load_keys, store_keys, nsteps + 1, tq, ck, topk, idx_bits)
        thr_ref[0] = jnp.broadcast_to(thr, (tq, LANES))


def _sample_attn_kernel(pt_ref, q_ref, key_ref, thr_ref, kn_ref, vn_ref, ga_ref, *rest, npg, nsteps, tnew):
    k_refs = rest[:npg]
    v_refs = rest[npg:2 * npg]
    o_ref = rest[2 * npg]
    m_scr, l_scr, acc_scr = rest[2 * npg + 1:]
    s = pl.program_id(1)
    last = s == nsteps
    tq = tnew

    @pl.when(s == 0)
    def _():
        m_scr[...] = jnp.full(m_scr.shape, NEG_BIG, F32)
        l_scr[...] = jnp.zeros(l_scr.shape, F32)
        acc_scr[...] = jnp.zeros(acc_scr.shape, F32)

    sel = key_ref[0] > thr_ref[0][:, 0:1]
    sel2 = jnp.concatenate([sel, sel], axis=0)
    for m in range(KV_HEADS):
        kps = [k_refs[p][m] for p in range(npg)]
        vps = [v_refs[p][m] for p in range(npg)]
        kps[0] = jnp.where(last, kn_ref[0, m], kps[0])
        vps[0] = jnp.where(last, vn_ref[0, m], vps[0])
        kmt = jnp.concatenate(kps, axis=1).astype(BF16)
        vmt = jnp.concatenate(vps, axis=1).astype(BF16)
        lg = jnp.where(sel2, _dot(q_ref[0, m], kmt), NEG_BIG)
        mx = m_scr[m]
        mn = jnp.maximum(mx, jnp.max(lg, axis=-1, keepdims=True))
        pr = jnp.exp2(lg - mn)
        alpha = jnp.exp2(mx - mn)
        l_scr[m] = l_scr[m] * alpha + jnp.sum(pr, axis=-1, keepdims=True)
        acc_scr[m] = acc_scr[m] * alpha + _dot_nt(pr.astype(BF16), vmt)
        m_scr[m] = mn

    @pl.when(last)
    def _():
        pieces = []
        for m in range(KV_HEADS):
            o = acc_scr[m] / l_scr[m]
            pieces += [o[:tq], o[tq:]]
        a = jnp.concatenate(pieces, axis=1)
        a = a * lax.rsqrt(jnp.mean(a * a, axis=-1, keepdims=True) + EPS) * ga_ref[...]
        o_ref[0] = a.astype(BF16)


def _sample_attention(q, qi, wi, ki_new, k_new, v_new, cache_k, cache_v, cache_ki, layer, page_table, ga):
    db, tn, _ = q.shape
    page = cache_ki.shape[2]
    cache_k = cache_k.transpose(0, 1, 3, 4, 2)
    cache_v = cache_v.transpose(0, 1, 3, 4, 2)
    cache_ki = cache_ki.transpose(0, 1, 3, 2)
    n_pages = page_table.shape[1]
    npg_s = math.gcd(SAMPLE_SCORE_PAGES, n_pages)
    npg_a = math.gcd(SAMPLE_ATTN_PAGES, npg_s)
    nsteps_s, nsteps_a = n_pages // npg_s, n_pages // npg_a
    lpad = (nsteps_s + 1) * npg_s * page
    topk = min(TOPK_MAX, (n_pages * page + tn) // 4)
    assert tn <= page
    pad_keys = lambda a: jnp.pad(a, [(0, 0)] * (a.ndim - 1) + [(0, page - tn)])
    kin = pad_keys(ki_new.transpose(0, 2, 1))
    heads = lambda a: pad_keys(a.reshape(db, tn, KV_HEADS, HEAD_DIM).transpose(0, 2, 3, 1))
    qi_hq = qi.reshape(db, tn, IDX_HEADS, IDX_DIM).transpose(0, 2, 1, 3).reshape(db, IDX_HEADS * tn, IDX_DIM)
    qi_hi = qi_hq.astype(BF16)
    qi_hq = jnp.concatenate([qi_hi, qi_hi, (qi_hq - qi_hi.astype(F32)).astype(BF16)], axis=-1)
    w_hq = wi.transpose(0, 2, 1).reshape(db, IDX_HEADS * tn, 1)
    q2 = (q.reshape(db, tn, KV_HEADS, ATT_HEADS // KV_HEADS, HEAD_DIM).transpose(0, 2, 3, 1, 4)
          .reshape(db, KV_HEADS, (ATT_HEADS // KV_HEADS) * tn, HEAD_DIM))
    pt_flat = page_table.reshape(-1).astype(I32)

    def page_spec(tail, p, npg):
        def imap(b, s, pt):
            return (layer, pt[b * n_pages + jnp.minimum(s * npg + p, n_pages - 1)]) + (0,) * len(tail)
        return pl.BlockSpec((None, None) + tail, imap)

    per_b = lambda *tail: pl.BlockSpec((1,) + tail, lambda b, s, pt: (b,) + (0,) * len(tail))
    keys, thr = pl.pallas_call(
        functools.partial(_sample_score_kernel, npg=npg_s, nsteps=nsteps_s, tnew=tn, topk=topk,
                          idx_bits=max(1, (lpad - 1).bit_length())),
        out_shape=[jax.ShapeDtypeStruct((db, tn, lpad), I32),
                   jax.ShapeDtypeStruct((db, tn, LANES), I32)],
        grid_spec=pltpu.PrefetchScalarGridSpec(
            num_scalar_prefetch=1,
            grid=(db, nsteps_s + 1),
            in_specs=[per_b(IDX_HEADS * tn, 3 * IDX_DIM), per_b(IDX_HEADS * tn, 1), per_b(IDX_DIM, page)]
                     + [page_spec((IDX_DIM, page), p, npg_s) for p in range(npg_s)],
            out_specs=[per_b(tn, lpad), per_b(tn, LANES)]),
        compiler_params=_cparams(("arbitrary", "arbitrary")),
        name="sample_score",
    )(pt_flat, qi_hq, w_hq, kin, *([cache_ki] * npg_s))

    kv_pages = [page_spec((KV_HEADS, HEAD_DIM, page), p, npg_a) for p in range(npg_a)]
    a = pl.pallas_call(
        functools.partial(_sample_attn_kernel, npg=npg_a, nsteps=nsteps_a, tnew=tn),
        out_shape=jax.ShapeDtypeStruct((db, tn, ATT_WIDTH), BF16),
        grid_spec=pltpu.PrefetchScalarGridSpec(
            num_scalar_prefetch=1,
            grid=(db, nsteps_a + 1),
            in_specs=[per_b(KV_HEADS, (ATT_HEADS // KV_HEADS) * tn, HEAD_DIM),
                      pl.BlockSpec((1, tn, npg_a * page), lambda b, s, pt: (b, 0, s)),
                      per_b(tn, LANES),
                      per_b(KV_HEADS, HEAD_DIM, page), per_b(KV_HEADS, HEAD_DIM, page),
                      pl.BlockSpec((1, ATT_WIDTH), lambda b, s, pt: (0, 0))]
                     + kv_pages + kv_pages,
            out_specs=per_b(tn, ATT_WIDTH),
            scratch_shapes=[pltpu.VMEM((KV_HEADS, 2 * tn, 1), F32), pltpu.VMEM((KV_HEADS, 2 * tn, 1), F32),
                            pltpu.VMEM((KV_HEADS, 2 * tn, HEAD_DIM), F32)]),
        compiler_params=_cparams(("arbitrary", "arbitrary")),
        name="sample_attn",
    )(pt_flat, q2, keys, thr, heads(k_new), heads(v_new), ga.reshape(1, ATT_WIDTH),
      *([cache_k] * npg_a), *([cache_v] * npg_a))
    return a


def _outproj_kernel(a_ref, r_ref, x_ref, gt_ref, sc_ref, sh_ref, g_ref, wa_ref, wr_ref, wrt_ref, brt_ref,
                    x2_ref, h2_ref, route_ref):
    mix = _dot(a_ref[...], wa_ref[...]) + _dot(r_ref[...], wr_ref[...])
    x2 = x_ref[...] + gt_ref[0] * mix
    x2_ref[...] = x2
    h = x2 * lax.rsqrt(jnp.mean(x2 * x2, axis=-1, keepdims=True) + EPS) * g_ref[...]
    h = h * (1.0 + sc_ref[0]) + sh_ref[0]
    h2_ref[...] = h
    lg = (_dot3(h, wrt_ref[...]) + brt_ref[...]).T
    grp = [lg[g:g + 1] for g in range(N_GROUPS)]
    gmax = functools.reduce(jnp.maximum, grp)
    gden = functools.reduce(lambda u, v: u + v, [jnp.exp(g - gmax) for g in grp])
    gsel = jnp.full(gmax.shape, N_GROUPS - 1, I32)
    for g in range(N_GROUPS - 2, -1, -1):
        gsel = jnp.where(grp[g] == gmax, g, gsel)
    gw = 1.0 / gden
    el = []
    for e in range(EXPERTS_PER_GROUP):
        v = lg[N_GROUPS + e:N_GROUPS + e + 1]
        for g in range(1, N_GROUPS):
            row = N_GROUPS + g * EXPERTS_PER_GROUP + e
            v = jnp.where(gsel == g, lg[row:row + 1], v)
        el.append(v)
    emax = functools.reduce(jnp.maximum, el)
    e0 = jnp.full(emax.shape, EXPERTS_PER_GROUP - 1, I32)
    for e in range(EXPERTS_PER_GROUP - 2, -1, -1):
        e0 = jnp.where(el[e] == emax, e, e0)
    rest = [jnp.where(e0 == e, -jnp.inf, el[e]) for e in range(EXPERTS_PER_GROUP)]
    rmax = functools.reduce(jnp.maximum, rest)
    e1 = jnp.full(emax.shape, EXPERTS_PER_GROUP - 1, I32)
    for e in range(EXPERTS_PER_GROUP - 2, -1, -1):
        e1 = jnp.where(jnp.logical_and(rest[e] == rmax, e0 != e), e, e1)
    p1 = jnp.exp(rmax - emax)
    w0 = gw * (1.0 / (1.0 + p1))
    w1 = gw * (p1 / (1.0 + p1))
    base = gsel * EXPERTS_PER_GROUP
    rid = lax.broadcasted_iota(I32, lg.shape, 0)
    rt = jnp.where(rid == 0, (base + e0).astype(F32),
                   jnp.where(rid == 1, (base + e1).astype(F32),
                             jnp.where(rid == 2, w0, jnp.where(rid == 3, w1, 0.0))))
    route_ref[...] = rt.T


def _outproj(a, r, x2d, gt, sc, sh, g2, wa, wr, wrt, brt, tiles_per_group):
    n, d = x2d.shape
    tm = min(PROJ_TILE, n)
    rr = gt.shape[1]
    row = lambda w: pl.BlockSpec((tm, w), lambda i: (i, 0))
    mod = pl.BlockSpec((1, rr, d), lambda i: (i // tiles_per_group, 0, 0))
    const = lambda s: pl.BlockSpec(s, lambda i: (0, 0))
    return pl.pallas_call(
        _outproj_kernel,
        out_shape=[jax.ShapeDtypeStruct((n, d), F32), jax.ShapeDtypeStruct((n, d), F32),
                   jax.ShapeDtypeStruct((n, LANES), F32)],
        grid=(n // tm,),
        in_specs=[row(ATT_WIDTH), row(HG_WIDTH), row(d), mod, mod, mod, const((1, d)),
                  const(wa.shape), const(wr.shape), const(wrt.shape), const(brt.shape)],
        out_specs=[row(d), row(d), row(LANES)],
        compiler_params=_cparams(("arbitrary",)),
        name="outproj",
    )(a, r, x2d, gt, sc, sh, g2.reshape(1, d), wa, wr, wrt, brt)


def _gather_pipeline(step, nsteps, idx_hbm, src_hbm, buf, idx_smem, isem, rsem):
    nrows = buf.shape[1]
    slot = step % 2

    def idx_copy(b, sl):
        return pltpu.make_async_copy(idx_hbm.at[b], idx_smem.at[sl], isem.at[sl])

    def start_rows(sl):
        for r in range(nrows):
            pltpu.make_async_copy(src_hbm.at[pl.ds(idx_smem[sl, 0, r], 1), :],
                                  buf.at[sl, pl.ds(r, 1), :], rsem.at[sl]).start()

    @pl.when(step == 0)
    def _():
        idx_copy(0, 0).start()
        idx_copy(0, 0).wait()
        start_rows(0)

        @pl.when(nsteps > 1)
        def _():
            idx_copy(1, 1).start()

    @pl.when(step + 1 < nsteps)
    def _():
        idx_copy(step + 1, 1 - slot).wait()
        start_rows(1 - slot)

    @pl.when(step + 2 < nsteps)
    def _():
        idx_copy(step + 2, slot).start()

    pltpu.make_async_copy(src_hbm.at[pl.ds(0, nrows), :], buf.at[slot], rsem.at[slot]).wait()
    return slot


def _ffn_kernel(be_ref, nb_ref, tok_ref, h_ref, w1_ref, w3_ref, w2_ref, y_ref, xbuf, idx_smem, isem, rsem):
    j = pl.program_id(0)

    @pl.when(j < nb_ref[0])
    def _():
        slot = _gather_pipeline(j, nb_ref[0], tok_ref, h_ref, xbuf, idx_smem, isem, rsem)
        xb = xbuf[slot].astype(BF16)
        u = _dot(xb, w1_ref[0])
        g = _dot(xb, w3_ref[0])
        y_ref[...] = _dot((_silu(u) * g).astype(BF16), w2_ref[0])

    @pl.when(j >= nb_ref[0])
    def _():
        y_ref[...] = jnp.zeros(y_ref.shape, F32)


def _ffn(blk_e, n_used, tok, h2, w1, w3, w2, blk):
    nb = tok.shape[0]
    d = h2.shape[1]
    wspec = lambda s: pl.BlockSpec((1,) + s, lambda j, be, nu: (be[j], 0, 0))
    return pl.pallas_call(
        _ffn_kernel,
        out_shape=jax.ShapeDtypeStruct((nb * blk, d), F32),
        grid_spec=pltpu.PrefetchScalarGridSpec(
            num_scalar_prefetch=2,
            grid=(nb,),
            in_specs=[pl.BlockSpec(memory_space=pl.ANY), pl.BlockSpec(memory_space=pl.ANY),
                      wspec(w1.shape[1:]), wspec(w3.shape[1:]), wspec(w2.shape[1:])],
            out_specs=pl.BlockSpec((blk, d), lambda j, be, nu: (j, 0)),
            scratch_shapes=[pltpu.VMEM((2, blk, d), F32), pltpu.SMEM((2, 1, blk), I32),
                            pltpu.SemaphoreType.DMA((2,)), pltpu.SemaphoreType.DMA((2,))]),
        compiler_params=_cparams(("arbitrary",)),
        name="moe_ffn",
    )(blk_e, n_used, tok, h2, w1, w3, w2)


def _combine_kernel(dd_ref, y_ref, x2_ref, route_ref, gt_ref, sc_ref, sh_ref, g_ref, o_ref,
                    ybuf, idx_smem, isem, rsem):
    tm = x2_ref.shape[0]
    slot = _gather_pipeline(pl.program_id(0), pl.num_programs(0), dd_ref, y_ref, ybuf, idx_smem, isem, rsem)
    rt = route_ref[...]
    moe = ybuf[slot, 0:tm] * rt[:, 2:3] + ybuf[slot, tm:2 * tm] * rt[:, 3:4]
    x = x2_ref[...] + gt_ref[0] * moe
    y = x * lax.rsqrt(jnp.mean(x * x, axis=-1, keepdims=True) + EPS) * g_ref[...]
    o_ref[...] = y * (1.0 + sc_ref[0]) + sh_ref[0]


def _combine(dd, yb, x2, route, gt, sc, sh, gf, tiles_per_group):
    n, d = x2.shape
    tm = dd.shape[2] // 2
    rr = gt.shape[1]
    row = pl.BlockSpec((tm, d), lambda i: (i, 0))
    mod = pl.BlockSpec((1, rr, d), lambda i: (i // tiles_per_group, 0, 0))
    hbm = pl.BlockSpec(memory_space=pl.ANY)
    return pl.pallas_call(
        _combine_kernel,
        out_shape=jax.ShapeDtypeStruct((n, d), F32),
        grid=(n // tm,),
        in_specs=[hbm, hbm, row, pl.BlockSpec((tm, LANES), lambda i: (i, 0)), mod, mod, mod,
                  pl.BlockSpec((1, d), lambda i: (0, 0))],
        out_specs=row,
        scratch_shapes=[pltpu.VMEM((2, 2 * tm, d), F32), pltpu.SMEM((2, 1, 2 * tm), I32),
                        pltpu.SemaphoreType.DMA((2,)), pltpu.SemaphoreType.DMA((2,))],
        compiler_params=_cparams(("arbitrary",)),
        name="moe_combine",
    )(dd, yb, x2, route, gt, sc, sh, gf.reshape(1, d))


def _dispatch(route, blk):
    n = route.shape[0]
    flat_e = route[:, 0:2].astype(I32).reshape(-1)
    a = flat_e.shape[0]
    onehot = (flat_e[:, None] == jnp.arange(N_EXPERTS, dtype=I32)[None, :]).astype(I32)
    csum = jnp.cumsum(onehot, axis=0)
    rank = jnp.sum((csum - onehot) * onehot, axis=1)
    counts = csum[-1]
    padded = (counts + blk - 1) // blk * blk
    pad_end = jnp.cumsum(padded)
    pad_start = pad_end - padded
    dest = pad_start[flat_e] + rank
    nb = -(-a // blk) + N_EXPERTS
    tok = jnp.zeros((nb * blk,), I32).at[dest].set(jnp.arange(a, dtype=I32) // 2)
    blk_e = jnp.minimum(jnp.searchsorted(pad_end, jnp.arange(nb, dtype=I32) * blk, side='right'),
                        N_EXPERTS - 1).astype(I32)
    n_used = (pad_end[-1] // blk).astype(I32).reshape(1)
    dest2 = dest.reshape(n, 2)
    return tok.reshape(nb, 1, blk), blk_e, n_used, dest2[:, 0], dest2[:, 1]


def _mods(m, n_chunks, per_token_rows):
    parts = jnp.split(m, n_chunks, axis=-1)
    if per_token_rows is None:
        return [p[:, None, :] for p in parts]
    g, d = parts[0].shape
    tm = min(PROJ_TILE, g * per_token_rows)
    return [jnp.repeat(p, per_token_rows, axis=0).reshape(-1, tm, d) for p in parts]


def _layer(x, mod6, modf, pos, s0, attend, lb, wts, final_g, per_token):
    b, t, d = x.shape
    n = b * t
    (norm1_g, norm2_g, w_r, gk, ga, hg_out_g, wa, wr, wrt, brt, w1, w3, w2) = wts
    tm = min(PROJ_TILE, n)
    tiles_per_group = 1 if per_token else t // tm
    sh1, sc1, gt1, sh2, sc2, gt2 = _mods(mod6, 6, t if per_token else None)
    shf, scf = _mods(modf, 2, t if per_token else None)
    tabs = _rope_tables(jnp.tile(pos, tm // t) if per_token else pos)
    x2d = x.reshape(n, d)
    sq = lambda arr: arr.reshape(b, t, arr.shape[-1])
    if per_token:
        q, k, v, qi, ki, wi, hq, hf, hi, hg = _proj(x2d, sc1, sh1, norm1_g, w_r, gk, tabs, tiles_per_group, 0)
        a = attend(sq(q), sq(qi), sq(wi), sq(ki), sq(k), sq(v), ga)
        k_out, v_out, ki_out = k.reshape(b, t, KV_HEADS, HEAD_DIM), v.reshape(b, t, KV_HEADS, HEAD_DIM), sq(ki)
    else:
        q, kt, vt, kb, vto, qi, kit, k4, wt, hq, hf, hi, hg = _proj(x2d, sc1, sh1, norm1_g, w_r, gk, tabs,
                                                                   tiles_per_group, b)
        a = attend(sq(q), sq(qi), wt, sq(k4), sq(kb), vto, ga)
        heads = lambda xt: xt.reshape(b, KV_HEADS, HEAD_DIM, t).transpose(0, 3, 1, 2)
        k_out, v_out, ki_out = heads(kt), heads(vt), kit.transpose(0, 2, 1)
    r, s_t = _hgrn(sq(hq), sq(hf), sq(hi), sq(hg), lb, hg_out_g, s0)
    x2, h2, route = _outproj(a.reshape(n, -1), r.reshape(n, -1), x2d, gt1, sc2, sh2, norm2_g,
                             wa, wr, wrt, brt, tiles_per_group)
    blk = MOE_BLOCK if n >= 8 * MOE_BLOCK else 64
    tok, blk_e, n_used, d0, d1 = _dispatch(route, blk)
    yb = _ffn(blk_e, n_used, tok, h2, w1, w3, w2, blk)
    dd = jnp.concatenate([d0.reshape(n // tm, 1, tm), d1.reshape(n // tm, 1, tm)], axis=2)
    y = _combine(dd, yb, x2, route, gt2, scf, shf, final_g, tiles_per_group)
    return y.reshape(b, t, d), k_out, v_out, ki_out, s_t


def kernel(x_prompt, x_sample, cache_k, cache_v, cache_kidx, state_hgrn, page_table, c_prompt, c_sample,
           ada_w, ada_b, norm1_g, norm2_g, w_in, idx_k_g, hg_lb_logits, attn_out_g, hg_out_g, w_out,
           w_group, b_group, w_expert_router, b_expert_router, w1, w3, w2, final_g, ada_final_w, ada_final_b):
    depth = ada_w.shape[0]
    assert depth == 1, "the final adaLN norm is fused into the (single) layer's combine kernel"
    bp, tp, d = x_prompt.shape
    bs, ts, _ = x_sample.shape
    past = page_table.shape[1] * cache_kidx.shape[2]
    lb_all = jnp.cumsum(jax.nn.softmax(hg_lb_logits.astype(F32), axis=0), axis=0)
    n_c = bp + bs
    c_all = jnp.concatenate([c_prompt, c_sample, jnp.zeros((-n_c % 16, d), F32)], axis=0)
    modf = _ada(c_all, ada_final_w, ada_final_b)
    l = 0
    mod6 = _ada(c_all, ada_w[l], ada_b[l])

    seg = [0]
    for s in (ATT_WIDTH, KV_WIDTH, KV_WIDTH, IDX_HEADS * IDX_DIM, IDX_DIM, IDX_HEADS,
              HG_WIDTH, HG_WIDTH, HG_WIDTH, HG_WIDTH):
        seg.append(seg[-1] + s)
    wl = w_in[l]
    col = lambda i: wl[:, seg[i]:seg[i + 1]]
    zpad = lambda w: jnp.zeros((d, w), wl.dtype)
    w_r = jnp.concatenate([col(0), col(1), col(2), col(3), col(6), col(7), col(8), col(9),
                           col(4), zpad(LANES - IDX_DIM), col(5), zpad(LANES - IDX_HEADS)], axis=1).astype(BF16)
    gk = jnp.concatenate([idx_k_g[l], jnp.zeros((LANES - IDX_DIM,), F32)]).reshape(1, LANES)
    wa = w_out[l, :ATT_WIDTH].astype(BF16)
    wr = w_out[l, ATT_WIDTH:].astype(BF16)
    n_rt = N_GROUPS + N_EXPERTS
    wrt = jnp.concatenate([w_group[l], w_expert_router[l], jnp.zeros((d, LANES - n_rt), F32)], axis=1)
    brt = jnp.concatenate([b_group[l], b_expert_router[l], jnp.zeros((LANES - n_rt,), F32)]).reshape(1, LANES)
    wts = (norm1_g[l], norm2_g[l], w_r, gk, attn_out_g[l], hg_out_g[l], wa, wr, wrt, brt,
           w1[l].astype(BF16), w3[l].astype(BF16), w2[l].astype(BF16))

    def attend_s(q, qi, wi, ki, k, v, ga):
        return _sample_attention(q, qi, wi, ki, k, v, cache_k, cache_v, cache_kidx, l, page_table, ga)

    s0_p = jnp.zeros((bp, HG_HEADS, HG_DK, HG_DV), F32)
    yp, kp, vp, kip, sp = _layer(x_prompt, mod6[:bp], modf[:bp], jnp.arange(tp), s0_p, _prompt_attention,
                                 lb_all[l], wts, final_g, per_token=False)
    ys, ks, vs, kis, ss = _layer(x_sample, mod6[bp:n_c], modf[bp:n_c], past + jnp.arange(ts), state_hgrn[l],
                                 attend_s, lb_all[l], wts, final_g, per_token=True)
    return (yp, ys, kp[None], vp[None], kip[None], sp[None], ks[None], vs[None], kis[None], ss[None])
```

```python
import functools
import math

import jax
import jax.numpy as jnp
from jax import lax
from jax.experimental import pallas as pl
from jax.experimental.pallas import tpu as pltpu

F32 = jnp.float32
BF16 = jnp.bfloat16
I32 = jnp.int32

ATT_HEADS = 8
KV_HEADS = 4
HEAD_DIM = 64
ATT_WIDTH = ATT_HEADS * HEAD_DIM
KV_WIDTH = KV_HEADS * HEAD_DIM
ROT_HALF = HEAD_DIM // 8
ROPE_THETA = 500000.0
IDX_HEADS = 8
IDX_DIM = 64
TOPK_MAX = 256
HG_HEADS = 4
HG_DK = 128
HG_DV = 128
HG_WIDTH = HG_HEADS * HG_DV
HG_CHUNK = 64
N_GROUPS = 4
EXPERTS_PER_GROUP = 4
N_EXPERTS = N_GROUPS * EXPERTS_PER_GROUP
D_EXPERT = 512
MOE_BLOCK = 256
EPS = 1e-6

LANES = 128
INT_MIN = -2 ** 31
NEG_BIG = -1e30
VMEM_LIMIT = 56 * 1024 * 1024
PROJ_TILE = 256
ATT_CHUNK = 1024
VT_ROWS = HEAD_DIM + 16
LOG2E = 1.4426950408889634
BISECT_BITS_PER_TRIP = 4
SAMPLE_SCORE_PAGES = 16
SAMPLE_ATTN_PAGES = 16

_Z_Q, _Z_K, _Z_V, _Z_QI, _Z_HQ, _Z_HF, _Z_HI, _Z_HG, _Z_KI, _Z_WI, _Z_END = (
    0, 512, 768, 1024, 1536, 2048, 2560, 3072, 3584, 3712, 3840)


def _cparams(sem):
    return pltpu.CompilerParams(dimension_semantics=sem, vmem_limit_bytes=VMEM_LIMIT)


def _split_bf16(x):
    hi = x.astype(BF16)
    lo = (x - hi.astype(F32)).astype(BF16)
    return hi, lo


def _dot(a, b):
    return jnp.dot(a, b, preferred_element_type=F32)


def _dot_nt(a, b):
    return lax.dot_general(a, b, (((1,), (1,)), ((), ())), preferred_element_type=F32)


def _dot_tn(a, b):
    return lax.dot_general(a, b, (((0,), (0,)), ((), ())), preferred_element_type=F32)


def _dot3(a, b):
    ah, al = _split_bf16(a)
    bh, bl = _split_bf16(b)
    return _dot(ah, bh) + (_dot(ah, bl) + _dot(al, bh))


def _dot3_nt(a, b):
    ah, al = _split_bf16(a)
    bh, bl = _split_bf16(b)
    return _dot_nt(ah, bh) + (_dot_nt(ah, bl) + _dot_nt(al, bh))


def _silu(x):
    return x * (1.0 / (1.0 + jnp.exp(-x)))


def _sigmoid(x):
    return 1.0 / (1.0 + jnp.exp(-x))


def _ada_kernel(c_ref, w_ref, b_ref, o_ref):
    o_ref[...] = _dot3(_silu(c_ref[...]), w_ref[...]) + b_ref[...]


def _ada(c, w, b):
    r, d = c.shape
    e = w.shape[1]
    te = 1024
    return pl.pallas_call(
        _ada_kernel,
        out_shape=jax.ShapeDtypeStruct((r, e), F32),
        grid=(e // te,),
        in_specs=[pl.BlockSpec((r, d), lambda j: (0, 0)),
                  pl.BlockSpec((d, te), lambda j: (0, j)),
                  pl.BlockSpec((1, te), lambda j: (0, j))],
        out_specs=pl.BlockSpec((r, te), lambda j: (0, j)),
        compiler_params=_cparams(("arbitrary",)),
        name="ada",
    )(c, w, b.reshape(1, e))


def _rope_tables(pos):
    p = pos.shape[0]
    inv = ROPE_THETA ** (-jnp.arange(ROT_HALF, dtype=F32) * (2.0 / (2 * ROT_HALF)))
    ang = pos.astype(F32)[:, None] * inv[None, :]
    c, s = jnp.cos(ang), jnp.sin(ang)
    rest = HEAD_DIM - 2 * ROT_HALF
    one, zero, z8 = jnp.ones((p, rest), F32), jnp.zeros((p, rest), F32), jnp.zeros((p, ROT_HALF), F32)
    cos64 = jnp.concatenate([c, c, one], axis=1)
    sa64 = jnp.concatenate([-s, z8, zero], axis=1)
    sb64 = jnp.concatenate([z8, s, zero], axis=1)
    dup = lambda t: jnp.concatenate([t, t], axis=1)
    return dup(cos64), dup(sa64), dup(sb64)


def _proj_kernel(x_ref, sc_ref, sh_ref, g_ref, w_ref, gk_ref, cos_ref, sa_ref, sb_ref, *out_refs, seq_major):
    x = x_ref[...]
    ms = jnp.mean(x * x, axis=-1, keepdims=True)
    h = x * lax.rsqrt(ms + EPS) * g_ref[...]
    h = h * (1.0 + sc_ref[0]) + sh_ref[0]
    z = _dot(h.astype(BF16), w_ref[...])
    cos, sa, sb = cos_ref[...], sa_ref[...], sb_ref[...]

    def rope(zs):
        n = zs.shape[1] // LANES
        rep = (lambda t: jnp.concatenate([t] * n, axis=1)) if n > 1 else (lambda t: t)
        w = zs.shape[1]
        return (zs * rep(cos) + pltpu.roll(zs, w - ROT_HALF, 1) * rep(sa)
                + pltpu.roll(zs, ROT_HALF, 1) * rep(sb))

    q = (rope(z[:, _Z_Q:_Z_K]) * (HEAD_DIM ** -0.5 * LOG2E)).astype(BF16)
    k = rope(z[:, _Z_K:_Z_V])
    v = z[:, _Z_V:_Z_QI]
    qi = rope(z[:, _Z_QI:_Z_HQ]) * (IDX_DIM ** -0.5)
    ks = z[:, _Z_KI:_Z_WI]
    kms = jnp.sum(ks * ks, axis=-1, keepdims=True) * (1.0 / IDX_DIM)
    kn = rope(ks * lax.rsqrt(kms + EPS) * gk_ref[...])
    ws = z[:, _Z_WI:_Z_END] * (IDX_HEADS ** -0.5)
    if seq_major:
        q_ref, kt_ref, vt_ref, kb_ref, vto_ref, qi_ref, kit_ref, k4_ref, wt_ref = out_refs[:9]
        kt_ref[0] = k.T
        vt = v.T
        vt_ref[0] = vt
        kb_ref[...] = k.astype(BF16)
        for m in range(KV_HEADS):
            vto_ref[0, m, 0:HEAD_DIM, :] = vt[m * HEAD_DIM:(m + 1) * HEAD_DIM].astype(BF16)
            vto_ref[0, m, HEAD_DIM:VT_ROWS, :] = jnp.ones((VT_ROWS - HEAD_DIM, x.shape[0]), BF16)
        kit_ref[0] = kn.T[:IDX_DIM]
        hi = kn.astype(BF16).astype(F32)
        lo = kn - hi
        k4_ref[...] = jnp.concatenate([hi + pltpu.roll(lo, IDX_DIM, 1), hi + pltpu.roll(hi, IDX_DIM, 1)],
                                      axis=1).astype(BF16)
        wt_ref[0] = ws.T[:IDX_HEADS]
    else:
        q_ref, k_ref, v_ref, qi_ref, ki_ref, wi_ref = out_refs[:6]
        k_ref[...] = k
        v_ref[...] = v
        ki_ref[...] = kn[:, :IDX_DIM]
        wi_ref[...] = ws[:, :IDX_HEADS]
    q_ref[...] = q
    qi_ref[...] = qi
    hq_ref, hf_ref, hi_ref, hg_ref = out_refs[-4:]
    hq_ref[...] = z[:, _Z_HQ:_Z_HF]
    hf_ref[...] = z[:, _Z_HF:_Z_HI]
    hi_ref[...] = z[:, _Z_HI:_Z_HG]
    hg_ref[...] = z[:, _Z_HG:_Z_KI]


def _proj(x2d, sc, sh, g1, w_r, gk, tabs, tiles_per_group, seq_batch):
    n, d = x2d.shape
    tm = min(PROJ_TILE, n)
    nt = n // tm
    r = sc.shape[1]
    pt = tabs[0].shape[0] // tm
    row = lambda w, t: (jax.ShapeDtypeStruct((n, w), t), pl.BlockSpec((tm, w), lambda i: (i, 0)))
    mod = pl.BlockSpec((1, r, d), lambda i: (i // tiles_per_group, 0, 0))
    tab = pl.BlockSpec((tm, LANES), lambda i: (i % pt, 0))
    if seq_batch:
        t = n // seq_batch
        tps = t // tm
        seq = lambda w: (jax.ShapeDtypeStruct((seq_batch, w, t), F32),
                         pl.BlockSpec((1, w, tm), lambda i: (i // tps, 0, i % tps)))
        vto = (jax.ShapeDtypeStruct((seq_batch, KV_HEADS, VT_ROWS, t), BF16),
               pl.BlockSpec((1, KV_HEADS, VT_ROWS, tm), lambda i: (i // tps, 0, 0, i % tps)))
        outs = [row(ATT_WIDTH, BF16), seq(KV_WIDTH), seq(KV_WIDTH), row(KV_WIDTH, BF16), vto,
                row(ATT_WIDTH, F32), seq(IDX_DIM), row(4 * IDX_DIM, BF16), seq(IDX_HEADS)]
    else:
        outs = [row(ATT_WIDTH, BF16), row(KV_WIDTH, F32), row(KV_WIDTH, F32),
                row(ATT_WIDTH, F32), row(IDX_DIM, F32), row(IDX_HEADS, F32)]
    outs += [row(HG_WIDTH, F32)] * 4
    return pl.pallas_call(
        functools.partial(_proj_kernel, seq_major=bool(seq_batch)),
        out_shape=[s for s, _ in outs],
        grid=(nt,),
        in_specs=[pl.BlockSpec((tm, d), lambda i: (i, 0)), mod, mod,
                  pl.BlockSpec((1, d), lambda i: (0, 0)),
                  pl.BlockSpec(w_r.shape, lambda i: (0, 0)),
                  pl.BlockSpec((1, LANES), lambda i: (0, 0)),
                  tab, tab, tab],
        out_specs=[b for _, b in outs],
        compiler_params=_cparams(("arbitrary",)),
        name="proj",
    )(x2d, sc, sh, g1.reshape(1, d), w_r, gk, *tabs)


def _hgrn_kernel(hq_ref, hf_ref, hi_ref, hg_ref, lb_ref, g_ref, s0_ref, r_ref, st_ref, s_scr, *, chunk, nchunk):
    t = pl.program_id(1)

    @pl.when(t == 0)
    def _():
        for hh in range(HG_HEADS):
            s_scr[hh] = s0_ref[0, hh].T

    tt = chunk * nchunk
    row = lax.broadcasted_iota(I32, (tt, tt), 0)
    col = lax.broadcasted_iota(I32, (tt, tt), 1)
    causal = jnp.logical_and(col <= row, row // chunk == col // chunk)
    tri = causal.astype(BF16)
    rows = lambda c: slice(c * chunk, (c + 1) * chunk)
    lanes = lambda hh: slice(hh * HG_DK, (hh + 1) * HG_DK)
    lb = lb_ref[...]
    f = lb + (1.0 - lb) * _sigmoid(hf_ref[0])
    logf = jnp.log(f)
    kk = 1.0 - f
    l0 = logf.astype(BF16)
    r1 = logf - l0.astype(F32)
    l1 = r1.astype(BF16)
    l2 = (r1 - l1.astype(F32)).astype(BF16)
    b = _dot(tri, l0) + (_dot(tri, l1) + _dot(tri, l2))
    bls = [b[(c + 1) * chunk - 1:(c + 1) * chunk, :] for c in range(nchunk)]
    bl_rows = jnp.concatenate([jnp.broadcast_to(bl, (chunk, bl.shape[1])) for bl in bls], axis=0)
    q_in = (hq_ref[0] * (HG_DK ** -0.5) * jnp.exp(b)).astype(BF16)
    k_in = (kk * jnp.exp(-b)).astype(BF16)
    k_st = (kk * jnp.exp(bl_rows - b)).astype(BF16)
    vb = hi_ref[0].astype(BF16)
    intra = []
    for hh in range(HG_HEADS):
        att = jnp.where(causal, _dot_nt(q_in[:, lanes(hh)], k_in[:, lanes(hh)]), 0.0)
        intra.append(_dot(att.astype(BF16), vb[:, lanes(hh)]))
    upd = {(c, hh): _dot_tn(vb[rows(c), lanes(hh)], k_st[rows(c), lanes(hh)])
           for c in range(nchunk) for hh in range(HG_HEADS)}
    decay = [jnp.exp(bl) for bl in bls]
    for hh in range(HG_HEADS):
        os = []
        for c in range(nchunk):
            st = s_scr[hh]
            os.append(_dot_nt(q_in[rows(c), lanes(hh)], st.astype(BF16)))
            s_scr[hh] = st * decay[c][:, lanes(hh)] + upd[c, hh]
        o = jnp.concatenate(os, axis=0) + intra[hh]
        on = o * lax.rsqrt(jnp.mean(o * o, axis=-1, keepdims=True) + EPS) * g_ref[:, lanes(hh)]
        r_ref[0, :, lanes(hh)] = (on * _silu(hg_ref[0, :, lanes(hh)])).astype(BF16)

    @pl.when(t == pl.num_programs(1) - 1)
    def _():
        for hh in range(HG_HEADS):
            st_ref[0, hh] = s_scr[hh].T


def _hgrn(hq, hf, hi, hg, lb, out_g, s0):
    b, t, w = hq.shape
    chunk = min(HG_CHUNK, t)
    assert t % chunk == 0
    nchunk = min(4, t // chunk)
    tt = chunk * nchunk
    seq = pl.BlockSpec((1, tt, w), lambda i, j: (i, j, 0))
    vec = pl.BlockSpec((1, w), lambda i, j: (0, 0))
    state = pl.BlockSpec((1, HG_HEADS, HG_DK, HG_DV), lambda i, j: (i, 0, 0, 0))
    return pl.pallas_call(
        functools.partial(_hgrn_kernel, chunk=chunk, nchunk=nchunk),
        out_shape=[jax.ShapeDtypeStruct((b, t, w), BF16), jax.ShapeDtypeStruct(s0.shape, F32)],
        grid=(b, t // tt),
        in_specs=[seq, seq, seq, seq, vec, vec, state],
        out_specs=[seq, state],
        scratch_shapes=[pltpu.VMEM((HG_HEADS, HG_DV, HG_DK), F32)],
        compiler_params=_cparams(("arbitrary", "arbitrary")),
        name="hgrn",
    )(hq, hf, hi, hg, lb.reshape(1, w), out_g.reshape(1, w), s0)


def _order_key(score):
    bits = pltpu.bitcast(score, I32)
    bits = jnp.where(bits == INT_MIN, 0, bits)
    return jnp.where(bits >= 0, bits, bits ^ 0x7FFFFFFF)


def _select_threshold(load_keys, store_keys, nchunk, tq, ck, topk, idx_bits, keys_on_rows=False):
    kax = 0 if keys_on_rows else 1
    kshape = (ck, tq) if keys_on_rows else (tq, ck)
    vshape = (1, tq) if keys_on_rows else (tq, 1)

    def count(pred):
        def body(c, acc):
            m = jnp.where(pred(load_keys(c), c), 1.0, 0.0)
            if keys_on_rows:
                parts = [m[j * 8:(j + 1) * 8] for j in range(8)]
                for j in range(8, ck // 8):
                    parts[j % 8] = parts[j % 8] + m[j * 8:(j + 1) * 8]
                while len(parts) > 1:
                    parts = [a + b for a, b in zip(parts[0::2], parts[1::2])]
                part = parts[0]
            else:
                part = m[:, 0:LANES]
                for j in range(1, ck // LANES):
                    part = part + m[:, j * LANES:(j + 1) * LANES]
            return acc + part
        acc = lax.fori_loop(0, nchunk, body, jnp.zeros((8, tq) if keys_on_rows else (tq, LANES), F32))
        return jnp.sum(acc, axis=kax, keepdims=True)

    kf = float(topk)
    zero_i = jnp.zeros(vshape, I32)

    def bisect(nbits, count_ge, done0):
        def cond(st):
            return jnp.logical_and(st[0] < nbits, jnp.min(st[2]) < 0.5)

        def body(st):
            it, tu, done, hit_u = st
            for _ in range(BISECT_BITS_PER_TRIP):
                cand = tu | lax.shift_left(jnp.int32(1), nbits - 1 - it)
                cnt = count_ge(cand)
                active = done < 0.5
                tu = jnp.where(jnp.logical_and(active, cnt >= kf), cand, tu)
                hit = jnp.logical_and(active, cnt == kf)
                hit_u = jnp.where(hit, cand, hit_u)
                done = jnp.where(hit, 1.0, done)
                it = it + 1
            return it, tu, done, hit_u
        return lax.while_loop(cond, body, (jnp.int32(0), zero_i, done0, zero_i))[1:]

    tu, done, hit_u = bisect(32, lambda u: count(lambda key, c: key >= (u ^ INT_MIN)), jnp.zeros(vshape, F32))
    ts = tu ^ INT_MIN
    tsel = (hit_u ^ INT_MIN) - 1
    need_tie = jnp.logical_and(done < 0.5, tu != 0)
    thr = jnp.where(done > 0.5, tsel, ts)

    @pl.when(jnp.max(jnp.where(need_tie, 1.0, 0.0)) > 0.5)
    def _():
        rank = kf - count(lambda key, c: key > ts)
        pos_of = lambda c: c * ck + lax.broadcasted_iota(I32, kshape, kax)

        def jbody(it, ju):
            cand = ju | lax.shift_left(jnp.int32(1), idx_bits - 1 - it)
            pred = lambda key, c: jnp.logical_and(key == ts, pos_of(c) < cand)
            return jnp.where(count(pred) < rank, cand, ju)
        ju = lax.fori_loop(0, idx_bits, jbody, zero_i)

        def fix(c, carry):
            key = load_keys(c)
            bump = jnp.logical_and(jnp.logical_and(key == ts, pos_of(c) <= ju), need_tie)
            store_keys(c, jnp.where(bump, key + 1, key))
            return carry
        lax.fori_loop(0, nchunk, fix, 0)
    return thr


def _prompt_attn_kernel(q_ref, qi_ref, wt_ref, k4_ref, kb_ref, vt_ref, ga_ref, o_ref,
                        key_scr, mm_scr, m_scr, acc_scr, *, topk, idx_bits):
    tq, ck = q_ref.shape[1], ATT_CHUNK
    i = pl.program_id(1)
    nchunk = (i * tq + tq + ck - 1) // ck
    qpos = i * tq + lax.broadcasted_iota(I32, (1, tq), 1)

    qit = qi_ref[0].T
    hit = qit.astype(BF16)
    lot = (qit - hit.astype(F32)).astype(BF16)
    zero = jnp.zeros((IDX_DIM, tq), BF16)

    def idx_rhs(hh):
        hs = slice(hh * IDX_DIM, (hh + 1) * IDX_DIM)
        return jnp.concatenate([hit[hs], hit[hs], lot[hs], zero], axis=0)
    rhs_s = [jnp.concatenate([idx_rhs(2 * p), idx_rhs(2 * p + 1)], axis=1) for p in range(IDX_HEADS // 2)]
    wt = wt_ref[0]

    chunk = lambda c: pl.ds(pl.multiple_of(c * ck, ck), ck)

    def pipelined(mxu_stage, vpu_stage):
        mxu_stage(0, 0)

        def pair_body(j, carry):
            mxu_stage(2 * j + 1, 1)
            vpu_stage(2 * j, 0)
            mxu_stage(jnp.minimum(2 * j + 2, nchunk - 1), 0)
            vpu_stage(2 * j + 1, 1)
            return carry
        lax.fori_loop(0, nchunk // 2, pair_body, 0)

        @pl.when(nchunk % 2 == 1)
        def _():
            vpu_stage(nchunk - 1, 0)

    def score_products(c, slot):
        k4 = k4_ref[0, chunk(c), :]
        for p in range(IDX_HEADS // 2):
            mm_scr[slot, p] = _dot(k4, rhs_s[p])

    def score_keys(c, slot):
        acc = jnp.zeros((ck, tq), F32)
        for p in range(IDX_HEADS // 2):
            s = mm_scr[slot, p]
            acc = acc + jnp.maximum(s[:, :tq], 0.0) * wt[2 * p:2 * p + 1]
            acc = acc + jnp.maximum(s[:, tq:], 0.0) * wt[2 * p + 1:2 * p + 2]
        kpos = c * ck + lax.broadcasted_iota(I32, (ck, 1), 0)
        key_scr[chunk(c), :] = jnp.where(kpos <= qpos, _order_key(acc), INT_MIN)
    pipelined(score_products, score_keys)

    def store_keys(c, val):
        key_scr[chunk(c), :] = val
    thr = _select_threshold(lambda c: key_scr[chunk(c), :], store_keys, nchunk, tq, ck, topk, idx_bits,
                            keys_on_rows=True)

    qt = q_ref[0].astype(F32).T.astype(BF16)
    rhs_a = []
    for m in range(KV_HEADS):
        blk = jnp.concatenate([qt[(2 * m) * HEAD_DIM:(2 * m + 1) * HEAD_DIM],
                               qt[(2 * m + 1) * HEAD_DIM:(2 * m + 2) * HEAD_DIM]], axis=1)
        parts = [jnp.zeros((HEAD_DIM, 2 * tq), BF16)] * m + [blk] + [jnp.zeros((HEAD_DIM, 2 * tq), BF16)] * (KV_HEADS - 1 - m)
        rhs_a.append(jnp.concatenate(parts, axis=0))
    m_scr[...] = jnp.full(m_scr.shape, NEG_BIG, F32)
    acc_scr[...] = jnp.zeros(acc_scr.shape, F32)

    def logits(c, slot):
        kc = kb_ref[0, chunk(c), :]
        for m in range(KV_HEADS):
            mm_scr[slot, m] = _dot(kc, rhs_a[m])

    def softmax_pv(c, slot):
        sel = key_scr[chunk(c), :] > thr
        pas = []
        for m in range(KV_HEADS):
            lg = mm_scr[slot, m]
            lg = jnp.concatenate([jnp.where(sel, lg[:, :tq], NEG_BIG), jnp.where(sel, lg[:, tq:], NEG_BIG)], axis=1)
            mx = m_scr[m]
            mn = jnp.maximum(mx, jnp.max(lg, axis=0, keepdims=True))
            m_scr[m] = mn
            pas.append((jnp.exp2(lg - mn).astype(BF16), jnp.exp2(mx - mn)))
        for m in range(KV_HEADS):
            p, alpha = pas[m]
            acc_scr[m] = acc_scr[m] * alpha + _dot(vt_ref[0, m, :, chunk(c)], p)

    pipelined(logits, softmax_pv)
    pieces = []
    for m in range(KV_HEADS):
        acc = acc_scr[m]
        o = acc[0:HEAD_DIM] / acc[HEAD_DIM:HEAD_DIM + 1]
        pieces += [o[:, :tq], o[:, tq:]]
    at = jnp.concatenate(pieces, axis=0)
    at = at * lax.rsqrt(jnp.mean(at * at, axis=0, keepdims=True) + EPS)
    o_ref[0] = (at.T * ga_ref[...]).astype(BF16)


def _prompt_attention(q, qi, wt, k4, kb, vt, ga):
    b, t, _ = q.shape
    tq = LANES
    assert t % ATT_CHUNK == 0
    topk = min(TOPK_MAX, t // 4)
    blk = lambda w: pl.BlockSpec((1, tq, w), lambda bi, i: (bi, i, 0))
    full = lambda r, c: pl.BlockSpec((1, r, c), lambda bi, i: (bi, 0, 0))
    return pl.pallas_call(
        functools.partial(_prompt_attn_kernel, topk=topk, idx_bits=max(1, (t - 1).bit_length())),
        out_shape=jax.ShapeDtypeStruct((b, t, ATT_WIDTH), BF16),
        grid=(b, t // tq),
        in_specs=[blk(ATT_WIDTH), blk(ATT_WIDTH),
                  pl.BlockSpec((1, IDX_HEADS, tq), lambda bi, i: (bi, 0, i)),
                  full(t, 4 * IDX_DIM), full(t, KV_WIDTH),
                  pl.BlockSpec((1, KV_HEADS, VT_ROWS, t), lambda bi, i: (bi, 0, 0, 0)),
                  pl.BlockSpec((1, ATT_WIDTH), lambda bi, i: (0, 0))],
        out_specs=blk(ATT_WIDTH),
        scratch_shapes=[pltpu.VMEM((t, tq), I32), pltpu.VMEM((2, KV_HEADS, ATT_CHUNK, 2 * tq), F32),
                        pltpu.VMEM((KV_HEADS, 1, 2 * tq), F32),
                        pltpu.VMEM((KV_HEADS, VT_ROWS, 2 * tq), F32)],
        compiler_params=_cparams(("arbitrary", "arbitrary")),
        name="prompt_attn",
    )(q, qi, wt, k4, kb, vt, ga.reshape(1, ATT_WIDTH))


def _sample_score_kernel(pt_ref, qi_ref, w_ref, kin_ref, *rest, npg, nsteps, tnew, topk, idx_bits):
    page_refs = rest[:npg]
    key_ref, thr_ref = rest[npg:npg + 2]
    s = pl.program_id(1)
    last = s == nsteps
    tq = tnew
    page = page_refs[0].shape[1]
    ck = npg * page
    pages = [page_refs[p][...] for p in range(npg)]
    pages[0] = jnp.where(last, kin_ref[0], pages[0])
    k_hi, k_lo = _split_bf16(jnp.concatenate(pages, axis=1))
    sc = _dot(qi_ref[0], jnp.concatenate([k_hi, k_lo, k_hi], axis=0))
    sc = jnp.maximum(sc, 0.0) * w_ref[0]
    acc = sc[0:tq]
    for hh in range(1, IDX_HEADS):
        acc = acc + sc[hh * tq:(hh + 1) * tq]
    qrow = lax.broadcasted_iota(I32, (tq, ck), 0)
    col = lax.broadcasted_iota(I32, (tq, ck), 1)
    ok = jnp.logical_or(jnp.logical_not(last), jnp.logical_and(col <= qrow, col < tnew))
    key_ref[0, :, pl.ds(pl.multiple_of(s * ck, ck), ck)] = jnp.where(ok, _order_key(acc), INT_MIN)

    @pl.when(last)
    def _():
        def load_keys(c):
            return key_ref[0, :, pl.ds(pl.multiple_of(c * ck, ck), ck)]

        def store_keys(c, val):
            key_ref[0, :, pl.ds(pl.multiple_of(c * ck, ck), ck)] = val
        thr = _select_threshold(load_keys, store_keys, nsteps + 1, tq, ck, topk, idx_bits)
        thr_ref[0] = jnp.broadcast_to(thr, (tq, LANES))


def _sample_attn_kernel(pt_ref, q_ref, key_ref, thr_ref, kn_ref, vn_ref, ga_ref, *rest, npg, nsteps, tnew):
    k_refs = rest[:npg]
    v_refs = rest[npg:2 * npg]
    o_ref = rest[2 * npg]
    m_scr, l_scr, acc_scr = rest[2 * npg + 1:]
    s = pl.program_id(1)
    last = s == nsteps
    tq = tnew

    @pl.when(s == 0)
    def _():
        m_scr[...] = jnp.full(m_scr.shape, NEG_BIG, F32)
        l_scr[...] = jnp.zeros(l_scr.shape, F32)
        acc_scr[...] = jnp.zeros(acc_scr.shape, F32)

    sel = key_ref[0] > thr_ref[0][:, 0:1]
    sel2 = jnp.concatenate([sel, sel], axis=0)
    def pages(refs, new_ref, m):
        ps = [refs[p][m] for p in range(npg)]
        ps[0] = jnp.where(last, new_ref[0, m], ps[0])
        return jnp.concatenate(ps, axis=1).astype(BF16)

    lgs = [_dot(q_ref[0, m], pages(k_refs, kn_ref, m)) for m in range(KV_HEADS)]
    prs = []
    for m in range(KV_HEADS):
        lg = jnp.where(sel2, lgs[m], NEG_BIG)
        mx = m_scr[m]
        mn = jnp.maximum(mx, jnp.max(lg, axis=-1, keepdims=True))
        pr = jnp.exp2(lg - mn)
        alpha = jnp.exp2(mx - mn)
        l_scr[m] = l_scr[m] * alpha + jnp.sum(pr, axis=-1, keepdims=True)
        m_scr[m] = mn
        prs.append((pr.astype(BF16), alpha))
    for m in range(KV_HEADS):
        pr, alpha = prs[m]
        acc_scr[m] = acc_scr[m] * alpha + _dot_nt(pr, pages(v_refs, vn_ref, m))

    @pl.when(last)
    def _():
        pieces = []
        for m in range(KV_HEADS):
            o = acc_scr[m] / l_scr[m]
            pieces += [o[:tq], o[tq:]]
        a = jnp.concatenate(pieces, axis=1)
        a = a * lax.rsqrt(jnp.mean(a * a, axis=-1, keepdims=True) + EPS) * ga_ref[...]
        o_ref[0] = a.astype(BF16)


def _sample_attention(q, qi, wi, ki_new, k_new, v_new, cache_k, cache_v, cache_ki, layer, page_table, ga):
    db, tn, _ = q.shape
    page = cache_ki.shape[2]
    cache_k = cache_k.transpose(0, 1, 3, 4, 2)
    cache_v = cache_v.transpose(0, 1, 3, 4, 2)
    cache_ki = cache_ki.transpose(0, 1, 3, 2)
    n_pages = page_table.shape[1]
    npg_s = math.gcd(SAMPLE_SCORE_PAGES, n_pages)
    npg_a = math.gcd(SAMPLE_ATTN_PAGES, npg_s)
    nsteps_s, nsteps_a = n_pages // npg_s, n_pages // npg_a
    lpad = (nsteps_s + 1) * npg_s * page
    topk = min(TOPK_MAX, (n_pages * page + tn) // 4)
    assert tn <= page
    pad_keys = lambda a: jnp.pad(a, [(0, 0)] * (a.ndim - 1) + [(0, page - tn)])
    kin = pad_keys(ki_new.transpose(0, 2, 1))
    heads = lambda a: pad_keys(a.reshape(db, tn, KV_HEADS, HEAD_DIM).transpose(0, 2, 3, 1))
    qi_hq = qi.reshape(db, tn, IDX_HEADS, IDX_DIM).transpose(0, 2, 1, 3).reshape(db, IDX_HEADS * tn, IDX_DIM)
    qi_hi = qi_hq.astype(BF16)
    qi_hq = jnp.concatenate([qi_hi, qi_hi, (qi_hq - qi_hi.astype(F32)).astype(BF16)], axis=-1)
    w_hq = wi.transpose(0, 2, 1).reshape(db, IDX_HEADS * tn, 1)
    q2 = (q.reshape(db, tn, KV_HEADS, ATT_HEADS // KV_HEADS, HEAD_DIM).transpose(0, 2, 3, 1, 4)
          .reshape(db, KV_HEADS, (ATT_HEADS // KV_HEADS) * tn, HEAD_DIM))
    pt_flat = page_table.reshape(-1).astype(I32)

    def page_spec(tail, p, npg):
        def imap(b, s, pt):
            return (layer, pt[b * n_pages + jnp.minimum(s * npg + p, n_pages - 1)]) + (0,) * len(tail)
        return pl.BlockSpec((None, None) + tail, imap)

    per_b = lambda *tail: pl.BlockSpec((1,) + tail, lambda b, s, pt: (b,) + (0,) * len(tail))
    keys, thr = pl.pallas_call(
        functools.partial(_sample_score_kernel, npg=npg_s, nsteps=nsteps_s, tnew=tn, topk=topk,
                          idx_bits=max(1, (lpad - 1).bit_length())),
        out_shape=[jax.ShapeDtypeStruct((db, tn, lpad), I32),
                   jax.ShapeDtypeStruct((db, tn, LANES), I32)],
        grid_spec=pltpu.PrefetchScalarGridSpec(
            num_scalar_prefetch=1,
            grid=(db, nsteps_s + 1),
            in_specs=[per_b(IDX_HEADS * tn, 3 * IDX_DIM), per_b(IDX_HEADS * tn, 1), per_b(IDX_DIM, page)]
                     + [page_spec((IDX_DIM, page), p, npg_s) for p in range(npg_s)],
            out_specs=[per_b(tn, lpad), per_b(tn, LANES)]),
        compiler_params=_cparams(("arbitrary", "arbitrary")),
        name="sample_score",
    )(pt_flat, qi_hq, w_hq, kin, *([cache_ki] * npg_s))

    kv_pages = [page_spec((KV_HEADS, HEAD_DIM, page), p, npg_a) for p in range(npg_a)]
    a = pl.pallas_call(
        functools.partial(_sample_attn_kernel, npg=npg_a, nsteps=nsteps_a, tnew=tn),
        out_shape=jax.ShapeDtypeStruct((db, tn, ATT_WIDTH), BF16),
        grid_spec=pltpu.PrefetchScalarGridSpec(
            num_scalar_prefetch=1,
            grid=(db, nsteps_a + 1),
            in_specs=[per_b(KV_HEADS, (ATT_HEADS // KV_HEADS) * tn, HEAD_DIM),
                      pl.BlockSpec((1, tn, npg_a * page), lambda b, s, pt: (b, 0, s)),
                      per_b(tn, LANES),
                      per_b(KV_HEADS, HEAD_DIM, page), per_b(KV_HEADS, HEAD_DIM, page),
                      pl.BlockSpec((1, ATT_WIDTH), lambda b, s, pt: (0, 0))]
                     + kv_pages + kv_pages,
            out_specs=per_b(tn, ATT_WIDTH),
            scratch_shapes=[pltpu.VMEM((KV_HEADS, 2 * tn, 1), F32), pltpu.VMEM((KV_HEADS, 2 * tn, 1), F32),
                            pltpu.VMEM((KV_HEADS, 2 * tn, HEAD_DIM), F32)]),
        compiler_params=_cparams(("arbitrary", "arbitrary")),
        name="sample_attn",
    )(pt_flat, q2, keys, thr, heads(k_new), heads(v_new), ga.reshape(1, ATT_WIDTH),
      *([cache_k] * npg_a), *([cache_v] * npg_a))
    return a


def _outproj_kernel(a_ref, r_ref, x_ref, gt_ref, sc_ref, sh_ref, g_ref, wa_ref, wr_ref, wrt_ref, brt_ref,
                    x2_ref, h2_ref, route_ref):
    mix = _dot(a_ref[...], wa_ref[...]) + _dot(r_ref[...], wr_ref[...])
    x2 = x_ref[...] + gt_ref[0] * mix
    x2_ref[...] = x2
    h = x2 * lax.rsqrt(jnp.mean(x2 * x2, axis=-1, keepdims=True) + EPS) * g_ref[...]
    h = h * (1.0 + sc_ref[0]) + sh_ref[0]
    h2_ref[...] = h
    lg = (_dot3(h, wrt_ref[...]) + brt_ref[...]).T
    grp = [lg[g:g + 1] for g in range(N_GROUPS)]
    gmax = functools.reduce(jnp.maximum, grp)
    gden = functools.reduce(lambda u, v: u + v, [jnp.exp(g - gmax) for g in grp])
    gsel = jnp.full(gmax.shape, N_GROUPS - 1, I32)
    for g in range(N_GROUPS - 2, -1, -1):
        gsel = jnp.where(grp[g] == gmax, g, gsel)
    gw = 1.0 / gden
    el = []
    for e in range(EXPERTS_PER_GROUP):
        v = lg[N_GROUPS + e:N_GROUPS + e + 1]
        for g in range(1, N_GROUPS):
            row = N_GROUPS + g * EXPERTS_PER_GROUP + e
            v = jnp.where(gsel == g, lg[row:row + 1], v)
        el.append(v)
    emax = functools.reduce(jnp.maximum, el)
    e0 = jnp.full(emax.shape, EXPERTS_PER_GROUP - 1, I32)
    for e in range(EXPERTS_PER_GROUP - 2, -1, -1):
        e0 = jnp.where(el[e] == emax, e, e0)
    rest = [jnp.where(e0 == e, -jnp.inf, el[e]) for e in range(EXPERTS_PER_GROUP)]
    rmax = functools.reduce(jnp.maximum, rest)
    e1 = jnp.full(emax.shape, EXPERTS_PER_GROUP - 1, I32)
    for e in range(EXPERTS_PER_GROUP - 2, -1, -1):
        e1 = jnp.where(jnp.logical_and(rest[e] == rmax, e0 != e), e, e1)
    p1 = jnp.exp(rmax - emax)
    w0 = gw * (1.0 / (1.0 + p1))
    w1 = gw * (p1 / (1.0 + p1))
    base = gsel * EXPERTS_PER_GROUP
    rid = lax.broadcasted_iota(I32, lg.shape, 0)
    rt = jnp.where(rid == 0, (base + e0).astype(F32),
                   jnp.where(rid == 1, (base + e1).astype(F32),
                             jnp.where(rid == 2, w0, jnp.where(rid == 3, w1, 0.0))))
    route_ref[...] = rt.T


def _outproj(a, r, x2d, gt, sc, sh, g2, wa, wr, wrt, brt, tiles_per_group):
    n, d = x2d.shape
    tm = min(PROJ_TILE, n)
    rr = gt.shape[1]
    row = lambda w: pl.BlockSpec((tm, w), lambda i: (i, 0))
    mod = pl.BlockSpec((1, rr, d), lambda i: (i // tiles_per_group, 0, 0))
    const = lambda s: pl.BlockSpec(s, lambda i: (0, 0))
    return pl.pallas_call(
        _outproj_kernel,
        out_shape=[jax.ShapeDtypeStruct((n, d), F32), jax.ShapeDtypeStruct((n, d), F32),
                   jax.ShapeDtypeStruct((n, LANES), F32)],
        grid=(n // tm,),
        in_specs=[row(ATT_WIDTH), row(HG_WIDTH), row(d), mod, mod, mod, const((1, d)),
                  const(wa.shape), const(wr.shape), const(wrt.shape), const(brt.shape)],
        out_specs=[row(d), row(d), row(LANES)],
        compiler_params=_cparams(("arbitrary",)),
        name="outproj",
    )(a, r, x2d, gt, sc, sh, g2.reshape(1, d), wa, wr, wrt, brt)


def _gather_pipeline(step, nsteps, idx_hbm, src_hbm, buf, idx_smem, isem, rsem):
    nrows = buf.shape[1]
    slot = step % 2

    def idx_copy(b, sl):
        return pltpu.make_async_copy(idx_hbm.at[b], idx_smem.at[sl], isem.at[sl])

    def start_rows(sl):
        for r in range(nrows):
            pltpu.make_async_copy(src_hbm.at[pl.ds(idx_smem[sl, 0, r], 1), :],
                                  buf.at[sl, pl.ds(r, 1), :], rsem.at[sl]).start()

    @pl.when(step == 0)
    def _():
        idx_copy(0, 0).start()
        idx_copy(0, 0).wait()
        start_rows(0)

        @pl.when(nsteps > 1)
        def _():
            idx_copy(1, 1).start()

    @pl.when(step + 1 < nsteps)
    def _():
        idx_copy(step + 1, 1 - slot).wait()
        start_rows(1 - slot)

    @pl.when(step + 2 < nsteps)
    def _():
        idx_copy(step + 2, slot).start()

    pltpu.make_async_copy(src_hbm.at[pl.ds(0, nrows), :], buf.at[slot], rsem.at[slot]).wait()
    return slot


def _ffn_kernel(be_ref, nb_ref, tok_ref, h_ref, w1_ref, w3_ref, w2_ref, y_ref, xbuf, idx_smem, isem, rsem):
    j = pl.program_id(0)

    @pl.when(j < nb_ref[0])
    def _():
        slot = _gather_pipeline(j, nb_ref[0], tok_ref, h_ref, xbuf, idx_smem, isem, rsem)
        xb = xbuf[slot].astype(BF16)
        u = _dot(xb, w1_ref[0])
        g = _dot(xb, w3_ref[0])
        y_ref[...] = _dot((_silu(u) * g).astype(BF16), w2_ref[0])

    @pl.when(j >= nb_ref[0])
    def _():
        y_ref[...] = jnp.zeros(y_ref.shape, F32)


def _ffn(blk_e, n_used, tok, h2, w1, w3, w2, blk):
    nb = tok.shape[0]
    d = h2.shape[1]
    wspec = lambda s: pl.BlockSpec((1,) + s, lambda j, be, nu: (be[j], 0, 0))
    return pl.pallas_call(
        _ffn_kernel,
        out_shape=jax.ShapeDtypeStruct((nb * blk, d), F32),
        grid_spec=pltpu.PrefetchScalarGridSpec(
            num_scalar_prefetch=2,
            grid=(nb,),
            in_specs=[pl.BlockSpec(memory_space=pl.ANY), pl.BlockSpec(memory_space=pl.ANY),
                      wspec(w1.shape[1:]), wspec(w3.shape[1:]), wspec(w2.shape[1:])],
            out_specs=pl.BlockSpec((blk, d), lambda j, be, nu: (j, 0)),
            scratch_shapes=[pltpu.VMEM((2, blk, d), F32), pltpu.SMEM((2, 1, blk), I32),
                            pltpu.SemaphoreType.DMA((2,)), pltpu.SemaphoreType.DMA((2,))]),
        compiler_params=_cparams(("arbitrary",)),
        name="moe_ffn",
    )(blk_e, n_used, tok, h2, w1, w3, w2)


def _combine_kernel(dd_ref, y_ref, x2_ref, route_ref, gt_ref, sc_ref, sh_ref, g_ref, o_ref,
                    ybuf, idx_smem, isem, rsem):
    tm = x2_ref.shape[0]
    slot = _gather_pipeline(pl.program_id(0), pl.num_programs(0), dd_ref, y_ref, ybuf, idx_smem, isem, rsem)
    rt = route_ref[...]
    moe = ybuf[slot, 0:tm] * rt[:, 2:3] + ybuf[slot, tm:2 * tm] * rt[:, 3:4]
    x = x2_ref[...] + gt_ref[0] * moe
    y = x * lax.rsqrt(jnp.mean(x * x, axis=-1, keepdims=True) + EPS) * g_ref[...]
    o_ref[...] = y * (1.0 + sc_ref[0]) + sh_ref[0]


def _combine(dd, yb, x2, route, gt, sc, sh, gf, tiles_per_group):
    n, d = x2.shape
    tm = dd.shape[2] // 2
    rr = gt.shape[1]
    row = pl.BlockSpec((tm, d), lambda i: (i, 0))
    mod = pl.BlockSpec((1, rr, d), lambda i: (i // tiles_per_group, 0, 0))
    hbm = pl.BlockSpec(memory_space=pl.ANY)
    return pl.pallas_call(
        _combine_kernel,
        out_shape=jax.ShapeDtypeStruct((n, d), F32),
        grid=(n // tm,),
        in_specs=[hbm, hbm, row, pl.BlockSpec((tm, LANES), lambda i: (i, 0)), mod, mod, mod,
                  pl.BlockSpec((1, d), lambda i: (0, 0))],
        out_specs=row,
        scratch_shapes=[pltpu.VMEM((2, 2 * tm, d), F32), pltpu.SMEM((2, 1, 2 * tm), I32),
                        pltpu.SemaphoreType.DMA((2,)), pltpu.SemaphoreType.DMA((2,))],
        compiler_params=_cparams(("arbitrary",)),
        name="moe_combine",
    )(dd, yb, x2, route, gt, sc, sh, gf.reshape(1, d))


def _dispatch(route, blk):
    n = route.shape[0]
    flat_e = route[:, 0:2].astype(I32).reshape(-1)
    a = flat_e.shape[0]
    onehot = (flat_e[:, None] == jnp.arange(N_EXPERTS, dtype=I32)[None, :]).astype(I32)
    csum = jnp.cumsum(onehot, axis=0)
    rank = jnp.sum((csum - onehot) * onehot, axis=1)
    counts = csum[-1]
    padded = (counts + blk - 1) // blk * blk
    pad_end = jnp.cumsum(padded)
    pad_start = pad_end - padded
    dest = pad_start[flat_e] + rank
    nb = -(-a // blk) + N_EXPERTS
    tok = jnp.zeros((nb * blk,), I32).at[dest].set(jnp.arange(a, dtype=I32) // 2)
    blk_e = jnp.minimum(jnp.searchsorted(pad_end, jnp.arange(nb, dtype=I32) * blk, side='right'),
                        N_EXPERTS - 1).astype(I32)
    n_used = (pad_end[-1] // blk).astype(I32).reshape(1)
    dest2 = dest.reshape(n, 2)
    return tok.reshape(nb, 1, blk), blk_e, n_used, dest2[:, 0], dest2[:, 1]


def _mods(m, n_chunks, per_token_rows):
    parts = jnp.split(m, n_chunks, axis=-1)
    if per_token_rows is None:
        return [p[:, None, :] for p in parts]
    g, d = parts[0].shape
    tm = min(PROJ_TILE, g * per_token_rows)
    return [jnp.repeat(p, per_token_rows, axis=0).reshape(-1, tm, d) for p in parts]


def _layer(x, mod6, modf, pos, s0, attend, lb, wts, final_g, per_token):
    b, t, d = x.shape
    n = b * t
    (norm1_g, norm2_g, w_r, gk, ga, hg_out_g, wa, wr, wrt, brt, w1, w3, w2) = wts
    tm = min(PROJ_TILE, n)
    tiles_per_group = 1 if per_token else t // tm
    sh1, sc1, gt1, sh2, sc2, gt2 = _mods(mod6, 6, t if per_token else None)
    shf, scf = _mods(modf, 2, t if per_token else None)
    tabs = _rope_tables(jnp.tile(pos, tm // t) if per_token else pos)
    x2d = x.reshape(n, d)
    sq = lambda arr: arr.reshape(b, t, arr.shape[-1])
    if per_token:
        q, k, v, qi, ki, wi, hq, hf, hi, hg = _proj(x2d, sc1, sh1, norm1_g, w_r, gk, tabs, tiles_per_group, 0)
        a = attend(sq(q), sq(qi), sq(wi), sq(ki), sq(k), sq(v), ga)
        k_out, v_out, ki_out = k.reshape(b, t, KV_HEADS, HEAD_DIM), v.reshape(b, t, KV_HEADS, HEAD_DIM), sq(ki)
    else:
        q, kt, vt, kb, vto, qi, kit, k4, wt, hq, hf, hi, hg = _proj(x2d, sc1, sh1, norm1_g, w_r, gk, tabs,
                                                                   tiles_per_group, b)
        a = attend(sq(q), sq(qi), wt, sq(k4), sq(kb), vto, ga)
        heads = lambda xt: xt.reshape(b, KV_HEADS, HEAD_DIM, t).transpose(0, 3, 1, 2)
        k_out, v_out, ki_out = heads(kt), heads(vt), kit.transpose(0, 2, 1)
    r, s_t = _hgrn(sq(hq), sq(hf), sq(hi), sq(hg), lb, hg_out_g, s0)
    x2, h2, route = _outproj(a.reshape(n, -1), r.reshape(n, -1), x2d, gt1, sc2, sh2, norm2_g,
                             wa, wr, wrt, brt, tiles_per_group)
    blk = MOE_BLOCK if n >= 8 * MOE_BLOCK else 64
    tok, blk_e, n_used, d0, d1 = _dispatch(route, blk)
    yb = _ffn(blk_e, n_used, tok, h2, w1, w3, w2, blk)
    dd = jnp.concatenate([d0.reshape(n // tm, 1, tm), d1.reshape(n // tm, 1, tm)], axis=2)
    y = _combine(dd, yb, x2, route, gt2, scf, shf, final_g, tiles_per_group)
    return y.reshape(b, t, d), k_out, v_out, ki_out, s_t


def kernel(x_prompt, x_sample, cache_k, cache_v, cache_kidx, state_hgrn, page_table, c_prompt, c_sample,
           ada_w, ada_b, norm1_g, norm2_g, w_in, idx_k_g, hg_lb_logits, attn_out_g, hg_out_g, w_out,
           w_group, b_group, w_expert_router, b_expert_router, w1, w3, w2, final_g, ada_final_w, ada_final_b):
    depth = ada_w.shape[0]
    assert depth == 1, "the final adaLN norm is fused into the (single) layer's combine kernel"
    bp, tp, d = x_prompt.shape
    bs, ts, _ = x_sample.shape
    past = page_table.shape[1] * cache_kidx.shape[2]
    lb_all = jnp.cumsum(jax.nn.softmax(hg_lb_logits.astype(F32), axis=0), axis=0)
    n_c = bp + bs
    c_all = jnp.concatenate([c_prompt, c_sample, jnp.zeros((-n_c % 16, d), F32)], axis=0)
    modf = _ada(c_all, ada_final_w, ada_final_b)
    l = 0
    mod6 = _ada(c_all, ada_w[l], ada_b[l])

    seg = [0]
    for s in (ATT_WIDTH, KV_WIDTH, KV_WIDTH, IDX_HEADS * IDX_DIM, IDX_DIM, IDX_HEADS,
              HG_WIDTH, HG_WIDTH, HG_WIDTH, HG_WIDTH):
        seg.append(seg[-1] + s)
    wl = w_in[l]
    col = lambda i: wl[:, seg[i]:seg[i + 1]]
    zpad = lambda w: jnp.zeros((d, w), wl.dtype)
    w_r = jnp.concatenate([col(0), col(1), col(2), col(3), col(6), col(7), col(8), col(9),
                           col(4), zpad(LANES - IDX_DIM), col(5), zpad(LANES - IDX_HEADS)], axis=1).astype(BF16)
    gk = jnp.concatenate([idx_k_g[l], jnp.zeros((LANES - IDX_DIM,), F32)]).reshape(1, LANES)
    wa = w_out[l, :ATT_WIDTH].astype(BF16)
    wr = w_out[l, ATT_WIDTH:].astype(BF16)
    n_rt = N_GROUPS + N_EXPERTS
    wrt = jnp.concatenate([w_group[l], w_expert_router[l], jnp.zeros((d, LANES - n_rt), F32)], axis=1)
    brt = jnp.concatenate([b_group[l], b_expert_router[l], jnp.zeros((LANES - n_rt,), F32)]).reshape(1, LANES)
    wts = (norm1_g[l], norm2_g[l], w_r, gk, attn_out_g[l], hg_out_g[l], wa, wr, wrt, brt,
           w1[l].astype(BF16), w3[l].astype(BF16), w2[l].astype(BF16))

    def attend_s(q, qi, wi, ki, k, v, ga):
        return _sample_attention(q, qi, wi, ki, k, v, cache_k, cache_v, cache_kidx, l, page_table, ga)

    s0_p = jnp.zeros((bp, HG_HEADS, HG_DK, HG_DV), F32)
    yp, kp, vp, kip, sp = _layer(x_prompt, mod6[:bp], modf[:bp], jnp.arange(tp), s0_p, _prompt_attention,
                                 lb_all[l], wts, final_g, per_token=False)
    ys, ks, vs, kis, ss = _layer(x_sample, mod6[bp:n_c], modf[bp:n_c], past + jnp.arange(ts), state_hgrn[l],
                                 attend_s, lb_all[l], wts, final_g, per_token=True)
    return (yp, ys, kp[None], vp[None], kip[None], sp[None], ks[None], vs[None], kis[None], ss[None])
```

```python
import functools
import math

import jax
import jax.numpy as jnp
from jax import lax
from jax.experimental import pallas as pl
from jax.experimental.pallas import tpu as pltpu

F32 = jnp.float32
BF16 = jnp.bfloat16
I32 = jnp.int32

ATT_HEADS = 8
KV_HEADS = 4
HEAD_DIM = 64
ATT_WIDTH = ATT_HEADS * HEAD_DIM
KV_WIDTH = KV_HEADS * HEAD_DIM
ROT_HALF = HEAD_DIM // 8
ROPE_THETA = 500000.0
IDX_HEADS = 8
IDX_DIM = 64
TOPK_MAX = 256
HG_HEADS = 4
HG_DK = 128
HG_DV = 128
HG_WIDTH = HG_HEADS * HG_DV
HG_CHUNK = 64
N_GROUPS = 4
EXPERTS_PER_GROUP = 4
N_EXPERTS = N_GROUPS * EXPERTS_PER_GROUP
D_EXPERT = 512
MOE_BLOCK = 256
EPS = 1e-6

LANES = 128
INT_MIN = -2 ** 31
NEG_BIG = -1e30
VMEM_LIMIT = 56 * 1024 * 1024
PROJ_TILE = 256
ATT_CHUNK = 1024
VT_ROWS = HEAD_DIM + 16
LOG2E = 1.4426950408889634
BISECT_BITS_PER_TRIP = 4
SAMPLE_SCORE_PAGES = 16
SAMPLE_ATTN_PAGES = 16

_Z_Q, _Z_K, _Z_V, _Z_QI, _Z_HQ, _Z_HF, _Z_HI, _Z_HG, _Z_KI, _Z_WI, _Z_END = (
    0, 512, 768, 1024, 1536, 2048, 2560, 3072, 3584, 3712, 3840)


def _cparams(sem):
    return pltpu.CompilerParams(dimension_semantics=sem, vmem_limit_bytes=VMEM_LIMIT)


def _split_bf16(x):
    hi = x.astype(BF16)
    lo = (x - hi.astype(F32)).astype(BF16)
    return hi, lo


def _dot(a, b):
    return jnp.dot(a, b, preferred_element_type=F32)


def _dot_nt(a, b):
    return lax.dot_general(a, b, (((1,), (1,)), ((), ())), preferred_element_type=F32)


def _dot_tn(a, b):
    return lax.dot_general(a, b, (((0,), (0,)), ((), ())), preferred_element_type=F32)


def _dot3(a, b):
    ah, al = _split_bf16(a)
    bh, bl = _split_bf16(b)
    return _dot(ah, bh) + (_dot(ah, bl) + _dot(al, bh))


def _dot3_nt(a, b):
    ah, al = _split_bf16(a)
    bh, bl = _split_bf16(b)
    return _dot_nt(ah, bh) + (_dot_nt(ah, bl) + _dot_nt(al, bh))


def _silu(x):
    return x * (1.0 / (1.0 + jnp.exp(-x)))


def _sigmoid(x):
    return 1.0 / (1.0 + jnp.exp(-x))


def _ada_kernel(c_ref, w_ref, b_ref, o_ref):
    o_ref[...] = _dot3(_silu(c_ref[...]), w_ref[...]) + b_ref[...]


def _ada(c, w, b):
    r, d = c.shape
    e = w.shape[1]
    te = 1024
    return pl.pallas_call(
        _ada_kernel,
        out_shape=jax.ShapeDtypeStruct((r, e), F32),
        grid=(e // te,),
        in_specs=[pl.BlockSpec((r, d), lambda j: (0, 0)),
                  pl.BlockSpec((d, te), lambda j: (0, j)),
                  pl.BlockSpec((1, te), lambda j: (0, j))],
        out_specs=pl.BlockSpec((r, te), lambda j: (0, j)),
        compiler_params=_cparams(("arbitrary",)),
        name="ada",
    )(c, w, b.reshape(1, e))


def _rope_tables(pos):
    p = pos.shape[0]
    inv = ROPE_THETA ** (-jnp.arange(ROT_HALF, dtype=F32) * (2.0 / (2 * ROT_HALF)))
    ang = pos.astype(F32)[:, None] * inv[None, :]
    c, s = jnp.cos(ang), jnp.sin(ang)
    rest = HEAD_DIM - 2 * ROT_HALF
    one, zero, z8 = jnp.ones((p, rest), F32), jnp.zeros((p, rest), F32), jnp.zeros((p, ROT_HALF), F32)
    cos64 = jnp.concatenate([c, c, one], axis=1)
    sa64 = jnp.concatenate([-s, z8, zero], axis=1)
    sb64 = jnp.concatenate([z8, s, zero], axis=1)
    dup = lambda t: jnp.concatenate([t, t], axis=1)
    return dup(cos64), dup(sa64), dup(sb64)


def _proj_kernel(x_ref, sc_ref, sh_ref, g_ref, w_ref, gk_ref, cos_ref, sa_ref, sb_ref, *out_refs, seq_major):
    x = x_ref[...]
    ms = jnp.mean(x * x, axis=-1, keepdims=True)
    h = x * lax.rsqrt(ms + EPS) * g_ref[...]
    h = h * (1.0 + sc_ref[0]) + sh_ref[0]
    z = _dot(h.astype(BF16), w_ref[...])
    cos, sa, sb = cos_ref[...], sa_ref[...], sb_ref[...]

    def rope(zs):
        n = zs.shape[1] // LANES
        rep = (lambda t: jnp.concatenate([t] * n, axis=1)) if n > 1 else (lambda t: t)
        w = zs.shape[1]
        return (zs * rep(cos) + pltpu.roll(zs, w - ROT_HALF, 1) * rep(sa)
                + pltpu.roll(zs, ROT_HALF, 1) * rep(sb))

    q = (rope(z[:, _Z_Q:_Z_K]) * (HEAD_DIM ** -0.5 * LOG2E)).astype(BF16)
    k = rope(z[:, _Z_K:_Z_V])
    v = z[:, _Z_V:_Z_QI]
    qi = rope(z[:, _Z_QI:_Z_HQ]) * (IDX_DIM ** -0.5)
    ks = z[:, _Z_KI:_Z_WI]
    kms = jnp.sum(ks * ks, axis=-1, keepdims=True) * (1.0 / IDX_DIM)
    kn = rope(ks * lax.rsqrt(kms + EPS) * gk_ref[...])
    ws = z[:, _Z_WI:_Z_END] * (IDX_HEADS ** -0.5)
    if seq_major:
        q_ref, kt_ref, vt_ref, kb_ref, vto_ref, qi_ref, kit_ref, k4_ref, wt_ref = out_refs[:9]
        kt_ref[0] = k.T
        vt = v.T
        vt_ref[0] = vt
        kb_ref[...] = k.astype(BF16)
        for m in range(KV_HEADS):
            vto_ref[0, m, 0:HEAD_DIM, :] = vt[m * HEAD_DIM:(m + 1) * HEAD_DIM].astype(BF16)
            vto_ref[0, m, HEAD_DIM:VT_ROWS, :] = jnp.ones((VT_ROWS - HEAD_DIM, x.shape[0]), BF16)
        kit_ref[0] = kn.T[:IDX_DIM]
        hi = kn.astype(BF16).astype(F32)
        lo = kn - hi
        k4_ref[...] = jnp.concatenate([hi + pltpu.roll(lo, IDX_DIM, 1), hi + pltpu.roll(hi, IDX_DIM, 1)],
                                      axis=1).astype(BF16)
        wt_ref[0] = ws.T[:IDX_HEADS]
    else:
        q_ref, k_ref, v_ref, qi_ref, ki_ref, wi_ref = out_refs[:6]
        k_ref[...] = k
        v_ref[...] = v
        ki_ref[...] = kn[:, :IDX_DIM]
        wi_ref[...] = ws[:, :IDX_HEADS]
    q_ref[...] = q
    qi_ref[...] = qi
    hq_ref, hf_ref, hi_ref, hg_ref = out_refs[-4:]
    hq_ref[...] = z[:, _Z_HQ:_Z_HF]
    hf_ref[...] = z[:, _Z_HF:_Z_HI]
    hi_ref[...] = z[:, _Z_HI:_Z_HG]
    hg_ref[...] = z[:, _Z_HG:_Z_KI]


def _proj(x2d, sc, sh, g1, w_r, gk, tabs, tiles_per_group, seq_batch):
    n, d = x2d.shape
    tm = min(PROJ_TILE, n)
    nt = n // tm
    r = sc.shape[1]
    pt = tabs[0].shape[0] // tm
    row = lambda w, t: (jax.ShapeDtypeStruct((n, w), t), pl.BlockSpec((tm, w), lambda i: (i, 0)))
    mod = pl.BlockSpec((1, r, d), lambda i: (i // tiles_per_group, 0, 0))
    tab = pl.BlockSpec((tm, LANES), lambda i: (i % pt, 0))
    if seq_batch:
        t = n // seq_batch
        tps = t // tm
        seq = lambda w: (jax.ShapeDtypeStruct((seq_batch, w, t), F32),
                         pl.BlockSpec((1, w, tm), lambda i: (i // tps, 0, i % tps)))
        vto = (jax.ShapeDtypeStruct((seq_batch, KV_HEADS, VT_ROWS, t), BF16),
               pl.BlockSpec((1, KV_HEADS, VT_ROWS, tm), lambda i: (i // tps, 0, 0, i % tps)))
        outs = [row(ATT_WIDTH, BF16), seq(KV_WIDTH), seq(KV_WIDTH), row(KV_WIDTH, BF16), vto,
                row(ATT_WIDTH, F32), seq(IDX_DIM), row(4 * IDX_DIM, BF16), seq(IDX_HEADS)]
    else:
        outs = [row(ATT_WIDTH, BF16), row(KV_WIDTH, F32), row(KV_WIDTH, F32),
                row(ATT_WIDTH, F32), row(IDX_DIM, F32), row(IDX_HEADS, F32)]
    outs += [row(HG_WIDTH, F32)] * 4
    return pl.pallas_call(
        functools.partial(_proj_kernel, seq_major=bool(seq_batch)),
        out_shape=[s for s, _ in outs],
        grid=(nt,),
        in_specs=[pl.BlockSpec((tm, d), lambda i: (i, 0)), mod, mod,
                  pl.BlockSpec((1, d), lambda i: (0, 0)),
                  pl.BlockSpec(w_r.shape, lambda i: (0, 0)),
                  pl.BlockSpec((1, LANES), lambda i: (0, 0)),
                  tab, tab, tab],
        out_specs=[b for _, b in outs],
        compiler_params=_cparams(("arbitrary",)),
        name="proj",
    )(x2d, sc, sh, g1.reshape(1, d), w_r, gk, *tabs)


def _hgrn_kernel(hq_ref, hf_ref, hi_ref, hg_ref, lb_ref, g_ref, s0_ref, r_ref, st_ref, s_scr, *, chunk, nchunk):
    t = pl.program_id(1)

    @pl.when(t == 0)
    def _():
        for hh in range(HG_HEADS):
            s_scr[hh] = s0_ref[0, hh].T

    tt = chunk * nchunk
    row = lax.broadcasted_iota(I32, (tt, tt), 0)
    col = lax.broadcasted_iota(I32, (tt, tt), 1)
    causal = jnp.logical_and(col <= row, row // chunk == col // chunk)
    tri = causal.astype(BF16)
    rows = lambda c: slice(c * chunk, (c + 1) * chunk)
    lanes = lambda hh: slice(hh * HG_DK, (hh + 1) * HG_DK)
    lb = lb_ref[...]
    f = lb + (1.0 - lb) * _sigmoid(hf_ref[0])
    logf = jnp.log(f)
    kk = 1.0 - f
    l0 = logf.astype(BF16)
    r1 = logf - l0.astype(F32)
    l1 = r1.astype(BF16)
    l2 = (r1 - l1.astype(F32)).astype(BF16)
    b = _dot(tri, l0) + (_dot(tri, l1) + _dot(tri, l2))
    bls = [b[(c + 1) * chunk - 1:(c + 1) * chunk, :] for c in range(nchunk)]
    bl_rows = jnp.concatenate([jnp.broadcast_to(bl, (chunk, bl.shape[1])) for bl in bls], axis=0)
    q_in = (hq_ref[0] * (HG_DK ** -0.5) * jnp.exp(b)).astype(BF16)
    k_in = (kk * jnp.exp(-b)).astype(BF16)
    k_st = (kk * jnp.exp(bl_rows - b)).astype(BF16)
    vb = hi_ref[0].astype(BF16)
    intra = []
    for hh in range(HG_HEADS):
        att = jnp.where(causal, _dot_nt(q_in[:, lanes(hh)], k_in[:, lanes(hh)]), 0.0)
        intra.append(_dot(att.astype(BF16), vb[:, lanes(hh)]))
    upd = {(c, hh): _dot_tn(vb[rows(c), lanes(hh)], k_st[rows(c), lanes(hh)])
           for c in range(nchunk) for hh in range(HG_HEADS)}
    decay = [jnp.exp(bl) for bl in bls]
    for hh in range(HG_HEADS):
        os = []
        for c in range(nchunk):
            st = s_scr[hh]
            os.append(_dot_nt(q_in[rows(c), lanes(hh)], st.astype(BF16)))
            s_scr[hh] = st * decay[c][:, lanes(hh)] + upd[c, hh]
        o = jnp.concatenate(os, axis=0) + intra[hh]
        on = o * lax.rsqrt(jnp.mean(o * o, axis=-1, keepdims=True) + EPS) * g_ref[:, lanes(hh)]
        r_ref[0, :, lanes(hh)] = (on * _silu(hg_ref[0, :, lanes(hh)])).astype(BF16)

    @pl.when(t == pl.num_programs(1) - 1)
    def _():
        for hh in range(HG_HEADS):
            st_ref[0, hh] = s_scr[hh].T


def _hgrn(hq, hf, hi, hg, lb, out_g, s0):
    b, t, w = hq.shape
    chunk = min(HG_CHUNK, t)
    assert t % chunk == 0
    nchunk = min(4, t // chunk)
    tt = chunk * nchunk
    seq = pl.BlockSpec((1, tt, w), lambda i, j: (i, j, 0))
    vec = pl.BlockSpec((1, w), lambda i, j: (0, 0))
    state = pl.BlockSpec((1, HG_HEADS, HG_DK, HG_DV), lambda i, j: (i, 0, 0, 0))
    return pl.pallas_call(
        functools.partial(_hgrn_kernel, chunk=chunk, nchunk=nchunk),
        out_shape=[jax.ShapeDtypeStruct((b, t, w), BF16), jax.ShapeDtypeStruct(s0.shape, F32)],
        grid=(b, t // tt),
        in_specs=[seq, seq, seq, seq, vec, vec, state],
        out_specs=[seq, state],
        scratch_shapes=[pltpu.VMEM((HG_HEADS, HG_DV, HG_DK), F32)],
        compiler_params=_cparams(("arbitrary", "arbitrary")),
        name="hgrn",
    )(hq, hf, hi, hg, lb.reshape(1, w), out_g.reshape(1, w), s0)


def _order_key(score):
    bits = pltpu.bitcast(score, I32)
    bits = jnp.where(bits == INT_MIN, 0, bits)
    return jnp.where(bits >= 0, bits, bits ^ 0x7FFFFFFF)


def _select_threshold(load_keys, store_keys, nchunk, tq, ck, topk, idx_bits, keys_on_rows=False):
    kax = 0 if keys_on_rows else 1
    kshape = (ck, tq) if keys_on_rows else (tq, ck)
    vshape = (1, tq) if keys_on_rows else (tq, 1)

    def count(pred):
        def body(c, acc):
            m = jnp.where(pred(load_keys(c), c), 1.0, 0.0)
            if keys_on_rows:
                parts = [m[j * 8:(j + 1) * 8] for j in range(8)]
                for j in range(8, ck // 8):
                    parts[j % 8] = parts[j % 8] + m[j * 8:(j + 1) * 8]
                while len(parts) > 1:
                    parts = [a + b for a, b in zip(parts[0::2], parts[1::2])]
                part = parts[0]
            else:
                parts = [m[:, j * LANES:(j + 1) * LANES] for j in range(ck // LANES)]
                while len(parts) > 1:
                    parts = [a + b for a, b in zip(parts[0::2], parts[1::2])] + ([parts[-1]] if len(parts) % 2 else [])
                part = parts[0]
            return acc + part
        acc = jnp.zeros((8, tq) if keys_on_rows else (tq, LANES), F32)
        if isinstance(nchunk, int):
            for c in range(nchunk):
                acc = body(c, acc)
        else:
            acc = lax.fori_loop(0, nchunk, body, acc)
        return jnp.sum(acc, axis=kax, keepdims=True)

    kf = float(topk)
    zero_i = jnp.zeros(vshape, I32)

    def bisect(nbits, count_ge, done0):
        def cond(st):
            return jnp.logical_and(st[0] < nbits, jnp.min(st[2]) < 0.5)

        def body(st):
            it, tu, done, hit_u = st
            for _ in range(BISECT_BITS_PER_TRIP):
                cand = tu | lax.shift_left(jnp.int32(1), nbits - 1 - it)
                cnt = count_ge(cand)
                active = done < 0.5
                tu = jnp.where(jnp.logical_and(active, cnt >= kf), cand, tu)
                hit = jnp.logical_and(active, cnt == kf)
                hit_u = jnp.where(hit, cand, hit_u)
                done = jnp.where(hit, 1.0, done)
                it = it + 1
            return it, tu, done, hit_u
        return lax.while_loop(cond, body, (jnp.int32(0), zero_i, done0, zero_i))[1:]

    tu, done, hit_u = bisect(32, lambda u: count(lambda key, c: key >= (u ^ INT_MIN)), jnp.zeros(vshape, F32))
    ts = tu ^ INT_MIN
    tsel = (hit_u ^ INT_MIN) - 1
    need_tie = jnp.logical_and(done < 0.5, tu != 0)
    thr = jnp.where(done > 0.5, tsel, ts)

    @pl.when(jnp.max(jnp.where(need_tie, 1.0, 0.0)) > 0.5)
    def _():
        rank = kf - count(lambda key, c: key > ts)
        pos_of = lambda c: c * ck + lax.broadcasted_iota(I32, kshape, kax)

        def jbody(it, ju):
            cand = ju | lax.shift_left(jnp.int32(1), idx_bits - 1 - it)
            pred = lambda key, c: jnp.logical_and(key == ts, pos_of(c) < cand)
            return jnp.where(count(pred) < rank, cand, ju)
        ju = lax.fori_loop(0, idx_bits, jbody, zero_i)

        def fix(c, carry):
            key = load_keys(c)
            bump = jnp.logical_and(jnp.logical_and(key == ts, pos_of(c) <= ju), need_tie)
            store_keys(c, jnp.where(bump, key + 1, key))
            return carry
        lax.fori_loop(0, nchunk, fix, 0)
    return thr


def _prompt_attn_kernel(q_ref, qi_ref, wt_ref, k4_ref, kb_ref, vt_ref, ga_ref, o_ref,
                        key_scr, mm_scr, m_scr, acc_scr, *, topk, idx_bits):
    tq, ck = q_ref.shape[1], ATT_CHUNK
    i = pl.program_id(1)
    nchunk = (i * tq + tq + ck - 1) // ck
    qpos = i * tq + lax.broadcasted_iota(I32, (1, tq), 1)

    qit = qi_ref[0].T
    hit = qit.astype(BF16)
    lot = (qit - hit.astype(F32)).astype(BF16)
    zero = jnp.zeros((IDX_DIM, tq), BF16)

    def idx_rhs(hh):
        hs = slice(hh * IDX_DIM, (hh + 1) * IDX_DIM)
        return jnp.concatenate([hit[hs], hit[hs], lot[hs], zero], axis=0)
    rhs_s = [jnp.concatenate([idx_rhs(2 * p), idx_rhs(2 * p + 1)], axis=1) for p in range(IDX_HEADS // 2)]
    wt = wt_ref[0]

    chunk = lambda c: pl.ds(pl.multiple_of(c * ck, ck), ck)

    def pipelined(mxu_stage, vpu_stage):
        mxu_stage(0, 0)

        def pair_body(j, carry):
            mxu_stage(2 * j + 1, 1)
            vpu_stage(2 * j, 0)
            mxu_stage(jnp.minimum(2 * j + 2, nchunk - 1), 0)
            vpu_stage(2 * j + 1, 1)
            return carry
        lax.fori_loop(0, nchunk // 2, pair_body, 0)

        @pl.when(nchunk % 2 == 1)
        def _():
            vpu_stage(nchunk - 1, 0)

    def score_products(c, slot):
        k4 = k4_ref[0, chunk(c), :]
        for p in range(IDX_HEADS // 2):
            mm_scr[slot, p] = _dot(k4, rhs_s[p])

    def score_keys(c, slot):
        acc = jnp.zeros((ck, tq), F32)
        for p in range(IDX_HEADS // 2):
            s = mm_scr[slot, p]
            acc = acc + jnp.maximum(s[:, :tq], 0.0) * wt[2 * p:2 * p + 1]
            acc = acc + jnp.maximum(s[:, tq:], 0.0) * wt[2 * p + 1:2 * p + 2]
        kpos = c * ck + lax.broadcasted_iota(I32, (ck, 1), 0)
        key_scr[chunk(c), :] = jnp.where(kpos <= qpos, _order_key(acc), INT_MIN)
    pipelined(score_products, score_keys)

    def store_keys(c, val):
        key_scr[chunk(c), :] = val
    thr = _select_threshold(lambda c: key_scr[chunk(c), :], store_keys, nchunk, tq, ck, topk, idx_bits,
                            keys_on_rows=True)

    qt = q_ref[0].astype(F32).T.astype(BF16)
    rhs_a = []
    for m in range(KV_HEADS):
        blk = jnp.concatenate([qt[(2 * m) * HEAD_DIM:(2 * m + 1) * HEAD_DIM],
                               qt[(2 * m + 1) * HEAD_DIM:(2 * m + 2) * HEAD_DIM]], axis=1)
        parts = [jnp.zeros((HEAD_DIM, 2 * tq), BF16)] * m + [blk] + [jnp.zeros((HEAD_DIM, 2 * tq), BF16)] * (KV_HEADS - 1 - m)
        rhs_a.append(jnp.concatenate(parts, axis=0))
    m_scr[...] = jnp.full(m_scr.shape, NEG_BIG, F32)
    acc_scr[...] = jnp.zeros(acc_scr.shape, F32)

    def logits(c, slot):
        kc = kb_ref[0, chunk(c), :]
        for m in range(KV_HEADS):
            mm_scr[slot, m] = _dot(kc, rhs_a[m])

    def softmax_pv(c, slot):
        sel = key_scr[chunk(c), :] > thr
        pas = []
        for m in range(KV_HEADS):
            lg = mm_scr[slot, m]
            lg = jnp.concatenate([jnp.where(sel, lg[:, :tq], NEG_BIG), jnp.where(sel, lg[:, tq:], NEG_BIG)], axis=1)
            mx = m_scr[m]
            mn = jnp.maximum(mx, jnp.max(lg, axis=0, keepdims=True))
            m_scr[m] = mn
            pas.append((jnp.exp2(lg - mn).astype(BF16), jnp.exp2(mx - mn)))
        for m in range(KV_HEADS):
            p, alpha = pas[m]
            acc_scr[m] = acc_scr[m] * alpha + _dot(vt_ref[0, m, :, chunk(c)], p)

    pipelined(logits, softmax_pv)
    pieces = []
    for m in range(KV_HEADS):
        acc = acc_scr[m]
        o = acc[0:HEAD_DIM] / acc[HEAD_DIM:HEAD_DIM + 1]
        pieces += [o[:, :tq], o[:, tq:]]
    at = jnp.concatenate(pieces, axis=0)
    at = at * lax.rsqrt(jnp.mean(at * at, axis=0, keepdims=True) + EPS)
    o_ref[0] = (at.T * ga_ref[...]).astype(BF16)


def _prompt_attention(q, qi, wt, k4, kb, vt, ga):
    b, t, _ = q.shape
    tq = LANES
    assert t % ATT_CHUNK == 0
    topk = min(TOPK_MAX, t // 4)
    blk = lambda w: pl.BlockSpec((1, tq, w), lambda bi, i: (bi, i, 0))
    full = lambda r, c: pl.BlockSpec((1, r, c), lambda bi, i: (bi, 0, 0))
    return pl.pallas_call(
        functools.partial(_prompt_attn_kernel, topk=topk, idx_bits=max(1, (t - 1).bit_length())),
        out_shape=jax.ShapeDtypeStruct((b, t, ATT_WIDTH), BF16),
        grid=(b, t // tq),
        in_specs=[blk(ATT_WIDTH), blk(ATT_WIDTH),
                  pl.BlockSpec((1, IDX_HEADS, tq), lambda bi, i: (bi, 0, i)),
                  full(t, 4 * IDX_DIM), full(t, KV_WIDTH),
                  pl.BlockSpec((1, KV_HEADS, VT_ROWS, t), lambda bi, i: (bi, 0, 0, 0)),
                  pl.BlockSpec((1, ATT_WIDTH), lambda bi, i: (0, 0))],
        out_specs=blk(ATT_WIDTH),
        scratch_shapes=[pltpu.VMEM((t, tq), I32), pltpu.VMEM((2, KV_HEADS, ATT_CHUNK, 2 * tq), F32),
                        pltpu.VMEM((KV_HEADS, 1, 2 * tq), F32),
                        pltpu.VMEM((KV_HEADS, VT_ROWS, 2 * tq), F32)],
        compiler_params=_cparams(("arbitrary", "arbitrary")),
        name="prompt_attn",
    )(q, qi, wt, k4, kb, vt, ga.reshape(1, ATT_WIDTH))


def _sample_score_kernel(pt_ref, qi_ref, w_ref, kin_ref, cache_ref, key_ref, thr_ref, kbuf, sem, *,
                         layer, n_pages, npg, tnew, topk, idx_bits):
    b = pl.program_id(0)
    nb = pl.num_programs(0)
    slot = b % 2
    tq = tnew
    page = kbuf.shape[3]
    ck = npg * page
    nsteps = n_pages // npg

    def start_pages(row, sl):
        for p in range(n_pages):
            pltpu.make_async_copy(cache_ref.at[layer, pt_ref[row * n_pages + p]], kbuf.at[sl, p], sem.at[sl]).start()

    @pl.when(b == 0)
    def _():
        start_pages(0, 0)

    @pl.when(b + 1 < nb)
    def _():
        start_pages(b + 1, 1 - slot)

    pltpu.make_async_copy(cache_ref.at[layer, pl.ds(0, n_pages)], kbuf.at[slot], sem.at[slot]).wait()

    def keys_of(kt, ok):
        k_hi, k_lo = _split_bf16(kt)
        sc = _dot(qi_ref[0], jnp.concatenate([k_hi, k_lo, k_hi], axis=0))
        sc = jnp.maximum(sc, 0.0) * w_ref[0]
        acc = sc[0:tq]
        for hh in range(1, IDX_HEADS):
            acc = acc + sc[hh * tq:(hh + 1) * tq]
        return _order_key(acc) if ok is None else jnp.where(ok, _order_key(acc), INT_MIN)

    def chunk_body(c, carry):
        pages = kbuf[slot, pl.ds(c * npg, npg)]
        kt = jnp.concatenate([pages[p] for p in range(npg)], axis=1)
        key_ref[0, :, pl.ds(pl.multiple_of(c * ck, ck), ck)] = keys_of(kt, None)
        return carry
    lax.fori_loop(0, nsteps, chunk_body, 0)
    qrow = lax.broadcasted_iota(I32, (tq, ck), 0)
    col = lax.broadcasted_iota(I32, (tq, ck), 1)
    new = jnp.concatenate([kin_ref[0], jnp.zeros((kin_ref.shape[1], ck - page), F32)], axis=1)
    key_ref[0, :, pl.ds(nsteps * ck, ck)] = keys_of(new, jnp.logical_and(col <= qrow, col < tnew))

    def load_keys(c):
        return key_ref[0, :, pl.ds(pl.multiple_of(c * ck, ck), ck)]

    def store_keys(c, val):
        key_ref[0, :, pl.ds(pl.multiple_of(c * ck, ck), ck)] = val
    thr = _select_threshold(load_keys, store_keys, nsteps + 1, tq, ck, topk, idx_bits)
    thr_ref[0] = jnp.broadcast_to(thr, (tq, LANES))


def _sample_attn_kernel(pt_ref, q_ref, key_ref, thr_ref, kn_ref, vn_ref, ga_ref, *rest, npg, nsteps, tnew):
    k_refs = rest[:npg]
    v_refs = rest[npg:2 * npg]
    o_ref = rest[2 * npg]
    m_scr, l_scr, acc_scr = rest[2 * npg + 1:]
    s = pl.program_id(1)
    last = s == nsteps
    tq = tnew

    @pl.when(s == 0)
    def _():
        m_scr[...] = jnp.full(m_scr.shape, NEG_BIG, F32)
        l_scr[...] = jnp.zeros(l_scr.shape, F32)
        acc_scr[...] = jnp.zeros(acc_scr.shape, F32)

    sel = key_ref[0] > thr_ref[0][:, 0:1]
    sel2 = jnp.concatenate([sel, sel], axis=0)
    def pages(refs, new_ref, m):
        ps = [refs[p][m] for p in range(npg)]
        ps[0] = jnp.where(last, new_ref[0, m], ps[0])
        return jnp.concatenate(ps, axis=1).astype(BF16)

    lgs = [_dot(q_ref[0, m], pages(k_refs, kn_ref, m)) for m in range(KV_HEADS)]
    prs = []
    for m in range(KV_HEADS):
        lg = jnp.where(sel2, lgs[m], NEG_BIG)
        mx = m_scr[m]
        mn = jnp.maximum(mx, jnp.max(lg, axis=-1, keepdims=True))
        pr = jnp.exp2(lg - mn)
        alpha = jnp.exp2(mx - mn)
        l_scr[m] = l_scr[m] * alpha + jnp.sum(pr, axis=-1, keepdims=True)
        m_scr[m] = mn
        prs.append((pr.astype(BF16), alpha))
    for m in range(KV_HEADS):
        pr, alpha = prs[m]
        acc_scr[m] = acc_scr[m] * alpha + _dot_nt(pr, pages(v_refs, vn_ref, m))

    @pl.when(last)
    def _():
        pieces = []
        for m in range(KV_HEADS):
            o = acc_scr[m] / l_scr[m]
            pieces += [o[:tq], o[tq:]]
        a = jnp.concatenate(pieces, axis=1)
        a = a * lax.rsqrt(jnp.mean(a * a, axis=-1, keepdims=True) + EPS) * ga_ref[...]
        o_ref[0] = a.astype(BF16)


def _sample_attention(q, qi, wi, ki_new, k_new, v_new, cache_k, cache_v, cache_ki, layer, page_table, ga):
    db, tn, _ = q.shape
    page = cache_ki.shape[2]
    cache_k = cache_k.transpose(0, 1, 3, 4, 2)
    cache_v = cache_v.transpose(0, 1, 3, 4, 2)
    cache_ki = cache_ki.transpose(0, 1, 3, 2)
    n_pages = page_table.shape[1]
    npg_s = math.gcd(SAMPLE_SCORE_PAGES, n_pages)
    npg_a = math.gcd(SAMPLE_ATTN_PAGES, npg_s)
    nsteps_s, nsteps_a = n_pages // npg_s, n_pages // npg_a
    lpad = (nsteps_s + 1) * npg_s * page
    topk = min(TOPK_MAX, (n_pages * page + tn) // 4)
    assert tn <= page
    pad_keys = lambda a: jnp.pad(a, [(0, 0)] * (a.ndim - 1) + [(0, page - tn)])
    kin = pad_keys(ki_new.transpose(0, 2, 1))
    heads = lambda a: pad_keys(a.reshape(db, tn, KV_HEADS, HEAD_DIM).transpose(0, 2, 3, 1))
    qi_hq = qi.reshape(db, tn, IDX_HEADS, IDX_DIM).transpose(0, 2, 1, 3).reshape(db, IDX_HEADS * tn, IDX_DIM)
    qi_hi = qi_hq.astype(BF16)
    qi_hq = jnp.concatenate([qi_hi, qi_hi, (qi_hq - qi_hi.astype(F32)).astype(BF16)], axis=-1)
    w_hq = wi.transpose(0, 2, 1).reshape(db, IDX_HEADS * tn, 1)
    q2 = (q.reshape(db, tn, KV_HEADS, ATT_HEADS // KV_HEADS, HEAD_DIM).transpose(0, 2, 3, 1, 4)
          .reshape(db, KV_HEADS, (ATT_HEADS // KV_HEADS) * tn, HEAD_DIM))
    pt_flat = page_table.reshape(-1).astype(I32)

    def page_spec(tail, p, npg):
        def imap(b, s, pt):
            return (layer, pt[b * n_pages + jnp.minimum(s * npg + p, n_pages - 1)]) + (0,) * len(tail)
        return pl.BlockSpec((None, None) + tail, imap)

    per_b = lambda *tail: pl.BlockSpec((1,) + tail, lambda b, s, pt: (b,) + (0,) * len(tail))
    row_b = lambda *tail: pl.BlockSpec((1,) + tail, lambda b, pt: (b,) + (0,) * len(tail))
    keys, thr = pl.pallas_call(
        functools.partial(_sample_score_kernel, layer=layer, n_pages=n_pages, npg=npg_s, tnew=tn, topk=topk,
                          idx_bits=max(1, (lpad - 1).bit_length())),
        out_shape=[jax.ShapeDtypeStruct((db, tn, lpad), I32),
                   jax.ShapeDtypeStruct((db, tn, LANES), I32)],
        grid_spec=pltpu.PrefetchScalarGridSpec(
            num_scalar_prefetch=1,
            grid=(db,),
            in_specs=[row_b(IDX_HEADS * tn, 3 * IDX_DIM), row_b(IDX_HEADS * tn, 1), row_b(IDX_DIM, page),
                      pl.BlockSpec(memory_space=pl.ANY)],
            out_specs=[row_b(tn, lpad), row_b(tn, LANES)],
            scratch_shapes=[pltpu.VMEM((2, n_pages, IDX_DIM, page), F32), pltpu.SemaphoreType.DMA((2,))]),
        compiler_params=_cparams(("arbitrary",)),
        name="sample_score",
    )(pt_flat, qi_hq, w_hq, kin, cache_ki)

    kv_pages = [page_spec((KV_HEADS, HEAD_DIM, page), p, npg_a) for p in range(npg_a)]
    a = pl.pallas_call(
        functools.partial(_sample_attn_kernel, npg=npg_a, nsteps=nsteps_a, tnew=tn),
        out_shape=jax.ShapeDtypeStruct((db, tn, ATT_WIDTH), BF16),
        grid_spec=pltpu.PrefetchScalarGridSpec(
            num_scalar_prefetch=1,
            grid=(db, nsteps_a + 1),
            in_specs=[per_b(KV_HEADS, (ATT_HEADS // KV_HEADS) * tn, HEAD_DIM),
                      pl.BlockSpec((1, tn, npg_a * page), lambda b, s, pt: (b, 0, s)),
                      per_b(tn, LANES),
                      per_b(KV_HEADS, HEAD_DIM, page), per_b(KV_HEADS, HEAD_DIM, page),
                      pl.BlockSpec((1, ATT_WIDTH), lambda b, s, pt: (0, 0))]
                     + kv_pages + kv_pages,
            out_specs=per_b(tn, ATT_WIDTH),
            scratch_shapes=[pltpu.VMEM((KV_HEADS, 2 * tn, 1), F32), pltpu.VMEM((KV_HEADS, 2 * tn, 1), F32),
                            pltpu.VMEM((KV_HEADS, 2 * tn, HEAD_DIM), F32)]),
        compiler_params=_cparams(("arbitrary", "arbitrary")),
        name="sample_attn",
    )(pt_flat, q2, keys, thr, heads(k_new), heads(v_new), ga.reshape(1, ATT_WIDTH),
      *([cache_k] * npg_a), *([cache_v] * npg_a))
    return a


def _outproj_kernel(a_ref, r_ref, x_ref, gt_ref, sc_ref, sh_ref, g_ref, wa_ref, wr_ref, wrt_ref, brt_ref,
                    x2_ref, h2_ref, route_ref):
    mix = _dot(a_ref[...], wa_ref[...]) + _dot(r_ref[...], wr_ref[...])
    x2 = x_ref[...] + gt_ref[0] * mix
    x2_ref[...] = x2
    h = x2 * lax.rsqrt(jnp.mean(x2 * x2, axis=-1, keepdims=True) + EPS) * g_ref[...]
    h = h * (1.0 + sc_ref[0]) + sh_ref[0]
    h2_ref[...] = h
    lg = (_dot3(h, wrt_ref[...]) + brt_ref[...]).T
    grp = [lg[g:g + 1] for g in range(N_GROUPS)]
    gmax = functools.reduce(jnp.maximum, grp)
    gden = functools.reduce(lambda u, v: u + v, [jnp.exp(g - gmax) for g in grp])
    gsel = jnp.full(gmax.shape, N_GROUPS - 1, I32)
    for g in range(N_GROUPS - 2, -1, -1):
        gsel = jnp.where(grp[g] == gmax, g, gsel)
    gw = 1.0 / gden
    el = []
    for e in range(EXPERTS_PER_GROUP):
        v = lg[N_GROUPS + e:N_GROUPS + e + 1]
        for g in range(1, N_GROUPS):
            row = N_GROUPS + g * EXPERTS_PER_GROUP + e
            v = jnp.where(gsel == g, lg[row:row + 1], v)
        el.append(v)
    emax = functools.reduce(jnp.maximum, el)
    e0 = jnp.full(emax.shape, EXPERTS_PER_GROUP - 1, I32)
    for e in range(EXPERTS_PER_GROUP - 2, -1, -1):
        e0 = jnp.where(el[e] == emax, e, e0)
    rest = [jnp.where(e0 == e, -jnp.inf, el[e]) for e in range(EXPERTS_PER_GROUP)]
    rmax = functools.reduce(jnp.maximum, rest)
    e1 = jnp.full(emax.shape, EXPERTS_PER_GROUP - 1, I32)
    for e in range(EXPERTS_PER_GROUP - 2, -1, -1):
        e1 = jnp.where(jnp.logical_and(rest[e] == rmax, e0 != e), e, e1)
    p1 = jnp.exp(rmax - emax)
    w0 = gw * (1.0 / (1.0 + p1))
    w1 = gw * (p1 / (1.0 + p1))
    base = gsel * EXPERTS_PER_GROUP
    rid = lax.broadcasted_iota(I32, lg.shape, 0)
    rt = jnp.where(rid == 0, (base + e0).astype(F32),
                   jnp.where(rid == 1, (base + e1).astype(F32),
                             jnp.where(rid == 2, w0, jnp.where(rid == 3, w1, 0.0))))
    route_ref[...] = rt.T


def _outproj(a, r, x2d, gt, sc, sh, g2, wa, wr, wrt, brt, tiles_per_group):
    n, d = x2d.shape
    tm = min(PROJ_TILE, n)
    rr = gt.shape[1]
    row = lambda w: pl.BlockSpec((tm, w), lambda i: (i, 0))
    mod = pl.BlockSpec((1, rr, d), lambda i: (i // tiles_per_group, 0, 0))
    const = lambda s: pl.BlockSpec(s, lambda i: (0, 0))
    return pl.pallas_call(
        _outproj_kernel,
        out_shape=[jax.ShapeDtypeStruct((n, d), F32), jax.ShapeDtypeStruct((n, d), F32),
                   jax.ShapeDtypeStruct((n, LANES), F32)],
        grid=(n // tm,),
        in_specs=[row(ATT_WIDTH), row(HG_WIDTH), row(d), mod, mod, mod, const((1, d)),
                  const(wa.shape), const(wr.shape), const(wrt.shape), const(brt.shape)],
        out_specs=[row(d), row(d), row(LANES)],
        compiler_params=_cparams(("arbitrary",)),
        name="outproj",
    )(a, r, x2d, gt, sc, sh, g2.reshape(1, d), wa, wr, wrt, brt)


def _gather_pipeline(step, nsteps, idx_hbm, src_hbm, buf, idx_smem, isem, rsem):
    nrows = buf.shape[1]
    slot = step % 2

    def idx_copy(b, sl):
        return pltpu.make_async_copy(idx_hbm.at[b], idx_smem.at[sl], isem.at[sl])

    def start_rows(sl):
        for r in range(nrows):
            pltpu.make_async_copy(src_hbm.at[pl.ds(idx_smem[sl, 0, r], 1), :],
                                  buf.at[sl, pl.ds(r, 1), :], rsem.at[sl]).start()

    @pl.when(step == 0)
    def _():
        idx_copy(0, 0).start()
        idx_copy(0, 0).wait()
        start_rows(0)

        @pl.when(nsteps > 1)
        def _():
            idx_copy(1, 1).start()

    @pl.when(step + 1 < nsteps)
    def _():
        idx_copy(step + 1, 1 - slot).wait()
        start_rows(1 - slot)

    @pl.when(step + 2 < nsteps)
    def _():
        idx_copy(step + 2, slot).start()

    pltpu.make_async_copy(src_hbm.at[pl.ds(0, nrows), :], buf.at[slot], rsem.at[slot]).wait()
    return slot


def _ffn_kernel(be_ref, nb_ref, tok_ref, h_ref, w1_ref, w3_ref, w2_ref, y_ref, xbuf, idx_smem, isem, rsem):
    j = pl.program_id(0)

    @pl.when(j < nb_ref[0])
    def _():
        slot = _gather_pipeline(j, nb_ref[0], tok_ref, h_ref, xbuf, idx_smem, isem, rsem)
        xb = xbuf[slot].astype(BF16)
        u = _dot(xb, w1_ref[0])
        g = _dot(xb, w3_ref[0])
        y_ref[...] = _dot((_silu(u) * g).astype(BF16), w2_ref[0])

    @pl.when(j >= nb_ref[0])
    def _():
        y_ref[...] = jnp.zeros(y_ref.shape, F32)


def _ffn(blk_e, n_used, tok, h2, w1, w3, w2, blk):
    nb = tok.shape[0]
    d = h2.shape[1]
    wspec = lambda s: pl.BlockSpec((1,) + s, lambda j, be, nu: (be[j], 0, 0))
    return pl.pallas_call(
        _ffn_kernel,
        out_shape=jax.ShapeDtypeStruct((nb * blk, d), F32),
        grid_spec=pltpu.PrefetchScalarGridSpec(
            num_scalar_prefetch=2,
            grid=(nb,),
            in_specs=[pl.BlockSpec(memory_space=pl.ANY), pl.BlockSpec(memory_space=pl.ANY),
                      wspec(w1.shape[1:]), wspec(w3.shape[1:]), wspec(w2.shape[1:])],
            out_specs=pl.BlockSpec((blk, d), lambda j, be, nu: (j, 0)),
            scratch_shapes=[pltpu.VMEM((2, blk, d), F32), pltpu.SMEM((2, 1, blk), I32),
                            pltpu.SemaphoreType.DMA((2,)), pltpu.SemaphoreType.DMA((2,))]),
        compiler_params=_cparams(("arbitrary",)),
        name="moe_ffn",
    )(blk_e, n_used, tok, h2, w1, w3, w2)


def _combine_kernel(dd_ref, y_ref, x2_ref, route_ref, gt_ref, sc_ref, sh_ref, g_ref, o_ref,
                    ybuf, idx_smem, isem, rsem):
    tm = x2_ref.shape[0]
    slot = _gather_pipeline(pl.program_id(0), pl.num_programs(0), dd_ref, y_ref, ybuf, idx_smem, isem, rsem)
    rt = route_ref[...]
    moe = ybuf[slot, 0:tm] * rt[:, 2:3] + ybuf[slot, tm:2 * tm] * rt[:, 3:4]
    x = x2_ref[...] + gt_ref[0] * moe
    y = x * lax.rsqrt(jnp.mean(x * x, axis=-1, keepdims=True) + EPS) * g_ref[...]
    o_ref[...] = y * (1.0 + sc_ref[0]) + sh_ref[0]


def _combine(dd, yb, x2, route, gt, sc, sh, gf, tiles_per_group):
    n, d = x2.shape
    tm = dd.shape[2] // 2
    rr = gt.shape[1]
    row = pl.BlockSpec((tm, d), lambda i: (i, 0))
    mod = pl.BlockSpec((1, rr, d), lambda i: (i // tiles_per_group, 0, 0))
    hbm = pl.BlockSpec(memory_space=pl.ANY)
    return pl.pallas_call(
        _combine_kernel,
        out_shape=jax.ShapeDtypeStruct((n, d), F32),
        grid=(n // tm,),
        in_specs=[hbm, hbm, row, pl.BlockSpec((tm, LANES), lambda i: (i, 0)), mod, mod, mod,
                  pl.BlockSpec((1, d), lambda i: (0, 0))],
        out_specs=row,
        scratch_shapes=[pltpu.VMEM((2, 2 * tm, d), F32), pltpu.SMEM((2, 1, 2 * tm), I32),
                        pltpu.SemaphoreType.DMA((2,)), pltpu.SemaphoreType.DMA((2,))],
        compiler_params=_cparams(("arbitrary",)),
        name="moe_combine",
    )(dd, yb, x2, route, gt, sc, sh, gf.reshape(1, d))


def _dispatch(route, blk):
    n = route.shape[0]
    flat_e = route[:, 0:2].astype(I32).reshape(-1)
    a = flat_e.shape[0]
    onehot = (flat_e[:, None] == jnp.arange(N_EXPERTS, dtype=I32)[None, :]).astype(I32)
    csum = jnp.cumsum(onehot, axis=0)
    rank = jnp.sum((csum - onehot) * onehot, axis=1)
    counts = csum[-1]
    padded = (counts + blk - 1) // blk * blk
    pad_end = jnp.cumsum(padded)
    pad_start = pad_end - padded
    dest = pad_start[flat_e] + rank
    nb = -(-a // blk) + N_EXPERTS
    tok = jnp.zeros((nb * blk,), I32).at[dest].set(jnp.arange(a, dtype=I32) // 2)
    blk_e = jnp.minimum(jnp.searchsorted(pad_end, jnp.arange(nb, dtype=I32) * blk, side='right'),
                        N_EXPERTS - 1).astype(I32)
    n_used = (pad_end[-1] // blk).astype(I32).reshape(1)
    dest2 = dest.reshape(n, 2)
    return tok.reshape(nb, 1, blk), blk_e, n_used, dest2[:, 0], dest2[:, 1]


def _mods(m, n_chunks, per_token_rows):
    parts = jnp.split(m, n_chunks, axis=-1)
    if per_token_rows is None:
        return [p[:, None, :] for p in parts]
    g, d = parts[0].shape
    tm = min(PROJ_TILE, g * per_token_rows)
    return [jnp.repeat(p, per_token_rows, axis=0).reshape(-1, tm, d) for p in parts]


def _layer(x, mod6, modf, pos, s0, attend, lb, wts, final_g, per_token):
    b, t, d = x.shape
    n = b * t
    (norm1_g, norm2_g, w_r, gk, ga, hg_out_g, wa, wr, wrt, brt, w1, w3, w2) = wts
    tm = min(PROJ_TILE, n)
    tiles_per_group = 1 if per_token else t // tm
    sh1, sc1, gt1, sh2, sc2, gt2 = _mods(mod6, 6, t if per_token else None)
    shf, scf = _mods(modf, 2, t if per_token else None)
    tabs = _rope_tables(jnp.tile(pos, tm // t) if per_token else pos)
    x2d = x.reshape(n, d)
    sq = lambda arr: arr.reshape(b, t, arr.shape[-1])
    if per_token:
        q, k, v, qi, ki, wi, hq, hf, hi, hg = _proj(x2d, sc1, sh1, norm1_g, w_r, gk, tabs, tiles_per_group, 0)
        a = attend(sq(q), sq(qi), sq(wi), sq(ki), sq(k), sq(v), ga)
        k_out, v_out, ki_out = k.reshape(b, t, KV_HEADS, HEAD_DIM), v.reshape(b, t, KV_HEADS, HEAD_DIM), sq(ki)
    else:
        q, kt, vt, kb, vto, qi, kit, k4, wt, hq, hf, hi, hg = _proj(x2d, sc1, sh1, norm1_g, w_r, gk, tabs,
                                                                   tiles_per_group, b)
        a = attend(sq(q), sq(qi), wt, sq(k4), sq(kb), vto, ga)
        heads = lambda xt: xt.reshape(b, KV_HEADS, HEAD_DIM, t).transpose(0, 3, 1, 2)
        k_out, v_out, ki_out = heads(kt), heads(vt), kit.transpose(0, 2, 1)
    r, s_t = _hgrn(sq(hq), sq(hf), sq(hi), sq(hg), lb, hg_out_g, s0)
    x2, h2, route = _outproj(a.reshape(n, -1), r.reshape(n, -1), x2d, gt1, sc2, sh2, norm2_g,
                             wa, wr, wrt, brt, tiles_per_group)
    blk = MOE_BLOCK if n >= 8 * MOE_BLOCK else 64
    tok, blk_e, n_used, d0, d1 = _dispatch(route, blk)
    yb = _ffn(blk_e, n_used, tok, h2, w1, w3, w2, blk)
    dd = jnp.concatenate([d0.reshape(n // tm, 1, tm), d1.reshape(n // tm, 1, tm)], axis=2)
    y = _combine(dd, yb, x2, route, gt2, scf, shf, final_g, tiles_per_group)
    return y.reshape(b, t, d), k_out, v_out, ki_out, s_t


def kernel(x_prompt, x_sample, cache_k, cache_v, cache_kidx, state_hgrn, page_table, c_prompt, c_sample,
           ada_w, ada_b, norm1_g, norm2_g, w_in, idx_k_g, hg_lb_logits, attn_out_g, hg_out_g, w_out,
           w_group, b_group, w_expert_router, b_expert_router, w1, w3, w2, final_g, ada_final_w, ada_final_b):
    depth = ada_w.shape[0]
    assert depth == 1, "the final adaLN norm is fused into the (single) layer's combine kernel"
    bp, tp, d = x_prompt.shape
    bs, ts, _ = x_sample.shape
    past = page_table.shape[1] * cache_kidx.shape[2]
    lb_all = jnp.cumsum(jax.nn.softmax(hg_lb_logits.astype(F32), axis=0), axis=0)
    n_c = bp + bs
    c_all = jnp.concatenate([c_prompt, c_sample, jnp.zeros((-n_c % 16, d), F32)], axis=0)
    modf = _ada(c_all, ada_final_w, ada_final_b)
    l = 0
    mod6 = _ada(c_all, ada_w[l], ada_b[l])

    seg = [0]
    for s in (ATT_WIDTH, KV_WIDTH, KV_WIDTH, IDX_HEADS * IDX_DIM, IDX_DIM, IDX_HEADS,
              HG_WIDTH, HG_WIDTH, HG_WIDTH, HG_WIDTH):
        seg.append(seg[-1] + s)
    wl = w_in[l]
    col = lambda i: wl[:, seg[i]:seg[i + 1]]
    zpad = lambda w: jnp.zeros((d, w), wl.dtype)
    w_r = jnp.concatenate([col(0), col(1), col(2), col(3), col(6), col(7), col(8), col(9),
                           col(4), zpad(LANES - IDX_DIM), col(5), zpad(LANES - IDX_HEADS)], axis=1).astype(BF16)
    gk = jnp.concatenate([idx_k_g[l], jnp.zeros((LANES - IDX_DIM,), F32)]).reshape(1, LANES)
    wa = w_out[l, :ATT_WIDTH].astype(BF16)
    wr = w_out[l, ATT_WIDTH:].astype(BF16)
    n_rt = N_GROUPS + N_EXPERTS
    wrt = jnp.concatenate([w_group[l], w_expert_router[l], jnp.zeros((d, LANES - n_rt), F32)], axis=1)
    brt = jnp.concatenate([b_group[l], b_expert_router[l], jnp.zeros((LANES - n_rt,), F32)]).reshape(1, LANES)
    wts = (norm1_g[l], norm2_g[l], w_r, gk, attn_out_g[l], hg_out_g[l], wa, wr, wrt, brt,
           w1[l].astype(BF16), w3[l].astype(BF16), w2[l].astype(BF16))

    def attend_s(q, qi, wi, ki, k, v, ga):
        return _sample_attention(q, qi, wi, ki, k, v, cache_k, cache_v, cache_kidx, l, page_table, ga)

    s0_p = jnp.zeros((bp, HG_HEADS, HG_DK, HG_DV), F32)
    yp, kp, vp, kip, sp = _layer(x_prompt, mod6[:bp], modf[:bp], jnp.arange(tp), s0_p, _prompt_attention,
                                 lb_all[l], wts, final_g, per_token=False)
    ys, ks, vs, kis, ss = _layer(x_sample, mod6[bp:n_c], modf[bp:n_c], past + jnp.arange(ts), state_hgrn[l],
                                 attend_s, lb_all[l], wts, final_g, per_token=True)
    return (yp, ys, kp[None], vp[None], kip[None], sp[None], ks[None], vs[None], kis[None], ss[None])
```

```python
import functools
import math

import jax
import jax.numpy as jnp
from jax import lax
from jax.experimental import pallas as pl
from jax.experimental.pallas import tpu as pltpu

F32 = jnp.float32
BF16 = jnp.bfloat16
I32 = jnp.int32

ATT_HEADS = 8
KV_HEADS = 4
HEAD_DIM = 64
ATT_WIDTH = ATT_HEADS * HEAD_DIM
KV_WIDTH = KV_HEADS * HEAD_DIM
ROT_HALF = HEAD_DIM // 8
ROPE_THETA = 500000.0
IDX_HEADS = 8
IDX_DIM = 64
TOPK_MAX = 256
HG_HEADS = 4
HG_DK = 128
HG_DV = 128
HG_WIDTH = HG_HEADS * HG_DV
HG_CHUNK = 64
N_GROUPS = 4
EXPERTS_PER_GROUP = 4
N_EXPERTS = N_GROUPS * EXPERTS_PER_GROUP
D_EXPERT = 512
MOE_BLOCK = 256
EPS = 1e-6

LANES = 128
INT_MIN = -2 ** 31
NEG_BIG = -1e30
VMEM_LIMIT = 56 * 1024 * 1024
PROJ_TILE = 256
ATT_CHUNK = 1024
VT_ROWS = HEAD_DIM + 16
LOG2E = 1.4426950408889634
BISECT_BITS_PER_TRIP = 4
SAMPLE_SCORE_PAGES = 16
SAMPLE_ATTN_GROUP = 32

_Z_Q, _Z_K, _Z_V, _Z_QI, _Z_HQ, _Z_HF, _Z_HI, _Z_HG, _Z_KI, _Z_WI, _Z_END = (
    0, 512, 768, 1024, 1536, 2048, 2560, 3072, 3584, 3712, 3840)


def _cparams(sem):
    return pltpu.CompilerParams(dimension_semantics=sem, vmem_limit_bytes=VMEM_LIMIT)


def _split_bf16(x):
    hi = x.astype(BF16)
    lo = (x - hi.astype(F32)).astype(BF16)
    return hi, lo


def _dot(a, b):
    return jnp.dot(a, b, preferred_element_type=F32)


def _dot_nt(a, b):
    return lax.dot_general(a, b, (((1,), (1,)), ((), ())), preferred_element_type=F32)


def _dot_tn(a, b):
    return lax.dot_general(a, b, (((0,), (0,)), ((), ())), preferred_element_type=F32)


def _dot3(a, b):
    ah, al = _split_bf16(a)
    bh, bl = _split_bf16(b)
    return _dot(ah, bh) + (_dot(ah, bl) + _dot(al, bh))


def _dot3_nt(a, b):
    ah, al = _split_bf16(a)
    bh, bl = _split_bf16(b)
    return _dot_nt(ah, bh) + (_dot_nt(ah, bl) + _dot_nt(al, bh))


def _silu(x):
    return x * (1.0 / (1.0 + jnp.exp(-x)))


def _sigmoid(x):
    return 1.0 / (1.0 + jnp.exp(-x))


def _ada_kernel(c_ref, w_ref, b_ref, o_ref):
    o_ref[...] = _dot3(_silu(c_ref[...]), w_ref[...]) + b_ref[...]


def _ada(c, w, b):
    r, d = c.shape
    e = w.shape[1]
    te = 1024
    return pl.pallas_call(
        _ada_kernel,
        out_shape=jax.ShapeDtypeStruct((r, e), F32),
        grid=(e // te,),
        in_specs=[pl.BlockSpec((r, d), lambda j: (0, 0)),
                  pl.BlockSpec((d, te), lambda j: (0, j)),
                  pl.BlockSpec((1, te), lambda j: (0, j))],
        out_specs=pl.BlockSpec((r, te), lambda j: (0, j)),
        compiler_params=_cparams(("arbitrary",)),
        name="ada",
    )(c, w, b.reshape(1, e))


def _rope_tables(pos):
    p = pos.shape[0]
    inv = ROPE_THETA ** (-jnp.arange(ROT_HALF, dtype=F32) * (2.0 / (2 * ROT_HALF)))
    ang = pos.astype(F32)[:, None] * inv[None, :]
    c, s = jnp.cos(ang), jnp.sin(ang)
    rest = HEAD_DIM - 2 * ROT_HALF
    one, zero, z8 = jnp.ones((p, rest), F32), jnp.zeros((p, rest), F32), jnp.zeros((p, ROT_HALF), F32)
    cos64 = jnp.concatenate([c, c, one], axis=1)
    sa64 = jnp.concatenate([-s, z8, zero], axis=1)
    sb64 = jnp.concatenate([z8, s, zero], axis=1)
    dup = lambda t: jnp.concatenate([t, t], axis=1)
    return dup(cos64), dup(sa64), dup(sb64)


def _proj_kernel(x_ref, sc_ref, sh_ref, g_ref, w_ref, gk_ref, cos_ref, sa_ref, sb_ref, *out_refs, seq_major):
    x = x_ref[...]
    ms = jnp.mean(x * x, axis=-1, keepdims=True)
    h = x * lax.rsqrt(ms + EPS) * g_ref[...]
    h = h * (1.0 + sc_ref[0]) + sh_ref[0]
    z = _dot(h.astype(BF16), w_ref[...])
    cos, sa, sb = cos_ref[...], sa_ref[...], sb_ref[...]

    def rope(zs):
        n = zs.shape[1] // LANES
        rep = (lambda t: jnp.concatenate([t] * n, axis=1)) if n > 1 else (lambda t: t)
        w = zs.shape[1]
        return (zs * rep(cos) + pltpu.roll(zs, w - ROT_HALF, 1) * rep(sa)
                + pltpu.roll(zs, ROT_HALF, 1) * rep(sb))

    q = (rope(z[:, _Z_Q:_Z_K]) * (HEAD_DIM ** -0.5 * LOG2E)).astype(BF16)
    k = rope(z[:, _Z_K:_Z_V])
    v = z[:, _Z_V:_Z_QI]
    qi = rope(z[:, _Z_QI:_Z_HQ]) * (IDX_DIM ** -0.5)
    ks = z[:, _Z_KI:_Z_WI]
    kms = jnp.sum(ks * ks, axis=-1, keepdims=True) * (1.0 / IDX_DIM)
    kn = rope(ks * lax.rsqrt(kms + EPS) * gk_ref[...])
    ws = z[:, _Z_WI:_Z_END] * (IDX_HEADS ** -0.5)
    if seq_major:
        q_ref, kt_ref, vt_ref, kb_ref, vto_ref, qi_ref, kit_ref, k4_ref, wt_ref = out_refs[:9]
        kt_ref[0] = k.T
        vt = v.T
        vt_ref[0] = vt
        kb_ref[...] = k.astype(BF16)
        for m in range(KV_HEADS):
            vto_ref[0, m, 0:HEAD_DIM, :] = vt[m * HEAD_DIM:(m + 1) * HEAD_DIM].astype(BF16)
            vto_ref[0, m, HEAD_DIM:VT_ROWS, :] = jnp.ones((VT_ROWS - HEAD_DIM, x.shape[0]), BF16)
        kit_ref[0] = kn.T[:IDX_DIM]
        hi = kn.astype(BF16).astype(F32)
        lo = kn - hi
        k4_ref[...] = jnp.concatenate([hi + pltpu.roll(lo, IDX_DIM, 1), hi + pltpu.roll(hi, IDX_DIM, 1)],
                                      axis=1).astype(BF16)
        wt_ref[0] = ws.T[:IDX_HEADS]
    else:
        q_ref, k_ref, v_ref, qi_ref, ki_ref, wi_ref = out_refs[:6]
        k_ref[...] = k
        v_ref[...] = v
        ki_ref[...] = kn[:, :IDX_DIM]
        wi_ref[...] = ws[:, :IDX_HEADS]
    q_ref[...] = q
    qi_ref[...] = qi
    hq_ref, hf_ref, hi_ref, hg_ref = out_refs[-4:]
    hq_ref[...] = z[:, _Z_HQ:_Z_HF]
    hf_ref[...] = z[:, _Z_HF:_Z_HI]
    hi_ref[...] = z[:, _Z_HI:_Z_HG]
    hg_ref[...] = z[:, _Z_HG:_Z_KI]


def _proj(x2d, sc, sh, g1, w_r, gk, tabs, tiles_per_group, seq_batch):
    n, d = x2d.shape
    tm = min(PROJ_TILE, n)
    nt = n // tm
    r = sc.shape[1]
    pt = tabs[0].shape[0] // tm
    row = lambda w, t: (jax.ShapeDtypeStruct((n, w), t), pl.BlockSpec((tm, w), lambda i: (i, 0)))
    mod = pl.BlockSpec((1, r, d), lambda i: (i // tiles_per_group, 0, 0))
    tab = pl.BlockSpec((tm, LANES), lambda i: (i % pt, 0))
    if seq_batch:
        t = n // seq_batch
        tps = t // tm
        seq = lambda w: (jax.ShapeDtypeStruct((seq_batch, w, t), F32),
                         pl.BlockSpec((1, w, tm), lambda i: (i // tps, 0, i % tps)))
        vto = (jax.ShapeDtypeStruct((seq_batch, KV_HEADS, VT_ROWS, t), BF16),
               pl.BlockSpec((1, KV_HEADS, VT_ROWS, tm), lambda i: (i // tps, 0, 0, i % tps)))
        outs = [row(ATT_WIDTH, BF16), seq(KV_WIDTH), seq(KV_WIDTH), row(KV_WIDTH, BF16), vto,
                row(ATT_WIDTH, F32), seq(IDX_DIM), row(4 * IDX_DIM, BF16), seq(IDX_HEADS)]
    else:
        outs = [row(ATT_WIDTH, BF16), row(KV_WIDTH, F32), row(KV_WIDTH, F32),
                row(ATT_WIDTH, F32), row(IDX_DIM, F32), row(IDX_HEADS, F32)]
    outs += [row(HG_WIDTH, F32)] * 4
    return pl.pallas_call(
        functools.partial(_proj_kernel, seq_major=bool(seq_batch)),
        out_shape=[s for s, _ in outs],
        grid=(nt,),
        in_specs=[pl.BlockSpec((tm, d), lambda i: (i, 0)), mod, mod,
                  pl.BlockSpec((1, d), lambda i: (0, 0)),
                  pl.BlockSpec(w_r.shape, lambda i: (0, 0)),
                  pl.BlockSpec((1, LANES), lambda i: (0, 0)),
                  tab, tab, tab],
        out_specs=[b for _, b in outs],
        compiler_params=_cparams(("arbitrary",)),
        name="proj",
    )(x2d, sc, sh, g1.reshape(1, d), w_r, gk, *tabs)


def _hgrn_kernel(hq_ref, hf_ref, hi_ref, hg_ref, lb_ref, g_ref, s0_ref, r_ref, st_ref, s_scr, *, chunk, nchunk):
    t = pl.program_id(1)

    @pl.when(t == 0)
    def _():
        for hh in range(HG_HEADS):
            s_scr[hh] = s0_ref[0, hh].T

    tt = chunk * nchunk
    row = lax.broadcasted_iota(I32, (tt, tt), 0)
    col = lax.broadcasted_iota(I32, (tt, tt), 1)
    causal = jnp.logical_and(col <= row, row // chunk == col // chunk)
    tri = causal.astype(BF16)
    rows = lambda c: slice(c * chunk, (c + 1) * chunk)
    lanes = lambda hh: slice(hh * HG_DK, (hh + 1) * HG_DK)
    lb = lb_ref[...]
    f = lb + (1.0 - lb) * _sigmoid(hf_ref[0])
    logf = jnp.log(f)
    kk = 1.0 - f
    l0 = logf.astype(BF16)
    r1 = logf - l0.astype(F32)
    l1 = r1.astype(BF16)
    l2 = (r1 - l1.astype(F32)).astype(BF16)
    b = _dot(tri, l0) + (_dot(tri, l1) + _dot(tri, l2))
    bls = [b[(c + 1) * chunk - 1:(c + 1) * chunk, :] for c in range(nchunk)]
    bl_rows = jnp.concatenate([jnp.broadcast_to(bl, (chunk, bl.shape[1])) for bl in bls], axis=0)
    q_in = (hq_ref[0] * (HG_DK ** -0.5) * jnp.exp(b)).astype(BF16)
    k_in = (kk * jnp.exp(-b)).astype(BF16)
    k_st = (kk * jnp.exp(bl_rows - b)).astype(BF16)
    vb = hi_ref[0].astype(BF16)
    intra = []
    for hh in range(HG_HEADS):
        att = jnp.where(causal, _dot_nt(q_in[:, lanes(hh)], k_in[:, lanes(hh)]), 0.0)
        intra.append(_dot(att.astype(BF16), vb[:, lanes(hh)]))
    upd = {(c, hh): _dot_tn(vb[rows(c), lanes(hh)], k_st[rows(c), lanes(hh)])
           for c in range(nchunk) for hh in range(HG_HEADS)}
    decay = [jnp.exp(bl) for bl in bls]
    for hh in range(HG_HEADS):
        os = []
        for c in range(nchunk):
            st = s_scr[hh]
            os.append(_dot_nt(q_in[rows(c), lanes(hh)], st.astype(BF16)))
            s_scr[hh] = st * decay[c][:, lanes(hh)] + upd[c, hh]
        o = jnp.concatenate(os, axis=0) + intra[hh]
        on = o * lax.rsqrt(jnp.mean(o * o, axis=-1, keepdims=True) + EPS) * g_ref[:, lanes(hh)]
        r_ref[0, :, lanes(hh)] = (on * _silu(hg_ref[0, :, lanes(hh)])).astype(BF16)

    @pl.when(t == pl.num_programs(1) - 1)
    def _():
        for hh in range(HG_HEADS):
            st_ref[0, hh] = s_scr[hh].T


def _hgrn(hq, hf, hi, hg, lb, out_g, s0):
    b, t, w = hq.shape
    chunk = min(HG_CHUNK, t)
    assert t % chunk == 0
    nchunk = min(4, t // chunk)
    tt = chunk * nchunk
    seq = pl.BlockSpec((1, tt, w), lambda i, j: (i, j, 0))
    vec = pl.BlockSpec((1, w), lambda i, j: (0, 0))
    state = pl.BlockSpec((1, HG_HEADS, HG_DK, HG_DV), lambda i, j: (i, 0, 0, 0))
    return pl.pallas_call(
        functools.partial(_hgrn_kernel, chunk=chunk, nchunk=nchunk),
        out_shape=[jax.ShapeDtypeStruct((b, t, w), BF16), jax.ShapeDtypeStruct(s0.shape, F32)],
        grid=(b, t // tt),
        in_specs=[seq, seq, seq, seq, vec, vec, state],
        out_specs=[seq, state],
        scratch_shapes=[pltpu.VMEM((HG_HEADS, HG_DV, HG_DK), F32)],
        compiler_params=_cparams(("arbitrary", "arbitrary")),
        name="hgrn",
    )(hq, hf, hi, hg, lb.reshape(1, w), out_g.reshape(1, w), s0)


def _order_key(score):
    bits = pltpu.bitcast(score, I32)
    bits = jnp.where(bits == INT_MIN, 0, bits)
    return jnp.where(bits >= 0, bits, bits ^ 0x7FFFFFFF)


def _select_threshold(load_keys, store_keys, nchunk, tq, ck, topk, idx_bits, keys_on_rows=False):
    kax = 0 if keys_on_rows else 1
    kshape = (ck, tq) if keys_on_rows else (tq, ck)
    vshape = (1, tq) if keys_on_rows else (tq, 1)

    def count(pred):
        def body(c, acc):
            m = jnp.where(pred(load_keys(c), c), 1.0, 0.0)
            if keys_on_rows:
                parts = [m[j * 8:(j + 1) * 8] for j in range(8)]
                for j in range(8, ck // 8):
                    parts[j % 8] = parts[j % 8] + m[j * 8:(j + 1) * 8]
                while len(parts) > 1:
                    parts = [a + b for a, b in zip(parts[0::2], parts[1::2])]
                part = parts[0]
            else:
                parts = [m[:, j * LANES:(j + 1) * LANES] for j in range(ck // LANES)]
                while len(parts) > 1:
                    parts = [a + b for a, b in zip(parts[0::2], parts[1::2])] + ([parts[-1]] if len(parts) % 2 else [])
                part = parts[0]
            return acc + part
        acc = jnp.zeros((8, tq) if keys_on_rows else (tq, LANES), F32)
        if isinstance(nchunk, int):
            for c in range(nchunk):
                acc = body(c, acc)
        else:
            acc = lax.fori_loop(0, nchunk, body, acc)
        return jnp.sum(acc, axis=kax, keepdims=True)

    kf = float(topk)
    zero_i = jnp.zeros(vshape, I32)

    def bisect(nbits, count_ge, done0):
        def cond(st):
            return jnp.logical_and(st[0] < nbits, jnp.min(st[2]) < 0.5)

        def body(st):
            it, tu, done, hit_u = st
            for _ in range(BISECT_BITS_PER_TRIP):
                cand = tu | lax.shift_left(jnp.int32(1), nbits - 1 - it)
                cnt = count_ge(cand)
                active = done < 0.5
                tu = jnp.where(jnp.logical_and(active, cnt >= kf), cand, tu)
                hit = jnp.logical_and(active, cnt == kf)
                hit_u = jnp.where(hit, cand, hit_u)
                done = jnp.where(hit, 1.0, done)
                it = it + 1
            return it, tu, done, hit_u
        return lax.while_loop(cond, body, (jnp.int32(0), zero_i, done0, zero_i))[1:]

    tu, done, hit_u = bisect(32, lambda u: count(lambda key, c: key >= (u ^ INT_MIN)), jnp.zeros(vshape, F32))
    ts = tu ^ INT_MIN
    tsel = (hit_u ^ INT_MIN) - 1
    need_tie = jnp.logical_and(done < 0.5, tu != 0)
    thr = jnp.where(done > 0.5, tsel, ts)

    @pl.when(jnp.max(jnp.where(need_tie, 1.0, 0.0)) > 0.5)
    def _():
        rank = kf - count(lambda key, c: key > ts)
        pos_of = lambda c: c * ck + lax.broadcasted_iota(I32, kshape, kax)

        def jbody(it, ju):
            cand = ju | lax.shift_left(jnp.int32(1), idx_bits - 1 - it)
            pred = lambda key, c: jnp.logical_and(key == ts, pos_of(c) < cand)
            return jnp.where(count(pred) < rank, cand, ju)
        ju = lax.fori_loop(0, idx_bits, jbody, zero_i)

        def fix(c, carry):
            key = load_keys(c)
            bump = jnp.logical_and(jnp.logical_and(key == ts, pos_of(c) <= ju), need_tie)
            store_keys(c, jnp.where(bump, key + 1, key))
            return carry
        lax.fori_loop(0, nchunk, fix, 0)
    return thr


def _prompt_attn_kernel(q_ref, qi_ref, wt_ref, k4_ref, kb_ref, vt_ref, ga_ref, o_ref,
                        key_scr, mm_scr, m_scr, acc_scr, *, topk, idx_bits):
    tq, ck = q_ref.shape[1], ATT_CHUNK
    i = pl.program_id(1)
    nchunk = (i * tq + tq + ck - 1) // ck
    qpos = i * tq + lax.broadcasted_iota(I32, (1, tq), 1)

    qit = qi_ref[0].T
    hit = qit.astype(BF16)
    lot = (qit - hit.astype(F32)).astype(BF16)
    zero = jnp.zeros((IDX_DIM, tq), BF16)

    def idx_rhs(hh):
        hs = slice(hh * IDX_DIM, (hh + 1) * IDX_DIM)
        return jnp.concatenate([hit[hs], hit[hs], lot[hs], zero], axis=0)
    rhs_s = [jnp.concatenate([idx_rhs(2 * p), idx_rhs(2 * p + 1)], axis=1) for p in range(IDX_HEADS // 2)]
    wt = wt_ref[0]

    chunk = lambda c: pl.ds(pl.multiple_of(c * ck, ck), ck)

    def pipelined(mxu_stage, vpu_stage):
        mxu_stage(0, 0)

        def pair_body(j, carry):
            mxu_stage(2 * j + 1, 1)
            vpu_stage(2 * j, 0)
            mxu_stage(jnp.minimum(2 * j + 2, nchunk - 1), 0)
            vpu_stage(2 * j + 1, 1)
            return carry
        lax.fori_loop(0, nchunk // 2, pair_body, 0)

        @pl.when(nchunk % 2 == 1)
        def _():
            vpu_stage(nchunk - 1, 0)

    def score_products(c, slot):
        k4 = k4_ref[0, chunk(c), :]
        for p in range(IDX_HEADS // 2):
            mm_scr[slot, p] = _dot(k4, rhs_s[p])

    def score_keys(c, slot):
        acc = jnp.zeros((ck, tq), F32)
        for p in range(IDX_HEADS // 2):
            s = mm_scr[slot, p]
            acc = acc + jnp.maximum(s[:, :tq], 0.0) * wt[2 * p:2 * p + 1]
            acc = acc + jnp.maximum(s[:, tq:], 0.0) * wt[2 * p + 1:2 * p + 2]
        kpos = c * ck + lax.broadcasted_iota(I32, (ck, 1), 0)
        key_scr[chunk(c), :] = jnp.where(kpos <= qpos, _order_key(acc), INT_MIN)
    pipelined(score_products, score_keys)

    def store_keys(c, val):
        key_scr[chunk(c), :] = val
    thr = _select_threshold(lambda c: key_scr[chunk(c), :], store_keys, nchunk, tq, ck, topk, idx_bits,
                            keys_on_rows=True)

    qt = q_ref[0].astype(F32).T.astype(BF16)
    rhs_a = []
    for m in range(KV_HEADS):
        blk = jnp.concatenate([qt[(2 * m) * HEAD_DIM:(2 * m + 1) * HEAD_DIM],
                               qt[(2 * m + 1) * HEAD_DIM:(2 * m + 2) * HEAD_DIM]], axis=1)
        parts = [jnp.zeros((HEAD_DIM, 2 * tq), BF16)] * m + [blk] + [jnp.zeros((HEAD_DIM, 2 * tq), BF16)] * (KV_HEADS - 1 - m)
        rhs_a.append(jnp.concatenate(parts, axis=0))
    m_scr[...] = jnp.full(m_scr.shape, NEG_BIG, F32)
    acc_scr[...] = jnp.zeros(acc_scr.shape, F32)

    def logits(c, slot):
        kc = kb_ref[0, chunk(c), :]
        for m in range(KV_HEADS):
            mm_scr[slot, m] = _dot(kc, rhs_a[m])

    def softmax_pv(c, slot):
        sel = key_scr[chunk(c), :] > thr
        pas = []
        for m in range(KV_HEADS):
            lg = mm_scr[slot, m]
            lg = jnp.concatenate([jnp.where(sel, lg[:, :tq], NEG_BIG), jnp.where(sel, lg[:, tq:], NEG_BIG)], axis=1)
            mx = m_scr[m]
            mn = jnp.maximum(mx, jnp.max(lg, axis=0, keepdims=True))
            m_scr[m] = mn
            pas.append((jnp.exp2(lg - mn).astype(BF16), jnp.exp2(mx - mn)))
        for m in range(KV_HEADS):
            p, alpha = pas[m]
            acc_scr[m] = acc_scr[m] * alpha + _dot(vt_ref[0, m, :, chunk(c)], p)

    pipelined(logits, softmax_pv)
    pieces = []
    for m in range(KV_HEADS):
        acc = acc_scr[m]
        o = acc[0:HEAD_DIM] / acc[HEAD_DIM:HEAD_DIM + 1]
        pieces += [o[:, :tq], o[:, tq:]]
    at = jnp.concatenate(pieces, axis=0)
    at = at * lax.rsqrt(jnp.mean(at * at, axis=0, keepdims=True) + EPS)
    o_ref[0] = (at.T * ga_ref[...]).astype(BF16)


def _prompt_attention(q, qi, wt, k4, kb, vt, ga):
    b, t, _ = q.shape
    tq = LANES
    assert t % ATT_CHUNK == 0
    topk = min(TOPK_MAX, t // 4)
    blk = lambda w: pl.BlockSpec((1, tq, w), lambda bi, i: (bi, i, 0))
    full = lambda r, c: pl.BlockSpec((1, r, c), lambda bi, i: (bi, 0, 0))
    return pl.pallas_call(
        functools.partial(_prompt_attn_kernel, topk=topk, idx_bits=max(1, (t - 1).bit_length())),
        out_shape=jax.ShapeDtypeStruct((b, t, ATT_WIDTH), BF16),
        grid=(b, t // tq),
        in_specs=[blk(ATT_WIDTH), blk(ATT_WIDTH),
                  pl.BlockSpec((1, IDX_HEADS, tq), lambda bi, i: (bi, 0, i)),
                  full(t, 4 * IDX_DIM), full(t, KV_WIDTH),
                  pl.BlockSpec((1, KV_HEADS, VT_ROWS, t), lambda bi, i: (bi, 0, 0, 0)),
                  pl.BlockSpec((1, ATT_WIDTH), lambda bi, i: (0, 0))],
        out_specs=blk(ATT_WIDTH),
        scratch_shapes=[pltpu.VMEM((t, tq), I32), pltpu.VMEM((2, KV_HEADS, ATT_CHUNK, 2 * tq), F32),
                        pltpu.VMEM((KV_HEADS, 1, 2 * tq), F32),
                        pltpu.VMEM((KV_HEADS, VT_ROWS, 2 * tq), F32)],
        compiler_params=_cparams(("arbitrary", "arbitrary")),
        name="prompt_attn",
    )(q, qi, wt, k4, kb, vt, ga.reshape(1, ATT_WIDTH))


def _sample_score_kernel(pt_ref, qi_ref, w_ref, kin_ref, cache_ref, key_ref, thr_ref, kbuf, sem, *,
                         layer, n_pages, npg, tnew, topk, idx_bits):
    b = pl.program_id(0)
    nb = pl.num_programs(0)
    slot = b % 2
    tq = tnew
    page = kbuf.shape[3]
    ck = npg * page
    nsteps = n_pages // npg

    def start_pages(row, sl):
        for p in range(n_pages):
            pltpu.make_async_copy(cache_ref.at[layer, pt_ref[row * n_pages + p]], kbuf.at[sl, p], sem.at[sl]).start()

    @pl.when(b == 0)
    def _():
        start_pages(0, 0)

    @pl.when(b + 1 < nb)
    def _():
        start_pages(b + 1, 1 - slot)

    pltpu.make_async_copy(cache_ref.at[layer, pl.ds(0, n_pages)], kbuf.at[slot], sem.at[slot]).wait()

    def keys_of(kt, ok):
        k_hi, k_lo = _split_bf16(kt)
        sc = _dot(qi_ref[0], jnp.concatenate([k_hi, k_lo, k_hi], axis=0))
        sc = jnp.maximum(sc, 0.0) * w_ref[0]
        acc = sc[0:tq]
        for hh in range(1, IDX_HEADS):
            acc = acc + sc[hh * tq:(hh + 1) * tq]
        return _order_key(acc) if ok is None else jnp.where(ok, _order_key(acc), INT_MIN)

    def chunk_body(c, carry):
        pages = kbuf[slot, pl.ds(c * npg, npg)]
        kt = jnp.concatenate([pages[p] for p in range(npg)], axis=1)
        key_ref[0, :, pl.ds(pl.multiple_of(c * ck, ck), ck)] = keys_of(kt, None)
        return carry
    lax.fori_loop(0, nsteps, chunk_body, 0)
    qrow = lax.broadcasted_iota(I32, (tq, ck), 0)
    col = lax.broadcasted_iota(I32, (tq, ck), 1)
    new = jnp.concatenate([kin_ref[0], jnp.zeros((kin_ref.shape[1], ck - page), F32)], axis=1)
    key_ref[0, :, pl.ds(nsteps * ck, ck)] = keys_of(new, jnp.logical_and(col <= qrow, col < tnew))

    def load_keys(c):
        return key_ref[0, :, pl.ds(pl.multiple_of(c * ck, ck), ck)]

    def store_keys(c, val):
        key_ref[0, :, pl.ds(pl.multiple_of(c * ck, ck), ck)] = val
    thr = _select_threshold(load_keys, store_keys, nsteps + 1, tq, ck, topk, idx_bits)
    thr_ref[0] = jnp.broadcast_to(thr, (tq, LANES))


def _sample_attn_kernel(pt_ref, q_ref, key_ref, keyn_ref, thr_ref, kn_ref, vn_ref, ga_ref, ck_ref, cv_ref, o_ref,
                        kbuf, vbuf, sem, m_scr, l_scr, acc_scr, *, layer, n_pages, npg, tnew):
    b, s = pl.program_id(0), pl.program_id(1)
    ns = pl.num_programs(1)
    g = b * ns + s
    slot = g % 2
    pg = kbuf.shape[1]
    tq = tnew

    def start_pages(gg, sl):
        base = (gg // ns) * n_pages + (gg % ns) * pg
        for p in range(pg):
            pid = pt_ref[base + p]
            pltpu.make_async_copy(ck_ref.at[layer, pid], kbuf.at[sl, p], sem.at[0, sl]).start()
            pltpu.make_async_copy(cv_ref.at[layer, pid], vbuf.at[sl, p], sem.at[1, sl]).start()

    @pl.when(g == 0)
    def _():
        start_pages(0, 0)

    @pl.when(g + 1 < pl.num_programs(0) * ns)
    def _():
        start_pages(g + 1, 1 - slot)

    pltpu.make_async_copy(ck_ref.at[layer, pl.ds(0, pg)], kbuf.at[slot], sem.at[0, slot]).wait()
    pltpu.make_async_copy(cv_ref.at[layer, pl.ds(0, pg)], vbuf.at[slot], sem.at[1, slot]).wait()

    @pl.when(s == 0)
    def _():
        m_scr[...] = jnp.full(m_scr.shape, NEG_BIG, F32)
        l_scr[...] = jnp.zeros(l_scr.shape, F32)
        acc_scr[...] = jnp.zeros(acc_scr.shape, F32)

    thr = thr_ref[0][:, 0:1]

    def attend(keys, kget, vget):
        sel = keys > thr
        sel2 = jnp.concatenate([sel, sel], axis=0)
        lgs = [_dot(q_ref[0, m], kget(m)) for m in range(KV_HEADS)]
        prs = []
        for m in range(KV_HEADS):
            lg = jnp.where(sel2, lgs[m], NEG_BIG)
            mx = m_scr[m]
            mn = jnp.maximum(mx, jnp.max(lg, axis=-1, keepdims=True))
            pr = jnp.exp2(lg - mn)
            alpha = jnp.exp2(mx - mn)
            l_scr[m] = l_scr[m] * alpha + jnp.sum(pr, axis=-1, keepdims=True)
            m_scr[m] = mn
            prs.append((pr.astype(BF16), alpha))
        for m in range(KV_HEADS):
            pr, alpha = prs[m]
            acc_scr[m] = acc_scr[m] * alpha + _dot_nt(pr, vget(m))

    page = kbuf.shape[4]
    for j in range(pg // npg):
        cached = lambda buf: (lambda m: jnp.concatenate(
            [buf[slot, j * npg + p, m] for p in range(npg)], axis=1).astype(BF16))
        attend(key_ref[0, :, j * npg * page:(j + 1) * npg * page], cached(kbuf), cached(vbuf))

    @pl.when(s == ns - 1)
    def _():
        attend(keyn_ref[0], lambda m: kn_ref[0, m].astype(BF16), lambda m: vn_ref[0, m].astype(BF16))
        pieces = []
        for m in range(KV_HEADS):
            o = acc_scr[m] / l_scr[m]
            pieces += [o[:tq], o[tq:]]
        a = jnp.concatenate(pieces, axis=1)
        a = a * lax.rsqrt(jnp.mean(a * a, axis=-1, keepdims=True) + EPS) * ga_ref[...]
        o_ref[0] = a.astype(BF16)


def _sample_attention(q, qi, wi, ki_new, k_new, v_new, cache_k, cache_v, cache_ki, layer, page_table, ga):
    db, tn, _ = q.shape
    page = cache_ki.shape[2]
    cache_k = cache_k.transpose(0, 1, 3, 4, 2)
    cache_v = cache_v.transpose(0, 1, 3, 4, 2)
    cache_ki = cache_ki.transpose(0, 1, 3, 2)
    n_pages = page_table.shape[1]
    npg_s = math.gcd(SAMPLE_SCORE_PAGES, n_pages)
    lpad = (n_pages // npg_s + 1) * npg_s * page
    topk = min(TOPK_MAX, (n_pages * page + tn) // 4)
    assert tn <= page
    pad_keys = lambda a: jnp.pad(a, [(0, 0)] * (a.ndim - 1) + [(0, page - tn)])
    kin = pad_keys(ki_new.transpose(0, 2, 1))
    heads = lambda a: pad_keys(a.reshape(db, tn, KV_HEADS, HEAD_DIM).transpose(0, 2, 3, 1))
    qi_hq = qi.reshape(db, tn, IDX_HEADS, IDX_DIM).transpose(0, 2, 1, 3).reshape(db, IDX_HEADS * tn, IDX_DIM)
    qi_hi = qi_hq.astype(BF16)
    qi_hq = jnp.concatenate([qi_hi, qi_hi, (qi_hq - qi_hi.astype(F32)).astype(BF16)], axis=-1)
    w_hq = wi.transpose(0, 2, 1).reshape(db, IDX_HEADS * tn, 1)
    q2 = (q.reshape(db, tn, KV_HEADS, ATT_HEADS // KV_HEADS, HEAD_DIM).transpose(0, 2, 3, 1, 4)
          .reshape(db, KV_HEADS, (ATT_HEADS // KV_HEADS) * tn, HEAD_DIM))
    pt_flat = page_table.reshape(-1).astype(I32)

    per_b = lambda *tail: pl.BlockSpec((1,) + tail, lambda b, s, pt: (b,) + (0,) * len(tail))
    row_b = lambda *tail: pl.BlockSpec((1,) + tail, lambda b, pt: (b,) + (0,) * len(tail))
    keys, thr = pl.pallas_call(
        functools.partial(_sample_score_kernel, layer=layer, n_pages=n_pages, npg=npg_s, tnew=tn, topk=topk,
                          idx_bits=max(1, (lpad - 1).bit_length())),
        out_shape=[jax.ShapeDtypeStruct((db, tn, lpad), I32),
                   jax.ShapeDtypeStruct((db, tn, LANES), I32)],
        grid_spec=pltpu.PrefetchScalarGridSpec(
            num_scalar_prefetch=1,
            grid=(db,),
            in_specs=[row_b(IDX_HEADS * tn, 3 * IDX_DIM), row_b(IDX_HEADS * tn, 1), row_b(IDX_DIM, page),
                      pl.BlockSpec(memory_space=pl.ANY)],
            out_specs=[row_b(tn, lpad), row_b(tn, LANES)],
            scratch_shapes=[pltpu.VMEM((2, n_pages, IDX_DIM, page), F32), pltpu.SemaphoreType.DMA((2,))]),
        compiler_params=_cparams(("arbitrary",)),
        name="sample_score",
    )(pt_flat, qi_hq, w_hq, kin, cache_ki)

    pg = math.gcd(SAMPLE_ATTN_GROUP, n_pages)
    npg_a = math.gcd(npg_s, pg)
    hbm = pl.BlockSpec(memory_space=pl.ANY)
    a = pl.pallas_call(
        functools.partial(_sample_attn_kernel, layer=layer, n_pages=n_pages, npg=npg_a, tnew=tn),
        out_shape=jax.ShapeDtypeStruct((db, tn, ATT_WIDTH), BF16),
        grid_spec=pltpu.PrefetchScalarGridSpec(
            num_scalar_prefetch=1,
            grid=(db, n_pages // pg),
            in_specs=[per_b(KV_HEADS, (ATT_HEADS // KV_HEADS) * tn, HEAD_DIM),
                      pl.BlockSpec((1, tn, pg * page), lambda b, s, pt: (b, 0, s)),
                      pl.BlockSpec((1, tn, page), lambda b, s, pt: (b, 0, n_pages)),
                      per_b(tn, LANES),
                      per_b(KV_HEADS, HEAD_DIM, page), per_b(KV_HEADS, HEAD_DIM, page),
                      pl.BlockSpec((1, ATT_WIDTH), lambda b, s, pt: (0, 0)), hbm, hbm],
            out_specs=per_b(tn, ATT_WIDTH),
            scratch_shapes=[pltpu.VMEM((2, pg, KV_HEADS, HEAD_DIM, page), F32),
                            pltpu.VMEM((2, pg, KV_HEADS, HEAD_DIM, page), F32),
                            pltpu.SemaphoreType.DMA((2, 2)),
                            pltpu.VMEM((KV_HEADS, 2 * tn, 1), F32), pltpu.VMEM((KV_HEADS, 2 * tn, 1), F32),
                            pltpu.VMEM((KV_HEADS, 2 * tn, HEAD_DIM), F32)]),
        compiler_params=_cparams(("arbitrary", "arbitrary")),
        name="sample_attn",
    )(pt_flat, q2, keys, keys, thr, heads(k_new), heads(v_new), ga.reshape(1, ATT_WIDTH), cache_k, cache_v)
    return a


def _outproj_kernel(a_ref, r_ref, x_ref, gt_ref, sc_ref, sh_ref, g_ref, wa_ref, wr_ref, wrt_ref, brt_ref,
                    x2_ref, h2_ref, route_ref):
    mix = _dot(a_ref[...], wa_ref[...]) + _dot(r_ref[...], wr_ref[...])
    x2 = x_ref[...] + gt_ref[0] * mix
    x2_ref[...] = x2
    h = x2 * lax.rsqrt(jnp.mean(x2 * x2, axis=-1, keepdims=True) + EPS) * g_ref[...]
    h = h * (1.0 + sc_ref[0]) + sh_ref[0]
    h2_ref[...] = h
    lg = (_dot3(h, wrt_ref[...]) + brt_ref[...]).T
    grp = [lg[g:g + 1] for g in range(N_GROUPS)]
    gmax = functools.reduce(jnp.maximum, grp)
    gden = functools.reduce(lambda u, v: u + v, [jnp.exp(g - gmax) for g in grp])
    gsel = jnp.full(gmax.shape, N_GROUPS - 1, I32)
    for g in range(N_GROUPS - 2, -1, -1):
        gsel = jnp.where(grp[g] == gmax, g, gsel)
    gw = 1.0 / gden
    el = []
    for e in range(EXPERTS_PER_GROUP):
        v = lg[N_GROUPS + e:N_GROUPS + e + 1]
        for g in range(1, N_GROUPS):
            row = N_GROUPS + g * EXPERTS_PER_GROUP + e
            v = jnp.where(gsel == g, lg[row:row + 1], v)
        el.append(v)
    emax = functools.reduce(jnp.maximum, el)
    e0 = jnp.full(emax.shape, EXPERTS_PER_GROUP - 1, I32)
    for e in range(EXPERTS_PER_GROUP - 2, -1, -1):
        e0 = jnp.where(el[e] == emax, e, e0)
    rest = [jnp.where(e0 == e, -jnp.inf, el[e]) for e in range(EXPERTS_PER_GROUP)]
    rmax = functools.reduce(jnp.maximum, rest)
    e1 = jnp.full(emax.shape, EXPERTS_PER_GROUP - 1, I32)
    for e in range(EXPERTS_PER_GROUP - 2, -1, -1):
        e1 = jnp.where(jnp.logical_and(rest[e] == rmax, e0 != e), e, e1)
    p1 = jnp.exp(rmax - emax)
    w0 = gw * (1.0 / (1.0 + p1))
    w1 = gw * (p1 / (1.0 + p1))
    base = gsel * EXPERTS_PER_GROUP
    rid = lax.broadcasted_iota(I32, lg.shape, 0)
    rt = jnp.where(rid == 0, (base + e0).astype(F32),
                   jnp.where(rid == 1, (base + e1).astype(F32),
                             jnp.where(rid == 2, w0, jnp.where(rid == 3, w1, 0.0))))
    route_ref[...] = rt.T


def _outproj(a, r, x2d, gt, sc, sh, g2, wa, wr, wrt, brt, tiles_per_group):
    n, d = x2d.shape
    tm = min(PROJ_TILE, n)
    rr = gt.shape[1]
    row = lambda w: pl.BlockSpec((tm, w), lambda i: (i, 0))
    mod = pl.BlockSpec((1, rr, d), lambda i: (i // tiles_per_group, 0, 0))
    const = lambda s: pl.BlockSpec(s, lambda i: (0, 0))
    return pl.pallas_call(
        _outproj_kernel,
        out_shape=[jax.ShapeDtypeStruct((n, d), F32), jax.ShapeDtypeStruct((n, d), F32),
                   jax.ShapeDtypeStruct((n, LANES), F32)],
        grid=(n // tm,),
        in_specs=[row(ATT_WIDTH), row(HG_WIDTH), row(d), mod, mod, mod, const((1, d)),
                  const(wa.shape), const(wr.shape), const(wrt.shape), const(brt.shape)],
        out_specs=[row(d), row(d), row(LANES)],
        compiler_params=_cparams(("arbitrary",)),
        name="outproj",
    )(a, r, x2d, gt, sc, sh, g2.reshape(1, d), wa, wr, wrt, brt)


def _gather_pipeline(step, nsteps, idx_hbm, src_hbm, buf, idx_smem, isem, rsem):
    nrows = buf.shape[1]
    slot = step % 2

    def idx_copy(b, sl):
        return pltpu.make_async_copy(idx_hbm.at[b], idx_smem.at[sl], isem.at[sl])

    def start_rows(sl):
        for r in range(nrows):
            pltpu.make_async_copy(src_hbm.at[pl.ds(idx_smem[sl, 0, r], 1), :],
                                  buf.at[sl, pl.ds(r, 1), :], rsem.at[sl]).start()

    @pl.when(step == 0)
    def _():
        idx_copy(0, 0).start()
        idx_copy(0, 0).wait()
        start_rows(0)

        @pl.when(nsteps > 1)
        def _():
            idx_copy(1, 1).start()

    @pl.when(step + 1 < nsteps)
    def _():
        idx_copy(step + 1, 1 - slot).wait()
        start_rows(1 - slot)

    @pl.when(step + 2 < nsteps)
    def _():
        idx_copy(step + 2, slot).start()

    pltpu.make_async_copy(src_hbm.at[pl.ds(0, nrows), :], buf.at[slot], rsem.at[slot]).wait()
    return slot


def _ffn_kernel(be_ref, nb_ref, tok_ref, h_ref, w1_ref, w3_ref, w2_ref, y_ref, xbuf, idx_smem, isem, rsem):
    j = pl.program_id(0)

    @pl.when(j < nb_ref[0])
    def _():
        slot = _gather_pipeline(j, nb_ref[0], tok_ref, h_ref, xbuf, idx_smem, isem, rsem)
        xb = xbuf[slot].astype(BF16)
        u = _dot(xb, w1_ref[0])
        g = _dot(xb, w3_ref[0])
        y_ref[...] = _dot((_silu(u) * g).astype(BF16), w2_ref[0])

    @pl.when(j >= nb_ref[0])
    def _():
        y_ref[...] = jnp.zeros(y_ref.shape, F32)


def _ffn(blk_e, n_used, tok, h2, w1, w3, w2, blk):
    nb = tok.shape[0]
    d = h2.shape[1]
    wspec = lambda s: pl.BlockSpec((1,) + s, lambda j, be, nu: (be[j], 0, 0))
    return pl.pallas_call(
        _ffn_kernel,
        out_shape=jax.ShapeDtypeStruct((nb * blk, d), F32),
        grid_spec=pltpu.PrefetchScalarGridSpec(
            num_scalar_prefetch=2,
            grid=(nb,),
            in_specs=[pl.BlockSpec(memory_space=pl.ANY), pl.BlockSpec(memory_space=pl.ANY),
                      wspec(w1.shape[1:]), wspec(w3.shape[1:]), wspec(w2.shape[1:])],
            out_specs=pl.BlockSpec((blk, d), lambda j, be, nu: (j, 0)),
            scratch_shapes=[pltpu.VMEM((2, blk, d), F32), pltpu.SMEM((2, 1, blk), I32),
                            pltpu.SemaphoreType.DMA((2,)), pltpu.SemaphoreType.DMA((2,))]),
        compiler_params=_cparams(("arbitrary",)),
        name="moe_ffn",
    )(blk_e, n_used, tok, h2, w1, w3, w2)


def _combine_kernel(dd_ref, y_ref, x2_ref, route_ref, gt_ref, sc_ref, sh_ref, g_ref, o_ref,
                    ybuf, idx_smem, isem, rsem):
    tm = x2_ref.shape[0]
    slot = _gather_pipeline(pl.program_id(0), pl.num_programs(0), dd_ref, y_ref, ybuf, idx_smem, isem, rsem)
    rt = route_ref[...]
    moe = ybuf[slot, 0:tm] * rt[:, 2:3] + ybuf[slot, tm:2 * tm] * rt[:, 3:4]
    x = x2_ref[...] + gt_ref[0] * moe
    y = x * lax.rsqrt(jnp.mean(x * x, axis=-1, keepdims=True) + EPS) * g_ref[...]
    o_ref[...] = y * (1.0 + sc_ref[0]) + sh_ref[0]


def _combine(dd, yb, x2, route, gt, sc, sh, gf, tiles_per_group):
    n, d = x2.shape
    tm = dd.shape[2] // 2
    rr = gt.shape[1]
    row = pl.BlockSpec((tm, d), lambda i: (i, 0))
    mod = pl.BlockSpec((1, rr, d), lambda i: (i // tiles_per_group, 0, 0))
    hbm = pl.BlockSpec(memory_space=pl.ANY)
    return pl.pallas_call(
        _combine_kernel,
        out_shape=jax.ShapeDtypeStruct((n, d), F32),
        grid=(n // tm,),
        in_specs=[hbm, hbm, row, pl.BlockSpec((tm, LANES), lambda i: (i, 0)), mod, mod, mod,
                  pl.BlockSpec((1, d), lambda i: (0, 0))],
        out_specs=row,
        scratch_shapes=[pltpu.VMEM((2, 2 * tm, d), F32), pltpu.SMEM((2, 1, 2 * tm), I32),
                        pltpu.SemaphoreType.DMA((2,)), pltpu.SemaphoreType.DMA((2,))],
        compiler_params=_cparams(("arbitrary",)),
        name="moe_combine",
    )(dd, yb, x2, route, gt, sc, sh, gf.reshape(1, d))


def _dispatch(route, blk):
    n = route.shape[0]
    flat_e = route[:, 0:2].astype(I32).reshape(-1)
    a = flat_e.shape[0]
    onehot = (flat_e[:, None] == jnp.arange(N_EXPERTS, dtype=I32)[None, :]).astype(I32)
    csum = jnp.cumsum(onehot, axis=0)
    rank = jnp.sum((csum - onehot) * onehot, axis=1)
    counts = csum[-1]
    padded = (counts + blk - 1) // blk * blk
    pad_end = jnp.cumsum(padded)
    pad_start = pad_end - padded
    dest = pad_start[flat_e] + rank
    nb = -(-a // blk) + N_EXPERTS
    tok = jnp.zeros((nb * blk,), I32).at[dest].set(jnp.arange(a, dtype=I32) // 2)
    blk_e = jnp.minimum(jnp.searchsorted(pad_end, jnp.arange(nb, dtype=I32) * blk, side='right'),
                        N_EXPERTS - 1).astype(I32)
    n_used = (pad_end[-1] // blk).astype(I32).reshape(1)
    dest2 = dest.reshape(n, 2)
    return tok.reshape(nb, 1, blk), blk_e, n_used, dest2[:, 0], dest2[:, 1]


def _mods(m, n_chunks, per_token_rows):
    parts = jnp.split(m, n_chunks, axis=-1)
    if per_token_rows is None:
        return [p[:, None, :] for p in parts]
    g, d = parts[0].shape
    tm = min(PROJ_TILE, g * per_token_rows)
    return [jnp.repeat(p, per_token_rows, axis=0).reshape(-1, tm, d) for p in parts]


def _layer(x, mod6, modf, pos, s0, attend, lb, wts, final_g, per_token):
    b, t, d = x.shape
    n = b * t
    (norm1_g, norm2_g, w_r, gk, ga, hg_out_g, wa, wr, wrt, brt, w1, w3, w2) = wts
    tm = min(PROJ_TILE, n)
    tiles_per_group = 1 if per_token else t // tm
    sh1, sc1, gt1, sh2, sc2, gt2 = _mods(mod6, 6, t if per_token else None)
    shf, scf = _mods(modf, 2, t if per_token else None)
    tabs = _rope_tables(jnp.tile(pos, tm // t) if per_token else pos)
    x2d = x.reshape(n, d)
    sq = lambda arr: arr.reshape(b, t, arr.shape[-1])
    if per_token:
        q, k, v, qi, ki, wi, hq, hf, hi, hg = _proj(x2d, sc1, sh1, norm1_g, w_r, gk, tabs, tiles_per_group, 0)
        a = attend(sq(q), sq(qi), sq(wi), sq(ki), sq(k), sq(v), ga)
        k_out, v_out, ki_out = k.reshape(b, t, KV_HEADS, HEAD_DIM), v.reshape(b, t, KV_HEADS, HEAD_DIM), sq(ki)
    else:
        q, kt, vt, kb, vto, qi, kit, k4, wt, hq, hf, hi, hg = _proj(x2d, sc1, sh1, norm1_g, w_r, gk, tabs,
                                                                   tiles_per_group, b)
        a = attend(sq(q), sq(qi), wt, sq(k4), sq(kb), vto, ga)
        heads = lambda xt: xt.reshape(b, KV_HEADS, HEAD_DIM, t).transpose(0, 3, 1, 2)
        k_out, v_out, ki_out = heads(kt), heads(vt), kit.transpose(0, 2, 1)
    r, s_t = _hgrn(sq(hq), sq(hf), sq(hi), sq(hg), lb, hg_out_g, s0)
    x2, h2, route = _outproj(a.reshape(n, -1), r.reshape(n, -1), x2d, gt1, sc2, sh2, norm2_g,
                             wa, wr, wrt, brt, tiles_per_group)
    blk = MOE_BLOCK if n >= 8 * MOE_BLOCK else 64
    tok, blk_e, n_used, d0, d1 = _dispatch(route, blk)
    yb = _ffn(blk_e, n_used, tok, h2, w1, w3, w2, blk)
    dd = jnp.concatenate([d0.reshape(n // tm, 1, tm), d1.reshape(n // tm, 1, tm)], axis=2)
    y = _combine(dd, yb, x2, route, gt2, scf, shf, final_g, tiles_per_group)
    return y.reshape(b, t, d), k_out, v_out, ki_out, s_t


def kernel(x_prompt, x_sample, cache_k, cache_v, cache_kidx, state_hgrn, page_table, c_prompt, c_sample,
           ada_w, ada_b, norm1_g, norm2_g, w_in, idx_k_g, hg_lb_logits, attn_out_g, hg_out_g, w_out,
           w_group, b_group, w_expert_router, b_expert_router, w1, w3, w2, final_g, ada_final_w, ada_final_b):
    depth = ada_w.shape[0]
    assert depth == 1, "the final adaLN norm is fused into the (single) layer's combine kernel"
    bp, tp, d = x_prompt.shape
    bs, ts, _ = x_sample.shape
    past = page_table.shape[1] * cache_kidx.shape[2]
    lb_all = jnp.cumsum(jax.nn.softmax(hg_lb_logits.astype(F32), axis=0), axis=0)
    n_c = bp + bs
    c_all = jnp.concatenate([c_prompt, c_sample, jnp.zeros((-n_c % 16, d), F32)], axis=0)
    modf = _ada(c_all, ada_final_w, ada_final_b)
    l = 0
    mod6 = _ada(c_all, ada_w[l], ada_b[l])

    seg = [0]
    for s in (ATT_WIDTH, KV_WIDTH, KV_WIDTH, IDX_HEADS * IDX_DIM, IDX_DIM, IDX_HEADS,
              HG_WIDTH, HG_WIDTH, HG_WIDTH, HG_WIDTH):
        seg.append(seg[-1] + s)
    wl = w_in[l]
    col = lambda i: wl[:, seg[i]:seg[i + 1]]
    zpad = lambda w: jnp.zeros((d, w), wl.dtype)
    w_r = jnp.concatenate([col(0), col(1), col(2), col(3), col(6), col(7), col(8), col(9),
                           col(4), zpad(LANES - IDX_DIM), col(5), zpad(LANES - IDX_HEADS)], axis=1).astype(BF16)
    gk = jnp.concatenate([idx_k_g[l], jnp.zeros((LANES - IDX_DIM,), F32)]).reshape(1, LANES)
    wa = w_out[l, :ATT_WIDTH].astype(BF16)
    wr = w_out[l, ATT_WIDTH:].astype(BF16)
    n_rt = N_GROUPS + N_EXPERTS
    wrt = jnp.concatenate([w_group[l], w_expert_router[l], jnp.zeros((d, LANES - n_rt), F32)], axis=1)
    brt = jnp.concatenate([b_group[l], b_expert_router[l], jnp.zeros((LANES - n_rt,), F32)]).reshape(1, LANES)
    wts = (norm1_g[l], norm2_g[l], w_r, gk, attn_out_g[l], hg_out_g[l], wa, wr, wrt, brt,
           w1[l].astype(BF16), w3[l].astype(BF16), w2[l].astype(BF16))

    def attend_s(q, qi, wi, ki, k, v, ga):
        return _sample_attention(q, qi, wi, ki, k, v, cache_k, cache_v, cache_kidx, l, page_table, ga)

    s0_p = jnp.zeros((bp, HG_HEADS, HG_DK, HG_DV), F32)
    yp, kp, vp, kip, sp = _layer(x_prompt, mod6[:bp], modf[:bp], jnp.arange(tp), s0_p, _prompt_attention,
                                 lb_all[l], wts, final_g, per_token=False)
    ys, ks, vs, kis, ss = _layer(x_sample, mod6[bp:n_c], modf[bp:n_c], past + jnp.arange(ts), state_hgrn[l],
                                 attend_s, lb_all[l], wts, final_g, per_token=True)
    return (yp, ys, kp[None], vp[None], kip[None], sp[None], ks[None], vs[None], kis[None], ss[None])
```

```python
import functools
import math

import jax
import jax.numpy as jnp
from jax import lax
from jax.experimental import pallas as pl
from jax.experimental.pallas import tpu as pltpu

F32 = jnp.float32
BF16 = jnp.bfloat16
I32 = jnp.int32

ATT_HEADS = 8
KV_HEADS = 4
HEAD_DIM = 64
ATT_WIDTH = ATT_HEADS * HEAD_DIM
KV_WIDTH = KV_HEADS * HEAD_DIM
ROT_HALF = HEAD_DIM // 8
ROPE_THETA = 500000.0
IDX_HEADS = 8
IDX_DIM = 64
TOPK_MAX = 256
HG_HEADS = 4
HG_DK = 128
HG_DV = 128
HG_WIDTH = HG_HEADS * HG_DV
HG_CHUNK = 64
N_GROUPS = 4
EXPERTS_PER_GROUP = 4
N_EXPERTS = N_GROUPS * EXPERTS_PER_GROUP
D_EXPERT = 512
MOE_BLOCK = 256
EPS = 1e-6

LANES = 128
INT_MIN = -2 ** 31
NEG_BIG = -1e30
VMEM_LIMIT = 56 * 1024 * 1024
PROJ_TILE = 256
ATT_CHUNK = 512
VT_ROWS = HEAD_DIM + 16
LOG2E = 1.4426950408889634
BISECT_BITS_PER_TRIP = 4
SAMPLE_SCORE_PAGES = 16
SAMPLE_ATTN_GROUP = 32

_Z_Q, _Z_K, _Z_V, _Z_QI, _Z_HQ, _Z_HF, _Z_HI, _Z_HG, _Z_KI, _Z_WI, _Z_END = (
    0, 512, 768, 1024, 1536, 2048, 2560, 3072, 3584, 3712, 3840)


def _cparams(sem):
    return pltpu.CompilerParams(dimension_semantics=sem, vmem_limit_bytes=VMEM_LIMIT)


def _split_bf16(x):
    hi = x.astype(BF16)
    lo = (x - hi.astype(F32)).astype(BF16)
    return hi, lo


def _dot(a, b):
    return jnp.dot(a, b, preferred_element_type=F32)


def _dot_nt(a, b):
    return lax.dot_general(a, b, (((1,), (1,)), ((), ())), preferred_element_type=F32)


def _dot_tn(a, b):
    return lax.dot_general(a, b, (((0,), (0,)), ((), ())), preferred_element_type=F32)


def _dot3(a, b):
    ah, al = _split_bf16(a)
    bh, bl = _split_bf16(b)
    return _dot(ah, bh) + (_dot(ah, bl) + _dot(al, bh))


def _dot3_nt(a, b):
    ah, al = _split_bf16(a)
    bh, bl = _split_bf16(b)
    return _dot_nt(ah, bh) + (_dot_nt(ah, bl) + _dot_nt(al, bh))


def _silu(x):
    return x * (1.0 / (1.0 + jnp.exp(-x)))


def _sigmoid(x):
    return 1.0 / (1.0 + jnp.exp(-x))


def _ada_kernel(c_ref, w_ref, b_ref, o_ref):
    o_ref[...] = _dot3(_silu(c_ref[...]), w_ref[...]) + b_ref[...]


def _ada(c, w, b):
    r, d = c.shape
    e = w.shape[1]
    te = 1024
    return pl.pallas_call(
        _ada_kernel,
        out_shape=jax.ShapeDtypeStruct((r, e), F32),
        grid=(e // te,),
        in_specs=[pl.BlockSpec((r, d), lambda j: (0, 0)),
                  pl.BlockSpec((d, te), lambda j: (0, j)),
                  pl.BlockSpec((1, te), lambda j: (0, j))],
        out_specs=pl.BlockSpec((r, te), lambda j: (0, j)),
        compiler_params=_cparams(("arbitrary",)),
        name="ada",
    )(c, w, b.reshape(1, e))


def _rope_tables(pos):
    p = pos.shape[0]
    inv = ROPE_THETA ** (-jnp.arange(ROT_HALF, dtype=F32) * (2.0 / (2 * ROT_HALF)))
    ang = pos.astype(F32)[:, None] * inv[None, :]
    c, s = jnp.cos(ang), jnp.sin(ang)
    rest = HEAD_DIM - 2 * ROT_HALF
    one, zero, z8 = jnp.ones((p, rest), F32), jnp.zeros((p, rest), F32), jnp.zeros((p, ROT_HALF), F32)
    cos64 = jnp.concatenate([c, c, one], axis=1)
    sa64 = jnp.concatenate([-s, z8, zero], axis=1)
    sb64 = jnp.concatenate([z8, s, zero], axis=1)
    dup = lambda t: jnp.concatenate([t, t], axis=1)
    return dup(cos64), dup(sa64), dup(sb64)


def _proj_kernel(x_ref, sc_ref, sh_ref, g_ref, w_ref, gk_ref, cos_ref, sa_ref, sb_ref, *out_refs, seq_major):
    x = x_ref[...]
    ms = jnp.mean(x * x, axis=-1, keepdims=True)
    h = x * lax.rsqrt(ms + EPS) * g_ref[...]
    h = h * (1.0 + sc_ref[0]) + sh_ref[0]
    z = _dot(h.astype(BF16), w_ref[...])
    cos, sa, sb = cos_ref[...], sa_ref[...], sb_ref[...]

    def rope(zs):
        n = zs.shape[1] // LANES
        rep = (lambda t: jnp.concatenate([t] * n, axis=1)) if n > 1 else (lambda t: t)
        w = zs.shape[1]
        return (zs * rep(cos) + pltpu.roll(zs, w - ROT_HALF, 1) * rep(sa)
                + pltpu.roll(zs, ROT_HALF, 1) * rep(sb))

    q = (rope(z[:, _Z_Q:_Z_K]) * (HEAD_DIM ** -0.5 * LOG2E)).astype(BF16)
    k = rope(z[:, _Z_K:_Z_V])
    v = z[:, _Z_V:_Z_QI]
    qi = rope(z[:, _Z_QI:_Z_HQ]) * (IDX_DIM ** -0.5)
    ks = z[:, _Z_KI:_Z_WI]
    kms = jnp.sum(ks * ks, axis=-1, keepdims=True) * (1.0 / IDX_DIM)
    kn = rope(ks * lax.rsqrt(kms + EPS) * gk_ref[...])
    ws = z[:, _Z_WI:_Z_END] * (IDX_HEADS ** -0.5)
    if seq_major:
        q_ref, kt_ref, vt_ref, kb_ref, vto_ref, qi_ref, kit_ref, k4_ref, wt_ref = out_refs[:9]
        kt_ref[0] = k.T
        vt = v.T
        vt_ref[0] = vt
        kb_ref[...] = k.astype(BF16)
        for m in range(KV_HEADS):
            vto_ref[0, m, 0:HEAD_DIM, :] = vt[m * HEAD_DIM:(m + 1) * HEAD_DIM].astype(BF16)
            vto_ref[0, m, HEAD_DIM:VT_ROWS, :] = jnp.ones((VT_ROWS - HEAD_DIM, x.shape[0]), BF16)
        kit_ref[0] = kn.T[:IDX_DIM]
        hi = kn.astype(BF16).astype(F32)
        lo = kn - hi
        k4_ref[...] = jnp.concatenate([hi + pltpu.roll(lo, IDX_DIM, 1), hi + pltpu.roll(hi, IDX_DIM, 1)],
                                      axis=1).astype(BF16)
        wt_ref[0] = ws.T[:IDX_HEADS]
    else:
        q_ref, k_ref, v_ref, qi_ref, ki_ref, wi_ref = out_refs[:6]
        k_ref[...] = k
        v_ref[...] = v
        ki_ref[...] = kn[:, :IDX_DIM]
        wi_ref[...] = ws[:, :IDX_HEADS]
    q_ref[...] = q
    qi_ref[...] = qi
    hq_ref, hf_ref, hi_ref, hg_ref = out_refs[-4:]
    hq_ref[...] = z[:, _Z_HQ:_Z_HF]
    hf_ref[...] = z[:, _Z_HF:_Z_HI]
    hi_ref[...] = z[:, _Z_HI:_Z_HG]
    hg_ref[...] = z[:, _Z_HG:_Z_KI]


def _proj(x2d, sc, sh, g1, w_r, gk, tabs, tiles_per_group, seq_batch):
    n, d = x2d.shape
    tm = min(PROJ_TILE, n)
    nt = n // tm
    r = sc.shape[1]
    pt = tabs[0].shape[0] // tm
    row = lambda w, t: (jax.ShapeDtypeStruct((n, w), t), pl.BlockSpec((tm, w), lambda i: (i, 0)))
    mod = pl.BlockSpec((1, r, d), lambda i: (i // tiles_per_group, 0, 0))
    tab = pl.BlockSpec((tm, LANES), lambda i: (i % pt, 0))
    if seq_batch:
        t = n // seq_batch
        tps = t // tm
        seq = lambda w: (jax.ShapeDtypeStruct((seq_batch, w, t), F32),
                         pl.BlockSpec((1, w, tm), lambda i: (i // tps, 0, i % tps)))
        vto = (jax.ShapeDtypeStruct((seq_batch, KV_HEADS, VT_ROWS, t), BF16),
               pl.BlockSpec((1, KV_HEADS, VT_ROWS, tm), lambda i: (i // tps, 0, 0, i % tps)))
        outs = [row(ATT_WIDTH, BF16), seq(KV_WIDTH), seq(KV_WIDTH), row(KV_WIDTH, BF16), vto,
                row(ATT_WIDTH, F32), seq(IDX_DIM), row(4 * IDX_DIM, BF16), seq(IDX_HEADS)]
    else:
        outs = [row(ATT_WIDTH, BF16), row(KV_WIDTH, F32), row(KV_WIDTH, F32),
                row(ATT_WIDTH, F32), row(IDX_DIM, F32), row(IDX_HEADS, F32)]
    outs += [row(HG_WIDTH, F32)] * 4
    return pl.pallas_call(
        functools.partial(_proj_kernel, seq_major=bool(seq_batch)),
        out_shape=[s for s, _ in outs],
        grid=(nt,),
        in_specs=[pl.BlockSpec((tm, d), lambda i: (i, 0)), mod, mod,
                  pl.BlockSpec((1, d), lambda i: (0, 0)),
                  pl.BlockSpec(w_r.shape, lambda i: (0, 0)),
                  pl.BlockSpec((1, LANES), lambda i: (0, 0)),
                  tab, tab, tab],
        out_specs=[b for _, b in outs],
        compiler_params=_cparams(("arbitrary",)),
        name="proj",
    )(x2d, sc, sh, g1.reshape(1, d), w_r, gk, *tabs)


def _hgrn_kernel(hq_ref, hf_ref, hi_ref, hg_ref, lb_ref, g_ref, s0_ref, r_ref, st_ref, s_scr, *, chunk, nchunk):
    t = pl.program_id(1)

    @pl.when(t == 0)
    def _():
        for hh in range(HG_HEADS):
            s_scr[hh] = s0_ref[0, hh].T

    tt = chunk * nchunk
    row = lax.broadcasted_iota(I32, (tt, tt), 0)
    col = lax.broadcasted_iota(I32, (tt, tt), 1)
    causal = jnp.logical_and(col <= row, row // chunk == col // chunk)
    tri = causal.astype(BF16)
    rows = lambda c: slice(c * chunk, (c + 1) * chunk)
    lanes = lambda hh: slice(hh * HG_DK, (hh + 1) * HG_DK)
    lb = lb_ref[...]
    f = lb + (1.0 - lb) * _sigmoid(hf_ref[0])
    logf = jnp.log(f)
    kk = 1.0 - f
    l0 = logf.astype(BF16)
    r1 = logf - l0.astype(F32)
    l1 = r1.astype(BF16)
    l2 = (r1 - l1.astype(F32)).astype(BF16)
    b = _dot(tri, l0) + (_dot(tri, l1) + _dot(tri, l2))
    bls = [b[(c + 1) * chunk - 1:(c + 1) * chunk, :] for c in range(nchunk)]
    bl_rows = jnp.concatenate([jnp.broadcast_to(bl, (chunk, bl.shape[1])) for bl in bls], axis=0)
    q_in = (hq_ref[0] * (HG_DK ** -0.5) * jnp.exp(b)).astype(BF16)
    k_in = (kk * jnp.exp(-b)).astype(BF16)
    k_st = (kk * jnp.exp(bl_rows - b)).astype(BF16)
    vb = hi_ref[0].astype(BF16)
    intra = []
    for hh in range(HG_HEADS):
        att = jnp.where(causal, _dot_nt(q_in[:, lanes(hh)], k_in[:, lanes(hh)]), 0.0)
        intra.append(_dot(att.astype(BF16), vb[:, lanes(hh)]))
    upd = {(c, hh): _dot_tn(vb[rows(c), lanes(hh)], k_st[rows(c), lanes(hh)])
           for c in range(nchunk) for hh in range(HG_HEADS)}
    decay = [jnp.exp(bl) for bl in bls]
    for hh in range(HG_HEADS):
        os = []
        for c in range(nchunk):
            st = s_scr[hh]
            os.append(_dot_nt(q_in[rows(c), lanes(hh)], st.astype(BF16)))
            s_scr[hh] = st * decay[c][:, lanes(hh)] + upd[c, hh]
        o = jnp.concatenate(os, axis=0) + intra[hh]
        on = o * lax.rsqrt(jnp.mean(o * o, axis=-1, keepdims=True) + EPS) * g_ref[:, lanes(hh)]
        r_ref[0, :, lanes(hh)] = (on * _silu(hg_ref[0, :, lanes(hh)])).astype(BF16)

    @pl.when(t == pl.num_programs(1) - 1)
    def _():
        for hh in range(HG_HEADS):
            st_ref[0, hh] = s_scr[hh].T


def _hgrn(hq, hf, hi, hg, lb, out_g, s0):
    b, t, w = hq.shape
    chunk = min(HG_CHUNK, t)
    assert t % chunk == 0
    nchunk = min(4, t // chunk)
    tt = chunk * nchunk
    seq = pl.BlockSpec((1, tt, w), lambda i, j: (i, j, 0))
    vec = pl.BlockSpec((1, w), lambda i, j: (0, 0))
    state = pl.BlockSpec((1, HG_HEADS, HG_DK, HG_DV), lambda i, j: (i, 0, 0, 0))
    return pl.pallas_call(
        functools.partial(_hgrn_kernel, chunk=chunk, nchunk=nchunk),
        out_shape=[jax.ShapeDtypeStruct((b, t, w), BF16), jax.ShapeDtypeStruct(s0.shape, F32)],
        grid=(b, t // tt),
        in_specs=[seq, seq, seq, seq, vec, vec, state],
        out_specs=[seq, state],
        scratch_shapes=[pltpu.VMEM((HG_HEADS, HG_DV, HG_DK), F32)],
        compiler_params=_cparams(("arbitrary", "arbitrary")),
        name="hgrn",
    )(hq, hf, hi, hg, lb.reshape(1, w), out_g.reshape(1, w), s0)


def _order_key(score):
    bits = pltpu.bitcast(score, I32)
    bits = jnp.where(bits == INT_MIN, 0, bits)
    return jnp.where(bits >= 0, bits, bits ^ 0x7FFFFFFF)


def _select_threshold(load_keys, store_keys, nchunk, tq, ck, topk, idx_bits, keys_on_rows=False):
    kax = 0 if keys_on_rows else 1
    kshape = (ck, tq) if keys_on_rows else (tq, ck)
    vshape = (1, tq) if keys_on_rows else (tq, 1)

    def count(pred):
        def body(c, acc):
            m = jnp.where(pred(load_keys(c), c), 1.0, 0.0)
            if keys_on_rows:
                parts = [m[j * 8:(j + 1) * 8] for j in range(8)]
                for j in range(8, ck // 8):
                    parts[j % 8] = parts[j % 8] + m[j * 8:(j + 1) * 8]
                while len(parts) > 1:
                    parts = [a + b for a, b in zip(parts[0::2], parts[1::2])]
                part = parts[0]
            else:
                parts = [m[:, j * LANES:(j + 1) * LANES] for j in range(ck // LANES)]
                while len(parts) > 1:
                    parts = [a + b for a, b in zip(parts[0::2], parts[1::2])] + ([parts[-1]] if len(parts) % 2 else [])
                part = parts[0]
            return acc + part
        acc = jnp.zeros((8, tq) if keys_on_rows else (tq, LANES), F32)
        if isinstance(nchunk, int):
            for c in range(nchunk):
                acc = body(c, acc)
        else:
            acc = lax.fori_loop(0, nchunk, body, acc)
        return jnp.sum(acc, axis=kax, keepdims=True)

    kf = float(topk)
    zero_i = jnp.zeros(vshape, I32)

    def bisect(nbits, count_ge, done0):
        def cond(st):
            return jnp.logical_and(st[0] < nbits, jnp.min(st[2]) < 0.5)

        def body(st):
            it, tu, done, hit_u = st
            for _ in range(BISECT_BITS_PER_TRIP):
                cand = tu | lax.shift_left(jnp.int32(1), nbits - 1 - it)
                cnt = count_ge(cand)
                active = done < 0.5
                tu = jnp.where(jnp.logical_and(active, cnt >= kf), cand, tu)
                hit = jnp.logical_and(active, cnt == kf)
                hit_u = jnp.where(hit, cand, hit_u)
                done = jnp.where(hit, 1.0, done)
                it = it + 1
            return it, tu, done, hit_u
        return lax.while_loop(cond, body, (jnp.int32(0), zero_i, done0, zero_i))[1:]

    tu, done, hit_u = bisect(32, lambda u: count(lambda key, c: key >= (u ^ INT_MIN)), jnp.zeros(vshape, F32))
    ts = tu ^ INT_MIN
    tsel = (hit_u ^ INT_MIN) - 1
    need_tie = jnp.logical_and(done < 0.5, tu != 0)
    thr = jnp.where(done > 0.5, tsel, ts)

    @pl.when(jnp.max(jnp.where(need_tie, 1.0, 0.0)) > 0.5)
    def _():
        rank = kf - count(lambda key, c: key > ts)
        pos_of = lambda c: c * ck + lax.broadcasted_iota(I32, kshape, kax)

        def jbody(it, ju):
            cand = ju | lax.shift_left(jnp.int32(1), idx_bits - 1 - it)
            pred = lambda key, c: jnp.logical_and(key == ts, pos_of(c) < cand)
            return jnp.where(count(pred) < rank, cand, ju)
        ju = lax.fori_loop(0, idx_bits, jbody, zero_i)

        def fix(c, carry):
            key = load_keys(c)
            bump = jnp.logical_and(jnp.logical_and(key == ts, pos_of(c) <= ju), need_tie)
            store_keys(c, jnp.where(bump, key + 1, key))
            return carry
        lax.fori_loop(0, nchunk, fix, 0)
    return thr


def _prompt_attn_kernel(q_ref, qi_ref, wt_ref, k4_ref, kb_ref, vt_ref, ga_ref, o_ref,
                        key_scr, mm_scr, m_scr, acc_scr, *, topk, idx_bits):
    tq, ck = q_ref.shape[1], ATT_CHUNK
    i = pl.program_id(1)
    nchunk = (i * tq + tq + ck - 1) // ck
    qpos = i * tq + lax.broadcasted_iota(I32, (1, tq), 1)

    qit = qi_ref[0].T
    hit = qit.astype(BF16)
    lot = (qit - hit.astype(F32)).astype(BF16)
    zero = jnp.zeros((IDX_DIM, tq), BF16)

    def idx_rhs(hh):
        hs = slice(hh * IDX_DIM, (hh + 1) * IDX_DIM)
        return jnp.concatenate([hit[hs], hit[hs], lot[hs], zero], axis=0)
    rhs_s = [jnp.concatenate([idx_rhs(2 * p), idx_rhs(2 * p + 1)], axis=1) for p in range(IDX_HEADS // 2)]
    wt = wt_ref[0]

    chunk = lambda c: pl.ds(pl.multiple_of(c * ck, ck), ck)

    def pipelined(mxu_stage, vpu_stage):
        mxu_stage(0, 0)

        def pair_body(j, carry):
            mxu_stage(2 * j + 1, 1)
            vpu_stage(2 * j, 0)
            mxu_stage(jnp.minimum(2 * j + 2, nchunk - 1), 0)
            vpu_stage(2 * j + 1, 1)
            return carry
        lax.fori_loop(0, nchunk // 2, pair_body, 0)

        @pl.when(nchunk % 2 == 1)
        def _():
            vpu_stage(nchunk - 1, 0)

    def score_products(c, slot):
        k4 = k4_ref[0, chunk(c), :]
        for p in range(IDX_HEADS // 2):
            mm_scr[slot, p] = _dot(k4, rhs_s[p])

    def score_keys(c, slot):
        acc = jnp.zeros((ck, tq), F32)
        for p in range(IDX_HEADS // 2):
            s = mm_scr[slot, p]
            acc = acc + jnp.maximum(s[:, :tq], 0.0) * wt[2 * p:2 * p + 1]
            acc = acc + jnp.maximum(s[:, tq:], 0.0) * wt[2 * p + 1:2 * p + 2]
        kpos = c * ck + lax.broadcasted_iota(I32, (ck, 1), 0)
        key_scr[chunk(c), :] = jnp.where(kpos <= qpos, _order_key(acc), INT_MIN)
    pipelined(score_products, score_keys)

    def store_keys(c, val):
        key_scr[chunk(c), :] = val
    thr = _select_threshold(lambda c: key_scr[chunk(c), :], store_keys, nchunk, tq, ck, topk, idx_bits,
                            keys_on_rows=True)

    qt = q_ref[0].astype(F32).T.astype(BF16)
    rhs_a = []
    for m in range(KV_HEADS):
        blk = jnp.concatenate([qt[(2 * m) * HEAD_DIM:(2 * m + 1) * HEAD_DIM],
                               qt[(2 * m + 1) * HEAD_DIM:(2 * m + 2) * HEAD_DIM]], axis=1)
        parts = [jnp.zeros((HEAD_DIM, 2 * tq), BF16)] * m + [blk] + [jnp.zeros((HEAD_DIM, 2 * tq), BF16)] * (KV_HEADS - 1 - m)
        rhs_a.append(jnp.concatenate(parts, axis=0))
    m_scr[...] = jnp.full(m_scr.shape, NEG_BIG, F32)
    acc_scr[...] = jnp.zeros(acc_scr.shape, F32)

    def logits(c, slot):
        kc = kb_ref[0, chunk(c), :]
        for m in range(KV_HEADS):
            mm_scr[slot, m] = _dot(kc, rhs_a[m])

    def softmax_pv(c, slot):
        sel = key_scr[chunk(c), :] > thr
        pas = []
        for m in range(KV_HEADS):
            lg = mm_scr[slot, m]
            lg = jnp.concatenate([jnp.where(sel, lg[:, :tq], NEG_BIG), jnp.where(sel, lg[:, tq:], NEG_BIG)], axis=1)
            mx = m_scr[m]
            mn = jnp.maximum(mx, jnp.max(lg, axis=0, keepdims=True))
            m_scr[m] = mn
            pas.append((jnp.exp2(lg - mn).astype(BF16), jnp.exp2(mx - mn)))
        for m in range(KV_HEADS):
            p, alpha = pas[m]
            acc_scr[m] = acc_scr[m] * alpha + _dot(vt_ref[0, m, :, chunk(c)], p)

    pipelined(logits, softmax_pv)
    pieces = []
    for m in range(KV_HEADS):
        acc = acc_scr[m]
        o = acc[0:HEAD_DIM] / acc[HEAD_DIM:HEAD_DIM + 1]
        pieces += [o[:, :tq], o[:, tq:]]
    at = jnp.concatenate(pieces, axis=0)
    at = at * lax.rsqrt(jnp.mean(at * at, axis=0, keepdims=True) + EPS)
    o_ref[0] = (at.T * ga_ref[...]).astype(BF16)


def _prompt_attention(q, qi, wt, k4, kb, vt, ga):
    b, t, _ = q.shape
    tq = LANES
    assert t % ATT_CHUNK == 0
    topk = min(TOPK_MAX, t // 4)
    blk = lambda w: pl.BlockSpec((1, tq, w), lambda bi, i: (bi, i, 0))
    full = lambda r, c: pl.BlockSpec((1, r, c), lambda bi, i: (bi, 0, 0))
    return pl.pallas_call(
        functools.partial(_prompt_attn_kernel, topk=topk, idx_bits=max(1, (t - 1).bit_length())),
        out_shape=jax.ShapeDtypeStruct((b, t, ATT_WIDTH), BF16),
        grid=(b, t // tq),
        in_specs=[blk(ATT_WIDTH), blk(ATT_WIDTH),
                  pl.BlockSpec((1, IDX_HEADS, tq), lambda bi, i: (bi, 0, i)),
                  full(t, 4 * IDX_DIM), full(t, KV_WIDTH),
                  pl.BlockSpec((1, KV_HEADS, VT_ROWS, t), lambda bi, i: (bi, 0, 0, 0)),
                  pl.BlockSpec((1, ATT_WIDTH), lambda bi, i: (0, 0))],
        out_specs=blk(ATT_WIDTH),
        scratch_shapes=[pltpu.VMEM((t, tq), I32), pltpu.VMEM((2, KV_HEADS, ATT_CHUNK, 2 * tq), F32),
                        pltpu.VMEM((KV_HEADS, 1, 2 * tq), F32),
                        pltpu.VMEM((KV_HEADS, VT_ROWS, 2 * tq), F32)],
        compiler_params=_cparams(("arbitrary", "arbitrary")),
        name="prompt_attn",
    )(q, qi, wt, k4, kb, vt, ga.reshape(1, ATT_WIDTH))


def _sample_score_kernel(pt_ref, qi_ref, w_ref, kin_ref, cache_ref, key_ref, thr_ref, kbuf, sem, *,
                         layer, n_pages, npg, tnew, topk, idx_bits):
    b = pl.program_id(0)
    nb = pl.num_programs(0)
    slot = b % 2
    tq = tnew
    page = kbuf.shape[3]
    ck = npg * page
    nsteps = n_pages // npg

    def start_pages(row, sl):
        for p in range(n_pages):
            pltpu.make_async_copy(cache_ref.at[layer, pt_ref[row * n_pages + p]], kbuf.at[sl, p], sem.at[sl]).start()

    @pl.when(b == 0)
    def _():
        start_pages(0, 0)

    @pl.when(b + 1 < nb)
    def _():
        start_pages(b + 1, 1 - slot)

    pltpu.make_async_copy(cache_ref.at[layer, pl.ds(0, n_pages)], kbuf.at[slot], sem.at[slot]).wait()

    def keys_of(kt, ok):
        k_hi, k_lo = _split_bf16(kt)
        sc = _dot(qi_ref[0], jnp.concatenate([k_hi, k_lo, k_hi], axis=0))
        sc = jnp.maximum(sc, 0.0) * w_ref[0]
        acc = sc[0:tq]
        for hh in range(1, IDX_HEADS):
            acc = acc + sc[hh * tq:(hh + 1) * tq]
        return _order_key(acc) if ok is None else jnp.where(ok, _order_key(acc), INT_MIN)

    def chunk_body(c, carry):
        pages = kbuf[slot, pl.ds(c * npg, npg)]
        kt = jnp.concatenate([pages[p] for p in range(npg)], axis=1)
        key_ref[0, :, pl.ds(pl.multiple_of(c * ck, ck), ck)] = keys_of(kt, None)
        return carry
    lax.fori_loop(0, nsteps, chunk_body, 0)
    qrow = lax.broadcasted_iota(I32, (tq, ck), 0)
    col = lax.broadcasted_iota(I32, (tq, ck), 1)
    new = jnp.concatenate([kin_ref[0], jnp.zeros((kin_ref.shape[1], ck - page), F32)], axis=1)
    key_ref[0, :, pl.ds(nsteps * ck, ck)] = keys_of(new, jnp.logical_and(col <= qrow, col < tnew))

    def load_keys(c):
        return key_ref[0, :, pl.ds(pl.multiple_of(c * ck, ck), ck)]

    def store_keys(c, val):
        key_ref[0, :, pl.ds(pl.multiple_of(c * ck, ck), ck)] = val
    thr = _select_threshold(load_keys, store_keys, nsteps + 1, tq, ck, topk, idx_bits)
    thr_ref[0] = jnp.broadcast_to(thr, (tq, LANES))


def _sample_attn_kernel(pt_ref, q_ref, key_ref, keyn_ref, thr_ref, kn_ref, vn_ref, ga_ref, ck_ref, cv_ref, o_ref,
                        kbuf, vbuf, sem, m_scr, l_scr, acc_scr, *, layer, n_pages, npg, tnew):
    b, s = pl.program_id(0), pl.program_id(1)
    ns = pl.num_programs(1)
    g = b * ns + s
    slot = g % 2
    pg = kbuf.shape[1]
    tq = tnew

    def start_pages(gg, sl):
        base = (gg // ns) * n_pages + (gg % ns) * pg
        for p in range(pg):
            pid = pt_ref[base + p]
            pltpu.make_async_copy(ck_ref.at[layer, pid], kbuf.at[sl, p], sem.at[0, sl]).start()
            pltpu.make_async_copy(cv_ref.at[layer, pid], vbuf.at[sl, p], sem.at[1, sl]).start()

    @pl.when(g == 0)
    def _():
        start_pages(0, 0)

    @pl.when(g + 1 < pl.num_programs(0) * ns)
    def _():
        start_pages(g + 1, 1 - slot)

    pltpu.make_async_copy(ck_ref.at[layer, pl.ds(0, pg)], kbuf.at[slot], sem.at[0, slot]).wait()
    pltpu.make_async_copy(cv_ref.at[layer, pl.ds(0, pg)], vbuf.at[slot], sem.at[1, slot]).wait()

    @pl.when(s == 0)
    def _():
        m_scr[...] = jnp.full(m_scr.shape, NEG_BIG, F32)
        l_scr[...] = jnp.zeros(l_scr.shape, F32)
        acc_scr[...] = jnp.zeros(acc_scr.shape, F32)

    thr = thr_ref[0][:, 0:1]

    def attend(keys, kget, vget):
        sel = keys > thr
        sel2 = jnp.concatenate([sel, sel], axis=0)
        lgs = [_dot(q_ref[0, m], kget(m)) for m in range(KV_HEADS)]
        prs = []
        for m in range(KV_HEADS):
            lg = jnp.where(sel2, lgs[m], NEG_BIG)
            mx = m_scr[m]
            mn = jnp.maximum(mx, jnp.max(lg, axis=-1, keepdims=True))
            pr = jnp.exp2(lg - mn)
            alpha = jnp.exp2(mx - mn)
            l_scr[m] = l_scr[m] * alpha + jnp.sum(pr, axis=-1, keepdims=True)
            m_scr[m] = mn
            prs.append((pr.astype(BF16), alpha))
        for m in range(KV_HEADS):
            pr, alpha = prs[m]
            acc_scr[m] = acc_scr[m] * alpha + _dot_nt(pr, vget(m))

    page = kbuf.shape[4]
    for j in range(pg // npg):
        cached = lambda buf: (lambda m: jnp.concatenate(
            [buf[slot, j * npg + p, m] for p in range(npg)], axis=1).astype(BF16))
        attend(key_ref[0, :, j * npg * page:(j + 1) * npg * page], cached(kbuf), cached(vbuf))

    @pl.when(s == ns - 1)
    def _():
        attend(keyn_ref[0], lambda m: kn_ref[0, m].astype(BF16), lambda m: vn_ref[0, m].astype(BF16))
        pieces = []
        for m in range(KV_HEADS):
            o = acc_scr[m] / l_scr[m]
            pieces += [o[:tq], o[tq:]]
        a = jnp.concatenate(pieces, axis=1)
        a = a * lax.rsqrt(jnp.mean(a * a, axis=-1, keepdims=True) + EPS) * ga_ref[...]
        o_ref[0] = a.astype(BF16)


def _sample_attention(q, qi, wi, ki_new, k_new, v_new, cache_k, cache_v, cache_ki, layer, page_table, ga):
    db, tn, _ = q.shape
    page = cache_ki.shape[2]
    cache_k = cache_k.transpose(0, 1, 3, 4, 2)
    cache_v = cache_v.transpose(0, 1, 3, 4, 2)
    cache_ki = cache_ki.transpose(0, 1, 3, 2)
    n_pages = page_table.shape[1]
    npg_s = math.gcd(SAMPLE_SCORE_PAGES, n_pages)
    lpad = (n_pages // npg_s + 1) * npg_s * page
    topk = min(TOPK_MAX, (n_pages * page + tn) // 4)
    assert tn <= page
    pad_keys = lambda a: jnp.pad(a, [(0, 0)] * (a.ndim - 1) + [(0, page - tn)])
    kin = pad_keys(ki_new.transpose(0, 2, 1))
    heads = lambda a: pad_keys(a.reshape(db, tn, KV_HEADS, HEAD_DIM).transpose(0, 2, 3, 1))
    qi_hq = qi.reshape(db, tn, IDX_HEADS, IDX_DIM).transpose(0, 2, 1, 3).reshape(db, IDX_HEADS * tn, IDX_DIM)
    qi_hi = qi_hq.astype(BF16)
    qi_hq = jnp.concatenate([qi_hi, qi_hi, (qi_hq - qi_hi.astype(F32)).astype(BF16)], axis=-1)
    w_hq = wi.transpose(0, 2, 1).reshape(db, IDX_HEADS * tn, 1)
    q2 = (q.reshape(db, tn, KV_HEADS, ATT_HEADS // KV_HEADS, HEAD_DIM).transpose(0, 2, 3, 1, 4)
          .reshape(db, KV_HEADS, (ATT_HEADS // KV_HEADS) * tn, HEAD_DIM))
    pt_flat = page_table.reshape(-1).astype(I32)

    per_b = lambda *tail: pl.BlockSpec((1,) + tail, lambda b, s, pt: (b,) + (0,) * len(tail))
    row_b = lambda *tail: pl.BlockSpec((1,) + tail, lambda b, pt: (b,) + (0,) * len(tail))
    keys, thr = pl.pallas_call(
        functools.partial(_sample_score_kernel, layer=layer, n_pages=n_pages, npg=npg_s, tnew=tn, topk=topk,
                          idx_bits=max(1, (lpad - 1).bit_length())),
        out_shape=[jax.ShapeDtypeStruct((db, tn, lpad), I32),
                   jax.ShapeDtypeStruct((db, tn, LANES), I32)],
        grid_spec=pltpu.PrefetchScalarGridSpec(
            num_scalar_prefetch=1,
            grid=(db,),
            in_specs=[row_b(IDX_HEADS * tn, 3 * IDX_DIM), row_b(IDX_HEADS * tn, 1), row_b(IDX_DIM, page),
                      pl.BlockSpec(memory_space=pl.ANY)],
            out_specs=[row_b(tn, lpad), row_b(tn, LANES)],
            scratch_shapes=[pltpu.VMEM((2, n_pages, IDX_DIM, page), F32), pltpu.SemaphoreType.DMA((2,))]),
        compiler_params=_cparams(("arbitrary",)),
        name="sample_score",
    )(pt_flat, qi_hq, w_hq, kin, cache_ki)

    pg = math.gcd(SAMPLE_ATTN_GROUP, n_pages)
    npg_a = math.gcd(npg_s, pg)
    hbm = pl.BlockSpec(memory_space=pl.ANY)
    a = pl.pallas_call(
        functools.partial(_sample_attn_kernel, layer=layer, n_pages=n_pages, npg=npg_a, tnew=tn),
        out_shape=jax.ShapeDtypeStruct((db, tn, ATT_WIDTH), BF16),
        grid_spec=pltpu.PrefetchScalarGridSpec(
            num_scalar_prefetch=1,
            grid=(db, n_pages // pg),
            in_specs=[per_b(KV_HEADS, (ATT_HEADS // KV_HEADS) * tn, HEAD_DIM),
                      pl.BlockSpec((1, tn, pg * page), lambda b, s, pt: (b, 0, s)),
                      pl.BlockSpec((1, tn, page), lambda b, s, pt: (b, 0, n_pages)),
                      per_b(tn, LANES),
                      per_b(KV_HEADS, HEAD_DIM, page), per_b(KV_HEADS, HEAD_DIM, page),
                      pl.BlockSpec((1, ATT_WIDTH), lambda b, s, pt: (0, 0)), hbm, hbm],
            out_specs=per_b(tn, ATT_WIDTH),
            scratch_shapes=[pltpu.VMEM((2, pg, KV_HEADS, HEAD_DIM, page), F32),
                            pltpu.VMEM((2, pg, KV_HEADS, HEAD_DIM, page), F32),
                            pltpu.SemaphoreType.DMA((2, 2)),
                            pltpu.VMEM((KV_HEADS, 2 * tn, 1), F32), pltpu.VMEM((KV_HEADS, 2 * tn, 1), F32),
                            pltpu.VMEM((KV_HEADS, 2 * tn, HEAD_DIM), F32)]),
        compiler_params=_cparams(("arbitrary", "arbitrary")),
        name="sample_attn",
    )(pt_flat, q2, keys, keys, thr, heads(k_new), heads(v_new), ga.reshape(1, ATT_WIDTH), cache_k, cache_v)
    return a


def _outproj_kernel(a_ref, r_ref, x_ref, gt_ref, sc_ref, sh_ref, g_ref, wa_ref, wr_ref, wrt_ref, brt_ref,
                    x2_ref, h2_ref, route_ref):
    mix = _dot(a_ref[...], wa_ref[...]) + _dot(r_ref[...], wr_ref[...])
    x2 = x_ref[...] + gt_ref[0] * mix
    x2_ref[...] = x2
    h = x2 * lax.rsqrt(jnp.mean(x2 * x2, axis=-1, keepdims=True) + EPS) * g_ref[...]
    h = h * (1.0 + sc_ref[0]) + sh_ref[0]
    h2_ref[...] = h
    lg = (_dot3(h, wrt_ref[...]) + brt_ref[...]).T
    grp = [lg[g:g + 1] for g in range(N_GROUPS)]
    gmax = functools.reduce(jnp.maximum, grp)
    gden = functools.reduce(lambda u, v: u + v, [jnp.exp(g - gmax) for g in grp])
    gsel = jnp.full(gmax.shape, N_GROUPS - 1, I32)
    for g in range(N_GROUPS - 2, -1, -1):
        gsel = jnp.where(grp[g] == gmax, g, gsel)
    gw = 1.0 / gden
    el = []
    for e in range(EXPERTS_PER_GROUP):
        v = lg[N_GROUPS + e:N_GROUPS + e + 1]
        for g in range(1, N_GROUPS):
            row = N_GROUPS + g * EXPERTS_PER_GROUP + e
            v = jnp.where(gsel == g, lg[row:row + 1], v)
        el.append(v)
    emax = functools.reduce(jnp.maximum, el)
    e0 = jnp.full(emax.shape, EXPERTS_PER_GROUP - 1, I32)
    for e in range(EXPERTS_PER_GROUP - 2, -1, -1):
        e0 = jnp.where(el[e] == emax, e, e0)
    rest = [jnp.where(e0 == e, -jnp.inf, el[e]) for e in range(EXPERTS_PER_GROUP)]
    rmax = functools.reduce(jnp.maximum, rest)
    e1 = jnp.full(emax.shape, EXPERTS_PER_GROUP - 1, I32)
    for e in range(EXPERTS_PER_GROUP - 2, -1, -1):
        e1 = jnp.where(jnp.logical_and(rest[e] == rmax, e0 != e), e, e1)
    p1 = jnp.exp(rmax - emax)
    w0 = gw * (1.0 / (1.0 + p1))
    w1 = gw * (p1 / (1.0 + p1))
    base = gsel * EXPERTS_PER_GROUP
    rid = lax.broadcasted_iota(I32, lg.shape, 0)
    rt = jnp.where(rid == 0, (base + e0).astype(F32),
                   jnp.where(rid == 1, (base + e1).astype(F32),
                             jnp.where(rid == 2, w0, jnp.where(rid == 3, w1, 0.0))))
    route_ref[...] = rt.T


def _outproj(a, r, x2d, gt, sc, sh, g2, wa, wr, wrt, brt, tiles_per_group):
    n, d = x2d.shape
    tm = min(PROJ_TILE, n)
    rr = gt.shape[1]
    row = lambda w: pl.BlockSpec((tm, w), lambda i: (i, 0))
    mod = pl.BlockSpec((1, rr, d), lambda i: (i // tiles_per_group, 0, 0))
    const = lambda s: pl.BlockSpec(s, lambda i: (0, 0))
    return pl.pallas_call(
        _outproj_kernel,
        out_shape=[jax.ShapeDtypeStruct((n, d), F32), jax.ShapeDtypeStruct((n, d), F32),
                   jax.ShapeDtypeStruct((n, LANES), F32)],
        grid=(n // tm,),
        in_specs=[row(ATT_WIDTH), row(HG_WIDTH), row(d), mod, mod, mod, const((1, d)),
                  const(wa.shape), const(wr.shape), const(wrt.shape), const(brt.shape)],
        out_specs=[row(d), row(d), row(LANES)],
        compiler_params=_cparams(("arbitrary",)),
        name="outproj",
    )(a, r, x2d, gt, sc, sh, g2.reshape(1, d), wa, wr, wrt, brt)


def _gather_pipeline(step, nsteps, idx_hbm, src_hbm, buf, idx_smem, isem, rsem):
    nrows = buf.shape[1]
    slot = step % 2

    def idx_copy(b, sl):
        return pltpu.make_async_copy(idx_hbm.at[b], idx_smem.at[sl], isem.at[sl])

    def start_rows(sl):
        for r in range(nrows):
            pltpu.make_async_copy(src_hbm.at[pl.ds(idx_smem[sl, 0, r], 1), :],
                                  buf.at[sl, pl.ds(r, 1), :], rsem.at[sl]).start()

    @pl.when(step == 0)
    def _():
        idx_copy(0, 0).start()
        idx_copy(0, 0).wait()
        start_rows(0)

        @pl.when(nsteps > 1)
        def _():
            idx_copy(1, 1).start()

    @pl.when(step + 1 < nsteps)
    def _():
        idx_copy(step + 1, 1 - slot).wait()
        start_rows(1 - slot)

    @pl.when(step + 2 < nsteps)
    def _():
        idx_copy(step + 2, slot).start()

    pltpu.make_async_copy(src_hbm.at[pl.ds(0, nrows), :], buf.at[slot], rsem.at[slot]).wait()
    return slot


def _ffn_kernel(be_ref, nb_ref, tok_ref, h_ref, w1_ref, w3_ref, w2_ref, y_ref, xbuf, idx_smem, isem, rsem):
    j = pl.program_id(0)

    @pl.when(j < nb_ref[0])
    def _():
        slot = _gather_pipeline(j, nb_ref[0], tok_ref, h_ref, xbuf, idx_smem, isem, rsem)
        xb = xbuf[slot].astype(BF16)
        u = _dot(xb, w1_ref[0])
        g = _dot(xb, w3_ref[0])
        y_ref[...] = _dot((_silu(u) * g).astype(BF16), w2_ref[0])

    @pl.when(j >= nb_ref[0])
    def _():
        y_ref[...] = jnp.zeros(y_ref.shape, F32)


def _ffn(blk_e, n_used, tok, h2, w1, w3, w2, blk):
    nb = tok.shape[0]
    d = h2.shape[1]
    wspec = lambda s: pl.BlockSpec((1,) + s, lambda j, be, nu: (be[j], 0, 0))
    return pl.pallas_call(
        _ffn_kernel,
        out_shape=jax.ShapeDtypeStruct((nb * blk, d), F32),
        grid_spec=pltpu.PrefetchScalarGridSpec(
            num_scalar_prefetch=2,
            grid=(nb,),
            in_specs=[pl.BlockSpec(memory_space=pl.ANY), pl.BlockSpec(memory_space=pl.ANY),
                      wspec(w1.shape[1:]), wspec(w3.shape[1:]), wspec(w2.shape[1:])],
            out_specs=pl.BlockSpec((blk, d), lambda j, be, nu: (j, 0)),
            scratch_shapes=[pltpu.VMEM((2, blk, d), F32), pltpu.SMEM((2, 1, blk), I32),
                            pltpu.SemaphoreType.DMA((2,)), pltpu.SemaphoreType.DMA((2,))]),
        compiler_params=_cparams(("arbitrary",)),
        name="moe_ffn",
    )(blk_e, n_used, tok, h2, w1, w3, w2)


def _combine_kernel(dd_ref, y_ref, x2_ref, route_ref, gt_ref, sc_ref, sh_ref, g_ref, o_ref,
                    ybuf, idx_smem, isem, rsem):
    tm = x2_ref.shape[0]
    slot = _gather_pipeline(pl.program_id(0), pl.num_programs(0), dd_ref, y_ref, ybuf, idx_smem, isem, rsem)
    rt = route_ref[...]
    moe = ybuf[slot, 0:tm] * rt[:, 2:3] + ybuf[slot, tm:2 * tm] * rt[:, 3:4]
    x = x2_ref[...] + gt_ref[0] * moe
    y = x * lax.rsqrt(jnp.mean(x * x, axis=-1, keepdims=True) + EPS) * g_ref[...]
    o_ref[...] = y * (1.0 + sc_ref[0]) + sh_ref[0]


def _combine(dd, yb, x2, route, gt, sc, sh, gf, tiles_per_group):
    n, d = x2.shape
    tm = dd.shape[2] // 2
    rr = gt.shape[1]
    row = pl.BlockSpec((tm, d), lambda i: (i, 0))
    mod = pl.BlockSpec((1, rr, d), lambda i: (i // tiles_per_group, 0, 0))
    hbm = pl.BlockSpec(memory_space=pl.ANY)
    return pl.pallas_call(
        _combine_kernel,
        out_shape=jax.ShapeDtypeStruct((n, d), F32),
        grid=(n // tm,),
        in_specs=[hbm, hbm, row, pl.BlockSpec((tm, LANES), lambda i: (i, 0)), mod, mod, mod,
                  pl.BlockSpec((1, d), lambda i: (0, 0))],
        out_specs=row,
        scratch_shapes=[pltpu.VMEM((2, 2 * tm, d), F32), pltpu.SMEM((2, 1, 2 * tm), I32),
                        pltpu.SemaphoreType.DMA((2,)), pltpu.SemaphoreType.DMA((2,))],
        compiler_params=_cparams(("arbitrary",)),
        name="moe_combine",
    )(dd, yb, x2, route, gt, sc, sh, gf.reshape(1, d))


def _dispatch(route, blk):
    n = route.shape[0]
    flat_e = route[:, 0:2].astype(I32).reshape(-1)
    a = flat_e.shape[0]
    onehot = (flat_e[:, None] == jnp.arange(N_EXPERTS, dtype=I32)[None, :]).astype(I32)
    csum = jnp.cumsum(onehot, axis=0)
    rank = jnp.sum((csum - onehot) * onehot, axis=1)
    counts = csum[-1]
    padded = (counts + blk - 1) // blk * blk
    pad_end = jnp.cumsum(padded)
    pad_start = pad_end - padded
    dest = pad_start[flat_e] + rank
    nb = -(-a // blk) + N_EXPERTS
    tok = jnp.zeros((nb * blk,), I32).at[dest].set(jnp.arange(a, dtype=I32) // 2)
    blk_e = jnp.minimum(jnp.searchsorted(pad_end, jnp.arange(nb, dtype=I32) * blk, side='right'),
                        N_EXPERTS - 1).astype(I32)
    n_used = (pad_end[-1] // blk).astype(I32).reshape(1)
    dest2 = dest.reshape(n, 2)
    return tok.reshape(nb, 1, blk), blk_e, n_used, dest2[:, 0], dest2[:, 1]


def _mods(m, n_chunks, per_token_rows):
    parts = jnp.split(m, n_chunks, axis=-1)
    if per_token_rows is None:
        return [p[:, None, :] for p in parts]
    g, d = parts[0].shape
    tm = min(PROJ_TILE, g * per_token_rows)
    return [jnp.repeat(p, per_token_rows, axis=0).reshape(-1, tm, d) for p in parts]


def _layer(x, mod6, modf, pos, s0, attend, lb, wts, final_g, per_token):
    b, t, d = x.shape
    n = b * t
    (norm1_g, norm2_g, w_r, gk, ga, hg_out_g, wa, wr, wrt, brt, w1, w3, w2) = wts
    tm = min(PROJ_TILE, n)
    tiles_per_group = 1 if per_token else t // tm
    sh1, sc1, gt1, sh2, sc2, gt2 = _mods(mod6, 6, t if per_token else None)
    shf, scf = _mods(modf, 2, t if per_token else None)
    tabs = _rope_tables(jnp.tile(pos, tm // t) if per_token else pos)
    x2d = x.reshape(n, d)
    sq = lambda arr: arr.reshape(b, t, arr.shape[-1])
    if per_token:
        q, k, v, qi, ki, wi, hq, hf, hi, hg = _proj(x2d, sc1, sh1, norm1_g, w_r, gk, tabs, tiles_per_group, 0)
        a = attend(sq(q), sq(qi), sq(wi), sq(ki), sq(k), sq(v), ga)
        k_out, v_out, ki_out = k.reshape(b, t, KV_HEADS, HEAD_DIM), v.reshape(b, t, KV_HEADS, HEAD_DIM), sq(ki)
    else:
        q, kt, vt, kb, vto, qi, kit, k4, wt, hq, hf, hi, hg = _proj(x2d, sc1, sh1, norm1_g, w_r, gk, tabs,
                                                                   tiles_per_group, b)
        a = attend(sq(q), sq(qi), wt, sq(k4), sq(kb), vto, ga)
        heads = lambda xt: xt.reshape(b, KV_HEADS, HEAD_DIM, t).transpose(0, 3, 1, 2)
        k_out, v_out, ki_out = heads(kt), heads(vt), kit.transpose(0, 2, 1)
    r, s_t = _hgrn(sq(hq), sq(hf), sq(hi), sq(hg), lb, hg_out_g, s0)
    x2, h2, route = _outproj(a.reshape(n, -1), r.reshape(n, -1), x2d, gt1, sc2, sh2, norm2_g,
                             wa, wr, wrt, brt, tiles_per_group)
    blk = MOE_BLOCK if n >= 8 * MOE_BLOCK else 64
    tok, blk_e, n_used, d0, d1 = _dispatch(route, blk)
    yb = _ffn(blk_e, n_used, tok, h2, w1, w3, w2, blk)
    dd = jnp.concatenate([d0.reshape(n // tm, 1, tm), d1.reshape(n // tm, 1, tm)], axis=2)
    y = _combine(dd, yb, x2, route, gt2, scf, shf, final_g, tiles_per_group)
    return y.reshape(b, t, d), k_out, v_out, ki_out, s_t


def kernel(x_prompt, x_sample, cache_k, cache_v, cache_kidx, state_hgrn, page_table, c_prompt, c_sample,
           ada_w, ada_b, norm1_g, norm2_g, w_in, idx_k_g, hg_lb_logits, attn_out_g, hg_out_g, w_out,
           w_group, b_group, w_expert_router, b_expert_router, w1, w3, w2, final_g, ada_final_w, ada_final_b):
    depth = ada_w.shape[0]
    assert depth == 1, "the final adaLN norm is fused into the (single) layer's combine kernel"
    bp, tp, d = x_prompt.shape
    bs, ts, _ = x_sample.shape
    past = page_table.shape[1] * cache_kidx.shape[2]
    lb_all = jnp.cumsum(jax.nn.softmax(hg_lb_logits.astype(F32), axis=0), axis=0)
    n_c = bp + bs
    c_all = jnp.concatenate([c_prompt, c_sample, jnp.zeros((-n_c % 16, d), F32)], axis=0)
    modf = _ada(c_all, ada_final_w, ada_final_b)
    l = 0
    mod6 = _ada(c_all, ada_w[l], ada_b[l])

    seg = [0]
    for s in (ATT_WIDTH, KV_WIDTH, KV_WIDTH, IDX_HEADS * IDX_DIM, IDX_DIM, IDX_HEADS,
              HG_WIDTH, HG_WIDTH, HG_WIDTH, HG_WIDTH):
        seg.append(seg[-1] + s)
    wl = w_in[l]
    col = lambda i: wl[:, seg[i]:seg[i + 1]]
    zpad = lambda w: jnp.zeros((d, w), wl.dtype)
    w_r = jnp.concatenate([col(0), col(1), col(2), col(3), col(6), col(7), col(8), col(9),
                           col(4), zpad(LANES - IDX_DIM), col(5), zpad(LANES - IDX_HEADS)], axis=1).astype(BF16)
    gk = jnp.concatenate([idx_k_g[l], jnp.zeros((LANES - IDX_DIM,), F32)]).reshape(1, LANES)
    wa = w_out[l, :ATT_WIDTH].astype(BF16)
    wr = w_out[l, ATT_WIDTH:].astype(BF16)
    n_rt = N_GROUPS + N_EXPERTS
    wrt = jnp.concatenate([w_group[l], w_expert_router[l], jnp.zeros((d, LANES - n_rt), F32)], axis=1)
    brt = jnp.concatenate([b_group[l], b_expert_router[l], jnp.zeros((LANES - n_rt,), F32)]).reshape(1, LANES)
    wts = (norm1_g[l], norm2_g[l], w_r, gk, attn_out_g[l], hg_out_g[l], wa, wr, wrt, brt,
           w1[l].astype(BF16), w3[l].astype(BF16), w2[l].astype(BF16))

    def attend_s(q, qi, wi, ki, k, v, ga):
        return _sample_attention(q, qi, wi, ki, k, v, cache_k, cache_v, cache_kidx, l, page_table, ga)

    s0_p = jnp.zeros((bp, HG_HEADS, HG_DK, HG_DV), F32)
    yp, kp, vp, kip, sp = _layer(x_prompt, mod6[:bp], modf[:bp], jnp.arange(tp), s0_p, _prompt_attention,
                                 lb_all[l], wts, final_g, per_token=False)
    ys, ks, vs, kis, ss = _layer(x_sample, mod6[bp:n_c], modf[bp:n_c], past + jnp.arange(ts), state_hgrn[l],
                                 attend_s, lb_all[l], wts, final_g, per_token=True)
    return (yp, ys, kp[None], vp[None], kip[None], sp[None], ks[None], vs[None], kis[None], ss[None])
```

```python
import functools
import math

import jax
import jax.numpy as jnp
from jax import lax
from jax.experimental import pallas as pl
from jax.experimental.pallas import tpu as pltpu

F32 = jnp.float32
BF16 = jnp.bfloat16
I32 = jnp.int32

ATT_HEADS = 8
KV_HEADS = 4
HEAD_DIM = 64
ATT_WIDTH = ATT_HEADS * HEAD_DIM
KV_WIDTH = KV_HEADS * HEAD_DIM
ROT_HALF = HEAD_DIM // 8
ROPE_THETA = 500000.0
IDX_HEADS = 8
IDX_DIM = 64
TOPK_MAX = 256
HG_HEADS = 4
HG_DK = 128
HG_DV = 128
HG_WIDTH = HG_HEADS * HG_DV
HG_CHUNK = 64
N_GROUPS = 4
EXPERTS_PER_GROUP = 4
N_EXPERTS = N_GROUPS * EXPERTS_PER_GROUP
D_EXPERT = 512
MOE_BLOCK = 256
EPS = 1e-6

LANES = 128
INT_MIN = -2 ** 31
NEG_BIG = -1e30
VMEM_LIMIT = 56 * 1024 * 1024
PROJ_TILE = 256
ATT_CHUNK = 512
VT_ROWS = HEAD_DIM + 16
LOG2E = 1.4426950408889634
BISECT_BITS_PER_TRIP = 4
SAMPLE_SCORE_PAGES = 16
SAMPLE_ATTN_GROUP = 32

_Z_Q, _Z_K, _Z_V, _Z_QI, _Z_HQ, _Z_HF, _Z_HI, _Z_HG, _Z_KI, _Z_WI, _Z_END = (
    0, 512, 768, 1024, 1536, 2048, 2560, 3072, 3584, 3712, 3840)


def _cparams(sem):
    return pltpu.CompilerParams(dimension_semantics=sem, vmem_limit_bytes=VMEM_LIMIT)


def _split_bf16(x):
    hi = x.astype(BF16)
    lo = (x - hi.astype(F32)).astype(BF16)
    return hi, lo


def _dot(a, b):
    return jnp.dot(a, b, preferred_element_type=F32)


def _dot_nt(a, b):
    return lax.dot_general(a, b, (((1,), (1,)), ((), ())), preferred_element_type=F32)


def _dot_tn(a, b):
    return lax.dot_general(a, b, (((0,), (0,)), ((), ())), preferred_element_type=F32)


def _dot3(a, b):
    ah, al = _split_bf16(a)
    bh, bl = _split_bf16(b)
    return _dot(ah, bh) + (_dot(ah, bl) + _dot(al, bh))


def _dot3_nt(a, b):
    ah, al = _split_bf16(a)
    bh, bl = _split_bf16(b)
    return _dot_nt(ah, bh) + (_dot_nt(ah, bl) + _dot_nt(al, bh))


def _silu(x):
    return x * (1.0 / (1.0 + jnp.exp(-x)))


def _sigmoid(x):
    return 1.0 / (1.0 + jnp.exp(-x))


def _ada_kernel(c_ref, w_ref, b_ref, o_ref):
    o_ref[...] = _dot3(_silu(c_ref[...]), w_ref[...]) + b_ref[...]


def _ada(c, w, b):
    r, d = c.shape
    e = w.shape[1]
    te = 1024
    return pl.pallas_call(
        _ada_kernel,
        out_shape=jax.ShapeDtypeStruct((r, e), F32),
        grid=(e // te,),
        in_specs=[pl.BlockSpec((r, d), lambda j: (0, 0)),
                  pl.BlockSpec((d, te), lambda j: (0, j)),
                  pl.BlockSpec((1, te), lambda j: (0, j))],
        out_specs=pl.BlockSpec((r, te), lambda j: (0, j)),
        compiler_params=_cparams(("arbitrary",)),
        name="ada",
    )(c, w, b.reshape(1, e))


def _rope_tables(pos):
    p = pos.shape[0]
    inv = ROPE_THETA ** (-jnp.arange(ROT_HALF, dtype=F32) * (2.0 / (2 * ROT_HALF)))
    ang = pos.astype(F32)[:, None] * inv[None, :]
    c, s = jnp.cos(ang), jnp.sin(ang)
    rest = HEAD_DIM - 2 * ROT_HALF
    one, zero, z8 = jnp.ones((p, rest), F32), jnp.zeros((p, rest), F32), jnp.zeros((p, ROT_HALF), F32)
    cos64 = jnp.concatenate([c, c, one], axis=1)
    sa64 = jnp.concatenate([-s, z8, zero], axis=1)
    sb64 = jnp.concatenate([z8, s, zero], axis=1)
    dup = lambda t: jnp.concatenate([t, t], axis=1)
    return dup(cos64), dup(sa64), dup(sb64)


def _proj_kernel(x_ref, sc_ref, sh_ref, g_ref, w_ref, gk_ref, cos_ref, sa_ref, sb_ref, *out_refs, seq_major):
    x = x_ref[...]
    ms = jnp.mean(x * x, axis=-1, keepdims=True)
    h = x * lax.rsqrt(ms + EPS) * g_ref[...]
    h = h * (1.0 + sc_ref[0]) + sh_ref[0]
    z = _dot(h.astype(BF16), w_ref[...])
    cos, sa, sb = cos_ref[...], sa_ref[...], sb_ref[...]

    def rope(zs):
        n = zs.shape[1] // LANES
        rep = (lambda t: jnp.concatenate([t] * n, axis=1)) if n > 1 else (lambda t: t)
        w = zs.shape[1]
        return (zs * rep(cos) + pltpu.roll(zs, w - ROT_HALF, 1) * rep(sa)
                + pltpu.roll(zs, ROT_HALF, 1) * rep(sb))

    q = (rope(z[:, _Z_Q:_Z_K]) * (HEAD_DIM ** -0.5 * LOG2E)).astype(BF16)
    k = rope(z[:, _Z_K:_Z_V])
    v = z[:, _Z_V:_Z_QI]
    qi = rope(z[:, _Z_QI:_Z_HQ]) * (IDX_DIM ** -0.5)
    ks = z[:, _Z_KI:_Z_WI]
    kms = jnp.sum(ks * ks, axis=-1, keepdims=True) * (1.0 / IDX_DIM)
    kn = rope(ks * lax.rsqrt(kms + EPS) * gk_ref[...])
    ws = z[:, _Z_WI:_Z_END] * (IDX_HEADS ** -0.5)
    if seq_major:
        q_ref, kt_ref, vt_ref, kb_ref, vto_ref, qi_ref, kit_ref, k4_ref, wt_ref = out_refs[:9]
        kt_ref[0] = k.T
        vt = v.T
        vt_ref[0] = vt
        kb_ref[...] = k.astype(BF16)
        for m in range(KV_HEADS):
            vto_ref[0, m, 0:HEAD_DIM, :] = vt[m * HEAD_DIM:(m + 1) * HEAD_DIM].astype(BF16)
            vto_ref[0, m, HEAD_DIM:VT_ROWS, :] = jnp.ones((VT_ROWS - HEAD_DIM, x.shape[0]), BF16)
        kit_ref[0] = kn.T[:IDX_DIM]
        hi = kn.astype(BF16).astype(F32)
        lo = kn - hi
        k4_ref[...] = jnp.concatenate([hi + pltpu.roll(lo, IDX_DIM, 1), hi + pltpu.roll(hi, IDX_DIM, 1)],
                                      axis=1).astype(BF16)
        wt_ref[0] = ws.T[:IDX_HEADS]
    else:
        q_ref, k_ref, v_ref, qi_ref, ki_ref, wi_ref = out_refs[:6]
        k_ref[...] = k
        v_ref[...] = v
        ki_ref[...] = kn[:, :IDX_DIM]
        wi_ref[...] = ws[:, :IDX_HEADS]
    q_ref[...] = q
    qi_ref[...] = qi
    hq_ref, hf_ref, hi_ref, hg_ref = out_refs[-4:]
    hq_ref[...] = z[:, _Z_HQ:_Z_HF]
    hf_ref[...] = z[:, _Z_HF:_Z_HI]
    hi_ref[...] = z[:, _Z_HI:_Z_HG]
    hg_ref[...] = z[:, _Z_HG:_Z_KI]


def _proj(x2d, sc, sh, g1, w_r, gk, tabs, tiles_per_group, seq_batch):
    n, d = x2d.shape
    tm = min(PROJ_TILE, n)
    nt = n // tm
    r = sc.shape[1]
    pt = tabs[0].shape[0] // tm
    row = lambda w, t: (jax.ShapeDtypeStruct((n, w), t), pl.BlockSpec((tm, w), lambda i: (i, 0)))
    mod = pl.BlockSpec((1, r, d), lambda i: (i // tiles_per_group, 0, 0))
    tab = pl.BlockSpec((tm, LANES), lambda i: (i % pt, 0))
    if seq_batch:
        t = n // seq_batch
        tps = t // tm
        seq = lambda w: (jax.ShapeDtypeStruct((seq_batch, w, t), F32),
                         pl.BlockSpec((1, w, tm), lambda i: (i // tps, 0, i % tps)))
        vto = (jax.ShapeDtypeStruct((seq_batch, KV_HEADS, VT_ROWS, t), BF16),
               pl.BlockSpec((1, KV_HEADS, VT_ROWS, tm), lambda i: (i // tps, 0, 0, i % tps)))
        outs = [row(ATT_WIDTH, BF16), seq(KV_WIDTH), seq(KV_WIDTH), row(KV_WIDTH, BF16), vto,
                row(ATT_WIDTH, F32), seq(IDX_DIM), row(4 * IDX_DIM, BF16), seq(IDX_HEADS)]
    else:
        outs = [row(ATT_WIDTH, BF16), row(KV_WIDTH, F32), row(KV_WIDTH, F32),
                row(ATT_WIDTH, F32), row(IDX_DIM, F32), row(IDX_HEADS, F32)]
    outs += [row(HG_WIDTH, F32)] * 4
    return pl.pallas_call(
        functools.partial(_proj_kernel, seq_major=bool(seq_batch)),
        out_shape=[s for s, _ in outs],
        grid=(nt,),
        in_specs=[pl.BlockSpec((tm, d), lambda i: (i, 0)), mod, mod,
                  pl.BlockSpec((1, d), lambda i: (0, 0)),
                  pl.BlockSpec(w_r.shape, lambda i: (0, 0)),
                  pl.BlockSpec((1, LANES), lambda i: (0, 0)),
                  tab, tab, tab],
        out_specs=[b for _, b in outs],
        compiler_params=_cparams(("arbitrary",)),
        name="proj",
    )(x2d, sc, sh, g1.reshape(1, d), w_r, gk, *tabs)


def _hgrn_kernel(hq_ref, hf_ref, hi_ref, hg_ref, lb_ref, g_ref, s0_ref, r_ref, st_ref, s_scr, *, chunk, nchunk):
    t = pl.program_id(1)

    @pl.when(t == 0)
    def _():
        for hh in range(HG_HEADS):
            s_scr[hh] = s0_ref[0, hh].T

    tt = chunk * nchunk
    row = lax.broadcasted_iota(I32, (tt, tt), 0)
    col = lax.broadcasted_iota(I32, (tt, tt), 1)
    causal = jnp.logical_and(col <= row, row // chunk == col // chunk)
    tri = causal.astype(BF16)
    rows = lambda c: slice(c * chunk, (c + 1) * chunk)
    lanes = lambda hh: slice(hh * HG_DK, (hh + 1) * HG_DK)
    lb = lb_ref[...]
    f = lb + (1.0 - lb) * _sigmoid(hf_ref[0])
    logf = jnp.log(f)
    kk = 1.0 - f
    l0 = logf.astype(BF16)
    r1 = logf - l0.astype(F32)
    l1 = r1.astype(BF16)
    l2 = (r1 - l1.astype(F32)).astype(BF16)
    b = _dot(tri, l0) + (_dot(tri, l1) + _dot(tri, l2))
    bls = [b[(c + 1) * chunk - 1:(c + 1) * chunk, :] for c in range(nchunk)]
    bl_rows = jnp.concatenate([jnp.broadcast_to(bl, (chunk, bl.shape[1])) for bl in bls], axis=0)
    q_in = (hq_ref[0] * (HG_DK ** -0.5) * jnp.exp(b)).astype(BF16)
    k_in = (kk * jnp.exp(-b)).astype(BF16)
    k_st = (kk * jnp.exp(bl_rows - b)).astype(BF16)
    vb = hi_ref[0].astype(BF16)
    intra = []
    for hh in range(HG_HEADS):
        att = jnp.where(causal, _dot_nt(q_in[:, lanes(hh)], k_in[:, lanes(hh)]), 0.0)
        intra.append(_dot(att.astype(BF16), vb[:, lanes(hh)]))
    upd = {(c, hh): _dot_tn(vb[rows(c), lanes(hh)], k_st[rows(c), lanes(hh)])
           for c in range(nchunk) for hh in range(HG_HEADS)}
    decay = [jnp.exp(bl) for bl in bls]
    for hh in range(HG_HEADS):
        os = []
        for c in range(nchunk):
            st = s_scr[hh]
            os.append(_dot_nt(q_in[rows(c), lanes(hh)], st.astype(BF16)))
            s_scr[hh] = st * decay[c][:, lanes(hh)] + upd[c, hh]
        o = jnp.concatenate(os, axis=0) + intra[hh]
        on = o * lax.rsqrt(jnp.mean(o * o, axis=-1, keepdims=True) + EPS) * g_ref[:, lanes(hh)]
        r_ref[0, :, lanes(hh)] = (on * _silu(hg_ref[0, :, lanes(hh)])).astype(BF16)

    @pl.when(t == pl.num_programs(1) - 1)
    def _():
        for hh in range(HG_HEADS):
            st_ref[0, hh] = s_scr[hh].T


def _hgrn(hq, hf, hi, hg, lb, out_g, s0):
    b, t, w = hq.shape
    chunk = min(HG_CHUNK, t)
    assert t % chunk == 0
    nchunk = min(4, t // chunk)
    tt = chunk * nchunk
    seq = pl.BlockSpec((1, tt, w), lambda i, j: (i, j, 0))
    vec = pl.BlockSpec((1, w), lambda i, j: (0, 0))
    state = pl.BlockSpec((1, HG_HEADS, HG_DK, HG_DV), lambda i, j: (i, 0, 0, 0))
    return pl.pallas_call(
        functools.partial(_hgrn_kernel, chunk=chunk, nchunk=nchunk),
        out_shape=[jax.ShapeDtypeStruct((b, t, w), BF16), jax.ShapeDtypeStruct(s0.shape, F32)],
        grid=(b, t // tt),
        in_specs=[seq, seq, seq, seq, vec, vec, state],
        out_specs=[seq, state],
        scratch_shapes=[pltpu.VMEM((HG_HEADS, HG_DV, HG_DK), F32)],
        compiler_params=_cparams(("arbitrary", "arbitrary")),
        name="hgrn",
    )(hq, hf, hi, hg, lb.reshape(1, w), out_g.reshape(1, w), s0)


def _order_key(score):
    bits = pltpu.bitcast(score, I32)
    bits = jnp.where(bits == INT_MIN, 0, bits)
    return jnp.where(bits >= 0, bits, bits ^ 0x7FFFFFFF)


def _select_threshold(load_keys, store_keys, nchunk, tq, ck, topk, idx_bits, keys_on_rows=False):
    kax = 0 if keys_on_rows else 1
    kshape = (ck, tq) if keys_on_rows else (tq, ck)
    vshape = (1, tq) if keys_on_rows else (tq, 1)

    def count(pred):
        def body(c, acc):
            m = jnp.where(pred(load_keys(c), c), 1.0, 0.0)
            if keys_on_rows:
                parts = [m[j * 8:(j + 1) * 8] for j in range(8)]
                for j in range(8, ck // 8):
                    parts[j % 8] = parts[j % 8] + m[j * 8:(j + 1) * 8]
                while len(parts) > 1:
                    parts = [a + b for a, b in zip(parts[0::2], parts[1::2])]
                part = parts[0]
            else:
                parts = [m[:, j * LANES:(j + 1) * LANES] for j in range(ck // LANES)]
                while len(parts) > 1:
                    parts = [a + b for a, b in zip(parts[0::2], parts[1::2])] + ([parts[-1]] if len(parts) % 2 else [])
                part = parts[0]
            return acc + part
        acc = jnp.zeros((8, tq) if keys_on_rows else (tq, LANES), F32)
        if isinstance(nchunk, int):
            for c in range(nchunk):
                acc = body(c, acc)
        else:
            acc = lax.fori_loop(0, nchunk, body, acc)
        return jnp.sum(acc, axis=kax, keepdims=True)

    kf = float(topk)
    zero_i = jnp.zeros(vshape, I32)

    def bisect(nbits, count_ge, done0):
        def cond(st):
            return jnp.logical_and(st[0] < nbits, jnp.min(st[2]) < 0.5)

        def body(st):
            it, tu, done, hit_u = st
            for _ in range(BISECT_BITS_PER_TRIP):
                cand = tu | lax.shift_left(jnp.int32(1), nbits - 1 - it)
                cnt = count_ge(cand)
                active = done < 0.5
                tu = jnp.where(jnp.logical_and(active, cnt >= kf), cand, tu)
                hit = jnp.logical_and(active, cnt == kf)
                hit_u = jnp.where(hit, cand, hit_u)
                done = jnp.where(hit, 1.0, done)
                it = it + 1
            return it, tu, done, hit_u
        return lax.while_loop(cond, body, (jnp.int32(0), zero_i, done0, zero_i))[1:]

    tu, done, hit_u = bisect(32, lambda u: count(lambda key, c: key >= (u ^ INT_MIN)), jnp.zeros(vshape, F32))
    ts = tu ^ INT_MIN
    tsel = (hit_u ^ INT_MIN) - 1
    need_tie = jnp.logical_and(done < 0.5, tu != 0)
    thr = jnp.where(done > 0.5, tsel, ts)

    @pl.when(jnp.max(jnp.where(need_tie, 1.0, 0.0)) > 0.5)
    def _():
        rank = kf - count(lambda key, c: key > ts)
        pos_of = lambda c: c * ck + lax.broadcasted_iota(I32, kshape, kax)

        def jbody(it, ju):
            cand = ju | lax.shift_left(jnp.int32(1), idx_bits - 1 - it)
            pred = lambda key, c: jnp.logical_and(key == ts, pos_of(c) < cand)
            return jnp.where(count(pred) < rank, cand, ju)
        ju = lax.fori_loop(0, idx_bits, jbody, zero_i)

        def fix(c, carry):
            key = load_keys(c)
            bump = jnp.logical_and(jnp.logical_and(key == ts, pos_of(c) <= ju), need_tie)
            store_keys(c, jnp.where(bump, key + 1, key))
            return carry
        lax.fori_loop(0, nchunk, fix, 0)
    return thr


def _prompt_attn_kernel(q_ref, qi_ref, wt_ref, k4_ref, kb_ref, vt_ref, ga_ref, o_ref,
                        key_scr, mm_scr, m_scr, acc_scr, *, topk, idx_bits):
    tq, ck = q_ref.shape[1], ATT_CHUNK
    i = pl.program_id(1)
    nchunk = (i * tq + tq + ck - 1) // ck
    qpos = i * tq + lax.broadcasted_iota(I32, (1, tq), 1)

    qit = qi_ref[0].T
    hit = qit.astype(BF16)
    lot = (qit - hit.astype(F32)).astype(BF16)
    zero = jnp.zeros((IDX_DIM, tq), BF16)

    def idx_rhs(hh):
        hs = slice(hh * IDX_DIM, (hh + 1) * IDX_DIM)
        return jnp.concatenate([hit[hs], hit[hs], lot[hs], zero], axis=0)
    rhs_s = [jnp.concatenate([idx_rhs(2 * p), idx_rhs(2 * p + 1)], axis=1) for p in range(IDX_HEADS // 2)]
    wt = wt_ref[0]

    chunk = lambda c: pl.ds(pl.multiple_of(c * ck, ck), ck)

    def pipelined(mxu_stage, vpu_stage):
        mxu_stage(0, 0)

        def pair_body(j, carry):
            mxu_stage(2 * j + 1, 1)
            vpu_stage(2 * j, 0)
            mxu_stage(jnp.minimum(2 * j + 2, nchunk - 1), 0)
            vpu_stage(2 * j + 1, 1)
            return carry
        lax.fori_loop(0, nchunk // 2, pair_body, 0)

        @pl.when(nchunk % 2 == 1)
        def _():
            vpu_stage(nchunk - 1, 0)

    def score_products(c, slot):
        k4 = k4_ref[0, chunk(c), :]
        for p in range(IDX_HEADS // 2):
            mm_scr[slot, p] = _dot(k4, rhs_s[p])

    def score_keys(c, slot):
        acc = jnp.zeros((ck, tq), F32)
        for p in range(IDX_HEADS // 2):
            s = mm_scr[slot, p]
            acc = acc + jnp.maximum(s[:, :tq], 0.0) * wt[2 * p:2 * p + 1]
            acc = acc + jnp.maximum(s[:, tq:], 0.0) * wt[2 * p + 1:2 * p + 2]
        kpos = c * ck + lax.broadcasted_iota(I32, (ck, 1), 0)
        key_scr[chunk(c), :] = jnp.where(kpos <= qpos, _order_key(acc), INT_MIN)
    pipelined(score_products, score_keys)

    def store_keys(c, val):
        key_scr[chunk(c), :] = val
    thr = _select_threshold(lambda c: key_scr[chunk(c), :], store_keys, nchunk, tq, ck, topk, idx_bits,
                            keys_on_rows=True)

    qt = q_ref[0].astype(F32).T.astype(BF16)
    rhs_a = []
    for m in range(KV_HEADS):
        blk = jnp.concatenate([qt[(2 * m) * HEAD_DIM:(2 * m + 1) * HEAD_DIM],
                               qt[(2 * m + 1) * HEAD_DIM:(2 * m + 2) * HEAD_DIM]], axis=1)
        parts = [jnp.zeros((HEAD_DIM, 2 * tq), BF16)] * m + [blk] + [jnp.zeros((HEAD_DIM, 2 * tq), BF16)] * (KV_HEADS - 1 - m)
        rhs_a.append(jnp.concatenate(parts, axis=0))
    m_scr[...] = jnp.full(m_scr.shape, NEG_BIG, F32)
    acc_scr[...] = jnp.zeros(acc_scr.shape, F32)

    def logits(c, slot):
        kc = kb_ref[0, chunk(c), :]
        for m in range(KV_HEADS):
            mm_scr[slot, m] = _dot(kc, rhs_a[m])

    def softmax_pv(c, slot):
        sel = key_scr[chunk(c), :] > thr
        pas = []
        for m in range(KV_HEADS):
            lg = mm_scr[slot, m]
            lg = jnp.concatenate([jnp.where(sel, lg[:, :tq], NEG_BIG), jnp.where(sel, lg[:, tq:], NEG_BIG)], axis=1)
            mx = m_scr[m]
            mn = jnp.maximum(mx, jnp.max(lg, axis=0, keepdims=True))
            m_scr[m] = mn
            pas.append((jnp.exp2(lg - mn).astype(BF16), jnp.exp2(mx - mn)))
        for m in range(KV_HEADS):
            p, alpha = pas[m]
            acc_scr[m] = acc_scr[m] * alpha + _dot(vt_ref[0, m, :, chunk(c)], p)

    pipelined(logits, softmax_pv)
    pieces = []
    for m in range(KV_HEADS):
        acc = acc_scr[m]
        o = acc[0:HEAD_DIM] / acc[HEAD_DIM:HEAD_DIM + 1]
        pieces += [o[:, :tq], o[:, tq:]]
    at = jnp.concatenate(pieces, axis=0)
    at = at * lax.rsqrt(jnp.mean(at * at, axis=0, keepdims=True) + EPS)
    o_ref[0] = (at.T * ga_ref[...]).astype(BF16)


def _prompt_attention(q, qi, wt, k4, kb, vt, ga):
    b, t, _ = q.shape
    tq = LANES
    assert t % ATT_CHUNK == 0
    topk = min(TOPK_MAX, t // 4)
    blk = lambda w: pl.BlockSpec((1, tq, w), lambda bi, i: (bi, i, 0))
    full = lambda r, c: pl.BlockSpec((1, r, c), lambda bi, i: (bi, 0, 0))
    return pl.pallas_call(
        functools.partial(_prompt_attn_kernel, topk=topk, idx_bits=max(1, (t - 1).bit_length())),
        out_shape=jax.ShapeDtypeStruct((b, t, ATT_WIDTH), BF16),
        grid=(b, t // tq),
        in_specs=[blk(ATT_WIDTH), blk(ATT_WIDTH),
                  pl.BlockSpec((1, IDX_HEADS, tq), lambda bi, i: (bi, 0, i)),
                  full(t, 4 * IDX_DIM), full(t, KV_WIDTH),
                  pl.BlockSpec((1, KV_HEADS, VT_ROWS, t), lambda bi, i: (bi, 0, 0, 0)),
                  pl.BlockSpec((1, ATT_WIDTH), lambda bi, i: (0, 0))],
        out_specs=blk(ATT_WIDTH),
        scratch_shapes=[pltpu.VMEM((t, tq), I32), pltpu.VMEM((2, KV_HEADS, ATT_CHUNK, 2 * tq), F32),
                        pltpu.VMEM((KV_HEADS, 1, 2 * tq), F32),
                        pltpu.VMEM((KV_HEADS, VT_ROWS, 2 * tq), F32)],
        compiler_params=_cparams(("arbitrary", "arbitrary")),
        name="prompt_attn",
    )(q, qi, wt, k4, kb, vt, ga.reshape(1, ATT_WIDTH))


def _sample_score_kernel(pt_ref, qi_ref, w_ref, kin_ref, cache_ref, key_ref, thr_ref, kbuf, sem, *,
                         layer, n_pages, npg, tnew, topk, idx_bits):
    b = pl.program_id(0)
    nb = pl.num_programs(0)
    slot = b % 2
    tq = tnew
    page = kbuf.shape[3]
    ck = npg * page
    nsteps = n_pages // npg

    def start_pages(row, sl):
        for p in range(n_pages):
            pltpu.make_async_copy(cache_ref.at[layer, pt_ref[row * n_pages + p]], kbuf.at[sl, p], sem.at[sl]).start()

    @pl.when(b == 0)
    def _():
        start_pages(0, 0)

    @pl.when(b + 1 < nb)
    def _():
        start_pages(b + 1, 1 - slot)

    pltpu.make_async_copy(cache_ref.at[layer, pl.ds(0, n_pages)], kbuf.at[slot], sem.at[slot]).wait()

    def keys_of(kt, ok):
        k_hi, k_lo = _split_bf16(kt)
        sc = _dot(qi_ref[0], jnp.concatenate([k_hi, k_lo, k_hi], axis=0))
        sc = jnp.maximum(sc, 0.0) * w_ref[0]
        acc = sc[0:tq]
        for hh in range(1, IDX_HEADS):
            acc = acc + sc[hh * tq:(hh + 1) * tq]
        return _order_key(acc) if ok is None else jnp.where(ok, _order_key(acc), INT_MIN)

    def chunk_body(c, carry):
        pages = kbuf[slot, pl.ds(c * npg, npg)]
        kt = jnp.concatenate([pages[p] for p in range(npg)], axis=1)
        key_ref[0, :, pl.ds(pl.multiple_of(c * ck, ck), ck)] = keys_of(kt, None)
        return carry
    lax.fori_loop(0, nsteps, chunk_body, 0)
    qrow = lax.broadcasted_iota(I32, (tq, ck), 0)
    col = lax.broadcasted_iota(I32, (tq, ck), 1)
    new = jnp.concatenate([kin_ref[0], jnp.zeros((kin_ref.shape[1], ck - page), F32)], axis=1)
    key_ref[0, :, pl.ds(nsteps * ck, ck)] = keys_of(new, jnp.logical_and(col <= qrow, col < tnew))

    def load_keys(c):
        return key_ref[0, :, pl.ds(pl.multiple_of(c * ck, ck), ck)]

    def store_keys(c, val):
        key_ref[0, :, pl.ds(pl.multiple_of(c * ck, ck), ck)] = val
    thr = _select_threshold(load_keys, store_keys, nsteps + 1, tq, ck, topk, idx_bits)
    thr_ref[0] = jnp.broadcast_to(thr, (tq, LANES))


def _sample_attn_kernel(pt_ref, q_ref, key_ref, keyn_ref, thr_ref, kn_ref, vn_ref, ga_ref, ck_ref, cv_ref, o_ref,
                        kbuf, vbuf, sem, m_scr, l_scr, acc_scr, *, layer, n_pages, npg, tnew):
    b, s = pl.program_id(0), pl.program_id(1)
    ns = pl.num_programs(1)
    g = b * ns + s
    slot = g % 2
    pg = kbuf.shape[1]
    tq = tnew

    def start_pages(gg, sl):
        base = (gg // ns) * n_pages + (gg % ns) * pg
        for p in range(pg):
            pid = pt_ref[base + p]
            pltpu.make_async_copy(ck_ref.at[layer, pid], kbuf.at[sl, p], sem.at[0, sl]).start()
            pltpu.make_async_copy(cv_ref.at[layer, pid], vbuf.at[sl, p], sem.at[1, sl]).start()

    @pl.when(g == 0)
    def _():
        start_pages(0, 0)

    @pl.when(g + 1 < pl.num_programs(0) * ns)
    def _():
        start_pages(g + 1, 1 - slot)

    pltpu.make_async_copy(ck_ref.at[layer, pl.ds(0, pg)], kbuf.at[slot], sem.at[0, slot]).wait()
    pltpu.make_async_copy(cv_ref.at[layer, pl.ds(0, pg)], vbuf.at[slot], sem.at[1, slot]).wait()

    @pl.when(s == 0)
    def _():
        m_scr[...] = jnp.full(m_scr.shape, NEG_BIG, F32)
        l_scr[...] = jnp.zeros(l_scr.shape, F32)
        acc_scr[...] = jnp.zeros(acc_scr.shape, F32)

    thr = thr_ref[0][:, 0:1]

    def attend(keys, kget, vget):
        sel = keys > thr
        sel2 = jnp.concatenate([sel, sel], axis=0)
        lgs = [_dot(q_ref[0, m], kget(m)) for m in range(KV_HEADS)]
        prs = []
        for m in range(KV_HEADS):
            lg = jnp.where(sel2, lgs[m], NEG_BIG)
            mx = m_scr[m]
            mn = jnp.maximum(mx, jnp.max(lg, axis=-1, keepdims=True))
            pr = jnp.exp2(lg - mn)
            alpha = jnp.exp2(mx - mn)
            l_scr[m] = l_scr[m] * alpha + jnp.sum(pr, axis=-1, keepdims=True)
            m_scr[m] = mn
            prs.append((pr.astype(BF16), alpha))
        for m in range(KV_HEADS):
            pr, alpha = prs[m]
            acc_scr[m] = acc_scr[m] * alpha + _dot_nt(pr, vget(m))

    page = kbuf.shape[4]
    for j in range(pg // npg):
        cached = lambda buf: (lambda m: jnp.concatenate(
            [buf[slot, j * npg + p, m] for p in range(npg)], axis=1).astype(BF16))
        attend(key_ref[0, :, j * npg * page:(j + 1) * npg * page], cached(kbuf), cached(vbuf))

    @pl.when(s == ns - 1)
    def _():
        attend(keyn_ref[0], lambda m: kn_ref[0, m].astype(BF16), lambda m: vn_ref[0, m].astype(BF16))
        pieces = []
        for m in range(KV_HEADS):
            o = acc_scr[m] / l_scr[m]
            pieces += [o[:tq], o[tq:]]
        a = jnp.concatenate(pieces, axis=1)
        a = a * lax.rsqrt(jnp.mean(a * a, axis=-1, keepdims=True) + EPS) * ga_ref[...]
        o_ref[0] = a.astype(BF16)


def _sample_attention(q, qi, wi, ki_new, k_new, v_new, cache_k, cache_v, cache_ki, layer, page_table, ga):
    db, tn, _ = q.shape
    page = cache_ki.shape[2]
    cache_k = cache_k.transpose(0, 1, 3, 4, 2)
    cache_v = cache_v.transpose(0, 1, 3, 4, 2)
    cache_ki = cache_ki.transpose(0, 1, 3, 2)
    n_pages = page_table.shape[1]
    npg_s = math.gcd(SAMPLE_SCORE_PAGES, n_pages)
    lpad = (n_pages // npg_s + 1) * npg_s * page
    topk = min(TOPK_MAX, (n_pages * page + tn) // 4)
    assert tn <= page
    pad_keys = lambda a: jnp.pad(a, [(0, 0)] * (a.ndim - 1) + [(0, page - tn)])
    kin = pad_keys(ki_new.transpose(0, 2, 1))
    heads = lambda a: pad_keys(a.reshape(db, tn, KV_HEADS, HEAD_DIM).transpose(0, 2, 3, 1))
    qi_hq = qi.reshape(db, tn, IDX_HEADS, IDX_DIM).transpose(0, 2, 1, 3).reshape(db, IDX_HEADS * tn, IDX_DIM)
    qi_hi = qi_hq.astype(BF16)
    qi_hq = jnp.concatenate([qi_hi, qi_hi, (qi_hq - qi_hi.astype(F32)).astype(BF16)], axis=-1)
    w_hq = wi.transpose(0, 2, 1).reshape(db, IDX_HEADS * tn, 1)
    q2 = (q.reshape(db, tn, KV_HEADS, ATT_HEADS // KV_HEADS, HEAD_DIM).transpose(0, 2, 3, 1, 4)
          .reshape(db, KV_HEADS, (ATT_HEADS // KV_HEADS) * tn, HEAD_DIM))
    pt_flat = page_table.reshape(-1).astype(I32)

    per_b = lambda *tail: pl.BlockSpec((1,) + tail, lambda b, s, pt: (b,) + (0,) * len(tail))
    row_b = lambda *tail: pl.BlockSpec((1,) + tail, lambda b, pt: (b,) + (0,) * len(tail))
    keys, thr = pl.pallas_call(
        functools.partial(_sample_score_kernel, layer=layer, n_pages=n_pages, npg=npg_s, tnew=tn, topk=topk,
                          idx_bits=max(1, (lpad - 1).bit_length())),
        out_shape=[jax.ShapeDtypeStruct((db, tn, lpad), I32),
                   jax.ShapeDtypeStruct((db, tn, LANES), I32)],
        grid_spec=pltpu.PrefetchScalarGridSpec(
            num_scalar_prefetch=1,
            grid=(db,),
            in_specs=[row_b(IDX_HEADS * tn, 3 * IDX_DIM), row_b(IDX_HEADS * tn, 1), row_b(IDX_DIM, page),
                      pl.BlockSpec(memory_space=pl.ANY)],
            out_specs=[row_b(tn, lpad), row_b(tn, LANES)],
            scratch_shapes=[pltpu.VMEM((2, n_pages, IDX_DIM, page), F32), pltpu.SemaphoreType.DMA((2,))]),
        compiler_params=_cparams(("arbitrary",)),
        name="sample_score",
    )(pt_flat, qi_hq, w_hq, kin, cache_ki)

    pg = math.gcd(SAMPLE_ATTN_GROUP, n_pages)
    npg_a = math.gcd(npg_s, pg)
    hbm = pl.BlockSpec(memory_space=pl.ANY)
    a = pl.pallas_call(
        functools.partial(_sample_attn_kernel, layer=layer, n_pages=n_pages, npg=npg_a, tnew=tn),
        out_shape=jax.ShapeDtypeStruct((db, tn, ATT_WIDTH), BF16),
        grid_spec=pltpu.PrefetchScalarGridSpec(
            num_scalar_prefetch=1,
            grid=(db, n_pages // pg),
            in_specs=[per_b(KV_HEADS, (ATT_HEADS // KV_HEADS) * tn, HEAD_DIM),
                      pl.BlockSpec((1, tn, pg * page), lambda b, s, pt: (b, 0, s)),
                      pl.BlockSpec((1, tn, page), lambda b, s, pt: (b, 0, n_pages)),
                      per_b(tn, LANES),
                      per_b(KV_HEADS, HEAD_DIM, page), per_b(KV_HEADS, HEAD_DIM, page),
                      pl.BlockSpec((1, ATT_WIDTH), lambda b, s, pt: (0, 0)), hbm, hbm],
            out_specs=per_b(tn, ATT_WIDTH),
            scratch_shapes=[pltpu.VMEM((2, pg, KV_HEADS, HEAD_DIM, page), F32),
                            pltpu.VMEM((2, pg, KV_HEADS, HEAD_DIM, page), F32),
                            pltpu.SemaphoreType.DMA((2, 2)),
                            pltpu.VMEM((KV_HEADS, 2 * tn, 1), F32), pltpu.VMEM((KV_HEADS, 2 * tn, 1), F32),
                            pltpu.VMEM((KV_HEADS, 2 * tn, HEAD_DIM), F32)]),
        compiler_params=_cparams(("arbitrary", "arbitrary")),
        name="sample_attn",
    )(pt_flat, q2, keys, keys, thr, heads(k_new), heads(v_new), ga.reshape(1, ATT_WIDTH), cache_k, cache_v)
    return a


def _outproj_kernel(a_ref, r_ref, x_ref, gt_ref, sc_ref, sh_ref, g_ref, wa_ref, wr_ref, wrt_ref, brt_ref,
                    x2_ref, h2_ref, route_ref):
    mix = _dot(a_ref[...], wa_ref[...]) + _dot(r_ref[...], wr_ref[...])
    x2 = x_ref[...] + gt_ref[0] * mix
    x2_ref[...] = x2
    h = x2 * lax.rsqrt(jnp.mean(x2 * x2, axis=-1, keepdims=True) + EPS) * g_ref[...]
    h = h * (1.0 + sc_ref[0]) + sh_ref[0]
    h2_ref[...] = h
    lg = (_dot3(h, wrt_ref[...]) + brt_ref[...]).T
    grp = [lg[g:g + 1] for g in range(N_GROUPS)]
    gmax = functools.reduce(jnp.maximum, grp)
    gden = functools.reduce(lambda u, v: u + v, [jnp.exp(g - gmax) for g in grp])
    gsel = jnp.full(gmax.shape, N_GROUPS - 1, I32)
    for g in range(N_GROUPS - 2, -1, -1):
        gsel = jnp.where(grp[g] == gmax, g, gsel)
    gw = 1.0 / gden
    el = []
    for e in range(EXPERTS_PER_GROUP):
        v = lg[N_GROUPS + e:N_GROUPS + e + 1]
        for g in range(1, N_GROUPS):
            row = N_GROUPS + g * EXPERTS_PER_GROUP + e
            v = jnp.where(gsel == g, lg[row:row + 1], v)
        el.append(v)
    emax = functools.reduce(jnp.maximum, el)
    e0 = jnp.full(emax.shape, EXPERTS_PER_GROUP - 1, I32)
    for e in range(EXPERTS_PER_GROUP - 2, -1, -1):
        e0 = jnp.where(el[e] == emax, e, e0)
    rest = [jnp.where(e0 == e, -jnp.inf, el[e]) for e in range(EXPERTS_PER_GROUP)]
    rmax = functools.reduce(jnp.maximum, rest)
    e1 = jnp.full(emax.shape, EXPERTS_PER_GROUP - 1, I32)
    for e in range(EXPERTS_PER_GROUP - 2, -1, -1):
        e1 = jnp.where(jnp.logical_and(rest[e] == rmax, e0 != e), e, e1)
    p1 = jnp.exp(rmax - emax)
    w0 = gw * (1.0 / (1.0 + p1))
    w1 = gw * (p1 / (1.0 + p1))
    base = gsel * EXPERTS_PER_GROUP
    rid = lax.broadcasted_iota(I32, lg.shape, 0)
    rt = jnp.where(rid == 0, (base + e0).astype(F32),
                   jnp.where(rid == 1, (base + e1).astype(F32),
                             jnp.where(rid == 2, w0, jnp.where(rid == 3, w1, 0.0))))
    route_ref[...] = rt.T


def _outproj(a, r, x2d, gt, sc, sh, g2, wa, wr, wrt, brt, tiles_per_group):
    n, d = x2d.shape
    tm = min(PROJ_TILE, n)
    rr = gt.shape[1]
    row = lambda w: pl.BlockSpec((tm, w), lambda i: (i, 0))
    mod = pl.BlockSpec((1, rr, d), lambda i: (i // tiles_per_group, 0, 0))
    const = lambda s: pl.BlockSpec(s, lambda i: (0, 0))
    return pl.pallas_call(
        _outproj_kernel,
        out_shape=[jax.ShapeDtypeStruct((n, d), F32), jax.ShapeDtypeStruct((n, d), F32),
                   jax.ShapeDtypeStruct((n, LANES), F32)],
        grid=(n // tm,),
        in_specs=[row(ATT_WIDTH), row(HG_WIDTH), row(d), mod, mod, mod, const((1, d)),
                  const(wa.shape), const(wr.shape), const(wrt.shape), const(brt.shape)],
        out_specs=[row(d), row(d), row(LANES)],
        compiler_params=_cparams(("arbitrary",)),
        name="outproj",
    )(a, r, x2d, gt, sc, sh, g2.reshape(1, d), wa, wr, wrt, brt)


def _gather_pipeline(step, nsteps, idx_hbm, src_hbm, buf, idx_smem, isem, rsem):
    nrows = buf.shape[1]
    slot = step % 2

    def idx_copy(b, sl):
        return pltpu.make_async_copy(idx_hbm.at[b], idx_smem.at[sl], isem.at[sl])

    def start_rows(sl):
        for r in range(nrows):
            pltpu.make_async_copy(src_hbm.at[pl.ds(idx_smem[sl, 0, r], 1), :],
                                  buf.at[sl, pl.ds(r, 1), :], rsem.at[sl]).start(priority=r % 2)

    @pl.when(step == 0)
    def _():
        idx_copy(0, 0).start()
        idx_copy(0, 0).wait()
        start_rows(0)

        @pl.when(nsteps > 1)
        def _():
            idx_copy(1, 1).start()

    @pl.when(step + 1 < nsteps)
    def _():
        idx_copy(step + 1, 1 - slot).wait()
        start_rows(1 - slot)

    @pl.when(step + 2 < nsteps)
    def _():
        idx_copy(step + 2, slot).start()

    pltpu.make_async_copy(src_hbm.at[pl.ds(0, nrows), :], buf.at[slot], rsem.at[slot]).wait()
    return slot


def _ffn_kernel(be_ref, nb_ref, tok_ref, h_ref, w1_ref, w3_ref, w2_ref, y_ref, xbuf, idx_smem, isem, rsem):
    j = pl.program_id(0)

    @pl.when(j < nb_ref[0])
    def _():
        slot = _gather_pipeline(j, nb_ref[0], tok_ref, h_ref, xbuf, idx_smem, isem, rsem)
        xb = xbuf[slot].astype(BF16)
        u = _dot(xb, w1_ref[0])
        g = _dot(xb, w3_ref[0])
        y_ref[...] = _dot((_silu(u) * g).astype(BF16), w2_ref[0])

    @pl.when(j >= nb_ref[0])
    def _():
        y_ref[...] = jnp.zeros(y_ref.shape, F32)


def _ffn(blk_e, n_used, tok, h2, w1, w3, w2, blk):
    nb = tok.shape[0]
    d = h2.shape[1]
    wspec = lambda s: pl.BlockSpec((1,) + s, lambda j, be, nu: (be[j], 0, 0))
    return pl.pallas_call(
        _ffn_kernel,
        out_shape=jax.ShapeDtypeStruct((nb * blk, d), F32),
        grid_spec=pltpu.PrefetchScalarGridSpec(
            num_scalar_prefetch=2,
            grid=(nb,),
            in_specs=[pl.BlockSpec(memory_space=pl.ANY), pl.BlockSpec(memory_space=pl.ANY),
                      wspec(w1.shape[1:]), wspec(w3.shape[1:]), wspec(w2.shape[1:])],
            out_specs=pl.BlockSpec((blk, d), lambda j, be, nu: (j, 0)),
            scratch_shapes=[pltpu.VMEM((2, blk, d), F32), pltpu.SMEM((2, 1, blk), I32),
                            pltpu.SemaphoreType.DMA((2,)), pltpu.SemaphoreType.DMA((2,))]),
        compiler_params=_cparams(("arbitrary",)),
        name="moe_ffn",
    )(blk_e, n_used, tok, h2, w1, w3, w2)


def _combine_kernel(dd_ref, y_ref, x2_ref, route_ref, gt_ref, sc_ref, sh_ref, g_ref, o_ref,
                    ybuf, idx_smem, isem, rsem):
    tm = x2_ref.shape[0]
    slot = _gather_pipeline(pl.program_id(0), pl.num_programs(0), dd_ref, y_ref, ybuf, idx_smem, isem, rsem)
    rt = route_ref[...]
    moe = ybuf[slot, 0:tm] * rt[:, 2:3] + ybuf[slot, tm:2 * tm] * rt[:, 3:4]
    x = x2_ref[...] + gt_ref[0] * moe
    y = x * lax.rsqrt(jnp.mean(x * x, axis=-1, keepdims=True) + EPS) * g_ref[...]
    o_ref[...] = y * (1.0 + sc_ref[0]) + sh_ref[0]


def _combine(dd, yb, x2, route, gt, sc, sh, gf, tiles_per_group):
    n, d = x2.shape
    tm = dd.shape[2] // 2
    rr = gt.shape[1]
    row = pl.BlockSpec((tm, d), lambda i: (i, 0))
    mod = pl.BlockSpec((1, rr, d), lambda i: (i // tiles_per_group, 0, 0))
    hbm = pl.BlockSpec(memory_space=pl.ANY)
    return pl.pallas_call(
        _combine_kernel,
        out_shape=jax.ShapeDtypeStruct((n, d), F32),
        grid=(n // tm,),
        in_specs=[hbm, hbm, row, pl.BlockSpec((tm, LANES), lambda i: (i, 0)), mod, mod, mod,
                  pl.BlockSpec((1, d), lambda i: (0, 0))],
        out_specs=row,
        scratch_shapes=[pltpu.VMEM((2, 2 * tm, d), F32), pltpu.SMEM((2, 1, 2 * tm), I32),
                        pltpu.SemaphoreType.DMA((2,)), pltpu.SemaphoreType.DMA((2,))],
        compiler_params=_cparams(("arbitrary",)),
        name="moe_combine",
    )(dd, yb, x2, route, gt, sc, sh, gf.reshape(1, d))


def _dispatch(route, blk):
    n = route.shape[0]
    flat_e = route[:, 0:2].astype(I32).reshape(-1)
    a = flat_e.shape[0]
    onehot = (flat_e[:, None] == jnp.arange(N_EXPERTS, dtype=I32)[None, :]).astype(I32)
    csum = jnp.cumsum(onehot, axis=0)
    rank = jnp.sum((csum - onehot) * onehot, axis=1)
    counts = csum[-1]
    padded = (counts + blk - 1) // blk * blk
    pad_end = jnp.cumsum(padded)
    pad_start = pad_end - padded
    dest = pad_start[flat_e] + rank
    nb = -(-a // blk) + N_EXPERTS
    tok = jnp.zeros((nb * blk,), I32).at[dest].set(jnp.arange(a, dtype=I32) // 2)
    blk_e = jnp.minimum(jnp.searchsorted(pad_end, jnp.arange(nb, dtype=I32) * blk, side='right'),
                        N_EXPERTS - 1).astype(I32)
    n_used = (pad_end[-1] // blk).astype(I32).reshape(1)
    dest2 = dest.reshape(n, 2)
    return tok.reshape(nb, 1, blk), blk_e, n_used, dest2[:, 0], dest2[:, 1]


def _mods(m, n_chunks, per_token_rows):
    parts = jnp.split(m, n_chunks, axis=-1)
    if per_token_rows is None:
        return [p[:, None, :] for p in parts]
    g, d = parts[0].shape
    tm = min(PROJ_TILE, g * per_token_rows)
    return [jnp.repeat(p, per_token_rows, axis=0).reshape(-1, tm, d) for p in parts]


def _layer(x, mod6, modf, pos, s0, attend, lb, wts, final_g, per_token):
    b, t, d = x.shape
    n = b * t
    (norm1_g, norm2_g, w_r, gk, ga, hg_out_g, wa, wr, wrt, brt, w1, w3, w2) = wts
    tm = min(PROJ_TILE, n)
    tiles_per_group = 1 if per_token else t // tm
    sh1, sc1, gt1, sh2, sc2, gt2 = _mods(mod6, 6, t if per_token else None)
    shf, scf = _mods(modf, 2, t if per_token else None)
    tabs = _rope_tables(jnp.tile(pos, tm // t) if per_token else pos)
    x2d = x.reshape(n, d)
    sq = lambda arr: arr.reshape(b, t, arr.shape[-1])
    if per_token:
        q, k, v, qi, ki, wi, hq, hf, hi, hg = _proj(x2d, sc1, sh1, norm1_g, w_r, gk, tabs, tiles_per_group, 0)
        a = attend(sq(q), sq(qi), sq(wi), sq(ki), sq(k), sq(v), ga)
        k_out, v_out, ki_out = k.reshape(b, t, KV_HEADS, HEAD_DIM), v.reshape(b, t, KV_HEADS, HEAD_DIM), sq(ki)
    else:
        q, kt, vt, kb, vto, qi, kit, k4, wt, hq, hf, hi, hg = _proj(x2d, sc1, sh1, norm1_g, w_r, gk, tabs,
                                                                   tiles_per_group, b)
        a = attend(sq(q), sq(qi), wt, sq(k4), sq(kb), vto, ga)
        heads = lambda xt: xt.reshape(b, KV_HEADS, HEAD_DIM, t).transpose(0, 3, 1, 2)
        k_out, v_out, ki_out = heads(kt), heads(vt), kit.transpose(0, 2, 1)
    r, s_t = _hgrn(sq(hq), sq(hf), sq(hi), sq(hg), lb, hg_out_g, s0)
    x2, h2, route = _outproj(a.reshape(n, -1), r.reshape(n, -1), x2d, gt1, sc2, sh2, norm2_g,
                             wa, wr, wrt, brt, tiles_per_group)
    blk = MOE_BLOCK if n >= 8 * MOE_BLOCK else 64
    tok, blk_e, n_used, d0, d1 = _dispatch(route, blk)
    yb = _ffn(blk_e, n_used, tok, h2, w1, w3, w2, blk)
    dd = jnp.concatenate([d0.reshape(n // tm, 1, tm), d1.reshape(n // tm, 1, tm)], axis=2)
    y = _combine(dd, yb, x2, route, gt2, scf, shf, final_g, tiles_per_group)
    return y.reshape(b, t, d), k_out, v_out, ki_out, s_t


def kernel(x_prompt, x_sample, cache_k, cache_v, cache_kidx, state_hgrn, page_table, c_prompt, c_sample,
           ada_w, ada_b, norm1_g, norm2_g, w_in, idx_k_g, hg_lb_logits, attn_out_g, hg_out_g, w_out,
           w_group, b_group, w_expert_router, b_expert_router, w1, w3, w2, final_g, ada_final_w, ada_final_b):
    depth = ada_w.shape[0]
    assert depth == 1, "the final adaLN norm is fused into the (single) layer's combine kernel"
    bp, tp, d = x_prompt.shape
    bs, ts, _ = x_sample.shape
    past = page_table.shape[1] * cache_kidx.shape[2]
    lb_all = jnp.cumsum(jax.nn.softmax(hg_lb_logits.astype(F32), axis=0), axis=0)
    n_c = bp + bs
    c_all = jnp.concatenate([c_prompt, c_sample, jnp.zeros((-n_c % 16, d), F32)], axis=0)
    modf = _ada(c_all, ada_final_w, ada_final_b)
    l = 0
    mod6 = _ada(c_all, ada_w[l], ada_b[l])

    seg = [0]
    for s in (ATT_WIDTH, KV_WIDTH, KV_WIDTH, IDX_HEADS * IDX_DIM, IDX_DIM, IDX_HEADS,
              HG_WIDTH, HG_WIDTH, HG_WIDTH, HG_WIDTH):
        seg.append(seg[-1] + s)
    wl = w_in[l]
    col = lambda i: wl[:, seg[i]:seg[i + 1]]
    zpad = lambda w: jnp.zeros((d, w), wl.dtype)
    w_r = jnp.concatenate([col(0), col(1), col(2), col(3), col(6), col(7), col(8), col(9),
                           col(4), zpad(LANES - IDX_DIM), col(5), zpad(LANES - IDX_HEADS)], axis=1).astype(BF16)
    gk = jnp.concatenate([idx_k_g[l], jnp.zeros((LANES - IDX_DIM,), F32)]).reshape(1, LANES)
    wa = w_out[l, :ATT_WIDTH].astype(BF16)
    wr = w_out[l, ATT_WIDTH:].astype(BF16)
    n_rt = N_GROUPS + N_EXPERTS
    wrt = jnp.concatenate([w_group[l], w_expert_router[l], jnp.zeros((d, LANES - n_rt), F32)], axis=1)
    brt = jnp.concatenate([b_group[l], b_expert_router[l], jnp.zeros((LANES - n_rt,), F32)]).reshape(1, LANES)
    wts = (norm1_g[l], norm2_g[l], w_r, gk, attn_out_g[l], hg_out_g[l], wa, wr, wrt, brt,
           w1[l].astype(BF16), w3[l].astype(BF16), w2[l].astype(BF16))

    def attend_s(q, qi, wi, ki, k, v, ga):
        return _sample_attention(q, qi, wi, ki, k, v, cache_k, cache_v, cache_kidx, l, page_table, ga)

    s0_p = jnp.zeros((bp, HG_HEADS, HG_DK, HG_DV), F32)
    yp, kp, vp, kip, sp = _layer(x_prompt, mod6[:bp], modf[:bp], jnp.arange(tp), s0_p, _prompt_attention,
                                 lb_all[l], wts, final_g, per_token=False)
    ys, ks, vs, kis, ss = _layer(x_sample, mod6[bp:n_c], modf[bp:n_c], past + jnp.arange(ts), state_hgrn[l],
                                 attend_s, lb_all[l], wts, final_g, per_token=True)
    return (yp, ys, kp[None], vp[None], kip[None], sp[None], ks[None], vs[None], kis[None], ss[None])
```
